```python
import math
import jax, jax.numpy as jnp
from jax import lax
import numpy as np

D_MODEL = 1024
BATCH = 8
SEQ = 2048
DEPTH = 2
DEC_BATCH = 128
DEC_SEQ = 1
PAST_LEN = 16384
PAGE_SIZE = 128

D_A = D_MODEL // 2
D_B = D_MODEL // 2
D_MIX = D_A + D_B
N_POOL = 4
POOL_WINDOWS = (2, 4, 8, 16)
POOL_GC = D_A // N_POOL
POOL_BUF = max(POOL_WINDOWS) - 1
HEAD_DIM = 128
N_HEADS_B = D_B // HEAD_DIM
D_QKV = 3 * D_B
CONV_K = 4
GDN_CHUNK = 64
D_IN = 2 * D_A + D_QKV + D_B + 2 * N_HEADS_B
EPS = 1e-6

kernel_name = 'hybrid_pool_gdn_decoder_step'


def rms_norm(x, w):
    xf = x.astype(jnp.float32)
    y = xf * lax.rsqrt(jnp.mean(xf * xf, axis=-1, keepdims=True) + EPS)
    return (y * w.astype(jnp.float32)).astype(x.dtype)


def l2_normalize(x):
    return x * lax.rsqrt(jnp.sum(x * x, axis=-1, keepdims=True) + EPS)


def causal_conv(u, buf, w):
    T = u.shape[1]
    xx = jnp.concatenate([buf.astype(u.dtype), u], axis=1)
    out = sum(xx[:, j:j + T] * w[j] for j in range(CONV_K))
    return jax.nn.silu(out), xx[:, -(CONV_K - 1):]


def pool_mix(u, buf, start_pos, w_pool, scale):
    B, T, _ = u.shape
    xx = jnp.concatenate([buf.astype(u.dtype), u], axis=1)
    xf = xx.astype(jnp.float32)
    cs = jnp.concatenate([jnp.zeros((B, 1, D_A), jnp.float32), jnp.cumsum(xf, axis=1)], axis=1)
    pos = start_pos + jnp.arange(T)
    outs = []
    for gi, w in enumerate(POOL_WINDOWS):
        sl = slice(gi * POOL_GC, (gi + 1) * POOL_GC)
        win = cs[:, POOL_BUF + 1:POOL_BUF + 1 + T, sl] - cs[:, POOL_BUF + 1 - w:POOL_BUF + 1 - w + T, sl]
        cnt = jnp.minimum(pos + 1, w).astype(jnp.float32)
        outs.append(win / cnt[None, :, None])
    pooled = jnp.concatenate(outs, axis=-1) - u.astype(jnp.float32)
    pooled = pooled.reshape(B, T, N_POOL, POOL_GC)
    y = jnp.einsum('btgc,gcd->btgd', pooled, w_pool.astype(jnp.float32)).reshape(B, T, D_A)
    y = y * scale.astype(jnp.float32)
    return y.astype(u.dtype), xx[:, -POOL_BUF:]


def gdn_chunked(q, k, v, g, beta, S0):
    B, T, H, DK = q.shape
    DV = v.shape[-1]
    C = GDN_CHUNK
    n = -(-T // C)
    pad = n * C - T

    def to_chunks(a):
        a = jnp.pad(a, [(0, 0), (0, pad)] + [(0, 0)] * (a.ndim - 2))
        a = a.reshape((B, n, C) + a.shape[2:])
        return jnp.moveaxis(a, (1, 3), (0, 2))

    qc, kc, vc, gc, bc = (to_chunks(a) for a in (q, k, v, g, beta))
    gc = jnp.cumsum(gc, axis=-1)
    idx = jnp.arange(C)
    causal = idx[:, None] >= idx[None, :]
    strict = idx[:, None] > idx[None, :]
    decay = jnp.exp(jnp.where(causal, gc[..., :, None] - gc[..., None, :], -jnp.inf))
    kb = kc * bc[..., None]
    vb = vc * bc[..., None]
    Lm = jnp.where(strict, jnp.einsum('nbhid,nbhjd->nbhij', kb, kc) * decay, 0.0)
    A = Lm + jnp.eye(C, dtype=Lm.dtype)
    rhs = jnp.concatenate([vb, kb * jnp.exp(gc)[..., None]], axis=-1)
    sol = lax.linalg.triangular_solve(A, rhs, left_side=True, lower=True, unit_diagonal=True)
    uc, wc = sol[..., :DV], sol[..., DV:]
    qk = jnp.einsum('nbhid,nbhjd->nbhij', qc, kc) * decay
    qg = qc * jnp.exp(gc)[..., None]
    kg = kc * jnp.exp(gc[..., -1:] - gc)[..., None]
    g_last = jnp.exp(gc[..., -1])

    def step(S, xs):
        qg_i, kg_i, u_i, w_i, qk_i, gl_i = xs
        v_new = u_i - jnp.einsum('bhcd,bhde->bhce', w_i, S)
        o = jnp.einsum('bhcd,bhde->bhce', qg_i, S) + jnp.einsum('bhij,bhje->bhie', qk_i, v_new)
        S = S * gl_i[..., None, None] + jnp.einsum('bhcd,bhce->bhde', kg_i, v_new)
        return S, o

    S, o = lax.scan(step, S0, (qg, kg, uc, wc, qk, g_last))
    o = jnp.moveaxis(o, (0, 2), (1, 3)).reshape(B, n * C, H, DV)[:, :T]
    return o, S


def gdn_recurrent(q, k, v, g, beta, S0):
    xs = tuple(jnp.moveaxis(a, 1, 0) for a in (q, k, v, g, beta))

    def step(S, xt):
        q_t, k_t, v_t, g_t, b_t = xt
        S = S * jnp.exp(g_t)[..., None, None]
        kv = jnp.einsum('bhd,bhde->bhe', k_t, S)
        delta = (v_t - kv) * b_t[..., None]
        S = S + jnp.einsum('bhd,bhe->bhde', k_t, delta)
        o = jnp.einsum('bhd,bhde->bhe', q_t, S)
        return S, o

    S, o = lax.scan(step, S0, xs)
    return jnp.moveaxis(o, 0, 1), S


def mixer_layer(x, c, start_pos, pool_buf, conv_buf, S0, chunked,
                w_ada, b_ada, pre_norm_w, post_norm_w, w_in, conv_w, pool_w, pool_scale,
                a_log, dt_bias, o_norm_w, w_out):
    B, T, _ = x.shape
    mod = jax.nn.silu(c) @ w_ada + b_ada
    shift, scale, gate = jnp.split(mod, 3, axis=-1)
    h = rms_norm(x, pre_norm_w) * (1.0 + scale[:, None]) + shift[:, None]
    proj = h @ w_in
    cuts = np.cumsum([D_A, D_A, D_QKV, D_B, N_HEADS_B]).tolist()
    u_a, z_a, qkv, z_b, b_raw, a_raw = jnp.split(proj, cuts, axis=-1)

    y_a, new_pool = pool_mix(u_a, pool_buf, start_pos, pool_w, pool_scale)
    y_a = y_a * jax.nn.silu(z_a)

    qkv_c, new_conv = causal_conv(qkv, conv_buf, conv_w)
    qkv_c = qkv_c.astype(jnp.float32)
    q, k, v = (a.reshape(B, T, N_HEADS_B, HEAD_DIM) for a in jnp.split(qkv_c, 3, axis=-1))
    q = l2_normalize(q) * (HEAD_DIM ** -0.5)
    k = l2_normalize(k)
    beta = jax.nn.sigmoid(b_raw.astype(jnp.float32))
    g = -jnp.exp(a_log.astype(jnp.float32)) * jax.nn.softplus(a_raw.astype(jnp.float32) + dt_bias.astype(jnp.float32))
    S0f = S0.astype(jnp.float32)
    if chunked:
        o, S = gdn_chunked(q, k, v, g, beta, S0f)
    else:
        o, S = gdn_recurrent(q, k, v, g, beta, S0f)
    o = rms_norm(o, o_norm_w) * jax.nn.silu(z_b.astype(jnp.float32)).reshape(B, T, N_HEADS_B, HEAD_DIM)
    y_b = o.reshape(B, T, D_B).astype(x.dtype)

    y = jnp.concatenate([y_a, y_b], axis=-1) @ w_out
    x = x + gate[:, None] * rms_norm(y, post_norm_w)
    return x, new_pool, new_conv, S.astype(S0.dtype)


def setup_inputs(seed: int = 0) -> dict:
    key = jax.random.key(seed)
    ks = jax.random.split(key, 20)

    def nrm(k, shape, s):
        return jax.random.normal(k, shape, jnp.float32) * s

    dt = jnp.exp(jax.random.uniform(ks[16], (DEPTH, N_HEADS_B), jnp.float32, math.log(1e-3), math.log(1e-1)))
    return {
        'x_prompt': nrm(ks[0], (BATCH, SEQ, D_MODEL), 1.0),
        'x_sample': nrm(ks[1], (DEC_BATCH, DEC_SEQ, D_MODEL), 1.0),
        'c_prompt': nrm(ks[2], (BATCH, D_MODEL), 1.0),
        'c_sample': nrm(ks[3], (DEC_BATCH, D_MODEL), 1.0),
        'state_pool': nrm(ks[4], (DEPTH, DEC_BATCH, POOL_BUF, D_A), 1.0),
        'state_conv': nrm(ks[5], (DEPTH, DEC_BATCH, CONV_K - 1, D_QKV), 1.0),
        'state_ssm': nrm(ks[6], (DEPTH, DEC_BATCH, N_HEADS_B, HEAD_DIM, HEAD_DIM), 0.05),
        'w_ada': nrm(ks[7], (DEPTH, D_MODEL, 3 * D_MODEL), 0.1 * D_MODEL ** -0.5),
        'b_ada': nrm(ks[8], (DEPTH, 3 * D_MODEL), 0.01),
        'pre_norm_w': 1.0 + nrm(ks[9], (DEPTH, D_MODEL), 0.05),
        'post_norm_w': 1.0 + nrm(ks[10], (DEPTH, D_MODEL), 0.05),
        'w_in': nrm(ks[11], (DEPTH, D_MODEL, D_IN), D_MODEL ** -0.5),
        'conv_w': nrm(ks[12], (DEPTH, CONV_K, D_QKV), CONV_K ** -0.5),
        'pool_w': nrm(ks[13], (DEPTH, N_POOL, POOL_GC, POOL_GC), POOL_GC ** -0.5),
        'pool_scale': 1.0 + nrm(ks[14], (DEPTH, D_A), 0.05),
        'a_log': jnp.log(jax.random.uniform(ks[15], (DEPTH, N_HEADS_B), jnp.float32, 1.0, 16.0)),
        'dt_bias': dt + jnp.log(-jnp.expm1(-dt)),
        'o_norm_w': 1.0 + nrm(ks[17], (DEPTH, HEAD_DIM), 0.05),
        'w_out': nrm(ks[18], (DEPTH, D_MIX, D_MODEL), D_MIX ** -0.5),
    }


def reference(x_prompt, x_sample, c_prompt, c_sample, state_pool, state_conv, state_ssm,
              w_ada, b_ada, pre_norm_w, post_norm_w, w_in, conv_w, pool_w, pool_scale,
              a_log, dt_bias, o_norm_w, w_out):
    bp = x_prompt.shape[0]
    yp, ys = x_prompt, x_sample
    pp, cp, sp, ps, cs, ss = [], [], [], [], [], []
    for l in range(DEPTH):
        params = (w_ada[l], b_ada[l], pre_norm_w[l], post_norm_w[l], w_in[l], conv_w[l],
                  pool_w[l], pool_scale[l], a_log[l], dt_bias[l], o_norm_w[l], w_out[l])
        zero_pool = jnp.zeros((bp, POOL_BUF, D_A), x_prompt.dtype)
        zero_conv = jnp.zeros((bp, CONV_K - 1, D_QKV), x_prompt.dtype)
        zero_ssm = jnp.zeros((bp, N_HEADS_B, HEAD_DIM, HEAD_DIM), state_ssm.dtype)
        yp, a1, a2, a3 = mixer_layer(yp, c_prompt, 0, zero_pool, zero_conv, zero_ssm, True, *params)
        ys, b1, b2, b3 = mixer_layer(ys, c_sample, PAST_LEN, state_pool[l], state_conv[l], state_ssm[l], False, *params)
        pp.append(a1); cp.append(a2); sp.append(a3)
        ps.append(b1); cs.append(b2); ss.append(b3)
    return (yp, ys, jnp.stack(pp), jnp.stack(cp), jnp.stack(sp), jnp.stack(ps), jnp.stack(cs), jnp.stack(ss))
```

```python
import functools

import jax
import jax.numpy as jnp
import numpy as np
from jax import lax
from jax.experimental import pallas as pl
from jax.experimental.pallas import tpu as pltpu

F32 = jnp.float32
BF16 = jnp.bfloat16

D_MODEL = 1024
D_A = 512
D_B = 512
N_POOL = 4
POOL_WINDOWS = (2, 4, 8, 16)
POOL_GC = 128
POOL_BUF = 15
HEAD_DIM = 128
N_HEADS = 4
D_QKV = 3 * D_B
CONV_K = 4
D_MAIN = 2 * D_A + D_QKV + D_B
EPS = 1e-6
CHUNK = 64
LANES = 128
VMEM_LIMIT = 56 * 1024 * 1024


def _dot(a, b):
    return jnp.dot(a.astype(BF16), b.astype(BF16), preferred_element_type=F32)


def _dot_nt(a, b):
    return lax.dot_general(a.astype(BF16), b.astype(BF16), (((1,), (1,)), ((), ())), preferred_element_type=F32)


def _dot_tn(a, b):
    return lax.dot_general(a, b, (((0,), (0,)), ((), ())), preferred_element_type=F32)


def _silu(x):
    return x * jax.nn.sigmoid(x)


def _softplus(x):
    return jnp.maximum(x, 0.0) + jnp.log1p(jnp.exp(-jnp.abs(x)))


def _block_diag(a, b):
    top = jnp.concatenate([a, jnp.zeros((a.shape[0], b.shape[1]), a.dtype)], axis=1)
    bot = jnp.concatenate([jnp.zeros((b.shape[0], a.shape[1]), b.dtype), b], axis=1)
    return jnp.concatenate([top, bot], axis=0)


def _mod_kernel(c_ref, w_ref, b_ref, o_ref):
    o_ref[0] = jnp.dot(_silu(c_ref[...]), w_ref[0], preferred_element_type=F32) + b_ref[0]


def _mod_call(c_all, w_ada, b_ada):
    depth, _, n3 = w_ada.shape
    rows = c_all.shape[0]
    nblk = n3 // D_MODEL
    return pl.pallas_call(
        _mod_kernel,
        grid=(depth, nblk),
        in_specs=[
            pl.BlockSpec((rows, D_MODEL), lambda l, j: (0, 0)),
            pl.BlockSpec((1, D_MODEL, D_MODEL), lambda l, j: (l, 0, j)),
            pl.BlockSpec((1, 1, D_MODEL), lambda l, j: (l, 0, j)),
        ],
        out_specs=pl.BlockSpec((1, rows, D_MODEL), lambda l, j: (l, 0, j)),
        out_shape=jax.ShapeDtypeStruct((depth, rows, n3), F32),
        compiler_params=pltpu.CompilerParams(dimension_semantics=("arbitrary", "arbitrary")),
        name="adaln_mod",
    )(c_all, w_ada, b_ada.reshape(depth, 1, n3))


def _prompt_kernel(x_ref, mod_ref, prew_ref, postw_ref, win_ref, wba_ref, wbat_ref, convw_ref, poolw_ref,
                   pscale_ref, alogc_ref, dtbc_ref, alogr_ref, dtbr_ref, onw_ref, wout_ref, band_ref, ltri_ref,
                   y_ref, npool_ref, nconv_ref, nssm_ref,
                   qkv_ext, ua_ext, s_ref, q_s, k_s, v_s, o_s, beta_s, gc_s, eg_s, gcrow_s):
    step = pl.program_id(0)
    nb = x_ref.shape[0]
    rows = nb * CHUNK

    @pl.when(step == 0)
    def _():
        qkv_ext[:, 0:8, :] = jnp.zeros((nb, 8, D_QKV), F32)
        ua_ext[:, 0:CHUNK, :] = jnp.zeros((nb, CHUNK, D_A), F32)
        s_ref[...] = jnp.zeros(s_ref.shape, F32)

    x = x_ref[...]
    mod = mod_ref[...]
    shift = mod[:, :, 0:D_MODEL]
    scale = mod[:, :, D_MODEL:2 * D_MODEL]
    gate = mod[:, :, 2 * D_MODEL:3 * D_MODEL]
    a_mul = prew_ref[...][None] * (1.0 + scale)
    ms = jnp.mean(x * x, axis=-1, keepdims=True)
    h = x * lax.rsqrt(ms + EPS) * a_mul + shift
    hb = h.reshape(rows, D_MODEL).astype(BF16)

    ua = jnp.dot(hb, win_ref[:, 0:D_A], preferred_element_type=F32)
    ua3 = ua.reshape(nb, CHUNK, D_A)
    ua_ext[:, CHUNK:2 * CHUNK, :] = ua3
    pos = lax.broadcasted_iota(jnp.int32, (CHUNK, POOL_GC), 0) + step * CHUNK
    pooled_groups = []
    for gi, w in enumerate(POOL_WINDOWS):
        cnt = jnp.minimum(pos + 1, w).astype(F32)
        per_b = []
        for b in range(nb):
            ext = ua_ext[b, :, gi * POOL_GC:(gi + 1) * POOL_GC]
            win = jnp.dot(band_ref[gi], ext.astype(BF16), preferred_element_type=F32)
            per_b.append(win / cnt - ua3[b, :, gi * POOL_GC:(gi + 1) * POOL_GC])
        pooled_groups.append(jnp.concatenate(per_b, axis=0))
    ya = jnp.concatenate([_dot(pooled_groups[gi], poolw_ref[gi]) for gi in range(N_POOL)], axis=1)
    za = jnp.dot(hb, win_ref[:, D_A:2 * D_A], preferred_element_type=F32)
    ya = ya * pscale_ref[...] * _silu(za)

    qkv3 = jnp.dot(hb, win_ref[:, 2 * D_A:2 * D_A + D_QKV], preferred_element_type=F32).reshape(nb, CHUNK, D_QKV)
    qkv_ext[:, 8:8 + CHUNK, :] = qkv3
    cw = convw_ref[...]
    acc = qkv3 * cw[CONV_K - 1:CONV_K][None]
    for j in range(CONV_K - 1):
        acc = acc + qkv_ext[:, pl.ds(8 - (CONV_K - 1) + j, CHUNK), :] * cw[j:j + 1][None]
    qkvc = _silu(acc).reshape(rows, D_QKV)
    for hh in range(N_HEADS):
        sl = slice(hh * HEAD_DIM, (hh + 1) * HEAD_DIM)
        qh = qkvc[:, hh * HEAD_DIM:(hh + 1) * HEAD_DIM]
        kh = qkvc[:, D_B + hh * HEAD_DIM:D_B + (hh + 1) * HEAD_DIM]
        q_s[:, sl] = qh * lax.rsqrt(jnp.sum(qh * qh, axis=-1, keepdims=True) + EPS) * (HEAD_DIM ** -0.5)
        k_s[:, sl] = kh * lax.rsqrt(jnp.sum(kh * kh, axis=-1, keepdims=True) + EPS)
    v_s[...] = qkvc[:, 2 * D_B:3 * D_B]

    bac = jnp.dot(hb, wba_ref[...], preferred_element_type=F32)
    bar = lax.dot_general(wbat_ref[...], hb, (((1,), (1,)), ((), ())), preferred_element_type=F32)
    sig = jax.nn.sigmoid(bac)
    gcol = -jnp.exp(alogc_ref[...]) * _softplus(bac + dtbc_ref[...])
    g_hi = gcol.astype(BF16)
    g_lo = (gcol - g_hi.astype(F32)).astype(BF16)
    gc_parts = []
    for b in range(nb):
        rs = slice(b * CHUNK, (b + 1) * CHUNK)
        gc_parts.append(jnp.dot(ltri_ref[...], jnp.concatenate([g_hi[rs], g_lo[rs]], axis=0),
                                preferred_element_type=F32))
    gc = jnp.concatenate(gc_parts, axis=0)
    for hh in range(N_HEADS):
        beta_s[hh] = jnp.broadcast_to(sig[:, hh:hh + 1], (rows, LANES))
        gcb = jnp.broadcast_to(gc[:, N_HEADS + hh:N_HEADS + hh + 1], (rows, LANES))
        gc_s[hh] = gcb
        eg_s[hh] = jnp.exp(gcb)
    grow = -jnp.exp(alogr_ref[...]) * _softplus(bar + dtbr_ref[...])
    lane_in_chunk = lax.broadcasted_iota(jnp.int32, grow.shape, 1) % CHUNK
    sh = 1
    while sh < CHUNK:
        grow = grow + jnp.where(lane_in_chunk >= sh, pltpu.roll(grow, sh, 1), 0.0)
        sh *= 2
    gcrow_s[...] = grow

    ri = lax.broadcasted_iota(jnp.int32, (CHUNK, LANES), 0)
    li = lax.broadcasted_iota(jnp.int32, (CHUNK, LANES), 1)
    lj = li % CHUNK
    lo_half = li < CHUNK
    strict = ri > lj
    causal = ri >= lj
    eye2 = (ri == lj).astype(F32)

    def pair_mul(xp, yp):
        ybd = jnp.concatenate([jnp.where(lo_half, yp, 0.0), jnp.where(lo_half, 0.0, yp)], axis=0)
        return _dot(xp, ybd)

    for p in range(nb // 2):
        ra = slice(2 * p * CHUNK, (2 * p + 1) * CHUNK)
        rb = slice((2 * p + 1) * CHUNK, (2 * p + 2) * CHUNK)
        for hh in range(N_HEADS):
            sl = slice(hh * HEAD_DIM, (hh + 1) * HEAD_DIM)
            ka, kb_ = k_s[ra, sl], k_s[rb, sl]
            qa, qb = q_s[ra, sl], q_s[rb, sl]
            va, vb_ = v_s[ra, sl], v_s[rb, sl]
            ba, bb = beta_s[hh, ra, :], beta_s[hh, rb, :]
            gca, gcb = gc_s[hh, ra, :], gc_s[hh, rb, :]
            ega, egb = eg_s[hh, ra, :], eg_s[hh, rb, :]
            kba, kbb = ka * ba, kb_ * bb
            kbd = _block_diag(ka, kb_)
            lhs1 = jnp.concatenate([jnp.concatenate([kba, kbb], axis=1), jnp.concatenate([qa, qb], axis=1)], axis=0)
            kkqk = _dot_nt(lhs1, kbd)
            gcol_p = jnp.where(lo_half, gca, gcb)
            grow_p = jnp.broadcast_to(gcrow_s[N_HEADS + hh:N_HEADS + hh + 1, 2 * p * CHUNK:(2 * p + 2) * CHUNK],
                                      (CHUNK, LANES))
            dec = jnp.exp(jnp.minimum(gcol_p - grow_p, 0.0))
            n1 = jnp.where(strict, kkqk[0:CHUNK] * dec, 0.0)
            qkm = jnp.where(causal, kkqk[CHUNK:2 * CHUNK] * dec, 0.0)
            t = eye2 - n1
            npow = n1
            for _ in range(5):
                npow = pair_mul(npow, npow)
                t = t + pair_mul(t, npow)
            rhs = jnp.concatenate([_block_diag(va * ba, vb_ * bb), _block_diag(kba * ega, kbb * egb)], axis=1)
            uw = _dot(t, rhs)
            ia, ib = 2 * p * N_HEADS + hh, (2 * p + 1) * N_HEADS + hh
            s_a, s_b = s_ref[ia], s_ref[ib]
            lhs3 = jnp.concatenate([uw[:, 2 * HEAD_DIM:4 * HEAD_DIM],
                                    jnp.concatenate([qa * ega, qb * egb], axis=1)], axis=0)
            r3 = _dot(lhs3, _block_diag(s_a, s_b))
            vn = uw[:, 0:2 * HEAD_DIM] - r3[0:CHUNK]
            vna, vnb = vn[:, 0:HEAD_DIM], vn[:, HEAD_DIM:2 * HEAD_DIM]
            o = r3[CHUNK:2 * CHUNK] + _dot(qkm, _block_diag(vna, vnb))
            o_s[ra, sl] = o[:, 0:HEAD_DIM]
            o_s[rb, sl] = o[:, HEAD_DIM:2 * HEAD_DIM]
            gla = gc_s[hh, (2 * p + 1) * CHUNK - 1:(2 * p + 1) * CHUNK, :]
            glb = gc_s[hh, (2 * p + 2) * CHUNK - 1:(2 * p + 2) * CHUNK, :]
            kg = jnp.concatenate([ka * jnp.exp(gla - gca), kb_ * jnp.exp(glb - gcb)], axis=1)
            upd = _dot_tn(kg, vn)
            s_ref[ia] = s_a * jnp.exp(gla) + upd[0:HEAD_DIM, 0:HEAD_DIM]
            s_ref[ib] = s_b * jnp.exp(glb) + upd[HEAD_DIM:2 * HEAD_DIM, HEAD_DIM:2 * HEAD_DIM]

    zb = jnp.dot(hb, win_ref[:, 2 * D_A + D_QKV:D_MAIN], preferred_element_type=F32)
    o_all = o_s[...]
    yb_parts = []
    for hh in range(N_HEADS):
        oh = o_all[:, hh * HEAD_DIM:(hh + 1) * HEAD_DIM]
        yb_parts.append(oh * lax.rsqrt(jnp.mean(oh * oh, axis=-1, keepdims=True) + EPS) * onw_ref[...])
    yb = jnp.concatenate(yb_parts, axis=1) * _silu(zb)
    ymix = jnp.concatenate([ya, yb], axis=1)
    yo = jnp.dot(ymix.astype(BF16), wout_ref[...], preferred_element_type=F32)
    yn = yo * lax.rsqrt(jnp.mean(yo * yo, axis=-1, keepdims=True) + EPS) * postw_ref[...]
    y_ref[...] = x + gate * yn.reshape(nb, CHUNK, D_MODEL)

    qkv_ext[:, 0:8, :] = qkv_ext[:, CHUNK:CHUNK + 8, :]
    ua_ext[:, 0:CHUNK, :] = ua3

    @pl.when(step == pl.num_programs(0) - 1)
    def _():
        npool_ref[...] = ua3[:, CHUNK - POOL_BUF:CHUNK, :]
        nconv_ref[...] = qkv3[:, CHUNK - (CONV_K - 1):CHUNK, :]
        nssm_ref[...] = s_ref[...]


def _prompt_layer(x, mod, consts, lw):
    nb, seq, _ = x.shape
    rows = nb * CHUNK
    nstep = seq // CHUNK
    full = lambda shape: pl.BlockSpec(shape, lambda s, _n=len(shape): (0,) * _n)
    in_specs = [
        pl.BlockSpec((nb, CHUNK, D_MODEL), lambda s: (0, s, 0)),
        full((nb, 1, 3 * D_MODEL)),
        full((1, D_MODEL)), full((1, D_MODEL)),
        full((D_MODEL, D_MAIN)), full((D_MODEL, LANES)), full((8, D_MODEL)),
        full((CONV_K, D_QKV)), full((N_POOL, POOL_GC, POOL_GC)), full((1, D_A)),
        full((1, LANES)), full((1, LANES)), full((8, rows)), full((8, rows)),
        full((1, HEAD_DIM)), full((D_MODEL, D_MODEL)),
        full((N_POOL, CHUNK, 2 * CHUNK)), full((CHUNK, 2 * CHUNK)),
    ]
    out_specs = [
        pl.BlockSpec((nb, CHUNK, D_MODEL), lambda s: (0, s, 0)),
        full((nb, POOL_BUF, D_A)), full((nb, CONV_K - 1, D_QKV)), full((nb * N_HEADS, HEAD_DIM, HEAD_DIM)),
    ]
    out_shape = [
        jax.ShapeDtypeStruct((nb, seq, D_MODEL), F32),
        jax.ShapeDtypeStruct((nb, POOL_BUF, D_A), F32),
        jax.ShapeDtypeStruct((nb, CONV_K - 1, D_QKV), F32),
        jax.ShapeDtypeStruct((nb * N_HEADS, HEAD_DIM, HEAD_DIM), F32),
    ]
    scratch = [
        pltpu.VMEM((nb, CHUNK + 8, D_QKV), F32),
        pltpu.VMEM((nb, 2 * CHUNK, D_A), F32),
        pltpu.VMEM((nb * N_HEADS, HEAD_DIM, HEAD_DIM), F32),
        pltpu.VMEM((rows, D_B), F32), pltpu.VMEM((rows, D_B), F32), pltpu.VMEM((rows, D_B), F32),
        pltpu.VMEM((rows, D_B), F32),
        pltpu.VMEM((N_HEADS, rows, LANES), F32), pltpu.VMEM((N_HEADS, rows, LANES), F32),
        pltpu.VMEM((N_HEADS, rows, LANES), F32),
        pltpu.VMEM((8, rows), F32),
    ]
    y, npool, nconv, nssm = pl.pallas_call(
        _prompt_kernel,
        grid=(nstep,),
        in_specs=in_specs,
        out_specs=out_specs,
        out_shape=out_shape,
        scratch_shapes=scratch,
        compiler_params=pltpu.CompilerParams(dimension_semantics=("arbitrary",), vmem_limit_bytes=VMEM_LIMIT),
        name="prompt_layer",
    )(x, mod, lw["prew"], lw["postw"], lw["win"], lw["wba"], lw["wbat"], lw["convw"], lw["poolw"], lw["pscale"],
      lw["alogc"], lw["dtbc"], lw["alogr_p"], lw["dtbr_p"], lw["onw"], lw["wout"], consts["band"], consts["ltri"])
    return y, npool, nconv, nssm.reshape(nb, N_HEADS, HEAD_DIM, HEAD_DIM)


SAMPLE_BLOCK = 8


def _sample_kernel(x_ref, mod_ref, spool_ref, sconv_ref, ssm_ref, prew_ref, postw_ref, win_ref, wba_ref, convw_ref,
                   poolw_ref, pscale_ref, alogc_ref, dtbc_ref, onw_ref, wout_ref,
                   y_ref, npool_ref, nconv_ref, nssm_ref,
                   q_s, k_s, v_s, o_s, beta_s, eg_s, qk_s, ya_s, zb_s, *, past_len):
    step = pl.program_id(0)
    nseq = x_ref.shape[0]

    @pl.when(step == 0)
    def _():
        x = x_ref[...]
        mod = mod_ref[...]
        shift = mod[:, 0:D_MODEL]
        scale = mod[:, D_MODEL:2 * D_MODEL]
        a_mul = prew_ref[...] * (1.0 + scale)
        ms = jnp.mean(x * x, axis=-1, keepdims=True)
        hb = (x * lax.rsqrt(ms + EPS) * a_mul + shift).astype(BF16)

        ua = jnp.dot(hb, win_ref[:, 0:D_A], preferred_element_type=F32)
        ya_parts = []
        for gi, w in enumerate(POOL_WINDOWS):
            gs = slice(gi * POOL_GC, (gi + 1) * POOL_GC)
            win = ua[:, gs]
            for d in range(1, w):
                win = win + spool_ref[:, POOL_BUF - d, gs]
            cnt = float(min(past_len + 1, w))
            pooled = win / cnt - ua[:, gs]
            ya_parts.append(_dot(pooled, poolw_ref[gi]))
        za = jnp.dot(hb, win_ref[:, D_A:2 * D_A], preferred_element_type=F32)
        ya_s[...] = jnp.concatenate(ya_parts, axis=1) * pscale_ref[...] * _silu(za)
        npool_ref[:, 0:POOL_BUF - 1, :] = spool_ref[:, 1:POOL_BUF, :]
        npool_ref[:, POOL_BUF - 1, :] = ua

        qkv = jnp.dot(hb, win_ref[:, 2 * D_A:2 * D_A + D_QKV], preferred_element_type=F32)
        cw = convw_ref[...]
        acc = qkv * cw[CONV_K - 1:CONV_K]
        for j in range(CONV_K - 1):
            acc = acc + sconv_ref[:, j, :] * cw[j:j + 1]
        nconv_ref[:, 0:CONV_K - 2, :] = sconv_ref[:, 1:CONV_K - 1, :]
        nconv_ref[:, CONV_K - 2, :] = qkv
        qkvc = _silu(acc)
        bac = jnp.dot(hb, wba_ref[...], preferred_element_type=F32)
        sig = jax.nn.sigmoid(bac)
        eg = jnp.exp(-jnp.exp(alogc_ref[...]) * _softplus(bac + dtbc_ref[...]))
        for hh in range(N_HEADS):
            sl = slice(hh * HEAD_DIM, (hh + 1) * HEAD_DIM)
            qh = qkvc[:, hh * HEAD_DIM:(hh + 1) * HEAD_DIM]
            kh = qkvc[:, D_B + hh * HEAD_DIM:D_B + (hh + 1) * HEAD_DIM]
            qn = qh * lax.rsqrt(jnp.sum(qh * qh, axis=-1, keepdims=True) + EPS) * (HEAD_DIM ** -0.5)
            kn = kh * lax.rsqrt(jnp.sum(kh * kh, axis=-1, keepdims=True) + EPS)
            q_s[:, sl] = qn
            k_s[:, sl] = kn
            qk_s[hh] = jnp.broadcast_to(jnp.sum(qn * kn, axis=-1, keepdims=True), (nseq, LANES))
            beta_s[hh] = jnp.broadcast_to(sig[:, hh:hh + 1], (nseq, LANES))
            eg_s[hh] = jnp.broadcast_to(eg[:, N_HEADS + hh:N_HEADS + hh + 1], (nseq, LANES))
        v_s[...] = qkvc[:, 2 * D_B:3 * D_B]
        zb_s[...] = jnp.dot(hb, win_ref[:, 2 * D_A + D_QKV:D_MAIN], preferred_element_type=F32)

    r0 = pl.multiple_of(step * SAMPLE_BLOCK, SAMPLE_BLOCK)
    kblk = k_s[pl.ds(r0, SAMPLE_BLOCK), :]
    qblk = q_s[pl.ds(r0, SAMPLE_BLOCK), :]
    vblk = v_s[pl.ds(r0, SAMPLE_BLOCK), :]
    row8 = lax.broadcasted_iota(jnp.int32, (8, HEAD_DIM), 0)
    o_rows = []
    for i in range(SAMPLE_BLOCK):
        o_heads = []
        for hh in range(N_HEADS):
            sl = slice(hh * HEAD_DIM, (hh + 1) * HEAD_DIM)
            krow, qrow, vrow = kblk[i:i + 1, sl], qblk[i:i + 1, sl], vblk[i:i + 1, sl]
            beta = beta_s[hh, pl.ds(r0 + i, 1), :]
            eg = eg_s[hh, pl.ds(r0 + i, 1), :]
            qk = qk_s[hh, pl.ds(r0 + i, 1), :]
            s_old = ssm_ref[i * N_HEADS + hh]
            lhs = jnp.where(row8 == 0, krow, jnp.where(row8 == 1, qrow, 0.0))
            r = _dot(lhs, s_old)
            delta = (vrow - eg * r[0:1]) * beta
            o_heads.append(eg * r[1:2] + qk * delta)
            k_hi = krow.astype(BF16).astype(F32)
            d_hi = delta.astype(BF16).astype(F32)
            kp = jnp.where(row8 == 0, k_hi, jnp.where(row8 == 1, krow - k_hi, jnp.where(row8 == 2, k_hi, 0.0)))
            dp = jnp.where(row8 == 0, d_hi, jnp.where(row8 == 1, d_hi, jnp.where(row8 == 2, delta - d_hi, 0.0)))
            nssm_ref[i * N_HEADS + hh] = s_old * eg + _dot_tn(kp, dp)
        o_rows.append(jnp.concatenate(o_heads, axis=1))
    o_s[pl.ds(r0, SAMPLE_BLOCK), :] = jnp.concatenate(o_rows, axis=0)

    @pl.when(step == pl.num_programs(0) - 1)
    def _():
        x = x_ref[...]
        gate = mod_ref[:, 2 * D_MODEL:3 * D_MODEL]
        o_all = o_s[...]
        yb_parts = []
        for hh in range(N_HEADS):
            oh = o_all[:, hh * HEAD_DIM:(hh + 1) * HEAD_DIM]
            yb_parts.append(oh * lax.rsqrt(jnp.mean(oh * oh, axis=-1, keepdims=True) + EPS) * onw_ref[...])
        yb = jnp.concatenate(yb_parts, axis=1) * _silu(zb_s[...])
        ymix = jnp.concatenate([ya_s[...], yb], axis=1)
        yo = jnp.dot(ymix.astype(BF16), wout_ref[...], preferred_element_type=F32)
        yn = yo * lax.rsqrt(jnp.mean(yo * yo, axis=-1, keepdims=True) + EPS) * postw_ref[...]
        y_ref[...] = x + gate * yn


def _sample_layer(x, mod, spool, sconv, ssm, lw, past_len):
    nseq = x.shape[0]
    nstep = nseq // SAMPLE_BLOCK
    blk_states = SAMPLE_BLOCK * N_HEADS
    full = lambda shape: pl.BlockSpec(shape, lambda s, _n=len(shape): (0,) * _n)
    in_specs = [
        full((nseq, D_MODEL)), full((nseq, 3 * D_MODEL)),
        full((nseq, POOL_BUF, D_A)), full((nseq, CONV_K - 1, D_QKV)),
        pl.BlockSpec((blk_states, HEAD_DIM, HEAD_DIM), lambda s: (s, 0, 0)),
        full((1, D_MODEL)), full((1, D_MODEL)),
        full((D_MODEL, D_MAIN)), full((D_MODEL, LANES)),
        full((CONV_K, D_QKV)), full((N_POOL, POOL_GC, POOL_GC)), full((1, D_A)),
        full((1, LANES)), full((1, LANES)), full((1, HEAD_DIM)), full((D_MODEL, D_MODEL)),
    ]
    out_specs = [
        full((nseq, D_MODEL)), full((nseq, POOL_BUF, D_A)), full((nseq, CONV_K - 1, D_QKV)),
        pl.BlockSpec((blk_states, HEAD_DIM, HEAD_DIM), lambda s: (s, 0, 0)),
    ]
    out_shape = [
        jax.ShapeDtypeStruct((nseq, D_MODEL), F32),
        jax.ShapeDtypeStruct((nseq, POOL_BUF, D_A), F32),
        jax.ShapeDtypeStruct((nseq, CONV_K - 1, D_QKV), F32),
        jax.ShapeDtypeStruct((nseq * N_HEADS, HEAD_DIM, HEAD_DIM), F32),
    ]
    scratch = [
        pltpu.VMEM((nseq, D_B), F32), pltpu.VMEM((nseq, D_B), F32), pltpu.VMEM((nseq, D_B), F32),
        pltpu.VMEM((nseq, D_B), F32),
        pltpu.VMEM((N_HEADS, nseq, LANES), F32), pltpu.VMEM((N_HEADS, nseq, LANES), F32),
        pltpu.VMEM((N_HEADS, nseq, LANES), F32),
        pltpu.VMEM((nseq, D_A), F32), pltpu.VMEM((nseq, D_B), F32),
    ]
    y, npool, nconv, nssm = pl.pallas_call(
        functools.partial(_sample_kernel, past_len=past_len),
        grid=(nstep,),
        in_specs=in_specs,
        out_specs=out_specs,
        out_shape=out_shape,
        scratch_shapes=scratch,
        compiler_params=pltpu.CompilerParams(dimension_semantics=("arbitrary",), vmem_limit_bytes=VMEM_LIMIT),
        name="sample_layer",
    )(x, mod, spool, sconv, ssm.reshape(nseq * N_HEADS, HEAD_DIM, HEAD_DIM), lw["prew"], lw["postw"], lw["win"],
      lw["wba"], lw["convw"], lw["poolw"], lw["pscale"], lw["alogc"], lw["dtbc"], lw["onw"], lw["wout"])
    return y, npool, nconv, nssm.reshape(nseq, N_HEADS, HEAD_DIM, HEAD_DIM)


def _constants():
    t = np.arange(CHUNK)[:, None]
    j = np.arange(2 * CHUNK)[None, :]
    band = np.stack([((j <= CHUNK + t) & (j > CHUNK + t - w)) for w in POOL_WINDOWS]).astype(np.float32)
    tri = (np.arange(CHUNK)[None, :] <= t).astype(np.float32)
    ltri = np.concatenate([tri, tri], axis=1)
    return {"band": jnp.asarray(band, BF16), "ltri": jnp.asarray(ltri, BF16)}


def _layer_weights(l, rows_p, pre_norm_w, post_norm_w, w_in, conv_w, pool_w, pool_scale, a_log, dt_bias, o_norm_w,
                   w_out):
    wba = w_in[l][:, D_MAIN:D_MAIN + 2 * N_HEADS]
    pad_c = lambda v: jnp.zeros((1, LANES), F32).at[0, N_HEADS:2 * N_HEADS].set(v)
    pad_r = lambda v: jnp.broadcast_to(jnp.zeros((8,), F32).at[N_HEADS:2 * N_HEADS].set(v)[:, None], (8, rows_p))
    return {
        "prew": pre_norm_w[l].reshape(1, D_MODEL),
        "postw": post_norm_w[l].reshape(1, D_MODEL),
        "win": w_in[l][:, 0:D_MAIN].astype(BF16),
        "wba": jnp.zeros((D_MODEL, LANES), F32).at[:, 0:2 * N_HEADS].set(wba).astype(BF16),
        "wbat": wba.T.astype(BF16),
        "convw": conv_w[l],
        "poolw": pool_w[l].astype(BF16),
        "pscale": pool_scale[l].reshape(1, D_A),
        "alogc": pad_c(a_log[l]), "dtbc": pad_c(dt_bias[l]),
        "alogr_p": pad_r(a_log[l]), "dtbr_p": pad_r(dt_bias[l]),
        "onw": o_norm_w[l].reshape(1, HEAD_DIM),
        "wout": w_out[l].astype(BF16),
    }


def kernel(x_prompt, x_sample, c_prompt, c_sample, state_pool, state_conv, state_ssm, w_ada, b_ada, pre_norm_w,
           post_norm_w, w_in, conv_w, pool_w, pool_scale, a_log, dt_bias, o_norm_w, w_out):
    depth = w_in.shape[0]
    nb, seq, _ = x_prompt.shape
    nseq, dec_seq, _ = x_sample.shape
    assert dec_seq == 1 and seq % CHUNK == 0 and nb % 2 == 0 and nseq % SAMPLE_BLOCK == 0
    past_len = 16384
    consts = _constants()
    mod = _mod_call(jnp.concatenate([c_prompt, c_sample], axis=0), w_ada, b_ada)
    yp, ys = x_prompt, x_sample.reshape(nseq, D_MODEL)
    outs = [[] for _ in range(6)]
    for l in range(depth):
        lw = _layer_weights(l, nb * CHUNK, pre_norm_w, post_norm_w, w_in, conv_w, pool_w, pool_scale, a_log, dt_bias,
                            o_norm_w, w_out)
        yp, a1, a2, a3 = _prompt_layer(yp, mod[l, 0:nb].reshape(nb, 1, 3 * D_MODEL), consts, lw)
        ys, b1, b2, b3 = _sample_layer(ys, mod[l, nb:nb + nseq], state_pool[l], state_conv[l], state_ssm[l], lw,
                                       past_len)
        for lst, v in zip(outs, (a1, a2, a3, b1, b2, b3)):
            lst.append(v)
    return (yp, ys.reshape(nseq, dec_seq, D_MODEL)) + tuple(jnp.stack(o) for o in outs)
```

```python
import functools

import jax
import jax.numpy as jnp
import numpy as np
from jax import lax
from jax.experimental import pallas as pl
from jax.experimental.pallas import tpu as pltpu

F32 = jnp.float32
BF16 = jnp.bfloat16

D_MODEL = 1024
D_A = 512
D_B = 512
N_POOL = 4
POOL_WINDOWS = (2, 4, 8, 16)
POOL_GC = 128
POOL_BUF = 15
HEAD_DIM = 128
N_HEADS = 4
D_QKV = 3 * D_B
CONV_K = 4
D_MAIN = 2 * D_A + D_QKV + D_B
EPS = 1e-6
CHUNK = 64
GDN_GROUP = 8
LANES = 128
VMEM_LIMIT = 56 * 1024 * 1024


def _dot(a, b):
    return jnp.dot(a.astype(BF16), b.astype(BF16), preferred_element_type=F32)


def _dot_nt(a, b):
    return lax.dot_general(a.astype(BF16), b.astype(BF16), (((1,), (1,)), ((), ())), preferred_element_type=F32)


def _dot_tn(a, b):
    return lax.dot_general(a, b, (((0,), (0,)), ((), ())), preferred_element_type=F32)


def _silu(x):
    return x * jax.nn.sigmoid(x)


def _softplus(x):
    return jnp.maximum(x, 0.0) + jnp.log1p(jnp.exp(-jnp.abs(x)))


def _block_diag(a, b):
    top = jnp.concatenate([a, jnp.zeros((a.shape[0], b.shape[1]), a.dtype)], axis=1)
    bot = jnp.concatenate([jnp.zeros((b.shape[0], a.shape[1]), b.dtype), b], axis=1)
    return jnp.concatenate([top, bot], axis=0)


def _mod_kernel(c_ref, w_ref, b_ref, o_ref):
    o_ref[0] = jnp.dot(_silu(c_ref[...]), w_ref[0], preferred_element_type=F32) + b_ref[0]


def _mod_call(c_all, w_ada, b_ada):
    depth, _, n3 = w_ada.shape
    rows = c_all.shape[0]
    nblk = n3 // D_MODEL
    return pl.pallas_call(
        _mod_kernel,
        grid=(depth, nblk),
        in_specs=[
            pl.BlockSpec((rows, D_MODEL), lambda l, j: (0, 0)),
            pl.BlockSpec((1, D_MODEL, D_MODEL), lambda l, j: (l, 0, j)),
            pl.BlockSpec((1, 1, D_MODEL), lambda l, j: (l, 0, j)),
        ],
        out_specs=pl.BlockSpec((1, rows, D_MODEL), lambda l, j: (l, 0, j)),
        out_shape=jax.ShapeDtypeStruct((depth, rows, n3), F32),
        compiler_params=pltpu.CompilerParams(dimension_semantics=("arbitrary", "arbitrary")),
        name="adaln_mod",
    )(c_all, w_ada, b_ada.reshape(depth, 1, n3))


def _prompt_kernel(x_ref, mod_ref, prew_ref, postw_ref, win_ref, wba_ref, wbat_ref, convw_ref, poolw_ref,
                   pscale_ref, alogc_ref, dtbc_ref, alogr_ref, dtbr_ref, onw_ref, wout_ref, band_ref, ltri_ref,
                   y_ref, npool_ref, nconv_ref, nssm_ref,
                   qkv_ext, ua_ext, s_ref, q_s, k_s, v_s, o_s, beta_s, gc_s, eg_s, gcrow_s):
    step = pl.program_id(0)
    nb = x_ref.shape[0]
    rows = nb * CHUNK

    @pl.when(step == 0)
    def _():
        qkv_ext[:, 0:8, :] = jnp.zeros((nb, 8, D_QKV), F32)
        ua_ext[:, 0:CHUNK, :] = jnp.zeros((nb, CHUNK, D_A), F32)
        s_ref[...] = jnp.zeros(s_ref.shape, F32)

    x = x_ref[...]
    mod = mod_ref[...]
    shift = mod[:, :, 0:D_MODEL]
    scale = mod[:, :, D_MODEL:2 * D_MODEL]
    gate = mod[:, :, 2 * D_MODEL:3 * D_MODEL]
    a_mul = prew_ref[...][None] * (1.0 + scale)
    ms = jnp.mean(x * x, axis=-1, keepdims=True)
    h = x * lax.rsqrt(ms + EPS) * a_mul + shift
    hb = h.reshape(rows, D_MODEL).astype(BF16)

    ua = jnp.dot(hb, win_ref[:, 0:D_A], preferred_element_type=F32)
    ua3 = ua.reshape(nb, CHUNK, D_A)
    ua_ext[:, CHUNK:2 * CHUNK, :] = ua3
    pos = lax.broadcasted_iota(jnp.int32, (CHUNK, POOL_GC), 0) + step * CHUNK
    pooled_groups = []
    for gi, w in enumerate(POOL_WINDOWS):
        cnt = jnp.minimum(pos + 1, w).astype(F32)
        per_b = []
        for b in range(nb):
            ext = ua_ext[b, :, gi * POOL_GC:(gi + 1) * POOL_GC]
            win = jnp.dot(band_ref[gi], ext.astype(BF16), preferred_element_type=F32)
            per_b.append(win / cnt - ua3[b, :, gi * POOL_GC:(gi + 1) * POOL_GC])
        pooled_groups.append(jnp.concatenate(per_b, axis=0))
    ya = jnp.concatenate([_dot(pooled_groups[gi], poolw_ref[gi]) for gi in range(N_POOL)], axis=1)
    za = jnp.dot(hb, win_ref[:, D_A:2 * D_A], preferred_element_type=F32)
    ya = ya * pscale_ref[...] * _silu(za)

    qkv3 = jnp.dot(hb, win_ref[:, 2 * D_A:2 * D_A + D_QKV], preferred_element_type=F32).reshape(nb, CHUNK, D_QKV)
    qkv_ext[:, 8:8 + CHUNK, :] = qkv3
    cw = convw_ref[...]
    acc = qkv3 * cw[CONV_K - 1:CONV_K][None]
    for j in range(CONV_K - 1):
        acc = acc + qkv_ext[:, pl.ds(8 - (CONV_K - 1) + j, CHUNK), :] * cw[j:j + 1][None]
    qkvc = _silu(acc).reshape(rows, D_QKV)
    for hh in range(N_HEADS):
        sl = slice(hh * HEAD_DIM, (hh + 1) * HEAD_DIM)
        qh = qkvc[:, hh * HEAD_DIM:(hh + 1) * HEAD_DIM]
        kh = qkvc[:, D_B + hh * HEAD_DIM:D_B + (hh + 1) * HEAD_DIM]
        q_s[:, sl] = qh * lax.rsqrt(jnp.sum(qh * qh, axis=-1, keepdims=True) + EPS) * (HEAD_DIM ** -0.5)
        k_s[:, sl] = kh * lax.rsqrt(jnp.sum(kh * kh, axis=-1, keepdims=True) + EPS)
    v_s[...] = qkvc[:, 2 * D_B:3 * D_B]

    bac = jnp.dot(hb, wba_ref[...], preferred_element_type=F32)
    bar = lax.dot_general(wbat_ref[...], hb, (((1,), (1,)), ((), ())), preferred_element_type=F32)
    sig = jax.nn.sigmoid(bac)
    gcol = -jnp.exp(alogc_ref[...]) * _softplus(bac + dtbc_ref[...])
    g_hi = gcol.astype(BF16)
    g_lo = (gcol - g_hi.astype(F32)).astype(BF16)
    gc_parts = []
    for b in range(nb):
        rs = slice(b * CHUNK, (b + 1) * CHUNK)
        gc_parts.append(jnp.dot(ltri_ref[...], jnp.concatenate([g_hi[rs], g_lo[rs]], axis=0),
                                preferred_element_type=F32))
    gc = jnp.concatenate(gc_parts, axis=0)
    for hh in range(N_HEADS):
        beta_s[hh] = jnp.broadcast_to(sig[:, hh:hh + 1], (rows, LANES))
        gcb = jnp.broadcast_to(gc[:, N_HEADS + hh:N_HEADS + hh + 1], (rows, LANES))
        gc_s[hh] = gcb
        eg_s[hh] = jnp.exp(gcb)
    grow = -jnp.exp(alogr_ref[...]) * _softplus(bar + dtbr_ref[...])
    lane_in_chunk = lax.broadcasted_iota(jnp.int32, grow.shape, 1) % CHUNK
    sh = 1
    while sh < CHUNK:
        grow = grow + jnp.where(lane_in_chunk >= sh, pltpu.roll(grow, sh, 1), 0.0)
        sh *= 2
    gcrow_s[...] = grow

    ri = lax.broadcasted_iota(jnp.int32, (CHUNK, LANES), 0)
    li = lax.broadcasted_iota(jnp.int32, (CHUNK, LANES), 1)
    lj = li % CHUNK
    lo_half = li < CHUNK
    strict = ri > lj
    causal = ri >= lj
    eye2 = (ri == lj).astype(F32)

    def pair_mul(xp, yp):
        ybd = jnp.concatenate([jnp.where(lo_half, yp, 0.0), jnp.where(lo_half, 0.0, yp)], axis=0)
        return _dot(xp, ybd)

    chains = [(p, hh) for p in range(nb // 2) for hh in range(N_HEADS)]
    for g0 in range(0, len(chains), GDN_GROUP):
        st = []
        for p, hh in chains[g0:g0 + GDN_GROUP]:
            ra = slice(2 * p * CHUNK, (2 * p + 1) * CHUNK)
            rb = slice((2 * p + 1) * CHUNK, (2 * p + 2) * CHUNK)
            sl = slice(hh * HEAD_DIM, (hh + 1) * HEAD_DIM)
            ka, kb_ = k_s[ra, sl], k_s[rb, sl]
            kba, kbb = ka * beta_s[hh, ra, :], kb_ * beta_s[hh, rb, :]
            lhs1 = jnp.concatenate([jnp.concatenate([kba, kbb], axis=1),
                                    jnp.concatenate([q_s[ra, sl], q_s[rb, sl]], axis=1)], axis=0)
            kkqk = _dot_nt(lhs1, _block_diag(ka, kb_))
            st.append(dict(ra=ra, rb=rb, sl=sl, hh=hh, p=p, kkqk=kkqk))
        for c in st:
            hh, p = c["hh"], c["p"]
            gcol_p = jnp.where(lo_half, gc_s[hh, c["ra"], :], gc_s[hh, c["rb"], :])
            grow_p = jnp.broadcast_to(gcrow_s[N_HEADS + hh:N_HEADS + hh + 1, 2 * p * CHUNK:(2 * p + 2) * CHUNK],
                                      (CHUNK, LANES))
            dec = jnp.exp(jnp.minimum(gcol_p - grow_p, 0.0))
            kkqk = c.pop("kkqk")
            c["npow"] = jnp.where(strict, kkqk[0:CHUNK] * dec, 0.0)
            c["qkm"] = jnp.where(causal, kkqk[CHUNK:2 * CHUNK] * dec, 0.0)
            c["t"] = eye2 - c["npow"]
        for _ in range(5):
            for c in st:
                c["npow"] = pair_mul(c["npow"], c["npow"])
            for c in st:
                c["t"] = c["t"] + pair_mul(c["t"], c["npow"])
        for c in st:
            ra, rb, sl, hh = c["ra"], c["rb"], c["sl"], c["hh"]
            ba, bb = beta_s[hh, ra, :], beta_s[hh, rb, :]
            kba, kbb = k_s[ra, sl] * ba, k_s[rb, sl] * bb
            rhs = jnp.concatenate([_block_diag(v_s[ra, sl] * ba, v_s[rb, sl] * bb),
                                   _block_diag(kba * eg_s[hh, ra, :], kbb * eg_s[hh, rb, :])], axis=1)
            c["uw"] = _dot(c.pop("t"), rhs)
        for c in st:
            ra, rb, sl, hh, p = c["ra"], c["rb"], c["sl"], c["hh"], c["p"]
            c["ia"], c["ib"] = 2 * p * N_HEADS + hh, (2 * p + 1) * N_HEADS + hh
            lhs3 = jnp.concatenate([c["uw"][:, 2 * HEAD_DIM:4 * HEAD_DIM],
                                    jnp.concatenate([q_s[ra, sl] * eg_s[hh, ra, :], q_s[rb, sl] * eg_s[hh, rb, :]],
                                                    axis=1)], axis=0)
            c["r3"] = _dot(lhs3, _block_diag(s_ref[c["ia"]], s_ref[c["ib"]]))
        for c in st:
            ra, rb, sl, hh, p = c["ra"], c["rb"], c["sl"], c["hh"], c["p"]
            r3 = c.pop("r3")
            vn = c.pop("uw")[:, 0:2 * HEAD_DIM] - r3[0:CHUNK]
            o = r3[CHUNK:2 * CHUNK] + _dot(c.pop("qkm"), _block_diag(vn[:, 0:HEAD_DIM], vn[:, HEAD_DIM:2 * HEAD_DIM]))
            o_s[ra, sl] = o[:, 0:HEAD_DIM]
            o_s[rb, sl] = o[:, HEAD_DIM:2 * HEAD_DIM]
            gla = gc_s[hh, (2 * p + 1) * CHUNK - 1:(2 * p + 1) * CHUNK, :]
            glb = gc_s[hh, (2 * p + 2) * CHUNK - 1:(2 * p + 2) * CHUNK, :]
            kg = jnp.concatenate([k_s[ra, sl] * jnp.exp(gla - gc_s[hh, ra, :]),
                                  k_s[rb, sl] * jnp.exp(glb - gc_s[hh, rb, :])], axis=1)
            upd = _dot_tn(kg, vn)
            s_ref[c["ia"]] = s_ref[c["ia"]] * jnp.exp(gla) + upd[0:HEAD_DIM, 0:HEAD_DIM]
            s_ref[c["ib"]] = s_ref[c["ib"]] * jnp.exp(glb) + upd[HEAD_DIM:2 * HEAD_DIM, HEAD_DIM:2 * HEAD_DIM]

    zb = jnp.dot(hb, win_ref[:, 2 * D_A + D_QKV:D_MAIN], preferred_element_type=F32)
    o_all = o_s[...]
    yb_parts = []
    for hh in range(N_HEADS):
        oh = o_all[:, hh * HEAD_DIM:(hh + 1) * HEAD_DIM]
        yb_parts.append(oh * lax.rsqrt(jnp.mean(oh * oh, axis=-1, keepdims=True) + EPS) * onw_ref[...])
    yb = jnp.concatenate(yb_parts, axis=1) * _silu(zb)
    ymix = jnp.concatenate([ya, yb], axis=1)
    yo = jnp.dot(ymix.astype(BF16), wout_ref[...], preferred_element_type=F32)
    yn = yo * lax.rsqrt(jnp.mean(yo * yo, axis=-1, keepdims=True) + EPS) * postw_ref[...]
    y_ref[...] = x + gate * yn.reshape(nb, CHUNK, D_MODEL)

    qkv_ext[:, 0:8, :] = qkv_ext[:, CHUNK:CHUNK + 8, :]
    ua_ext[:, 0:CHUNK, :] = ua3

    @pl.when(step == pl.num_programs(0) - 1)
    def _():
        npool_ref[...] = ua3[:, CHUNK - POOL_BUF:CHUNK, :]
        nconv_ref[...] = qkv3[:, CHUNK - (CONV_K - 1):CHUNK, :]
        nssm_ref[...] = s_ref[...]


def _prompt_layer(x, mod, consts, lw):
    nb, seq, _ = x.shape
    rows = nb * CHUNK
    nstep = seq // CHUNK
    full = lambda shape: pl.BlockSpec(shape, lambda s, _n=len(shape): (0,) * _n)
    in_specs = [
        pl.BlockSpec((nb, CHUNK, D_MODEL), lambda s: (0, s, 0)),
        full((nb, 1, 3 * D_MODEL)),
        full((1, D_MODEL)), full((1, D_MODEL)),
        full((D_MODEL, D_MAIN)), full((D_MODEL, LANES)), full((8, D_MODEL)),
        full((CONV_K, D_QKV)), full((N_POOL, POOL_GC, POOL_GC)), full((1, D_A)),
        full((1, LANES)), full((1, LANES)), full((8, rows)), full((8, rows)),
        full((1, HEAD_DIM)), full((D_MODEL, D_MODEL)),
        full((N_POOL, CHUNK, 2 * CHUNK)), full((CHUNK, 2 * CHUNK)),
    ]
    out_specs = [
        pl.BlockSpec((nb, CHUNK, D_MODEL), lambda s: (0, s, 0)),
        full((nb, POOL_BUF, D_A)), full((nb, CONV_K - 1, D_QKV)), full((nb * N_HEADS, HEAD_DIM, HEAD_DIM)),
    ]
    out_shape = [
        jax.ShapeDtypeStruct((nb, seq, D_MODEL), F32),
        jax.ShapeDtypeStruct((nb, POOL_BUF, D_A), F32),
        jax.ShapeDtypeStruct((nb, CONV_K - 1, D_QKV), F32),
        jax.ShapeDtypeStruct((nb * N_HEADS, HEAD_DIM, HEAD_DIM), F32),
    ]
    scratch = [
        pltpu.VMEM((nb, CHUNK + 8, D_QKV), F32),
        pltpu.VMEM((nb, 2 * CHUNK, D_A), F32),
        pltpu.VMEM((nb * N_HEADS, HEAD_DIM, HEAD_DIM), F32),
        pltpu.VMEM((rows, D_B), F32), pltpu.VMEM((rows, D_B), F32), pltpu.VMEM((rows, D_B), F32),
        pltpu.VMEM((rows, D_B), F32),
        pltpu.VMEM((N_HEADS, rows, LANES), F32), pltpu.VMEM((N_HEADS, rows, LANES), F32),
        pltpu.VMEM((N_HEADS, rows, LANES), F32),
        pltpu.VMEM((8, rows), F32),
    ]
    y, npool, nconv, nssm = pl.pallas_call(
        _prompt_kernel,
        grid=(nstep,),
        in_specs=in_specs,
        out_specs=out_specs,
        out_shape=out_shape,
        scratch_shapes=scratch,
        compiler_params=pltpu.CompilerParams(dimension_semantics=("arbitrary",), vmem_limit_bytes=VMEM_LIMIT),
        name="prompt_layer",
    )(x, mod, lw["prew"], lw["postw"], lw["win"], lw["wba"], lw["wbat"], lw["convw"], lw["poolw"], lw["pscale"],
      lw["alogc"], lw["dtbc"], lw["alogr_p"], lw["dtbr_p"], lw["onw"], lw["wout"], consts["band"], consts["ltri"])
    return y, npool, nconv, nssm.reshape(nb, N_HEADS, HEAD_DIM, HEAD_DIM)


SAMPLE_BLOCK = 8


def _sample_kernel(x_ref, mod_ref, spool_ref, sconv_ref, ssm_ref, prew_ref, postw_ref, win_ref, wba_ref, convw_ref,
                   poolw_ref, pscale_ref, alogc_ref, dtbc_ref, onw_ref, wout_ref,
                   y_ref, npool_ref, nconv_ref, nssm_ref,
                   q_s, k_s, v_s, o_s, beta_s, eg_s, qk_s, ya_s, zb_s, *, past_len):
    step = pl.program_id(0)
    nseq = x_ref.shape[0]

    @pl.when(step == 0)
    def _():
        x = x_ref[...]
        mod = mod_ref[...]
        shift = mod[:, 0:D_MODEL]
        scale = mod[:, D_MODEL:2 * D_MODEL]
        a_mul = prew_ref[...] * (1.0 + scale)
        ms = jnp.mean(x * x, axis=-1, keepdims=True)
        hb = (x * lax.rsqrt(ms + EPS) * a_mul + shift).astype(BF16)

        ua = jnp.dot(hb, win_ref[:, 0:D_A], preferred_element_type=F32)
        ya_parts = []
        for gi, w in enumerate(POOL_WINDOWS):
            gs = slice(gi * POOL_GC, (gi + 1) * POOL_GC)
            win = ua[:, gs]
            for d in range(1, w):
                win = win + spool_ref[:, POOL_BUF - d, gs]
            cnt = float(min(past_len + 1, w))
            pooled = win / cnt - ua[:, gs]
            ya_parts.append(_dot(pooled, poolw_ref[gi]))
        za = jnp.dot(hb, win_ref[:, D_A:2 * D_A], preferred_element_type=F32)
        ya_s[...] = jnp.concatenate(ya_parts, axis=1) * pscale_ref[...] * _silu(za)
        npool_ref[:, 0:POOL_BUF - 1, :] = spool_ref[:, 1:POOL_BUF, :]
        npool_ref[:, POOL_BUF - 1, :] = ua

        qkv = jnp.dot(hb, win_ref[:, 2 * D_A:2 * D_A + D_QKV], preferred_element_type=F32)
        cw = convw_ref[...]
        acc = qkv * cw[CONV_K - 1:CONV_K]
        for j in range(CONV_K - 1):
            acc = acc + sconv_ref[:, j, :] * cw[j:j + 1]
        nconv_ref[:, 0:CONV_K - 2, :] = sconv_ref[:, 1:CONV_K - 1, :]
        nconv_ref[:, CONV_K - 2, :] = qkv
        qkvc = _silu(acc)
        bac = jnp.dot(hb, wba_ref[...], preferred_element_type=F32)
        sig = jax.nn.sigmoid(bac)
        eg = jnp.exp(-jnp.exp(alogc_ref[...]) * _softplus(bac + dtbc_ref[...]))
        for hh in range(N_HEADS):
            sl = slice(hh * HEAD_DIM, (hh + 1) * HEAD_DIM)
            qh = qkvc[:, hh * HEAD_DIM:(hh + 1) * HEAD_DIM]
            kh = qkvc[:, D_B + hh * HEAD_DIM:D_B + (hh + 1) * HEAD_DIM]
            qn = qh * lax.rsqrt(jnp.sum(qh * qh, axis=-1, keepdims=True) + EPS) * (HEAD_DIM ** -0.5)
            kn = kh * lax.rsqrt(jnp.sum(kh * kh, axis=-1, keepdims=True) + EPS)
            q_s[:, sl] = qn
            k_s[:, sl] = kn
            qk_s[hh] = jnp.broadcast_to(jnp.sum(qn * kn, axis=-1, keepdims=True), (nseq, LANES))
            beta_s[hh] = jnp.broadcast_to(sig[:, hh:hh + 1], (nseq, LANES))
            eg_s[hh] = jnp.broadcast_to(eg[:, N_HEADS + hh:N_HEADS + hh + 1], (nseq, LANES))
        v_s[...] = qkvc[:, 2 * D_B:3 * D_B]
        zb_s[...] = jnp.dot(hb, win_ref[:, 2 * D_A + D_QKV:D_MAIN], preferred_element_type=F32)

    r0 = pl.multiple_of(step * SAMPLE_BLOCK, SAMPLE_BLOCK)
    kblk = k_s[pl.ds(r0, SAMPLE_BLOCK), :]
    qblk = q_s[pl.ds(r0, SAMPLE_BLOCK), :]
    vblk = v_s[pl.ds(r0, SAMPLE_BLOCK), :]
    row8 = lax.broadcasted_iota(jnp.int32, (8, HEAD_DIM), 0)
    o_rows = []
    for i in range(SAMPLE_BLOCK):
        o_heads = []
        for hh in range(N_HEADS):
            sl = slice(hh * HEAD_DIM, (hh + 1) * HEAD_DIM)
            krow, qrow, vrow = kblk[i:i + 1, sl], qblk[i:i + 1, sl], vblk[i:i + 1, sl]
            beta = beta_s[hh, pl.ds(r0 + i, 1), :]
            eg = eg_s[hh, pl.ds(r0 + i, 1), :]
            qk = qk_s[hh, pl.ds(r0 + i, 1), :]
            s_old = ssm_ref[i * N_HEADS + hh]
            lhs = jnp.where(row8 == 0, krow, jnp.where(row8 == 1, qrow, 0.0))
            r = _dot(lhs, s_old)
            delta = (vrow - eg * r[0:1]) * beta
            o_heads.append(eg * r[1:2] + qk * delta)
            k_hi = krow.astype(BF16).astype(F32)
            d_hi = delta.astype(BF16).astype(F32)
            kp = jnp.where(row8 == 0, k_hi, jnp.where(row8 == 1, krow - k_hi, jnp.where(row8 == 2, k_hi, 0.0)))
            dp = jnp.where(row8 == 0, d_hi, jnp.where(row8 == 1, d_hi, jnp.where(row8 == 2, delta - d_hi, 0.0)))
            nssm_ref[i * N_HEADS + hh] = s_old * eg + _dot_tn(kp, dp)
        o_rows.append(jnp.concatenate(o_heads, axis=1))
    o_s[pl.ds(r0, SAMPLE_BLOCK), :] = jnp.concatenate(o_rows, axis=0)

    @pl.when(step == pl.num_programs(0) - 1)
    def _():
        x = x_ref[...]
        gate = mod_ref[:, 2 * D_MODEL:3 * D_MODEL]
        o_all = o_s[...]
        yb_parts = []
        for hh in range(N_HEADS):
            oh = o_all[:, hh * HEAD_DIM:(hh + 1) * HEAD_DIM]
            yb_parts.append(oh * lax.rsqrt(jnp.mean(oh * oh, axis=-1, keepdims=True) + EPS) * onw_ref[...])
        yb = jnp.concatenate(yb_parts, axis=1) * _silu(zb_s[...])
        ymix = jnp.concatenate([ya_s[...], yb], axis=1)
        yo = jnp.dot(ymix.astype(BF16), wout_ref[...], preferred_element_type=F32)
        yn = yo * lax.rsqrt(jnp.mean(yo * yo, axis=-1, keepdims=True) + EPS) * postw_ref[...]
        y_ref[...] = x + gate * yn


def _sample_layer(x, mod, spool, sconv, ssm, lw, past_len):
    nseq = x.shape[0]
    nstep = nseq // SAMPLE_BLOCK
    blk_states = SAMPLE_BLOCK * N_HEADS
    full = lambda shape: pl.BlockSpec(shape, lambda s, _n=len(shape): (0,) * _n)
    in_specs = [
        full((nseq, D_MODEL)), full((nseq, 3 * D_MODEL)),
        full((nseq, POOL_BUF, D_A)), full((nseq, CONV_K - 1, D_QKV)),
        pl.BlockSpec((blk_states, HEAD_DIM, HEAD_DIM), lambda s: (s, 0, 0)),
        full((1, D_MODEL)), full((1, D_MODEL)),
        full((D_MODEL, D_MAIN)), full((D_MODEL, LANES)),
        full((CONV_K, D_QKV)), full((N_POOL, POOL_GC, POOL_GC)), full((1, D_A)),
        full((1, LANES)), full((1, LANES)), full((1, HEAD_DIM)), full((D_MODEL, D_MODEL)),
    ]
    out_specs = [
        full((nseq, D_MODEL)), full((nseq, POOL_BUF, D_A)), full((nseq, CONV_K - 1, D_QKV)),
        pl.BlockSpec((blk_states, HEAD_DIM, HEAD_DIM), lambda s: (s, 0, 0)),
    ]
    out_shape = [
        jax.ShapeDtypeStruct((nseq, D_MODEL), F32),
        jax.ShapeDtypeStruct((nseq, POOL_BUF, D_A), F32),
        jax.ShapeDtypeStruct((nseq, CONV_K - 1, D_QKV), F32),
        jax.ShapeDtypeStruct((nseq * N_HEADS, HEAD_DIM, HEAD_DIM), F32),
    ]
    scratch = [
        pltpu.VMEM((nseq, D_B), F32), pltpu.VMEM((nseq, D_B), F32), pltpu.VMEM((nseq, D_B), F32),
        pltpu.VMEM((nseq, D_B), F32),
        pltpu.VMEM((N_HEADS, nseq, LANES), F32), pltpu.VMEM((N_HEADS, nseq, LANES), F32),
        pltpu.VMEM((N_HEADS, nseq, LANES), F32),
        pltpu.VMEM((nseq, D_A), F32), pltpu.VMEM((nseq, D_B), F32),
    ]
    y, npool, nconv, nssm = pl.pallas_call(
        functools.partial(_sample_kernel, past_len=past_len),
        grid=(nstep,),
        in_specs=in_specs,
        out_specs=out_specs,
        out_shape=out_shape,
        scratch_shapes=scratch,
        compiler_params=pltpu.CompilerParams(dimension_semantics=("arbitrary",), vmem_limit_bytes=VMEM_LIMIT),
        name="sample_layer",
    )(x, mod, spool, sconv, ssm.reshape(nseq * N_HEADS, HEAD_DIM, HEAD_DIM), lw["prew"], lw["postw"], lw["win"],
      lw["wba"], lw["convw"], lw["poolw"], lw["pscale"], lw["alogc"], lw["dtbc"], lw["onw"], lw["wout"])
    return y, npool, nconv, nssm.reshape(nseq, N_HEADS, HEAD_DIM, HEAD_DIM)


def _constants():
    t = np.arange(CHUNK)[:, None]
    j = np.arange(2 * CHUNK)[None, :]
    band = np.stack([((j <= CHUNK + t) & (j > CHUNK + t - w)) for w in POOL_WINDOWS]).astype(np.float32)
    tri = (np.arange(CHUNK)[None, :] <= t).astype(np.float32)
    ltri = np.concatenate([tri, tri], axis=1)
    return {"band": jnp.asarray(band, BF16), "ltri": jnp.asarray(ltri, BF16)}


def _layer_weights(l, rows_p, pre_norm_w, post_norm_w, w_in, conv_w, pool_w, pool_scale, a_log, dt_bias, o_norm_w,
                   w_out):
    wba = w_in[l][:, D_MAIN:D_MAIN + 2 * N_HEADS]
    pad_c = lambda v: jnp.zeros((1, LANES), F32).at[0, N_HEADS:2 * N_HEADS].set(v)
    pad_r = lambda v: jnp.broadcast_to(jnp.zeros((8,), F32).at[N_HEADS:2 * N_HEADS].set(v)[:, None], (8, rows_p))
    return {
        "prew": pre_norm_w[l].reshape(1, D_MODEL),
        "postw": post_norm_w[l].reshape(1, D_MODEL),
        "win": w_in[l][:, 0:D_MAIN].astype(BF16),
        "wba": jnp.zeros((D_MODEL, LANES), F32).at[:, 0:2 * N_HEADS].set(wba).astype(BF16),
        "wbat": wba.T.astype(BF16),
        "convw": conv_w[l],
        "poolw": pool_w[l].astype(BF16),
        "pscale": pool_scale[l].reshape(1, D_A),
        "alogc": pad_c(a_log[l]), "dtbc": pad_c(dt_bias[l]),
        "alogr_p": pad_r(a_log[l]), "dtbr_p": pad_r(dt_bias[l]),
        "onw": o_norm_w[l].reshape(1, HEAD_DIM),
        "wout": w_out[l].astype(BF16),
    }


def kernel(x_prompt, x_sample, c_prompt, c_sample, state_pool, state_conv, state_ssm, w_ada, b_ada, pre_norm_w,
           post_norm_w, w_in, conv_w, pool_w, pool_scale, a_log, dt_bias, o_norm_w, w_out):
    depth = w_in.shape[0]
    nb, seq, _ = x_prompt.shape
    nseq, dec_seq, _ = x_sample.shape
    assert dec_seq == 1 and seq % CHUNK == 0 and nb % 2 == 0 and nseq % SAMPLE_BLOCK == 0
    past_len = 16384
    consts = _constants()
    mod = _mod_call(jnp.concatenate([c_prompt, c_sample], axis=0), w_ada, b_ada)
    yp, ys = x_prompt, x_sample.reshape(nseq, D_MODEL)
    outs = [[] for _ in range(6)]
    for l in range(depth):
        lw = _layer_weights(l, nb * CHUNK, pre_norm_w, post_norm_w, w_in, conv_w, pool_w, pool_scale, a_log, dt_bias,
                            o_norm_w, w_out)
        yp, a1, a2, a3 = _prompt_layer(yp, mod[l, 0:nb].reshape(nb, 1, 3 * D_MODEL), consts, lw)
        ys, b1, b2, b3 = _sample_layer(ys, mod[l, nb:nb + nseq], state_pool[l], state_conv[l], state_ssm[l], lw,
                                       past_len)
        for lst, v in zip(outs, (a1, a2, a3, b1, b2, b3)):
            lst.append(v)
    return (yp, ys.reshape(nseq, dec_seq, D_MODEL)) + tuple(jnp.stack(o) for o in outs)
```

```python
import functools

import jax
import jax.numpy as jnp
import numpy as np
from jax import lax
from jax.experimental import pallas as pl
from jax.experimental.pallas import tpu as pltpu

F32 = jnp.float32
BF16 = jnp.bfloat16

D_MODEL = 1024
D_A = 512
D_B = 512
N_POOL = 4
POOL_WINDOWS = (2, 4, 8, 16)
POOL_GC = 128
POOL_BUF = 15
HEAD_DIM = 128
N_HEADS = 4
D_QKV = 3 * D_B
CONV_K = 4
D_MAIN = 2 * D_A + D_QKV + D_B
PAST_LEN = 16384
EPS = 1e-6
CHUNK = 64
GDN_GROUP = 16
SAMPLE_BLOCK = 8
LANES = 128
SUBLANES = 8
QKV_TILES = D_QKV // LANES
VMEM_LIMIT = 56 * 1024 * 1024


def _dot(a, b):
    return jnp.dot(a.astype(BF16), b.astype(BF16), preferred_element_type=F32)


def _dot_nt(a, b):
    return lax.dot_general(a.astype(BF16), b.astype(BF16), (((1,), (1,)), ((), ())), preferred_element_type=F32)


def _dot_tn(a, b):
    return lax.dot_general(a, b, (((0,), (0,)), ((), ())), preferred_element_type=F32)


def _silu(x):
    return x * jax.nn.sigmoid(x)


def _softplus(x):
    return jnp.maximum(x, 0.0) + jnp.log1p(jnp.exp(-jnp.abs(x)))


def _block_diag(a, b):
    top = jnp.concatenate([a, jnp.zeros((a.shape[0], b.shape[1]), a.dtype)], axis=1)
    bot = jnp.concatenate([jnp.zeros((b.shape[0], a.shape[1]), b.dtype), b], axis=1)
    return jnp.concatenate([top, bot], axis=0)


def _head_rms(o_all, w):
    parts = []
    for hh in range(N_HEADS):
        oh = o_all[:, hh * HEAD_DIM:(hh + 1) * HEAD_DIM]
        parts.append(oh * lax.rsqrt(jnp.mean(oh * oh, axis=-1, keepdims=True) + EPS) * w)
    return jnp.concatenate(parts, axis=1)


def _mod_kernel(cp_ref, cs_ref, w_ref, b_ref, op_ref, os_ref):
    w = w_ref[...]
    b = b_ref[...]
    op_ref[...] = jnp.dot(_silu(cp_ref[...]), w, preferred_element_type=F32) + b
    os_ref[...] = jnp.dot(_silu(cs_ref[...]), w, preferred_element_type=F32) + b


def _mod_call(c_prompt, c_sample, w_ada, b_ada):
    depth, _, n3 = w_ada.shape
    nb, nseq = c_prompt.shape[0], c_sample.shape[0]
    nblk = n3 // D_MODEL
    return pl.pallas_call(
        _mod_kernel,
        grid=(depth, nblk),
        in_specs=[
            pl.BlockSpec((nb, D_MODEL), lambda l, j: (0, 0)),
            pl.BlockSpec((nseq, D_MODEL), lambda l, j: (0, 0)),
            pl.BlockSpec((None, D_MODEL, D_MODEL), lambda l, j: (l, 0, j)),
            pl.BlockSpec((None, 1, D_MODEL), lambda l, j: (l, 0, j)),
        ],
        out_specs=[pl.BlockSpec((None, nb, D_MODEL), lambda l, j: (l, 0, j)),
                   pl.BlockSpec((None, nseq, D_MODEL), lambda l, j: (l, 0, j))],
        out_shape=[jax.ShapeDtypeStruct((depth, nb, n3), F32), jax.ShapeDtypeStruct((depth, nseq, n3), F32)],
        compiler_params=pltpu.CompilerParams(dimension_semantics=("arbitrary", "arbitrary")),
        name="adaln_mod",
    )(c_prompt, c_sample, w_ada, b_ada.reshape(depth, 1, n3))


def _prompt_kernel(x_ref, mod_ref, prew_ref, postw_ref, win_ref, wba_ref, wbat_ref, convw_ref, poolw_ref,
                   pscale_ref, alogc_ref, dtbc_ref, alogr_ref, dtbr_ref, onw_ref, wout_ref, band_ref, ltri_ref,
                   *rest, n_alias):
    rest = rest[n_alias:]
    y_ref, npool_ref, nconv_ref, nssm_ref = rest[0:4]
    qkv_ext, ua_ext, s_ref, q_s, k_s, v_s, o_s, beta_s, gc_s, eg_s, gcrow_s = rest[4:]
    step = pl.program_id(0)
    nb = x_ref.shape[0]
    rows = nb * CHUNK

    @pl.when(step == 0)
    def _():
        qkv_ext[:, :, 0:SUBLANES, :] = jnp.zeros((QKV_TILES, nb, SUBLANES, LANES), F32)
        ua_ext[:, 0:CHUNK, :] = jnp.zeros((nb, CHUNK, D_A), F32)
        s_ref[...] = jnp.zeros(s_ref.shape, F32)

    x = x_ref[...]
    mod = mod_ref[...]
    shift = mod[:, :, 0:D_MODEL]
    scale = mod[:, :, D_MODEL:2 * D_MODEL]
    gate = mod[:, :, 2 * D_MODEL:3 * D_MODEL]
    a_mul = prew_ref[...][None] * (1.0 + scale)
    ms = jnp.mean(x * x, axis=-1, keepdims=True)
    h = x * lax.rsqrt(ms + EPS) * a_mul + shift
    hb = h.reshape(rows, D_MODEL).astype(BF16)

    ua = jnp.dot(hb, win_ref[:, 0:D_A], preferred_element_type=F32)
    ua3 = ua.reshape(nb, CHUNK, D_A)
    ua_ext[:, CHUNK:2 * CHUNK, :] = ua3
    pos = lax.broadcasted_iota(jnp.int32, (CHUNK, POOL_GC), 0) + step * CHUNK
    pooled_groups = []
    for gi, w in enumerate(POOL_WINDOWS):
        cnt = jnp.minimum(pos + 1, w).astype(F32)
        per_b = []
        for b in range(nb):
            ext = ua_ext[b, :, gi * POOL_GC:(gi + 1) * POOL_GC]
            win = jnp.dot(band_ref[gi], ext.astype(BF16), preferred_element_type=F32)
            per_b.append(win / cnt - ua3[b, :, gi * POOL_GC:(gi + 1) * POOL_GC])
        pooled_groups.append(jnp.concatenate(per_b, axis=0))
    ya = jnp.concatenate([_dot(pooled_groups[gi], poolw_ref[gi]) for gi in range(N_POOL)], axis=1)
    za = jnp.dot(hb, win_ref[:, D_A:2 * D_A], preferred_element_type=F32)
    ya = ya * pscale_ref[...] * _silu(za)

    qkv = jnp.dot(hb, win_ref[:, 2 * D_A:2 * D_A + D_QKV], preferred_element_type=F32)
    qkv_t = jnp.stack([qkv[:, c * LANES:(c + 1) * LANES] for c in range(QKV_TILES)], axis=0)
    qkv_t = qkv_t.reshape(QKV_TILES, nb, CHUNK, LANES)
    qkv_ext[:, :, SUBLANES:SUBLANES + CHUNK, :] = qkv_t
    cw = convw_ref[...]
    cw_t = [jnp.stack([cw[j:j + 1, c * LANES:(c + 1) * LANES] for c in range(QKV_TILES)], axis=0)[:, None]
            for j in range(CONV_K)]
    acc = qkv_t * cw_t[CONV_K - 1]
    for j in range(CONV_K - 1):
        acc = acc + qkv_ext[:, :, pl.ds(SUBLANES - (CONV_K - 1) + j, CHUNK), :] * cw_t[j]
    act = _silu(acc).reshape(QKV_TILES, rows, LANES)
    for hh in range(N_HEADS):
        sl = slice(hh * HEAD_DIM, (hh + 1) * HEAD_DIM)
        qh, kh = act[hh], act[N_HEADS + hh]
        q_s[:, sl] = qh * lax.rsqrt(jnp.sum(qh * qh, axis=-1, keepdims=True) + EPS) * (HEAD_DIM ** -0.5)
        k_s[:, sl] = kh * lax.rsqrt(jnp.sum(kh * kh, axis=-1, keepdims=True) + EPS)
        v_s[:, sl] = act[2 * N_HEADS + hh]

    bac = jnp.dot(hb, wba_ref[...], preferred_element_type=F32)
    bar = lax.dot_general(wbat_ref[...], hb, (((1,), (1,)), ((), ())), preferred_element_type=F32)
    sig = jax.nn.sigmoid(bac)
    gcol = -jnp.exp(alogc_ref[...]) * _softplus(bac + dtbc_ref[...])
    g_hi = gcol.astype(BF16)
    g_lo = (gcol - g_hi.astype(F32)).astype(BF16)
    gc_parts = []
    for b in range(nb):
        rs = slice(b * CHUNK, (b + 1) * CHUNK)
        gc_parts.append(jnp.dot(ltri_ref[...], jnp.concatenate([g_hi[rs], g_lo[rs]], axis=0),
                                preferred_element_type=F32))
    gc = jnp.concatenate(gc_parts, axis=0)
    for hh in range(N_HEADS):
        beta_s[hh] = jnp.broadcast_to(sig[:, hh:hh + 1], (rows, LANES))
        gcb = jnp.broadcast_to(gc[:, N_HEADS + hh:N_HEADS + hh + 1], (rows, LANES))
        gc_s[hh] = gcb
        eg_s[hh] = jnp.exp(gcb)
    grow = -jnp.exp(alogr_ref[...]) * _softplus(bar + dtbr_ref[...])
    lane_in_chunk = lax.broadcasted_iota(jnp.int32, grow.shape, 1) % CHUNK
    sh = 1
    while sh < CHUNK:
        grow = grow + jnp.where(lane_in_chunk >= sh, pltpu.roll(grow, sh, 1), 0.0)
        sh *= 2
    gcrow_s[...] = grow

    ri = lax.broadcasted_iota(jnp.int32, (CHUNK, LANES), 0)
    li = lax.broadcasted_iota(jnp.int32, (CHUNK, LANES), 1)
    lj = li % CHUNK
    lo_half = li < CHUNK
    strict = ri > lj
    causal = ri >= lj
    eye2 = (ri == lj).astype(F32)

    def pair_mul(xp, yp):
        ybd = jnp.concatenate([jnp.where(lo_half, yp, 0.0), jnp.where(lo_half, 0.0, yp)], axis=0)
        return _dot(xp, ybd)

    chains = [(p, hh) for p in range(nb // 2) for hh in range(N_HEADS)]
    for g0 in range(0, len(chains), GDN_GROUP):
        st = []
        for p, hh in chains[g0:g0 + GDN_GROUP]:
            ra = slice(2 * p * CHUNK, (2 * p + 1) * CHUNK)
            rb = slice((2 * p + 1) * CHUNK, (2 * p + 2) * CHUNK)
            sl = slice(hh * HEAD_DIM, (hh + 1) * HEAD_DIM)
            ka, kb_ = k_s[ra, sl], k_s[rb, sl]
            kba, kbb = ka * beta_s[hh, ra, :], kb_ * beta_s[hh, rb, :]
            lhs1 = jnp.concatenate([jnp.concatenate([kba, kbb], axis=1),
                                    jnp.concatenate([q_s[ra, sl], q_s[rb, sl]], axis=1)], axis=0)
            kkqk = _dot_nt(lhs1, _block_diag(ka, kb_))
            st.append(dict(ra=ra, rb=rb, sl=sl, hh=hh, p=p, kkqk=kkqk))
        for c in st:
            hh, p = c["hh"], c["p"]
            gcol_p = jnp.where(lo_half, gc_s[hh, c["ra"], :], gc_s[hh, c["rb"], :])
            grow_p = jnp.broadcast_to(gcrow_s[N_HEADS + hh:N_HEADS + hh + 1, 2 * p * CHUNK:(2 * p + 2) * CHUNK],
                                      (CHUNK, LANES))
            dec = jnp.exp(jnp.minimum(gcol_p - grow_p, 0.0))
            kkqk = c.pop("kkqk")
            c["npow"] = jnp.where(strict, kkqk[0:CHUNK] * dec, 0.0)
            c["qkm"] = jnp.where(causal, kkqk[CHUNK:2 * CHUNK] * dec, 0.0)
            c["t"] = eye2 - c["npow"]
        for _ in range(5):
            for c in st:
                c["npow"] = pair_mul(c["npow"], c["npow"])
            for c in st:
                c["t"] = c["t"] + pair_mul(c["t"], c["npow"])
        for c in st:
            ra, rb, sl, hh = c["ra"], c["rb"], c["sl"], c["hh"]
            ba, bb = beta_s[hh, ra, :], beta_s[hh, rb, :]
            kba, kbb = k_s[ra, sl] * ba, k_s[rb, sl] * bb
            rhs = jnp.concatenate([_block_diag(v_s[ra, sl] * ba, v_s[rb, sl] * bb),
                                   _block_diag(kba * eg_s[hh, ra, :], kbb * eg_s[hh, rb, :])], axis=1)
            c["uw"] = _dot(c.pop("t"), rhs)
        for c in st:
            ra, rb, sl, hh, p = c["ra"], c["rb"], c["sl"], c["hh"], c["p"]
            c["ia"], c["ib"] = 2 * p * N_HEADS + hh, (2 * p + 1) * N_HEADS + hh
            lhs3 = jnp.concatenate([c["uw"][:, 2 * HEAD_DIM:4 * HEAD_DIM],
                                    jnp.concatenate([q_s[ra, sl] * eg_s[hh, ra, :], q_s[rb, sl] * eg_s[hh, rb, :]],
                                                    axis=1)], axis=0)
            c["r3"] = _dot(lhs3, _block_diag(s_ref[c["ia"]], s_ref[c["ib"]]))
        for c in st:
            ra, rb, sl, hh, p = c["ra"], c["rb"], c["sl"], c["hh"], c["p"]
            r3 = c.pop("r3")
            vn = c.pop("uw")[:, 0:2 * HEAD_DIM] - r3[0:CHUNK]
            o = r3[CHUNK:2 * CHUNK] + _dot(c.pop("qkm"), _block_diag(vn[:, 0:HEAD_DIM], vn[:, HEAD_DIM:2 * HEAD_DIM]))
            o_s[ra, sl] = o[:, 0:HEAD_DIM]
            o_s[rb, sl] = o[:, HEAD_DIM:2 * HEAD_DIM]
            gla = gc_s[hh, (2 * p + 1) * CHUNK - 1:(2 * p + 1) * CHUNK, :]
            glb = gc_s[hh, (2 * p + 2) * CHUNK - 1:(2 * p + 2) * CHUNK, :]
            kg = jnp.concatenate([k_s[ra, sl] * jnp.exp(gla - gc_s[hh, ra, :]),
                                  k_s[rb, sl] * jnp.exp(glb - gc_s[hh, rb, :])], axis=1)
            upd = _dot_tn(kg, vn)
            s_ref[c["ia"]] = s_ref[c["ia"]] * jnp.exp(gla) + upd[0:HEAD_DIM, 0:HEAD_DIM]
            s_ref[c["ib"]] = s_ref[c["ib"]] * jnp.exp(glb) + upd[HEAD_DIM:2 * HEAD_DIM, HEAD_DIM:2 * HEAD_DIM]

    zb = jnp.dot(hb, win_ref[:, 2 * D_A + D_QKV:D_MAIN], preferred_element_type=F32)
    yb = _head_rms(o_s[...], onw_ref[...]) * _silu(zb)
    ymix = jnp.concatenate([ya, yb], axis=1)
    yo = jnp.dot(ymix.astype(BF16), wout_ref[...], preferred_element_type=F32)
    yn = yo * lax.rsqrt(jnp.mean(yo * yo, axis=-1, keepdims=True) + EPS) * postw_ref[...]
    y_ref[...] = x + gate * yn.reshape(nb, CHUNK, D_MODEL)

    qkv_ext[:, :, 0:SUBLANES, :] = qkv_ext[:, :, CHUNK:CHUNK + SUBLANES, :]
    ua_ext[:, 0:CHUNK, :] = ua3

    @pl.when(step == pl.num_programs(0) - 1)
    def _():
        npool_ref[...] = ua3[:, CHUNK - POOL_BUF:CHUNK, :]
        nconv_ref[...] = qkv.reshape(nb, CHUNK, D_QKV)[:, CHUNK - (CONV_K - 1):CHUNK, :]
        nssm_ref[...] = s_ref[...]


def _layer_spec(shape, l):
    return pl.BlockSpec((None,) + tuple(shape), lambda s, _n=len(shape): (l,) + (0,) * _n)


def _prompt_layer(l, x, mod_p, lw, consts, prev):
    nb, seq, _ = x.shape
    depth = lw["win"].shape[0]
    rows = nb * CHUNK
    nstep = seq // CHUNK
    full = lambda shape: pl.BlockSpec(shape, lambda s, _n=len(shape): (0,) * _n)
    in_specs = [
        pl.BlockSpec((nb, CHUNK, D_MODEL), lambda s: (0, s, 0)),
        _layer_spec((nb, 1, 3 * D_MODEL), l),
        _layer_spec((1, D_MODEL), l), _layer_spec((1, D_MODEL), l),
        _layer_spec((D_MODEL, D_MAIN), l), _layer_spec((D_MODEL, LANES), l), _layer_spec((SUBLANES, D_MODEL), l),
        _layer_spec((CONV_K, D_QKV), l), _layer_spec((N_POOL, POOL_GC, POOL_GC), l), _layer_spec((1, D_A), l),
        _layer_spec((1, LANES), l), _layer_spec((1, LANES), l),
        _layer_spec((SUBLANES, rows), l), _layer_spec((SUBLANES, rows), l),
        _layer_spec((1, HEAD_DIM), l), _layer_spec((D_MODEL, D_MODEL), l),
        full((N_POOL, CHUNK, 2 * CHUNK)), full((CHUNK, 2 * CHUNK)),
    ] + [pl.BlockSpec(memory_space=pl.ANY)] * len(prev)
    out_specs = [
        pl.BlockSpec((nb, CHUNK, D_MODEL), lambda s: (0, s, 0)),
        _layer_spec((nb, POOL_BUF, D_A), l), _layer_spec((nb, CONV_K - 1, D_QKV), l),
        _layer_spec((nb * N_HEADS, HEAD_DIM, HEAD_DIM), l),
    ]
    out_shape = [
        jax.ShapeDtypeStruct((nb, seq, D_MODEL), F32),
        jax.ShapeDtypeStruct((depth, nb, POOL_BUF, D_A), F32),
        jax.ShapeDtypeStruct((depth, nb, CONV_K - 1, D_QKV), F32),
        jax.ShapeDtypeStruct((depth, nb * N_HEADS, HEAD_DIM, HEAD_DIM), F32),
    ]
    scratch = [
        pltpu.VMEM((QKV_TILES, nb, CHUNK + SUBLANES, LANES), F32),
        pltpu.VMEM((nb, 2 * CHUNK, D_A), F32),
        pltpu.VMEM((nb * N_HEADS, HEAD_DIM, HEAD_DIM), F32),
        pltpu.VMEM((rows, D_B), F32), pltpu.VMEM((rows, D_B), F32), pltpu.VMEM((rows, D_B), F32),
        pltpu.VMEM((rows, D_B), F32),
        pltpu.VMEM((N_HEADS, rows, LANES), F32), pltpu.VMEM((N_HEADS, rows, LANES), F32),
        pltpu.VMEM((N_HEADS, rows, LANES), F32),
        pltpu.VMEM((SUBLANES, rows), F32),
    ]
    n_in = len(in_specs) - len(prev)
    return pl.pallas_call(
        functools.partial(_prompt_kernel, n_alias=len(prev)),
        grid=(nstep,),
        in_specs=in_specs,
        out_specs=out_specs,
        out_shape=out_shape,
        scratch_shapes=scratch,
        input_output_aliases={n_in + i: 1 + i for i in range(len(prev))},
        compiler_params=pltpu.CompilerParams(dimension_semantics=("arbitrary",), vmem_limit_bytes=VMEM_LIMIT),
        name="prompt_layer",
    )(x, mod_p, lw["prew"], lw["postw"], lw["win"], lw["wba"], lw["wbat"], lw["convw"], lw["poolw"], lw["pscale"],
      lw["alogc"], lw["dtbc"], lw["alogr"], lw["dtbr"], lw["onw"], lw["wout"], consts["band"], consts["ltri"], *prev)


def _sample_kernel(x_ref, mod_ref, spool_ref, sconv_ref, ssm_ref, prew_ref, postw_ref, win_ref, wba_ref, convw_ref,
                   poolw_ref, pscale_ref, alogc_ref, dtbc_ref, onw_ref, wout_ref, *rest, n_alias):
    rest = rest[n_alias:]
    y_ref, npool_ref, nconv_ref, nssm_ref = rest[0:4]
    q_s, k_s, v_s, o_s, beta_s, eg_s, qk_s, ya_s, zb_s = rest[4:]
    step = pl.program_id(0)
    nseq = x_ref.shape[0]

    @pl.when(step == 0)
    def _():
        x = x_ref[...]
        mod = mod_ref[...]
        shift = mod[:, 0:D_MODEL]
        scale = mod[:, D_MODEL:2 * D_MODEL]
        a_mul = prew_ref[...] * (1.0 + scale)
        ms = jnp.mean(x * x, axis=-1, keepdims=True)
        hb = (x * lax.rsqrt(ms + EPS) * a_mul + shift).astype(BF16)

        ua = jnp.dot(hb, win_ref[:, 0:D_A], preferred_element_type=F32)
        ya_parts = []
        for gi, w in enumerate(POOL_WINDOWS):
            gs = slice(gi * POOL_GC, (gi + 1) * POOL_GC)
            win = ua[:, gs]
            for d in range(1, w):
                win = win + spool_ref[:, POOL_BUF - d, gs]
            cnt = float(min(PAST_LEN + 1, w))
            pooled = win / cnt - ua[:, gs]
            ya_parts.append(_dot(pooled, poolw_ref[gi]))
        za = jnp.dot(hb, win_ref[:, D_A:2 * D_A], preferred_element_type=F32)
        ya_s[...] = jnp.concatenate(ya_parts, axis=1) * pscale_ref[...] * _silu(za)
        npool_ref[:, 0:POOL_BUF - 1, :] = spool_ref[:, 1:POOL_BUF, :]
        npool_ref[:, POOL_BUF - 1, :] = ua

        qkv = jnp.dot(hb, win_ref[:, 2 * D_A:2 * D_A + D_QKV], preferred_element_type=F32)
        cw = convw_ref[...]
        acc = qkv * cw[CONV_K - 1:CONV_K]
        for j in range(CONV_K - 1):
            acc = acc + sconv_ref[:, j, :] * cw[j:j + 1]
        nconv_ref[:, 0:CONV_K - 2, :] = sconv_ref[:, 1:CONV_K - 1, :]
        nconv_ref[:, CONV_K - 2, :] = qkv
        qkvc = _silu(acc)
        bac = jnp.dot(hb, wba_ref[...], preferred_element_type=F32)
        sig = jax.nn.sigmoid(bac)
        eg = jnp.exp(-jnp.exp(alogc_ref[...]) * _softplus(bac + dtbc_ref[...]))
        for hh in range(N_HEADS):
            sl = slice(hh * HEAD_DIM, (hh + 1) * HEAD_DIM)
            qh = qkvc[:, hh * HEAD_DIM:(hh + 1) * HEAD_DIM]
            kh = qkvc[:, D_B + hh * HEAD_DIM:D_B + (hh + 1) * HEAD_DIM]
            qn = qh * lax.rsqrt(jnp.sum(qh * qh, axis=-1, keepdims=True) + EPS) * (HEAD_DIM ** -0.5)
            kn = kh * lax.rsqrt(jnp.sum(kh * kh, axis=-1, keepdims=True) + EPS)
            q_s[:, sl] = qn
            k_s[:, sl] = kn
            qk_s[hh] = jnp.broadcast_to(jnp.sum(qn * kn, axis=-1, keepdims=True), (nseq, LANES))
            beta_s[hh] = jnp.broadcast_to(sig[:, hh:hh + 1], (nseq, LANES))
            eg_s[hh] = jnp.broadcast_to(eg[:, N_HEADS + hh:N_HEADS + hh + 1], (nseq, LANES))
        v_s[...] = qkvc[:, 2 * D_B:3 * D_B]
        zb_s[...] = jnp.dot(hb, win_ref[:, 2 * D_A + D_QKV:D_MAIN], preferred_element_type=F32)

    r0 = pl.multiple_of(step * SAMPLE_BLOCK, SAMPLE_BLOCK)
    kblk = k_s[pl.ds(r0, SAMPLE_BLOCK), :]
    qblk = q_s[pl.ds(r0, SAMPLE_BLOCK), :]
    vblk = v_s[pl.ds(r0, SAMPLE_BLOCK), :]
    row8 = lax.broadcasted_iota(jnp.int32, (SUBLANES, HEAD_DIM), 0)
    o_rows = []
    for i in range(SAMPLE_BLOCK):
        o_heads = []
        for hh in range(N_HEADS):
            sl = slice(hh * HEAD_DIM, (hh + 1) * HEAD_DIM)
            krow, qrow, vrow = kblk[i:i + 1, sl], qblk[i:i + 1, sl], vblk[i:i + 1, sl]
            beta = beta_s[hh, pl.ds(r0 + i, 1), :]
            eg = eg_s[hh, pl.ds(r0 + i, 1), :]
            qk = qk_s[hh, pl.ds(r0 + i, 1), :]
            s_old = ssm_ref[i * N_HEADS + hh]
            lhs = jnp.where(row8 == 0, krow, jnp.where(row8 == 1, qrow, 0.0))
            r = _dot(lhs, s_old)
            delta = (vrow - eg * r[0:1]) * beta
            o_heads.append(eg * r[1:2] + qk * delta)
            k_hi = krow.astype(BF16).astype(F32)
            d_hi = delta.astype(BF16).astype(F32)
            kp = jnp.where(row8 == 0, k_hi, jnp.where(row8 == 1, krow - k_hi, jnp.where(row8 == 2, k_hi, 0.0)))
            dp = jnp.where(row8 == 0, d_hi, jnp.where(row8 == 1, d_hi, jnp.where(row8 == 2, delta - d_hi, 0.0)))
            nssm_ref[i * N_HEADS + hh] = s_old * eg + _dot_tn(kp, dp)
        o_rows.append(jnp.concatenate(o_heads, axis=1))
    o_s[pl.ds(r0, SAMPLE_BLOCK), :] = jnp.concatenate(o_rows, axis=0)

    @pl.when(step == pl.num_programs(0) - 1)
    def _():
        x = x_ref[...]
        gate = mod_ref[:, 2 * D_MODEL:3 * D_MODEL]
        yb = _head_rms(o_s[...], onw_ref[...]) * _silu(zb_s[...])
        ymix = jnp.concatenate([ya_s[...], yb], axis=1)
        yo = jnp.dot(ymix.astype(BF16), wout_ref[...], preferred_element_type=F32)
        yn = yo * lax.rsqrt(jnp.mean(yo * yo, axis=-1, keepdims=True) + EPS) * postw_ref[...]
        y_ref[...] = x + gate * yn


def _sample_layer(l, x, mod_s, state_pool, state_conv, state_ssm, lw, prev):
    nseq = x.shape[0]
    depth = lw["win"].shape[0]
    nstep = nseq // SAMPLE_BLOCK
    blk_states = SAMPLE_BLOCK * N_HEADS
    full = lambda shape: pl.BlockSpec(shape, lambda s, _n=len(shape): (0,) * _n)
    ssm_spec = pl.BlockSpec((None, blk_states, HEAD_DIM, HEAD_DIM), lambda s: (l, s, 0, 0))
    in_specs = [
        full((nseq, D_MODEL)), _layer_spec((nseq, 3 * D_MODEL), l),
        _layer_spec((nseq, POOL_BUF, D_A), l), _layer_spec((nseq, CONV_K - 1, D_QKV), l), ssm_spec,
        _layer_spec((1, D_MODEL), l), _layer_spec((1, D_MODEL), l),
        _layer_spec((D_MODEL, D_MAIN), l), _layer_spec((D_MODEL, LANES), l),
        _layer_spec((CONV_K, D_QKV), l), _layer_spec((N_POOL, POOL_GC, POOL_GC), l), _layer_spec((1, D_A), l),
        _layer_spec((1, LANES), l), _layer_spec((1, LANES), l), _layer_spec((1, HEAD_DIM), l),
        _layer_spec((D_MODEL, D_MODEL), l),
    ] + [pl.BlockSpec(memory_space=pl.ANY)] * len(prev)
    out_specs = [
        full((nseq, D_MODEL)), _layer_spec((nseq, POOL_BUF, D_A), l), _layer_spec((nseq, CONV_K - 1, D_QKV), l),
        ssm_spec,
    ]
    out_shape = [
        jax.ShapeDtypeStruct((nseq, D_MODEL), F32),
        jax.ShapeDtypeStruct((depth, nseq, POOL_BUF, D_A), F32),
        jax.ShapeDtypeStruct((depth, nseq, CONV_K - 1, D_QKV), F32),
        jax.ShapeDtypeStruct((depth, nseq * N_HEADS, HEAD_DIM, HEAD_DIM), F32),
    ]
    scratch = [
        pltpu.VMEM((nseq, D_B), F32), pltpu.VMEM((nseq, D_B), F32), pltpu.VMEM((nseq, D_B), F32),
        pltpu.VMEM((nseq, D_B), F32),
        pltpu.VMEM((N_HEADS, nseq, LANES), F32), pltpu.VMEM((N_HEADS, nseq, LANES), F32),
        pltpu.VMEM((N_HEADS, nseq, LANES), F32),
        pltpu.VMEM((nseq, D_A), F32), pltpu.VMEM((nseq, D_B), F32),
    ]
    n_in = len(in_specs) - len(prev)
    return pl.pallas_call(
        functools.partial(_sample_kernel, n_alias=len(prev)),
        grid=(nstep,),
        in_specs=in_specs,
        out_specs=out_specs,
        out_shape=out_shape,
        scratch_shapes=scratch,
        input_output_aliases={n_in + i: 1 + i for i in range(len(prev))},
        compiler_params=pltpu.CompilerParams(dimension_semantics=("arbitrary",), vmem_limit_bytes=VMEM_LIMIT),
        name="sample_layer",
    )(x, mod_s, state_pool, state_conv, state_ssm, lw["prew"], lw["postw"], lw["win"], lw["wba"], lw["convw"],
      lw["poolw"], lw["pscale"], lw["alogc"], lw["dtbc"], lw["onw"], lw["wout"], *prev)


def _constants():
    t = np.arange(CHUNK)[:, None]
    j = np.arange(2 * CHUNK)[None, :]
    band = np.stack([((j <= CHUNK + t) & (j > CHUNK + t - w)) for w in POOL_WINDOWS]).astype(np.float32)
    tri = (np.arange(CHUNK)[None, :] <= t).astype(np.float32)
    ltri = np.concatenate([tri, tri], axis=1)
    return {"band": jnp.asarray(band, BF16), "ltri": jnp.asarray(ltri, BF16)}


def _stacked_weights(rows_p, pre_norm_w, post_norm_w, w_in, conv_w, pool_w, pool_scale, a_log, dt_bias, o_norm_w,
                     w_out):
    depth = w_in.shape[0]
    wba = w_in[:, :, D_MAIN:D_MAIN + 2 * N_HEADS]
    pad_c = lambda v: jnp.zeros((depth, 1, LANES), F32).at[:, 0, N_HEADS:2 * N_HEADS].set(v)
    pad_r = lambda v: jnp.broadcast_to(
        jnp.zeros((depth, SUBLANES), F32).at[:, N_HEADS:2 * N_HEADS].set(v)[:, :, None], (depth, SUBLANES, rows_p))
    return {
        "prew": pre_norm_w.reshape(depth, 1, D_MODEL),
        "postw": post_norm_w.reshape(depth, 1, D_MODEL),
        "win": w_in[:, :, 0:D_MAIN].astype(BF16),
        "wba": jnp.zeros((depth, D_MODEL, LANES), F32).at[:, :, 0:2 * N_HEADS].set(wba).astype(BF16),
        "wbat": jnp.swapaxes(wba, 1, 2).astype(BF16),
        "convw": conv_w,
        "poolw": pool_w.astype(BF16),
        "pscale": pool_scale.reshape(depth, 1, D_A),
        "alogc": pad_c(a_log), "dtbc": pad_c(dt_bias),
        "alogr": pad_r(a_log), "dtbr": pad_r(dt_bias),
        "onw": o_norm_w.reshape(depth, 1, HEAD_DIM),
        "wout": w_out.astype(BF16),
    }


def kernel(x_prompt, x_sample, c_prompt, c_sample, state_pool, state_conv, state_ssm, w_ada, b_ada, pre_norm_w,
           post_norm_w, w_in, conv_w, pool_w, pool_scale, a_log, dt_bias, o_norm_w, w_out):
    depth = w_in.shape[0]
    nb, seq, _ = x_prompt.shape
    nseq, dec_seq, _ = x_sample.shape
    assert dec_seq == 1 and seq % CHUNK == 0 and nb % 2 == 0 and nseq % SAMPLE_BLOCK == 0
    consts = _constants()
    lw = _stacked_weights(nb * CHUNK, pre_norm_w, post_norm_w, w_in, conv_w, pool_w, pool_scale, a_log, dt_bias,
                          o_norm_w, w_out)
    mod_p, mod_s = _mod_call(c_prompt, c_sample, w_ada, b_ada)
    mod_p = mod_p.reshape(depth, nb, 1, 3 * D_MODEL)
    ssm_in = state_ssm.reshape(depth, nseq * N_HEADS, HEAD_DIM, HEAD_DIM)
    yp, ys = x_prompt, x_sample.reshape(nseq, D_MODEL)
    prev_p, prev_s = (), ()
    for l in range(depth):
        yp, *prev_p = _prompt_layer(l, yp, mod_p, lw, consts, tuple(prev_p))
        ys, *prev_s = _sample_layer(l, ys, mod_s, state_pool, state_conv, ssm_in, lw, tuple(prev_s))
    npool_p, nconv_p, nssm_p = prev_p
    npool_s, nconv_s, nssm_s = prev_s
    return (yp, ys.reshape(nseq, dec_seq, D_MODEL), npool_p, nconv_p,
            nssm_p.reshape(depth, nb, N_HEADS, HEAD_DIM, HEAD_DIM), npool_s, nconv_s,
            nssm_s.reshape(depth, nseq, N_HEADS, HEAD_DIM, HEAD_DIM))
```

```python
import functools

import jax
import jax.numpy as jnp
import numpy as np
from jax import lax
from jax.experimental import pallas as pl
from jax.experimental.pallas import tpu as pltpu

F32 = jnp.float32
BF16 = jnp.bfloat16

D_MODEL = 1024
D_A = 512
D_B = 512
N_POOL = 4
POOL_WINDOWS = (2, 4, 8, 16)
POOL_GC = 128
POOL_BUF = 15
HEAD_DIM = 128
N_HEADS = 4
D_QKV = 3 * D_B
CONV_K = 4
D_MAIN = 2 * D_A + D_QKV + D_B
PAST_LEN = 16384
EPS = 1e-6
CHUNK = 64
GDN_GROUP = 16
SAMPLE_BLOCK = 8
LANES = 128
SUBLANES = 8
QKV_TILES = D_QKV // LANES
VMEM_LIMIT = 56 * 1024 * 1024


def _dot(a, b):
    return jnp.dot(a.astype(BF16), b.astype(BF16), preferred_element_type=F32)


def _dot_nt(a, b):
    return lax.dot_general(a.astype(BF16), b.astype(BF16), (((1,), (1,)), ((), ())), preferred_element_type=F32)


def _dot_tn(a, b):
    return lax.dot_general(a, b, (((0,), (0,)), ((), ())), preferred_element_type=F32)


def _proj(hb, wt_ref, c0, c1):
    return lax.dot_general(hb, wt_ref[c0:c1, :], (((1,), (1,)), ((), ())), preferred_element_type=F32)


def _silu(x):
    return x * jax.nn.sigmoid(x)


def _softplus(x):
    return jnp.maximum(x, 0.0) + jnp.log1p(jnp.exp(-jnp.abs(x)))


def _block_diag(a, b):
    top = jnp.concatenate([a, jnp.zeros((a.shape[0], b.shape[1]), a.dtype)], axis=1)
    bot = jnp.concatenate([jnp.zeros((b.shape[0], a.shape[1]), b.dtype), b], axis=1)
    return jnp.concatenate([top, bot], axis=0)


def _head_rms(o_all, w):
    parts = []
    for hh in range(N_HEADS):
        oh = o_all[:, hh * HEAD_DIM:(hh + 1) * HEAD_DIM]
        parts.append(oh * lax.rsqrt(jnp.mean(oh * oh, axis=-1, keepdims=True) + EPS) * w)
    return jnp.concatenate(parts, axis=1)


def _mod_kernel(cp_ref, cs_ref, w_ref, b_ref, op_ref, os_ref):
    w = w_ref[...]
    b = b_ref[...]
    op_ref[...] = jnp.dot(_silu(cp_ref[...]), w, preferred_element_type=F32) + b
    os_ref[...] = jnp.dot(_silu(cs_ref[...]), w, preferred_element_type=F32) + b


def _mod_call(c_prompt, c_sample, w_ada, b_ada):
    depth, _, n3 = w_ada.shape
    nb, nseq = c_prompt.shape[0], c_sample.shape[0]
    nblk = n3 // D_MODEL
    return pl.pallas_call(
        _mod_kernel,
        grid=(depth, nblk),
        in_specs=[
            pl.BlockSpec((nb, D_MODEL), lambda l, j: (0, 0)),
            pl.BlockSpec((nseq, D_MODEL), lambda l, j: (0, 0)),
            pl.BlockSpec((None, D_MODEL, D_MODEL), lambda l, j: (l, 0, j)),
            pl.BlockSpec((None, 1, D_MODEL), lambda l, j: (l, 0, j)),
        ],
        out_specs=[pl.BlockSpec((None, nb, D_MODEL), lambda l, j: (l, 0, j)),
                   pl.BlockSpec((None, nseq, D_MODEL), lambda l, j: (l, 0, j))],
        out_shape=[jax.ShapeDtypeStruct((depth, nb, n3), F32), jax.ShapeDtypeStruct((depth, nseq, n3), F32)],
        compiler_params=pltpu.CompilerParams(dimension_semantics=("arbitrary", "arbitrary")),
        name="adaln_mod",
    )(c_prompt, c_sample, w_ada, b_ada.reshape(depth, 1, n3))


def _prompt_kernel(x_ref, mod_ref, prew_ref, postw_ref, win_ref, wba_ref, wbat_ref, convw_ref, poolw_ref,
                   pscale_ref, alogc_ref, dtbc_ref, alogr_ref, dtbr_ref, onw_ref, wout_ref, band_ref, ltri_ref,
                   *rest, n_alias):
    rest = rest[n_alias:]
    y_ref, npool_ref, nconv_ref, nssm_ref = rest[0:4]
    qkv_ext, ua_ext, s_ref, q_s, k_s, v_s, o_s, beta_s, gc_s, eg_s, gcrow_s = rest[4:]
    step = pl.program_id(0)
    nb = x_ref.shape[0]
    rows = nb * CHUNK

    @pl.when(step == 0)
    def _():
        qkv_ext[:, :, 0:SUBLANES, :] = jnp.zeros((QKV_TILES, nb, SUBLANES, LANES), F32)
        ua_ext[:, 0:CHUNK, :] = jnp.zeros((nb, CHUNK, D_A), F32)
        s_ref[...] = jnp.zeros(s_ref.shape, F32)

    x = x_ref[...]
    mod = mod_ref[...]
    shift = mod[:, :, 0:D_MODEL]
    scale = mod[:, :, D_MODEL:2 * D_MODEL]
    gate = mod[:, :, 2 * D_MODEL:3 * D_MODEL]
    a_mul = prew_ref[...][None] * (1.0 + scale)
    ms = jnp.mean(x * x, axis=-1, keepdims=True)
    h = x * lax.rsqrt(ms + EPS) * a_mul + shift
    hb = h.reshape(rows, D_MODEL).astype(BF16)

    ua = _proj(hb, win_ref, 0, D_A)
    ua3 = ua.reshape(nb, CHUNK, D_A)
    ua_ext[:, CHUNK:2 * CHUNK, :] = ua3
    pos = lax.broadcasted_iota(jnp.int32, (CHUNK, POOL_GC), 0) + step * CHUNK
    pooled_groups = []
    for gi, w in enumerate(POOL_WINDOWS):
        cnt = jnp.minimum(pos + 1, w).astype(F32)
        per_b = []
        for b in range(nb):
            ext = ua_ext[b, :, gi * POOL_GC:(gi + 1) * POOL_GC]
            win = jnp.dot(band_ref[gi], ext.astype(BF16), preferred_element_type=F32)
            per_b.append(win / cnt - ua3[b, :, gi * POOL_GC:(gi + 1) * POOL_GC])
        pooled_groups.append(jnp.concatenate(per_b, axis=0))
    ya = jnp.concatenate([_dot(pooled_groups[gi], poolw_ref[gi]) for gi in range(N_POOL)], axis=1)
    za = _proj(hb, win_ref, D_A, 2 * D_A)
    ya = ya * pscale_ref[...] * _silu(za)

    qkv = _proj(hb, win_ref, 2 * D_A, 2 * D_A + D_QKV)
    qkv_t = jnp.stack([qkv[:, c * LANES:(c + 1) * LANES] for c in range(QKV_TILES)], axis=0)
    qkv_t = qkv_t.reshape(QKV_TILES, nb, CHUNK, LANES)
    qkv_ext[:, :, SUBLANES:SUBLANES + CHUNK, :] = qkv_t
    cw = convw_ref[...]
    cw_t = [jnp.stack([cw[j:j + 1, c * LANES:(c + 1) * LANES] for c in range(QKV_TILES)], axis=0)[:, None]
            for j in range(CONV_K)]
    acc = qkv_t * cw_t[CONV_K - 1]
    for j in range(CONV_K - 1):
        acc = acc + qkv_ext[:, :, pl.ds(SUBLANES - (CONV_K - 1) + j, CHUNK), :] * cw_t[j]
    act = _silu(acc).reshape(QKV_TILES, rows, LANES)
    for hh in range(N_HEADS):
        sl = slice(hh * HEAD_DIM, (hh + 1) * HEAD_DIM)
        qh, kh = act[hh], act[N_HEADS + hh]
        q_s[:, sl] = qh * lax.rsqrt(jnp.sum(qh * qh, axis=-1, keepdims=True) + EPS) * (HEAD_DIM ** -0.5)
        k_s[:, sl] = kh * lax.rsqrt(jnp.sum(kh * kh, axis=-1, keepdims=True) + EPS)
        v_s[:, sl] = act[2 * N_HEADS + hh]

    bac = jnp.dot(hb, wba_ref[...], preferred_element_type=F32)
    bar = lax.dot_general(wbat_ref[...], hb, (((1,), (1,)), ((), ())), preferred_element_type=F32)
    sig = jax.nn.sigmoid(bac)
    gcol = -jnp.exp(alogc_ref[...]) * _softplus(bac + dtbc_ref[...])
    g_hi = gcol.astype(BF16)
    g_lo = (gcol - g_hi.astype(F32)).astype(BF16)
    gc_parts = []
    for b in range(nb):
        rs = slice(b * CHUNK, (b + 1) * CHUNK)
        gc_parts.append(jnp.dot(ltri_ref[...], jnp.concatenate([g_hi[rs], g_lo[rs]], axis=0),
                                preferred_element_type=F32))
    gc = jnp.concatenate(gc_parts, axis=0)
    for hh in range(N_HEADS):
        beta_s[hh] = jnp.broadcast_to(sig[:, hh:hh + 1], (rows, LANES))
        gcb = jnp.broadcast_to(gc[:, N_HEADS + hh:N_HEADS + hh + 1], (rows, LANES))
        gc_s[hh] = gcb
        eg_s[hh] = jnp.exp(gcb)
    grow = -jnp.exp(alogr_ref[...]) * _softplus(bar + dtbr_ref[...])
    lane_in_chunk = lax.broadcasted_iota(jnp.int32, grow.shape, 1) % CHUNK
    sh = 1
    while sh < CHUNK:
        grow = grow + jnp.where(lane_in_chunk >= sh, pltpu.roll(grow, sh, 1), 0.0)
        sh *= 2
    gcrow_s[...] = grow

    ri = lax.broadcasted_iota(jnp.int32, (CHUNK, LANES), 0)
    li = lax.broadcasted_iota(jnp.int32, (CHUNK, LANES), 1)
    lj = li % CHUNK
    lo_half = li < CHUNK
    strict = ri > lj
    causal = ri >= lj
    eye2 = (ri == lj).astype(F32)

    def pair_mul(xp, yp):
        ybd = jnp.concatenate([jnp.where(lo_half, yp, 0.0), jnp.where(lo_half, 0.0, yp)], axis=0)
        return _dot(xp, ybd)

    chains = [(p, hh) for p in range(nb // 2) for hh in range(N_HEADS)]
    for g0 in range(0, len(chains), GDN_GROUP):
        st = []
        for p, hh in chains[g0:g0 + GDN_GROUP]:
            ra = slice(2 * p * CHUNK, (2 * p + 1) * CHUNK)
            rb = slice((2 * p + 1) * CHUNK, (2 * p + 2) * CHUNK)
            sl = slice(hh * HEAD_DIM, (hh + 1) * HEAD_DIM)
            ka, kb_ = k_s[ra, sl], k_s[rb, sl]
            kba, kbb = ka * beta_s[hh, ra, :], kb_ * beta_s[hh, rb, :]
            lhs1 = jnp.concatenate([jnp.concatenate([kba, kbb], axis=1),
                                    jnp.concatenate([q_s[ra, sl], q_s[rb, sl]], axis=1)], axis=0)
            kkqk = _dot_nt(lhs1, _block_diag(ka, kb_))
            st.append(dict(ra=ra, rb=rb, sl=sl, hh=hh, p=p, kkqk=kkqk))
        for c in st:
            hh, p = c["hh"], c["p"]
            gcol_p = jnp.where(lo_half, gc_s[hh, c["ra"], :], gc_s[hh, c["rb"], :])
            grow_p = jnp.broadcast_to(gcrow_s[N_HEADS + hh:N_HEADS + hh + 1, 2 * p * CHUNK:(2 * p + 2) * CHUNK],
                                      (CHUNK, LANES))
            dec = jnp.exp(jnp.minimum(gcol_p - grow_p, 0.0))
            kkqk = c.pop("kkqk")
            c["npow"] = jnp.where(strict, kkqk[0:CHUNK] * dec, 0.0)
            c["qkm"] = jnp.where(causal, kkqk[CHUNK:2 * CHUNK] * dec, 0.0)
            c["t"] = eye2 - c["npow"]
        for _ in range(5):
            for c in st:
                c["npow"] = pair_mul(c["npow"], c["npow"])
            for c in st:
                c["t"] = c["t"] + pair_mul(c["t"], c["npow"])
        for c in st:
            ra, rb, sl, hh = c["ra"], c["rb"], c["sl"], c["hh"]
            ba, bb = beta_s[hh, ra, :], beta_s[hh, rb, :]
            kba, kbb = k_s[ra, sl] * ba, k_s[rb, sl] * bb
            rhs = jnp.concatenate([_block_diag(v_s[ra, sl] * ba, v_s[rb, sl] * bb),
                                   _block_diag(kba * eg_s[hh, ra, :], kbb * eg_s[hh, rb, :])], axis=1)
            c["uw"] = _dot(c.pop("t"), rhs)
        for c in st:
            ra, rb, sl, hh, p = c["ra"], c["rb"], c["sl"], c["hh"], c["p"]
            c["ia"], c["ib"] = 2 * p * N_HEADS + hh, (2 * p + 1) * N_HEADS + hh
            lhs3 = jnp.concatenate([c["uw"][:, 2 * HEAD_DIM:4 * HEAD_DIM],
                                    jnp.concatenate([q_s[ra, sl] * eg_s[hh, ra, :], q_s[rb, sl] * eg_s[hh, rb, :]],
                                                    axis=1)], axis=0)
            c["r3"] = _dot(lhs3, _block_diag(s_ref[c["ia"]], s_ref[c["ib"]]))
        for c in st:
            ra, rb, sl, hh, p = c["ra"], c["rb"], c["sl"], c["hh"], c["p"]
            r3 = c.pop("r3")
            vn = c.pop("uw")[:, 0:2 * HEAD_DIM] - r3[0:CHUNK]
            o = r3[CHUNK:2 * CHUNK] + _dot(c.pop("qkm"), _block_diag(vn[:, 0:HEAD_DIM], vn[:, HEAD_DIM:2 * HEAD_DIM]))
            o_s[ra, sl] = o[:, 0:HEAD_DIM]
            o_s[rb, sl] = o[:, HEAD_DIM:2 * HEAD_DIM]
            gla = gc_s[hh, (2 * p + 1) * CHUNK - 1:(2 * p + 1) * CHUNK, :]
            glb = gc_s[hh, (2 * p + 2) * CHUNK - 1:(2 * p + 2) * CHUNK, :]
            kg = jnp.concatenate([k_s[ra, sl] * jnp.exp(gla - gc_s[hh, ra, :]),
                                  k_s[rb, sl] * jnp.exp(glb - gc_s[hh, rb, :])], axis=1)
            upd = _dot_tn(kg, vn)
            s_ref[c["ia"]] = s_ref[c["ia"]] * jnp.exp(gla) + upd[0:HEAD_DIM, 0:HEAD_DIM]
            s_ref[c["ib"]] = s_ref[c["ib"]] * jnp.exp(glb) + upd[HEAD_DIM:2 * HEAD_DIM, HEAD_DIM:2 * HEAD_DIM]

    zb = _proj(hb, win_ref, 2 * D_A + D_QKV, D_MAIN)
    yb = _head_rms(o_s[...], onw_ref[...]) * _silu(zb)
    ymix = jnp.concatenate([ya, yb], axis=1)
    yo = jnp.dot(ymix.astype(BF16), wout_ref[...], preferred_element_type=F32)
    yn = yo * lax.rsqrt(jnp.mean(yo * yo, axis=-1, keepdims=True) + EPS) * postw_ref[...]
    y_ref[...] = x + gate * yn.reshape(nb, CHUNK, D_MODEL)

    qkv_ext[:, :, 0:SUBLANES, :] = qkv_ext[:, :, CHUNK:CHUNK + SUBLANES, :]
    ua_ext[:, 0:CHUNK, :] = ua3

    @pl.when(step == pl.num_programs(0) - 1)
    def _():
        npool_ref[...] = ua3[:, CHUNK - POOL_BUF:CHUNK, :]
        nconv_ref[...] = qkv.reshape(nb, CHUNK, D_QKV)[:, CHUNK - (CONV_K - 1):CHUNK, :]
        nssm_ref[...] = s_ref[...]


def _layer_spec(shape, l):
    return pl.BlockSpec((None,) + tuple(shape), lambda s, _n=len(shape): (l,) + (0,) * _n)


def _prompt_layer(l, x, mod_p, lw, consts, prev):
    nb, seq, _ = x.shape
    depth = lw["win"].shape[0]
    rows = nb * CHUNK
    nstep = seq // CHUNK
    full = lambda shape: pl.BlockSpec(shape, lambda s, _n=len(shape): (0,) * _n)
    in_specs = [
        pl.BlockSpec((nb, CHUNK, D_MODEL), lambda s: (0, s, 0)),
        _layer_spec((nb, 1, 3 * D_MODEL), l),
        _layer_spec((1, D_MODEL), l), _layer_spec((1, D_MODEL), l),
        _layer_spec((D_MAIN, D_MODEL), l), _layer_spec((D_MODEL, LANES), l), _layer_spec((SUBLANES, D_MODEL), l),
        _layer_spec((CONV_K, D_QKV), l), _layer_spec((N_POOL, POOL_GC, POOL_GC), l), _layer_spec((1, D_A), l),
        _layer_spec((1, LANES), l), _layer_spec((1, LANES), l),
        _layer_spec((SUBLANES, rows), l), _layer_spec((SUBLANES, rows), l),
        _layer_spec((1, HEAD_DIM), l), _layer_spec((D_MODEL, D_MODEL), l),
        full((N_POOL, CHUNK, 2 * CHUNK)), full((CHUNK, 2 * CHUNK)),
    ] + [pl.BlockSpec(memory_space=pl.ANY)] * len(prev)
    out_specs = [
        pl.BlockSpec((nb, CHUNK, D_MODEL), lambda s: (0, s, 0)),
        _layer_spec((nb, POOL_BUF, D_A), l), _layer_spec((nb, CONV_K - 1, D_QKV), l),
        _layer_spec((nb * N_HEADS, HEAD_DIM, HEAD_DIM), l),
    ]
    out_shape = [
        jax.ShapeDtypeStruct((nb, seq, D_MODEL), F32),
        jax.ShapeDtypeStruct((depth, nb, POOL_BUF, D_A), F32),
        jax.ShapeDtypeStruct((depth, nb, CONV_K - 1, D_QKV), F32),
        jax.ShapeDtypeStruct((depth, nb * N_HEADS, HEAD_DIM, HEAD_DIM), F32),
    ]
    scratch = [
        pltpu.VMEM((QKV_TILES, nb, CHUNK + SUBLANES, LANES), F32),
        pltpu.VMEM((nb, 2 * CHUNK, D_A), F32),
        pltpu.VMEM((nb * N_HEADS, HEAD_DIM, HEAD_DIM), F32),
        pltpu.VMEM((rows, D_B), F32), pltpu.VMEM((rows, D_B), F32), pltpu.VMEM((rows, D_B), F32),
        pltpu.VMEM((rows, D_B), F32),
        pltpu.VMEM((N_HEADS, rows, LANES), F32), pltpu.VMEM((N_HEADS, rows, LANES), F32),
        pltpu.VMEM((N_HEADS, rows, LANES), F32),
        pltpu.VMEM((SUBLANES, rows), F32),
    ]
    n_in = len(in_specs) - len(prev)
    return pl.pallas_call(
        functools.partial(_prompt_kernel, n_alias=len(prev)),
        grid=(nstep,),
        in_specs=in_specs,
        out_specs=out_specs,
        out_shape=out_shape,
        scratch_shapes=scratch,
        input_output_aliases={n_in + i: 1 + i for i in range(len(prev))},
        compiler_params=pltpu.CompilerParams(dimension_semantics=("arbitrary",), vmem_limit_bytes=VMEM_LIMIT),
        name="prompt_layer",
    )(x, mod_p, lw["prew"], lw["postw"], lw["win"], lw["wba"], lw["wbat"], lw["convw"], lw["poolw"], lw["pscale"],
      lw["alogc"], lw["dtbc"], lw["alogr"], lw["dtbr"], lw["onw"], lw["wout"], consts["band"], consts["ltri"], *prev)


def _sample_kernel(x_ref, mod_ref, spool_ref, sconv_ref, ssm_ref, prew_ref, postw_ref, win_ref, wba_ref, convw_ref,
                   poolw_ref, pscale_ref, alogc_ref, dtbc_ref, onw_ref, wout_ref, *rest, n_alias):
    rest = rest[n_alias:]
    y_ref, npool_ref, nconv_ref, nssm_ref = rest[0:4]
    q_s, k_s, v_s, o_s, beta_s, eg_s, qk_s, ya_s, zb_s = rest[4:]
    step = pl.program_id(0)
    nseq = x_ref.shape[0]

    @pl.when(step == 0)
    def _():
        x = x_ref[...]
        mod = mod_ref[...]
        shift = mod[:, 0:D_MODEL]
        scale = mod[:, D_MODEL:2 * D_MODEL]
        a_mul = prew_ref[...] * (1.0 + scale)
        ms = jnp.mean(x * x, axis=-1, keepdims=True)
        hb = (x * lax.rsqrt(ms + EPS) * a_mul + shift).astype(BF16)

        ua = _proj(hb, win_ref, 0, D_A)
        ya_parts = []
        for gi, w in enumerate(POOL_WINDOWS):
            gs = slice(gi * POOL_GC, (gi + 1) * POOL_GC)
            win = ua[:, gs]
            for d in range(1, w):
                win = win + spool_ref[POOL_BUF - d, :, gs]
            cnt = float(min(PAST_LEN + 1, w))
            pooled = win / cnt - ua[:, gs]
            ya_parts.append(_dot(pooled, poolw_ref[gi]))
        za = _proj(hb, win_ref, D_A, 2 * D_A)
        ya_s[...] = jnp.concatenate(ya_parts, axis=1) * pscale_ref[...] * _silu(za)
        npool_ref[0:POOL_BUF - 1] = spool_ref[1:POOL_BUF]
        npool_ref[POOL_BUF - 1] = ua

        qkv = _proj(hb, win_ref, 2 * D_A, 2 * D_A + D_QKV)
        cw = convw_ref[...]
        acc = qkv * cw[CONV_K - 1:CONV_K]
        for j in range(CONV_K - 1):
            acc = acc + sconv_ref[j] * cw[j:j + 1]
        nconv_ref[0:CONV_K - 2] = sconv_ref[1:CONV_K - 1]
        nconv_ref[CONV_K - 2] = qkv
        qkvc = _silu(acc)
        bac = jnp.dot(hb, wba_ref[...], preferred_element_type=F32)
        sig = jax.nn.sigmoid(bac)
        eg = jnp.exp(-jnp.exp(alogc_ref[...]) * _softplus(bac + dtbc_ref[...]))
        for hh in range(N_HEADS):
            sl = slice(hh * HEAD_DIM, (hh + 1) * HEAD_DIM)
            qh = qkvc[:, hh * HEAD_DIM:(hh + 1) * HEAD_DIM]
            kh = qkvc[:, D_B + hh * HEAD_DIM:D_B + (hh + 1) * HEAD_DIM]
            qn = qh * lax.rsqrt(jnp.sum(qh * qh, axis=-1, keepdims=True) + EPS) * (HEAD_DIM ** -0.5)
            kn = kh * lax.rsqrt(jnp.sum(kh * kh, axis=-1, keepdims=True) + EPS)
            q_s[:, sl] = qn
            k_s[:, sl] = kn
            qk_s[hh] = jnp.broadcast_to(jnp.sum(qn * kn, axis=-1, keepdims=True), (nseq, LANES))
            beta_s[hh] = jnp.broadcast_to(sig[:, hh:hh + 1], (nseq, LANES))
            eg_s[hh] = jnp.broadcast_to(eg[:, N_HEADS + hh:N_HEADS + hh + 1], (nseq, LANES))
        v_s[...] = qkvc[:, 2 * D_B:3 * D_B]
        zb_s[...] = _proj(hb, win_ref, 2 * D_A + D_QKV, D_MAIN)

    r0 = pl.multiple_of(step * SAMPLE_BLOCK, SAMPLE_BLOCK)
    kblk = k_s[pl.ds(r0, SAMPLE_BLOCK), :]
    qblk = q_s[pl.ds(r0, SAMPLE_BLOCK), :]
    vblk = v_s[pl.ds(r0, SAMPLE_BLOCK), :]
    row8 = lax.broadcasted_iota(jnp.int32, (SUBLANES, HEAD_DIM), 0)
    st = []
    for i in range(SAMPLE_BLOCK):
        for hh in range(N_HEADS):
            sl = slice(hh * HEAD_DIM, (hh + 1) * HEAD_DIM)
            krow, qrow = kblk[i:i + 1, sl], qblk[i:i + 1, sl]
            lhs = jnp.where(row8 == 0, krow, jnp.where(row8 == 1, qrow, 0.0))
            r = jnp.dot(lhs, ssm_ref[i * N_HEADS + hh], preferred_element_type=F32)
            st.append(dict(i=i, hh=hh, sl=sl, krow=krow, r=r))
    for c in st:
        i, hh = c["i"], c["hh"]
        eg = eg_s[hh, pl.ds(r0 + i, 1), :]
        r = c.pop("r")
        delta = (vblk[i:i + 1, c["sl"]] - eg * r[0:1]) * beta_s[hh, pl.ds(r0 + i, 1), :]
        c["o"] = eg * r[1:2] + qk_s[hh, pl.ds(r0 + i, 1), :] * delta
        krow = c.pop("krow")
        k_hi = krow.astype(BF16).astype(F32)
        d_hi = delta.astype(BF16).astype(F32)
        kp = jnp.where(row8 == 0, k_hi, jnp.where(row8 == 1, krow - k_hi, jnp.where(row8 == 2, k_hi, 0.0)))
        dp = jnp.where(row8 == 0, d_hi, jnp.where(row8 == 1, d_hi, jnp.where(row8 == 2, delta - d_hi, 0.0)))
        c["upd"] = _dot_tn(kp, dp)
    for c in st:
        idx = c["i"] * N_HEADS + c["hh"]
        nssm_ref[idx] = ssm_ref[idx] * eg_s[c["hh"], pl.ds(r0 + c["i"], 1), :] + c.pop("upd")
    o_rows = [jnp.concatenate([c["o"] for c in st[i * N_HEADS:(i + 1) * N_HEADS]], axis=1)
              for i in range(SAMPLE_BLOCK)]
    o_s[pl.ds(r0, SAMPLE_BLOCK), :] = jnp.concatenate(o_rows, axis=0)

    @pl.when(step == pl.num_programs(0) - 1)
    def _():
        x = x_ref[...]
        gate = mod_ref[:, 2 * D_MODEL:3 * D_MODEL]
        yb = _head_rms(o_s[...], onw_ref[...]) * _silu(zb_s[...])
        ymix = jnp.concatenate([ya_s[...], yb], axis=1)
        yo = jnp.dot(ymix.astype(BF16), wout_ref[...], preferred_element_type=F32)
        yn = yo * lax.rsqrt(jnp.mean(yo * yo, axis=-1, keepdims=True) + EPS) * postw_ref[...]
        y_ref[...] = x + gate * yn


def _sample_layer(l, x, mod_s, state_pool, state_conv, state_ssm, lw, prev):
    nseq = x.shape[0]
    depth = lw["win"].shape[0]
    nstep = nseq // SAMPLE_BLOCK
    blk_states = SAMPLE_BLOCK * N_HEADS
    full = lambda shape: pl.BlockSpec(shape, lambda s, _n=len(shape): (0,) * _n)
    ssm_spec = pl.BlockSpec((None, blk_states, HEAD_DIM, HEAD_DIM), lambda s: (l, s, 0, 0))
    in_specs = [
        full((nseq, D_MODEL)), _layer_spec((nseq, 3 * D_MODEL), l),
        _layer_spec((POOL_BUF, nseq, D_A), l), _layer_spec((CONV_K - 1, nseq, D_QKV), l), ssm_spec,
        _layer_spec((1, D_MODEL), l), _layer_spec((1, D_MODEL), l),
        _layer_spec((D_MAIN, D_MODEL), l), _layer_spec((D_MODEL, LANES), l),
        _layer_spec((CONV_K, D_QKV), l), _layer_spec((N_POOL, POOL_GC, POOL_GC), l), _layer_spec((1, D_A), l),
        _layer_spec((1, LANES), l), _layer_spec((1, LANES), l), _layer_spec((1, HEAD_DIM), l),
        _layer_spec((D_MODEL, D_MODEL), l),
    ] + [pl.BlockSpec(memory_space=pl.ANY)] * len(prev)
    out_specs = [
        full((nseq, D_MODEL)), _layer_spec((POOL_BUF, nseq, D_A), l), _layer_spec((CONV_K - 1, nseq, D_QKV), l),
        ssm_spec,
    ]
    out_shape = [
        jax.ShapeDtypeStruct((nseq, D_MODEL), F32),
        jax.ShapeDtypeStruct((depth, POOL_BUF, nseq, D_A), F32),
        jax.ShapeDtypeStruct((depth, CONV_K - 1, nseq, D_QKV), F32),
        jax.ShapeDtypeStruct((depth, nseq * N_HEADS, HEAD_DIM, HEAD_DIM), F32),
    ]
    scratch = [
        pltpu.VMEM((nseq, D_B), F32), pltpu.VMEM((nseq, D_B), F32), pltpu.VMEM((nseq, D_B), F32),
        pltpu.VMEM((nseq, D_B), F32),
        pltpu.VMEM((N_HEADS, nseq, LANES), F32), pltpu.VMEM((N_HEADS, nseq, LANES), F32),
        pltpu.VMEM((N_HEADS, nseq, LANES), F32),
        pltpu.VMEM((nseq, D_A), F32), pltpu.VMEM((nseq, D_B), F32),
    ]
    n_in = len(in_specs) - len(prev)
    return pl.pallas_call(
        functools.partial(_sample_kernel, n_alias=len(prev)),
        grid=(nstep,),
        in_specs=in_specs,
        out_specs=out_specs,
        out_shape=out_shape,
        scratch_shapes=scratch,
        input_output_aliases={n_in + i: 1 + i for i in range(len(prev))},
        compiler_params=pltpu.CompilerParams(dimension_semantics=("arbitrary",), vmem_limit_bytes=VMEM_LIMIT),
        name="sample_layer",
    )(x, mod_s, state_pool, state_conv, state_ssm, lw["prew"], lw["postw"], lw["win"], lw["wba"], lw["convw"],
      lw["poolw"], lw["pscale"], lw["alogc"], lw["dtbc"], lw["onw"], lw["wout"], *prev)


def _constants():
    t = np.arange(CHUNK)[:, None]
    j = np.arange(2 * CHUNK)[None, :]
    band = np.stack([((j <= CHUNK + t) & (j > CHUNK + t - w)) for w in POOL_WINDOWS]).astype(np.float32)
    tri = (np.arange(CHUNK)[None, :] <= t).astype(np.float32)
    ltri = np.concatenate([tri, tri], axis=1)
    return {"band": jnp.asarray(band, BF16), "ltri": jnp.asarray(ltri, BF16)}


def _stacked_weights(rows_p, pre_norm_w, post_norm_w, w_in, conv_w, pool_w, pool_scale, a_log, dt_bias, o_norm_w,
                     w_out):
    depth = w_in.shape[0]
    w_in_t = jnp.swapaxes(w_in, 1, 2)
    wbat = w_in_t[:, D_MAIN:D_MAIN + 2 * N_HEADS]
    pad_c = lambda v: jnp.zeros((depth, 1, LANES), F32).at[:, 0, N_HEADS:2 * N_HEADS].set(v)
    pad_r = lambda v: jnp.broadcast_to(
        jnp.zeros((depth, SUBLANES), F32).at[:, N_HEADS:2 * N_HEADS].set(v)[:, :, None], (depth, SUBLANES, rows_p))
    return {
        "prew": pre_norm_w.reshape(depth, 1, D_MODEL),
        "postw": post_norm_w.reshape(depth, 1, D_MODEL),
        "win": w_in_t[:, 0:D_MAIN].astype(BF16),
        "wba": jnp.zeros((depth, D_MODEL, LANES), F32).at[:, :, 0:2 * N_HEADS].set(
            jnp.swapaxes(wbat, 1, 2)).astype(BF16),
        "wbat": wbat.astype(BF16),
        "convw": conv_w,
        "poolw": pool_w.astype(BF16),
        "pscale": pool_scale.reshape(depth, 1, D_A),
        "alogc": pad_c(a_log), "dtbc": pad_c(dt_bias),
        "alogr": pad_r(a_log), "dtbr": pad_r(dt_bias),
        "onw": o_norm_w.reshape(depth, 1, HEAD_DIM),
        "wout": w_out.astype(BF16),
    }


def kernel(x_prompt, x_sample, c_prompt, c_sample, state_pool, state_conv, state_ssm, w_ada, b_ada, pre_norm_w,
           post_norm_w, w_in, conv_w, pool_w, pool_scale, a_log, dt_bias, o_norm_w, w_out):
    depth = w_in.shape[0]
    nb, seq, _ = x_prompt.shape
    nseq, dec_seq, _ = x_sample.shape
    assert dec_seq == 1 and seq % CHUNK == 0 and nb % 2 == 0 and nseq % SAMPLE_BLOCK == 0
    consts = _constants()
    lw = _stacked_weights(nb * CHUNK, pre_norm_w, post_norm_w, w_in, conv_w, pool_w, pool_scale, a_log, dt_bias,
                          o_norm_w, w_out)
    mod_p, mod_s = _mod_call(c_prompt, c_sample, w_ada, b_ada)
    mod_p = mod_p.reshape(depth, nb, 1, 3 * D_MODEL)
    ssm_in = state_ssm.reshape(depth, nseq * N_HEADS, HEAD_DIM, HEAD_DIM)
    pool_in = jnp.swapaxes(state_pool, 1, 2)
    conv_in = jnp.swapaxes(state_conv, 1, 2)
    yp, ys = x_prompt, x_sample.reshape(nseq, D_MODEL)
    prev_p, prev_s = (), ()
    for l in range(depth):
        yp, *prev_p = _prompt_layer(l, yp, mod_p, lw, consts, tuple(prev_p))
        ys, *prev_s = _sample_layer(l, ys, mod_s, pool_in, conv_in, ssm_in, lw, tuple(prev_s))
    npool_p, nconv_p, nssm_p = prev_p
    npool_s, nconv_s, nssm_s = prev_s
    return (yp, ys.reshape(nseq, dec_seq, D_MODEL), npool_p, nconv_p,
            nssm_p.reshape(depth, nb, N_HEADS, HEAD_DIM, HEAD_DIM),
            jnp.swapaxes(npool_s, 1, 2), jnp.swapaxes(nconv_s, 1, 2),
            nssm_s.reshape(depth, nseq, N_HEADS, HEAD_DIM, HEAD_DIM))
```

```python
import functools

import jax
import jax.numpy as jnp
import numpy as np
from jax import lax
from jax.experimental import pallas as pl
from jax.experimental.pallas import tpu as pltpu

F32 = jnp.float32
BF16 = jnp.bfloat16

D_MODEL = 1024
D_A = 512
D_B = 512
N_POOL = 4
POOL_WINDOWS = (2, 4, 8, 16)
POOL_GC = 128
POOL_BUF = 15
HEAD_DIM = 128
N_HEADS = 4
D_QKV = 3 * D_B
CONV_K = 4
D_MAIN = 2 * D_A + D_QKV + D_B
PAST_LEN = 16384
EPS = 1e-6
CHUNK = 64
SAMPLE_BLOCK = 8
LANES = 128
SUBLANES = 8
QKV_TILES = D_QKV // LANES
OUT_TILE = 256
VMEM_LIMIT = 60 * 1024 * 1024


def _dot(a, b):
    return jnp.dot(a.astype(BF16), b.astype(BF16), preferred_element_type=F32)


def _dot_nt(a, b):
    return lax.dot_general(a.astype(BF16), b.astype(BF16), (((1,), (1,)), ((), ())), preferred_element_type=F32)


def _dot_tn(a, b):
    return lax.dot_general(a, b, (((0,), (0,)), ((), ())), preferred_element_type=F32)


def _proj(hb, wt_ref, c0, c1):
    return lax.dot_general(hb, wt_ref[c0:c1, :], (((1,), (1,)), ((), ())), preferred_element_type=F32)


def _silu(x):
    return x * jax.nn.sigmoid(x)


def _softplus(x):
    return jnp.maximum(x, 0.0) + jnp.log1p(jnp.exp(-jnp.abs(x)))


def _block_diag(a, b):
    top = jnp.concatenate([a, jnp.zeros((a.shape[0], b.shape[1]), a.dtype)], axis=1)
    bot = jnp.concatenate([jnp.zeros((b.shape[0], a.shape[1]), b.dtype), b], axis=1)
    return jnp.concatenate([top, bot], axis=0)


def _head_rms(o_all, w):
    parts = []
    for hh in range(N_HEADS):
        oh = o_all[:, hh * HEAD_DIM:(hh + 1) * HEAD_DIM]
        parts.append(oh * lax.rsqrt(jnp.mean(oh * oh, axis=-1, keepdims=True) + EPS) * w)
    return jnp.concatenate(parts, axis=1)


def _mod_kernel(cp_ref, cs_ref, w_ref, b_ref, op_ref, os_ref):
    w = w_ref[...]
    b = b_ref[...]
    op_ref[...] = jnp.dot(_silu(cp_ref[...]), w, preferred_element_type=F32) + b
    os_ref[...] = jnp.dot(_silu(cs_ref[...]), w, preferred_element_type=F32) + b


def _mod_call(c_prompt, c_sample, w_ada, b_ada):
    depth, _, n3 = w_ada.shape
    nb, nseq = c_prompt.shape[0], c_sample.shape[0]
    nblk = n3 // D_MODEL
    return pl.pallas_call(
        _mod_kernel,
        grid=(depth, nblk),
        in_specs=[
            pl.BlockSpec((nb, D_MODEL), lambda l, j: (0, 0)),
            pl.BlockSpec((nseq, D_MODEL), lambda l, j: (0, 0)),
            pl.BlockSpec((None, D_MODEL, D_MODEL), lambda l, j: (l, 0, j)),
            pl.BlockSpec((None, 1, D_MODEL), lambda l, j: (l, 0, j)),
        ],
        out_specs=[pl.BlockSpec((None, nb, D_MODEL), lambda l, j: (l, 0, j)),
                   pl.BlockSpec((None, nseq, D_MODEL), lambda l, j: (l, 0, j))],
        out_shape=[jax.ShapeDtypeStruct((depth, nb, n3), F32), jax.ShapeDtypeStruct((depth, nseq, n3), F32)],
        compiler_params=pltpu.CompilerParams(dimension_semantics=("arbitrary", "arbitrary")),
        name="adaln_mod",
    )(c_prompt, c_sample, w_ada, b_ada.reshape(depth, 1, n3))


def _prompt_kernel(x_ref, xprev_ref, mod_ref, prew_ref, postw_ref, win_ref, wba_ref, wbat_ref, convw_ref, poolw_ref,
                   pscale_ref, alogc_ref, dtbc_ref, alogr_ref, dtbr_ref, onw_ref, wout_ref, band_ref, ltri_ref,
                   *rest, n_alias):
    rest = rest[n_alias:]
    y_ref, npool_ref, nconv_ref, nssm_ref = rest[0:4]
    qkv_ext, ua_ext, s_ref, q_s, k_s, v_s, beta_s, gc_s, gcrow_s, o_s, ya_s, zb_s = rest[4:]
    step = pl.program_id(0)
    nchunk = pl.num_programs(0) - 2
    cur = step % 2
    prv = 1 - cur
    nb = x_ref.shape[0]
    rows = nb * CHUNK

    @pl.when(step == 0)
    def _():
        qkv_ext[:, :, 0:SUBLANES, :] = jnp.zeros((QKV_TILES, nb, SUBLANES, LANES), F32)
        ua_ext[:, 0:CHUNK, :] = jnp.zeros((nb, CHUNK, D_A), F32)
        for ref in (s_ref, q_s, k_s, v_s, beta_s, gc_s, gcrow_s, o_s, ya_s, zb_s):
            ref[...] = jnp.zeros(ref.shape, F32)

    mod = mod_ref[...]
    shift = mod[:, :, 0:D_MODEL]
    scale = mod[:, :, D_MODEL:2 * D_MODEL]
    gate = mod[:, :, 2 * D_MODEL:3 * D_MODEL]

    yb_prev = _head_rms(o_s[...], onw_ref[...]) * _silu(zb_s[cur])
    ymix_prev = jnp.concatenate([ya_s[cur], yb_prev], axis=1).astype(BF16)

    x = x_ref[...]
    a_mul = prew_ref[...][None] * (1.0 + scale)
    ms = jnp.mean(x * x, axis=-1, keepdims=True)
    hb = (x * lax.rsqrt(ms + EPS) * a_mul + shift).reshape(rows, D_MODEL).astype(BF16)
    keep = {}

    def front_proj(t0):
        def run():
            c0 = 2 * D_A + t0 * LANES
            part = _proj(hb, win_ref, c0, c0 + N_HEADS * LANES)
            for i in range(N_HEADS):
                qkv_ext[t0 + i, :, SUBLANES:SUBLANES + CHUNK, :] = (
                    part[:, i * LANES:(i + 1) * LANES].reshape(nb, CHUNK, LANES))
        return run

    def front_conv(dst_ref, t0, normalise, post_scale):
        def run():
            cw = convw_ref[...]
            for i in range(N_HEADS):
                tl = t0 + i
                acc = None
                for j in range(CONV_K):
                    term = (qkv_ext[tl, :, pl.ds(SUBLANES - (CONV_K - 1) + j, CHUNK), :]
                            * cw[j:j + 1, tl * LANES:(tl + 1) * LANES][None])
                    acc = term if acc is None else acc + term
                a = _silu(acc).reshape(rows, LANES)
                if normalise:
                    a = a * lax.rsqrt(jnp.sum(a * a, axis=-1, keepdims=True) + EPS)
                    if post_scale != 1.0:
                        a = a * post_scale
                dst_ref[cur, :, i * LANES:(i + 1) * LANES] = a
        return run

    def front_gates():
        bac = jnp.dot(hb, wba_ref[...], preferred_element_type=F32)
        bar = lax.dot_general(wbat_ref[...], hb, (((1,), (1,)), ((), ())), preferred_element_type=F32)
        sig = jax.nn.sigmoid(bac)
        gcol = -jnp.exp(alogc_ref[...]) * _softplus(bac + dtbc_ref[...])
        g_hi = gcol.astype(BF16)
        g_lo = (gcol - g_hi.astype(F32)).astype(BF16)
        gc_parts = []
        for b in range(nb):
            rs = slice(b * CHUNK, (b + 1) * CHUNK)
            gc_parts.append(jnp.dot(ltri_ref[...], jnp.concatenate([g_hi[rs], g_lo[rs]], axis=0),
                                    preferred_element_type=F32))
        gc = jnp.concatenate(gc_parts, axis=0)
        for hh in range(N_HEADS):
            beta_s[cur, hh] = jnp.broadcast_to(sig[:, hh:hh + 1], (rows, LANES))
            gc_s[cur, hh] = jnp.broadcast_to(gc[:, N_HEADS + hh:N_HEADS + hh + 1], (rows, LANES))
        grow = -jnp.exp(alogr_ref[...]) * _softplus(bar + dtbr_ref[...])
        lane_in_chunk = lax.broadcasted_iota(jnp.int32, grow.shape, 1) % CHUNK
        sh = 1
        while sh < CHUNK:
            grow = grow + jnp.where(lane_in_chunk >= sh, pltpu.roll(grow, sh, 1), 0.0)
            sh *= 2
        gcrow_s[cur] = grow

    def front_ua():
        ua_ext[:, CHUNK:2 * CHUNK, :] = _proj(hb, win_ref, 0, D_A).reshape(nb, CHUNK, D_A)

    def front_za():
        ya_s[cur] = _proj(hb, win_ref, D_A, 2 * D_A)

    def front_pool():
        pos = lax.broadcasted_iota(jnp.int32, (CHUNK, POOL_GC), 0) + jnp.minimum(step, nchunk - 1) * CHUNK
        for gi, w in enumerate(POOL_WINDOWS):
            gs = slice(gi * POOL_GC, (gi + 1) * POOL_GC)
            cnt = jnp.minimum(pos + 1, w).astype(F32)
            per_b = []
            for b in range(nb):
                ext = ua_ext[b, :, gs]
                win = jnp.dot(band_ref[gi], ext.astype(BF16), preferred_element_type=F32)
                per_b.append(win / cnt - ua_ext[b, CHUNK:2 * CHUNK, gs])
            ya_g = _dot(jnp.concatenate(per_b, axis=0), poolw_ref[gi])
            ya_s[cur, :, gs] = ya_g * pscale_ref[:, gs] * _silu(ya_s[cur, :, gs])

    def front_zb():
        zb_s[cur] = _proj(hb, win_ref, 2 * D_A + D_QKV, D_MAIN)

    def back_out(j):
        def run():
            y_ref[:, :, j * OUT_TILE:(j + 1) * OUT_TILE] = jnp.dot(
                ymix_prev, wout_ref[:, j * OUT_TILE:(j + 1) * OUT_TILE],
                preferred_element_type=F32).reshape(nb, CHUNK, OUT_TILE)
        return run

    filler_plan = [
        [front_proj(0), front_proj(N_HEADS)],
        [front_conv(q_s, 0, True, HEAD_DIM ** -0.5), front_proj(2 * N_HEADS)],
        [front_conv(k_s, N_HEADS, True, 1.0), back_out(0)],
        [front_conv(v_s, 2 * N_HEADS, False, 1.0), back_out(1)],
        [front_gates, back_out(2)],
        [back_out(3), front_ua],
        [front_za, front_zb],
        [front_pool],
    ]

    def run_fillers(n=1):
        for _ in range(n):
            if filler_plan:
                for f in filler_plan.pop(0):
                    f()

    ri = lax.broadcasted_iota(jnp.int32, (CHUNK, LANES), 0)
    li = lax.broadcasted_iota(jnp.int32, (CHUNK, LANES), 1)
    lj = li % CHUNK
    lo_half = li < CHUNK
    strict = ri > lj
    causal = ri >= lj
    eye2 = (ri == lj).astype(F32)

    def pair_mul(xp, yp):
        ybd = jnp.concatenate([jnp.where(lo_half, yp, 0.0), jnp.where(lo_half, 0.0, yp)], axis=0)
        return _dot(xp, ybd)

    st = []
    for p in range(nb // 2):
        for hh in range(N_HEADS):
            ra = slice(2 * p * CHUNK, (2 * p + 1) * CHUNK)
            rb = slice((2 * p + 1) * CHUNK, (2 * p + 2) * CHUNK)
            sl = slice(hh * HEAD_DIM, (hh + 1) * HEAD_DIM)
            ka, kb_ = k_s[prv, ra, sl], k_s[prv, rb, sl]
            kba, kbb = ka * beta_s[prv, hh, ra, :], kb_ * beta_s[prv, hh, rb, :]
            lhs1 = jnp.concatenate([jnp.concatenate([kba, kbb], axis=1),
                                    jnp.concatenate([q_s[prv, ra, sl], q_s[prv, rb, sl]], axis=1)], axis=0)
            kkqk = _dot_nt(lhs1, _block_diag(ka, kb_))
            st.append(dict(ra=ra, rb=rb, sl=sl, hh=hh, p=p, kkqk=kkqk))
    run_fillers()
    for c in st:
        hh, p = c["hh"], c["p"]
        gcol_p = jnp.where(lo_half, gc_s[prv, hh, c["ra"], :], gc_s[prv, hh, c["rb"], :])
        grow_p = jnp.broadcast_to(
            gcrow_s[prv, N_HEADS + hh:N_HEADS + hh + 1, 2 * p * CHUNK:(2 * p + 2) * CHUNK], (CHUNK, LANES))
        dec = jnp.exp(jnp.minimum(gcol_p - grow_p, 0.0))
        kkqk = c.pop("kkqk")
        c["npow"] = jnp.where(strict, kkqk[0:CHUNK] * dec, 0.0)
        c["qkm"] = jnp.where(causal, kkqk[CHUNK:2 * CHUNK] * dec, 0.0)
        c["t"] = eye2 - c["npow"]
    for c in st:
        c["npow"] = pair_mul(c["npow"], c["npow"])
    run_fillers()
    for _ in range(4):
        for c in st:
            both = pair_mul(jnp.concatenate([c["t"], c["npow"]], axis=0), c["npow"])
            c["t"] = c["t"] + both[0:CHUNK]
            c["npow"] = both[CHUNK:2 * CHUNK]
        run_fillers()
    for c in st:
        c["t"] = c["t"] + pair_mul(c["t"], c["npow"])
    run_fillers()
    for c in st:
        ra, rb, sl, hh = c["ra"], c["rb"], c["sl"], c["hh"]
        ba, bb = beta_s[prv, hh, ra, :], beta_s[prv, hh, rb, :]
        kbga = k_s[prv, ra, sl] * ba * jnp.exp(gc_s[prv, hh, ra, :])
        kbgb = k_s[prv, rb, sl] * bb * jnp.exp(gc_s[prv, hh, rb, :])
        rhs = jnp.concatenate([_block_diag(v_s[prv, ra, sl] * ba, v_s[prv, rb, sl] * bb),
                               _block_diag(kbga, kbgb)], axis=1)
        c["uw"] = _dot(c.pop("t"), rhs)
    run_fillers()
    for c in st:
        ra, rb, sl, hh, p = c["ra"], c["rb"], c["sl"], c["hh"], c["p"]
        c["ia"], c["ib"] = 2 * p * N_HEADS + hh, (2 * p + 1) * N_HEADS + hh
        qg = jnp.concatenate([q_s[prv, ra, sl] * jnp.exp(gc_s[prv, hh, ra, :]),
                              q_s[prv, rb, sl] * jnp.exp(gc_s[prv, hh, rb, :])], axis=1)
        lhs3 = jnp.concatenate([c["uw"][:, 2 * HEAD_DIM:4 * HEAD_DIM], qg], axis=0)
        c["r3"] = _dot(lhs3, _block_diag(s_ref[c["ia"]], s_ref[c["ib"]]))
    run_fillers(len(filler_plan))

    yo = y_ref[...]
    yn = yo * lax.rsqrt(jnp.mean(yo * yo, axis=-1, keepdims=True) + EPS) * postw_ref[...][None]
    y_ref[...] = xprev_ref[...] + gate * yn

    for c in st:
        ra, rb, sl, hh, p = c["ra"], c["rb"], c["sl"], c["hh"], c["p"]
        r3 = c.pop("r3")
        vn = c.pop("uw")[:, 0:2 * HEAD_DIM] - r3[0:CHUNK]
        o = r3[CHUNK:2 * CHUNK] + _dot(c.pop("qkm"), _block_diag(vn[:, 0:HEAD_DIM], vn[:, HEAD_DIM:2 * HEAD_DIM]))
        o_s[ra, sl] = o[:, 0:HEAD_DIM]
        o_s[rb, sl] = o[:, HEAD_DIM:2 * HEAD_DIM]
        gla = gc_s[prv, hh, (2 * p + 1) * CHUNK - 1:(2 * p + 1) * CHUNK, :]
        glb = gc_s[prv, hh, (2 * p + 2) * CHUNK - 1:(2 * p + 2) * CHUNK, :]
        kg = jnp.concatenate([k_s[prv, ra, sl] * jnp.exp(gla - gc_s[prv, hh, ra, :]),
                              k_s[prv, rb, sl] * jnp.exp(glb - gc_s[prv, hh, rb, :])], axis=1)
        upd = _dot_tn(kg, vn)
        s_ref[c["ia"]] = s_ref[c["ia"]] * jnp.exp(gla) + upd[0:HEAD_DIM, 0:HEAD_DIM]
        s_ref[c["ib"]] = s_ref[c["ib"]] * jnp.exp(glb) + upd[HEAD_DIM:2 * HEAD_DIM, HEAD_DIM:2 * HEAD_DIM]

    @pl.when(step == nchunk - 1)
    def _():
        npool_ref[...] = ua_ext[:, 2 * CHUNK - POOL_BUF:2 * CHUNK, :]
        for tl in range(QKV_TILES):
            nconv_ref[:, :, tl * LANES:(tl + 1) * LANES] = qkv_ext[tl, :, SUBLANES + CHUNK - (CONV_K - 1):
                                                                   SUBLANES + CHUNK, :]

    @pl.when(step == nchunk)
    def _():
        nssm_ref[...] = s_ref[...]

    qkv_ext[:, :, 0:SUBLANES, :] = qkv_ext[:, :, CHUNK:CHUNK + SUBLANES, :]
    ua_ext[:, 0:CHUNK, :] = ua_ext[:, CHUNK:2 * CHUNK, :]


def _layer_spec(shape, l):
    return pl.BlockSpec((None,) + tuple(shape), lambda s, _n=len(shape): (l,) + (0,) * _n)


def _prompt_layer(l, x, mod_p, lw, consts, prev):
    nb, seq, _ = x.shape
    depth = lw["win"].shape[0]
    rows = nb * CHUNK
    nchunk = seq // CHUNK
    full = lambda shape: pl.BlockSpec(shape, lambda s, _n=len(shape): (0,) * _n)
    in_specs = [
        pl.BlockSpec((nb, CHUNK, D_MODEL), lambda s: (0, jnp.minimum(s, nchunk - 1), 0)),
        pl.BlockSpec((nb, CHUNK, D_MODEL), lambda s: (0, jnp.maximum(s - 2, 0), 0)),
        _layer_spec((nb, 1, 3 * D_MODEL), l),
        _layer_spec((1, D_MODEL), l), _layer_spec((1, D_MODEL), l),
        _layer_spec((D_MAIN, D_MODEL), l), _layer_spec((D_MODEL, LANES), l), _layer_spec((SUBLANES, D_MODEL), l),
        _layer_spec((CONV_K, D_QKV), l), _layer_spec((N_POOL, POOL_GC, POOL_GC), l), _layer_spec((1, D_A), l),
        _layer_spec((1, LANES), l), _layer_spec((1, LANES), l),
        _layer_spec((SUBLANES, rows), l), _layer_spec((SUBLANES, rows), l),
        _layer_spec((1, HEAD_DIM), l), _layer_spec((D_MODEL, D_MODEL), l),
        full((N_POOL, CHUNK, 2 * CHUNK)), full((CHUNK, 2 * CHUNK)),
    ] + [pl.BlockSpec(memory_space=pl.ANY)] * len(prev)
    out_specs = [
        pl.BlockSpec((nb, CHUNK, D_MODEL), lambda s: (0, jnp.maximum(s - 2, 0), 0)),
        _layer_spec((nb, POOL_BUF, D_A), l), _layer_spec((nb, CONV_K - 1, D_QKV), l),
        _layer_spec((nb * N_HEADS, HEAD_DIM, HEAD_DIM), l),
    ]
    out_shape = [
        jax.ShapeDtypeStruct((nb, seq, D_MODEL), F32),
        jax.ShapeDtypeStruct((depth, nb, POOL_BUF, D_A), F32),
        jax.ShapeDtypeStruct((depth, nb, CONV_K - 1, D_QKV), F32),
        jax.ShapeDtypeStruct((depth, nb * N_HEADS, HEAD_DIM, HEAD_DIM), F32),
    ]
    scratch = [
        pltpu.VMEM((QKV_TILES, nb, CHUNK + SUBLANES, LANES), F32),
        pltpu.VMEM((nb, 2 * CHUNK, D_A), F32),
        pltpu.VMEM((nb * N_HEADS, HEAD_DIM, HEAD_DIM), F32),
        pltpu.VMEM((2, rows, D_B), F32), pltpu.VMEM((2, rows, D_B), F32), pltpu.VMEM((2, rows, D_B), F32),
        pltpu.VMEM((2, N_HEADS, rows, LANES), F32), pltpu.VMEM((2, N_HEADS, rows, LANES), F32),
        pltpu.VMEM((2, SUBLANES, rows), F32),
        pltpu.VMEM((rows, D_B), F32),
        pltpu.VMEM((2, rows, D_A), F32), pltpu.VMEM((2, rows, D_B), F32),
    ]
    n_in = len(in_specs) - len(prev)
    return pl.pallas_call(
        functools.partial(_prompt_kernel, n_alias=len(prev)),
        grid=(nchunk + 2,),
        in_specs=in_specs,
        out_specs=out_specs,
        out_shape=out_shape,
        scratch_shapes=scratch,
        input_output_aliases={n_in + i: 1 + i for i in range(len(prev))},
        compiler_params=pltpu.CompilerParams(dimension_semantics=("arbitrary",), vmem_limit_bytes=VMEM_LIMIT),
        name="prompt_layer",
    )(x, x, mod_p, lw["prew"], lw["postw"], lw["win"], lw["wba"], lw["wbat"], lw["convw"], lw["poolw"],
      lw["pscale"], lw["alogc"], lw["dtbc"], lw["alogr"], lw["dtbr"], lw["onw"], lw["wout"], consts["band"],
      consts["ltri"], *prev)


def _sample_kernel(x_ref, mod_ref, spool_ref, sconv_ref, ssm_ref, prew_ref, postw_ref, win_ref, wba_ref, convw_ref,
                   poolw_ref, pscale_ref, alogc_ref, dtbc_ref, onw_ref, wout_ref, *rest, n_alias):
    rest = rest[n_alias:]
    y_ref, npool_ref, nconv_ref, nssm_ref = rest[0:4]
    q_s, k_s, v_s, o_s, beta_s, eg_s, qk_s, ya_s, zb_s = rest[4:]
    step = pl.program_id(0)
    nseq = x_ref.shape[0]

    @pl.when(step == 0)
    def _():
        x = x_ref[...]
        mod = mod_ref[...]
        shift = mod[:, 0:D_MODEL]
        scale = mod[:, D_MODEL:2 * D_MODEL]
        a_mul = prew_ref[...] * (1.0 + scale)
        ms = jnp.mean(x * x, axis=-1, keepdims=True)
        hb = (x * lax.rsqrt(ms + EPS) * a_mul + shift).astype(BF16)

        ua = _proj(hb, win_ref, 0, D_A)
        ya_parts = []
        for gi, w in enumerate(POOL_WINDOWS):
            gs = slice(gi * POOL_GC, (gi + 1) * POOL_GC)
            win = ua[:, gs]
            for d in range(1, w):
                win = win + spool_ref[POOL_BUF - d, :, gs]
            cnt = float(min(PAST_LEN + 1, w))
            pooled = win / cnt - ua[:, gs]
            ya_parts.append(_dot(pooled, poolw_ref[gi]))
        za = _proj(hb, win_ref, D_A, 2 * D_A)
        ya_s[...] = jnp.concatenate(ya_parts, axis=1) * pscale_ref[...] * _silu(za)
        npool_ref[0:POOL_BUF - 1] = spool_ref[1:POOL_BUF]
        npool_ref[POOL_BUF - 1] = ua

        qkv = _proj(hb, win_ref, 2 * D_A, 2 * D_A + D_QKV)
        cw = convw_ref[...]
        acc = qkv * cw[CONV_K - 1:CONV_K]
        for j in range(CONV_K - 1):
            acc = acc + sconv_ref[j] * cw[j:j + 1]
        nconv_ref[0:CONV_K - 2] = sconv_ref[1:CONV_K - 1]
        nconv_ref[CONV_K - 2] = qkv
        qkvc = _silu(acc)
        bac = jnp.dot(hb, wba_ref[...], preferred_element_type=F32)
        sig = jax.nn.sigmoid(bac)
        eg = jnp.exp(-jnp.exp(alogc_ref[...]) * _softplus(bac + dtbc_ref[...]))
        for hh in range(N_HEADS):
            sl = slice(hh * HEAD_DIM, (hh + 1) * HEAD_DIM)
            qh = qkvc[:, hh * HEAD_DIM:(hh + 1) * HEAD_DIM]
            kh = qkvc[:, D_B + hh * HEAD_DIM:D_B + (hh + 1) * HEAD_DIM]
            qn = qh * lax.rsqrt(jnp.sum(qh * qh, axis=-1, keepdims=True) + EPS) * (HEAD_DIM ** -0.5)
            kn = kh * lax.rsqrt(jnp.sum(kh * kh, axis=-1, keepdims=True) + EPS)
            q_s[:, sl] = qn
            k_s[:, sl] = kn
            qk_s[hh] = jnp.broadcast_to(jnp.sum(qn * kn, axis=-1, keepdims=True), (nseq, LANES))
            beta_s[hh] = jnp.broadcast_to(sig[:, hh:hh + 1], (nseq, LANES))
            eg_s[hh] = jnp.broadcast_to(eg[:, N_HEADS + hh:N_HEADS + hh + 1], (nseq, LANES))
        v_s[...] = qkvc[:, 2 * D_B:3 * D_B]
        zb_s[...] = _proj(hb, win_ref, 2 * D_A + D_QKV, D_MAIN)

    r0 = pl.multiple_of(step * SAMPLE_BLOCK, SAMPLE_BLOCK)
    kblk = k_s[pl.ds(r0, SAMPLE_BLOCK), :]
    qblk = q_s[pl.ds(r0, SAMPLE_BLOCK), :]
    vblk = v_s[pl.ds(r0, SAMPLE_BLOCK), :]
    row8 = lax.broadcasted_iota(jnp.int32, (SUBLANES, HEAD_DIM), 0)
    st = []
    for i in range(SAMPLE_BLOCK):
        for hh in range(N_HEADS):
            sl = slice(hh * HEAD_DIM, (hh + 1) * HEAD_DIM)
            krow, qrow = kblk[i:i + 1, sl], qblk[i:i + 1, sl]
            lhs = jnp.where(row8 == 0, krow, jnp.where(row8 == 1, qrow, 0.0))
            r = jnp.dot(lhs, ssm_ref[i * N_HEADS + hh], preferred_element_type=F32)
            st.append(dict(i=i, hh=hh, sl=sl, krow=krow, r=r))
    for c in st:
        i, hh = c["i"], c["hh"]
        eg = eg_s[hh, pl.ds(r0 + i, 1), :]
        r = c.pop("r")
        delta = (vblk[i:i + 1, c["sl"]] - eg * r[0:1]) * beta_s[hh, pl.ds(r0 + i, 1), :]
        c["o"] = eg * r[1:2] + qk_s[hh, pl.ds(r0 + i, 1), :] * delta
        krow = c.pop("krow")
        k_hi = krow.astype(BF16).astype(F32)
        d_hi = delta.astype(BF16).astype(F32)
        kp = jnp.where(row8 == 0, k_hi, jnp.where(row8 == 1, krow - k_hi, jnp.where(row8 == 2, k_hi, 0.0)))
        dp = jnp.where(row8 == 0, d_hi, jnp.where(row8 == 1, d_hi, jnp.where(row8 == 2, delta - d_hi, 0.0)))
        c["upd"] = _dot_tn(kp, dp)
    for c in st:
        idx = c["i"] * N_HEADS + c["hh"]
        nssm_ref[idx] = ssm_ref[idx] * eg_s[c["hh"], pl.ds(r0 + c["i"], 1), :] + c.pop("upd")
    o_rows = [jnp.concatenate([c["o"] for c in st[i * N_HEADS:(i + 1) * N_HEADS]], axis=1)
              for i in range(SAMPLE_BLOCK)]
    o_s[pl.ds(r0, SAMPLE_BLOCK), :] = jnp.concatenate(o_rows, axis=0)

    @pl.when(step == pl.num_programs(0) - 1)
    def _():
        x = x_ref[...]
        gate = mod_ref[:, 2 * D_MODEL:3 * D_MODEL]
        yb = _head_rms(o_s[...], onw_ref[...]) * _silu(zb_s[...])
        ymix = jnp.concatenate([ya_s[...], yb], axis=1)
        yo = jnp.dot(ymix.astype(BF16), wout_ref[...], preferred_element_type=F32)
        yn = yo * lax.rsqrt(jnp.mean(yo * yo, axis=-1, keepdims=True) + EPS) * postw_ref[...]
        y_ref[...] = x + gate * yn


def _sample_layer(l, x, mod_s, state_pool, state_conv, state_ssm, lw, prev):
    nseq = x.shape[0]
    depth = lw["win"].shape[0]
    nstep = nseq // SAMPLE_BLOCK
    blk_states = SAMPLE_BLOCK * N_HEADS
    full = lambda shape: pl.BlockSpec(shape, lambda s, _n=len(shape): (0,) * _n)
    ssm_spec = pl.BlockSpec((None, blk_states, HEAD_DIM, HEAD_DIM), lambda s: (l, s, 0, 0))
    in_specs = [
        full((nseq, D_MODEL)), _layer_spec((nseq, 3 * D_MODEL), l),
        _layer_spec((POOL_BUF, nseq, D_A), l), _layer_spec((CONV_K - 1, nseq, D_QKV), l), ssm_spec,
        _layer_spec((1, D_MODEL), l), _layer_spec((1, D_MODEL), l),
        _layer_spec((D_MAIN, D_MODEL), l), _layer_spec((D_MODEL, LANES), l),
        _layer_spec((CONV_K, D_QKV), l), _layer_spec((N_POOL, POOL_GC, POOL_GC), l), _layer_spec((1, D_A), l),
        _layer_spec((1, LANES), l), _layer_spec((1, LANES), l), _layer_spec((1, HEAD_DIM), l),
        _layer_spec((D_MODEL, D_MODEL), l),
    ] + [pl.BlockSpec(memory_space=pl.ANY)] * len(prev)
    out_specs = [
        full((nseq, D_MODEL)), _layer_spec((POOL_BUF, nseq, D_A), l), _layer_spec((CONV_K - 1, nseq, D_QKV), l),
        ssm_spec,
    ]
    out_shape = [
        jax.ShapeDtypeStruct((nseq, D_MODEL), F32),
        jax.ShapeDtypeStruct((depth, POOL_BUF, nseq, D_A), F32),
        jax.ShapeDtypeStruct((depth, CONV_K - 1, nseq, D_QKV), F32),
        jax.ShapeDtypeStruct((depth, nseq * N_HEADS, HEAD_DIM, HEAD_DIM), F32),
    ]
    scratch = [
        pltpu.VMEM((nseq, D_B), F32), pltpu.VMEM((nseq, D_B), F32), pltpu.VMEM((nseq, D_B), F32),
        pltpu.VMEM((nseq, D_B), F32),
        pltpu.VMEM((N_HEADS, nseq, LANES), F32), pltpu.VMEM((N_HEADS, nseq, LANES), F32),
        pltpu.VMEM((N_HEADS, nseq, LANES), F32),
        pltpu.VMEM((nseq, D_A), F32), pltpu.VMEM((nseq, D_B), F32),
    ]
    n_in = len(in_specs) - len(prev)
    return pl.pallas_call(
        functools.partial(_sample_kernel, n_alias=len(prev)),
        grid=(nstep,),
        in_specs=in_specs,
        out_specs=out_specs,
        out_shape=out_shape,
        scratch_shapes=scratch,
        input_output_aliases={n_in + i: 1 + i for i in range(len(prev))},
        compiler_params=pltpu.CompilerParams(dimension_semantics=("arbitrary",), vmem_limit_bytes=VMEM_LIMIT),
        name="sample_layer",
    )(x, mod_s, state_pool, state_conv, state_ssm, lw["prew"], lw["postw"], lw["win"], lw["wba"], lw["convw"],
      lw["poolw"], lw["pscale"], lw["alogc"], lw["dtbc"], lw["onw"], lw["wout"], *prev)


def _constants():
    t = np.arange(CHUNK)[:, None]
    j = np.arange(2 * CHUNK)[None, :]
    band = np.stack([((j <= CHUNK + t) & (j > CHUNK + t - w)) for w in POOL_WINDOWS]).astype(np.float32)
    tri = (np.arange(CHUNK)[None, :] <= t).astype(np.float32)
    ltri = np.concatenate([tri, tri], axis=1)
    return {"band": jnp.asarray(band, BF16), "ltri": jnp.asarray(ltri, BF16)}


def _stacked_weights(rows_p, pre_norm_w, post_norm_w, w_in, conv_w, pool_w, pool_scale, a_log, dt_bias, o_norm_w,
                     w_out):
    depth = w_in.shape[0]
    w_in_t = jnp.swapaxes(w_in, 1, 2)
    wbat = w_in_t[:, D_MAIN:D_MAIN + 2 * N_HEADS]
    pad_c = lambda v: jnp.zeros((depth, 1, LANES), F32).at[:, 0, N_HEADS:2 * N_HEADS].set(v)
    pad_r = lambda v: jnp.broadcast_to(
        jnp.zeros((depth, SUBLANES), F32).at[:, N_HEADS:2 * N_HEADS].set(v)[:, :, None], (depth, SUBLANES, rows_p))
    return {
        "prew": pre_norm_w.reshape(depth, 1, D_MODEL),
        "postw": post_norm_w.reshape(depth, 1, D_MODEL),
        "win": w_in_t[:, 0:D_MAIN].astype(BF16),
        "wba": jnp.zeros((depth, D_MODEL, LANES), F32).at[:, :, 0:2 * N_HEADS].set(
            jnp.swapaxes(wbat, 1, 2)).astype(BF16),
        "wbat": wbat.astype(BF16),
        "convw": conv_w,
        "poolw": pool_w.astype(BF16),
        "pscale": pool_scale.reshape(depth, 1, D_A),
        "alogc": pad_c(a_log), "dtbc": pad_c(dt_bias),
        "alogr": pad_r(a_log), "dtbr": pad_r(dt_bias),
        "onw": o_norm_w.reshape(depth, 1, HEAD_DIM),
        "wout": w_out.astype(BF16),
    }


def kernel(x_prompt, x_sample, c_prompt, c_sample, state_pool, state_conv, state_ssm, w_ada, b_ada, pre_norm_w,
           post_norm_w, w_in, conv_w, pool_w, pool_scale, a_log, dt_bias, o_norm_w, w_out):
    depth = w_in.shape[0]
    nb, seq, _ = x_prompt.shape
    nseq, dec_seq, _ = x_sample.shape
    assert dec_seq == 1 and seq % CHUNK == 0 and nb % 2 == 0 and nseq % SAMPLE_BLOCK == 0
    consts = _constants()
    lw = _stacked_weights(nb * CHUNK, pre_norm_w, post_norm_w, w_in, conv_w, pool_w, pool_scale, a_log, dt_bias,
                          o_norm_w, w_out)
    mod_p, mod_s = _mod_call(c_prompt, c_sample, w_ada, b_ada)
    mod_p = mod_p.reshape(depth, nb, 1, 3 * D_MODEL)
    ssm_in = state_ssm.reshape(depth, nseq * N_HEADS, HEAD_DIM, HEAD_DIM)
    pool_in = jnp.swapaxes(state_pool, 1, 2)
    conv_in = jnp.swapaxes(state_conv, 1, 2)
    yp, ys = x_prompt, x_sample.reshape(nseq, D_MODEL)
    prev_p, prev_s = (), ()
    for l in range(depth):
        yp, *prev_p = _prompt_layer(l, yp, mod_p, lw, consts, tuple(prev_p))
        ys, *prev_s = _sample_layer(l, ys, mod_s, pool_in, conv_in, ssm_in, lw, tuple(prev_s))
    npool_p, nconv_p, nssm_p = prev_p
    npool_s, nconv_s, nssm_s = prev_s
    return (yp, ys.reshape(nseq, dec_seq, D_MODEL), npool_p, nconv_p,
            nssm_p.reshape(depth, nb, N_HEADS, HEAD_DIM, HEAD_DIM),
            jnp.swapaxes(npool_s, 1, 2), jnp.swapaxes(nconv_s, 1, 2),
            nssm_s.reshape(depth, nseq, N_HEADS, HEAD_DIM, HEAD_DIM))
```

```python
import functools

import jax
import jax.numpy as jnp
import numpy as np
from jax import lax
from jax.experimental import pallas as pl
from jax.experimental.pallas import tpu as pltpu

F32 = jnp.float32
BF16 = jnp.bfloat16

D_MODEL = 1024
D_A = 512
D_B = 512
N_POOL = 4
POOL_WINDOWS = (2, 4, 8, 16)
POOL_GC = 128
POOL_BUF = 15
HEAD_DIM = 128
N_HEADS = 4
D_QKV = 3 * D_B
CONV_K = 4
D_MAIN = 2 * D_A + D_QKV + D_B
PAST_LEN = 16384
EPS = 1e-6
CHUNK = 64
GDN_GROUP = 16
SAMPLE_BLOCK = 8
LANES = 128
SUBLANES = 8
QKV_TILES = D_QKV // LANES
VMEM_LIMIT = 56 * 1024 * 1024


def _dot(a, b):
    return jnp.dot(a.astype(BF16), b.astype(BF16), preferred_element_type=F32)


def _dot_tn(a, b):
    return lax.dot_general(a, b, (((0,), (0,)), ((), ())), preferred_element_type=F32)


def _proj(hb, wt_ref, c0, c1):
    return lax.dot_general(hb, wt_ref[c0:c1, :], (((1,), (1,)), ((), ())), preferred_element_type=F32)


def _silu(x):
    return x * jax.nn.sigmoid(x)


def _softplus(x):
    return jnp.maximum(x, 0.0) + jnp.log1p(jnp.exp(-jnp.abs(x)))


def _block_diag(a, b):
    top = jnp.concatenate([a, jnp.zeros((a.shape[0], b.shape[1]), a.dtype)], axis=1)
    bot = jnp.concatenate([jnp.zeros((b.shape[0], a.shape[1]), b.dtype), b], axis=1)
    return jnp.concatenate([top, bot], axis=0)


def _head_rms(o_all, w):
    parts = []
    for hh in range(N_HEADS):
        oh = o_all[:, hh * HEAD_DIM:(hh + 1) * HEAD_DIM]
        parts.append(oh * lax.rsqrt(jnp.mean(oh * oh, axis=-1, keepdims=True) + EPS) * w)
    return jnp.concatenate(parts, axis=1)


def _mod_kernel(cp_ref, cs_ref, w_ref, b_ref, op_ref, os_ref):
    w = w_ref[...]
    b = b_ref[...]
    op_ref[...] = jnp.dot(_silu(cp_ref[...]), w, preferred_element_type=F32) + b
    os_ref[...] = jnp.dot(_silu(cs_ref[...]), w, preferred_element_type=F32) + b


def _mod_call(c_prompt, c_sample, w_ada, b_ada):
    depth, _, n3 = w_ada.shape
    nb, nseq = c_prompt.shape[0], c_sample.shape[0]
    nblk = n3 // D_MODEL
    return pl.pallas_call(
        _mod_kernel,
        grid=(depth, nblk),
        in_specs=[
            pl.BlockSpec((nb, D_MODEL), lambda l, j: (0, 0)),
            pl.BlockSpec((nseq, D_MODEL), lambda l, j: (0, 0)),
            pl.BlockSpec((None, D_MODEL, D_MODEL), lambda l, j: (l, 0, j)),
            pl.BlockSpec((None, 1, D_MODEL), lambda l, j: (l, 0, j)),
        ],
        out_specs=[pl.BlockSpec((None, nb, D_MODEL), lambda l, j: (l, 0, j)),
                   pl.BlockSpec((None, nseq, D_MODEL), lambda l, j: (l, 0, j))],
        out_shape=[jax.ShapeDtypeStruct((depth, nb, n3), F32), jax.ShapeDtypeStruct((depth, nseq, n3), F32)],
        compiler_params=pltpu.CompilerParams(dimension_semantics=("arbitrary", "arbitrary")),
        name="adaln_mod",
    )(c_prompt, c_sample, w_ada, b_ada.reshape(depth, 1, n3))


def _prompt_kernel(x_ref, mod_ref, prew_ref, postw_ref, win_ref, wba_ref, wbat_ref, convw_ref, poolw_ref,
                   pscale_ref, alogc_ref, dtbc_ref, alogr_ref, dtbr_ref, onw_ref, wout_ref, band_ref, ltri_ref,
                   *rest, n_alias):
    rest = rest[n_alias:]
    y_ref, npool_ref, nconv_ref, nssm_ref = rest[0:4]
    qkv_ext, ua_ext, s_ref, q_s, k_s, kt_s, v_s, o_s, beta_s, gc_s, eg_s, gcrow_s = rest[4:]
    step = pl.program_id(0)
    nb = x_ref.shape[0]
    rows = nb * CHUNK

    @pl.when(step == 0)
    def _():
        qkv_ext[:, :, 0:SUBLANES, :] = jnp.zeros((QKV_TILES, nb, SUBLANES, LANES), F32)
        ua_ext[:, 0:CHUNK, :] = jnp.zeros((nb, CHUNK, D_A), F32)
        s_ref[...] = jnp.zeros(s_ref.shape, F32)

    x = x_ref[...]
    mod = mod_ref[...]
    shift = mod[:, :, 0:D_MODEL]
    scale = mod[:, :, D_MODEL:2 * D_MODEL]
    gate = mod[:, :, 2 * D_MODEL:3 * D_MODEL]
    a_mul = prew_ref[...][None] * (1.0 + scale)
    ms = jnp.mean(x * x, axis=-1, keepdims=True)
    h = x * lax.rsqrt(ms + EPS) * a_mul + shift
    hb = h.reshape(rows, D_MODEL).astype(BF16)

    ua = _proj(hb, win_ref, 0, D_A)
    ua3 = ua.reshape(nb, CHUNK, D_A)
    ua_ext[:, CHUNK:2 * CHUNK, :] = ua3
    pos = lax.broadcasted_iota(jnp.int32, (CHUNK, POOL_GC), 0) + step * CHUNK
    pooled_groups = []
    for gi, w in enumerate(POOL_WINDOWS):
        cnt = jnp.minimum(pos + 1, w).astype(F32)
        per_b = []
        for b in range(nb):
            ext = ua_ext[b, :, gi * POOL_GC:(gi + 1) * POOL_GC]
            win = jnp.dot(band_ref[gi], ext.astype(BF16), preferred_element_type=F32)
            per_b.append(win / cnt - ua3[b, :, gi * POOL_GC:(gi + 1) * POOL_GC])
        pooled_groups.append(jnp.concatenate(per_b, axis=0))
    ya = jnp.concatenate([_dot(pooled_groups[gi], poolw_ref[gi]) for gi in range(N_POOL)], axis=1)
    za = _proj(hb, win_ref, D_A, 2 * D_A)
    ya = ya * pscale_ref[...] * _silu(za)

    qkv = _proj(hb, win_ref, 2 * D_A, 2 * D_A + D_QKV)
    qkv_t = jnp.stack([qkv[:, c * LANES:(c + 1) * LANES] for c in range(QKV_TILES)], axis=0)
    qkv_t = qkv_t.reshape(QKV_TILES, nb, CHUNK, LANES)
    qkv_ext[:, :, SUBLANES:SUBLANES + CHUNK, :] = qkv_t
    cw = convw_ref[...]
    cw_t = [jnp.stack([cw[j:j + 1, c * LANES:(c + 1) * LANES] for c in range(QKV_TILES)], axis=0)[:, None]
            for j in range(CONV_K)]
    acc = qkv_t * cw_t[CONV_K - 1]
    for j in range(CONV_K - 1):
        acc = acc + qkv_ext[:, :, pl.ds(SUBLANES - (CONV_K - 1) + j, CHUNK), :] * cw_t[j]
    act = _silu(acc).reshape(QKV_TILES, rows, LANES)
    for hh in range(N_HEADS):
        sl = slice(hh * HEAD_DIM, (hh + 1) * HEAD_DIM)
        qh, kh = act[hh], act[N_HEADS + hh]
        q_s[:, sl] = qh * lax.rsqrt(jnp.sum(qh * qh, axis=-1, keepdims=True) + EPS) * (HEAD_DIM ** -0.5)
        kn = kh * lax.rsqrt(jnp.sum(kh * kh, axis=-1, keepdims=True) + EPS)
        k_s[:, sl] = kn
        kt_s[hh] = kn.T
        v_s[:, sl] = act[2 * N_HEADS + hh]

    bac = jnp.dot(hb, wba_ref[...], preferred_element_type=F32)
    bar = lax.dot_general(wbat_ref[...], hb, (((1,), (1,)), ((), ())), preferred_element_type=F32)
    sig = jax.nn.sigmoid(bac)
    gcol = -jnp.exp(alogc_ref[...]) * _softplus(bac + dtbc_ref[...])
    g_hi = gcol.astype(BF16)
    g_lo = (gcol - g_hi.astype(F32)).astype(BF16)
    gc_parts = []
    for b in range(nb):
        rs = slice(b * CHUNK, (b + 1) * CHUNK)
        gc_parts.append(jnp.dot(ltri_ref[...], jnp.concatenate([g_hi[rs], g_lo[rs]], axis=0),
                                preferred_element_type=F32))
    gc = jnp.concatenate(gc_parts, axis=0)
    for hh in range(N_HEADS):
        beta_s[hh] = jnp.broadcast_to(sig[:, hh:hh + 1], (rows, LANES))
        gcb = jnp.broadcast_to(gc[:, N_HEADS + hh:N_HEADS + hh + 1], (rows, LANES))
        gc_s[hh] = gcb
        eg_s[hh] = jnp.exp(gcb)
    grow = -jnp.exp(alogr_ref[...]) * _softplus(bar + dtbr_ref[...])
    lane_in_chunk = lax.broadcasted_iota(jnp.int32, grow.shape, 1) % CHUNK
    sh = 1
    while sh < CHUNK:
        grow = grow + jnp.where(lane_in_chunk >= sh, pltpu.roll(grow, sh, 1), 0.0)
        sh *= 2
    gcrow_s[...] = grow

    ri = lax.broadcasted_iota(jnp.int32, (CHUNK, LANES), 0)
    li = lax.broadcasted_iota(jnp.int32, (CHUNK, LANES), 1)
    lj = li % CHUNK
    lo_half = li < CHUNK
    strict = ri > lj
    causal = ri >= lj
    eye2 = (ri == lj).astype(F32)
    lo_half_hd = lax.broadcasted_iota(jnp.int32, (HEAD_DIM, LANES), 1) < CHUNK

    def pair_mul(xp, yp):
        ybd = jnp.concatenate([jnp.where(lo_half, yp, 0.0), jnp.where(lo_half, 0.0, yp)], axis=0)
        return _dot(xp, ybd)

    chains = [(p, hh) for p in range(nb // 2) for hh in range(N_HEADS)]
    for g0 in range(0, len(chains), GDN_GROUP):
        st = []
        for p, hh in chains[g0:g0 + GDN_GROUP]:
            ra = slice(2 * p * CHUNK, (2 * p + 1) * CHUNK)
            rb = slice((2 * p + 1) * CHUNK, (2 * p + 2) * CHUNK)
            sl = slice(hh * HEAD_DIM, (hh + 1) * HEAD_DIM)
            kba, kbb = k_s[ra, sl] * beta_s[hh, ra, :], k_s[rb, sl] * beta_s[hh, rb, :]
            lhs1 = jnp.concatenate([jnp.concatenate([kba, kbb], axis=1),
                                    jnp.concatenate([q_s[ra, sl], q_s[rb, sl]], axis=1)], axis=0)
            kt = kt_s[hh, :, 2 * p * CHUNK:(2 * p + 2) * CHUNK]
            ktbd = jnp.concatenate([jnp.where(lo_half_hd, kt, 0.0), jnp.where(lo_half_hd, 0.0, kt)], axis=0)
            kkqk = _dot(lhs1, ktbd)
            st.append(dict(ra=ra, rb=rb, sl=sl, hh=hh, p=p, kkqk=kkqk))
        for c in st:
            hh, p = c["hh"], c["p"]
            gcol_p = jnp.where(lo_half, gc_s[hh, c["ra"], :], gc_s[hh, c["rb"], :])
            grow_p = jnp.broadcast_to(gcrow_s[N_HEADS + hh:N_HEADS + hh + 1, 2 * p * CHUNK:(2 * p + 2) * CHUNK],
                                      (CHUNK, LANES))
            dec = jnp.exp(jnp.minimum(gcol_p - grow_p, 0.0))
            kkqk = c.pop("kkqk")
            c["npow"] = jnp.where(strict, kkqk[0:CHUNK] * dec, 0.0)
            c["qkm"] = jnp.where(causal, kkqk[CHUNK:2 * CHUNK] * dec, 0.0)
            c["t"] = eye2 - c["npow"]
        for c in st:
            c["npow"] = pair_mul(c["npow"], c["npow"])
        for _ in range(4):
            for c in st:
                both = pair_mul(jnp.concatenate([c["t"], c["npow"]], axis=0), c["npow"])
                c["t"] = c["t"] + both[0:CHUNK]
                c["npow"] = both[CHUNK:2 * CHUNK]
        for c in st:
            c["t"] = c["t"] + pair_mul(c["t"], c["npow"])
        for c in st:
            ra, rb, sl, hh = c["ra"], c["rb"], c["sl"], c["hh"]
            ba, bb = beta_s[hh, ra, :], beta_s[hh, rb, :]
            kba, kbb = k_s[ra, sl] * ba, k_s[rb, sl] * bb
            rhs = jnp.concatenate([_block_diag(v_s[ra, sl] * ba, v_s[rb, sl] * bb),
                                   _block_diag(kba * eg_s[hh, ra, :], kbb * eg_s[hh, rb, :])], axis=1)
            c["uw"] = _dot(c.pop("t"), rhs)
        for c in st:
            ra, rb, sl, hh, p = c["ra"], c["rb"], c["sl"], c["hh"], c["p"]
            c["ia"], c["ib"] = 2 * p * N_HEADS + hh, (2 * p + 1) * N_HEADS + hh
            lhs3 = jnp.concatenate([c["uw"][:, 2 * HEAD_DIM:4 * HEAD_DIM],
                                    jnp.concatenate([q_s[ra, sl] * eg_s[hh, ra, :], q_s[rb, sl] * eg_s[hh, rb, :]],
                                                    axis=1)], axis=0)
            c["r3"] = _dot(lhs3, _block_diag(s_ref[c["ia"]], s_ref[c["ib"]]))
        for c in st:
            ra, rb, sl, hh, p = c["ra"], c["rb"], c["sl"], c["hh"], c["p"]
            r3 = c.pop("r3")
            vn = c.pop("uw")[:, 0:2 * HEAD_DIM] - r3[0:CHUNK]
            o = r3[CHUNK:2 * CHUNK] + _dot(c.pop("qkm"), _block_diag(vn[:, 0:HEAD_DIM], vn[:, HEAD_DIM:2 * HEAD_DIM]))
            o_s[ra, sl] = o[:, 0:HEAD_DIM]
            o_s[rb, sl] = o[:, HEAD_DIM:2 * HEAD_DIM]
            gla = gc_s[hh, (2 * p + 1) * CHUNK - 1:(2 * p + 1) * CHUNK, :]
            glb = gc_s[hh, (2 * p + 2) * CHUNK - 1:(2 * p + 2) * CHUNK, :]
            kg = jnp.concatenate([k_s[ra, sl] * jnp.exp(gla - gc_s[hh, ra, :]),
                                  k_s[rb, sl] * jnp.exp(glb - gc_s[hh, rb, :])], axis=1)
            upd = _dot_tn(kg, vn)
            s_ref[c["ia"]] = s_ref[c["ia"]] * jnp.exp(gla) + upd[0:HEAD_DIM, 0:HEAD_DIM]
            s_ref[c["ib"]] = s_ref[c["ib"]] * jnp.exp(glb) + upd[HEAD_DIM:2 * HEAD_DIM, HEAD_DIM:2 * HEAD_DIM]

    zb = _proj(hb, win_ref, 2 * D_A + D_QKV, D_MAIN)
    yb = _head_rms(o_s[...], onw_ref[...]) * _silu(zb)
    ymix = jnp.concatenate([ya, yb], axis=1)
    yo = jnp.dot(ymix.astype(BF16), wout_ref[...], preferred_element_type=F32)
    yn = yo * lax.rsqrt(jnp.mean(yo * yo, axis=-1, keepdims=True) + EPS) * postw_ref[...]
    y_ref[...] = x + gate * yn.reshape(nb, CHUNK, D_MODEL)

    qkv_ext[:, :, 0:SUBLANES, :] = qkv_ext[:, :, CHUNK:CHUNK + SUBLANES, :]
    ua_ext[:, 0:CHUNK, :] = ua3

    @pl.when(step == pl.num_programs(0) - 1)
    def _():
        npool_ref[...] = ua3[:, CHUNK - POOL_BUF:CHUNK, :]
        nconv_ref[...] = qkv.reshape(nb, CHUNK, D_QKV)[:, CHUNK - (CONV_K - 1):CHUNK, :]
        nssm_ref[...] = s_ref[...]


def _layer_spec(shape, l):
    return pl.BlockSpec((None,) + tuple(shape), lambda s, _n=len(shape): (l,) + (0,) * _n)


def _prompt_layer(l, x, mod_p, lw, consts, prev):
    nb, seq, _ = x.shape
    depth = lw["win"].shape[0]
    rows = nb * CHUNK
    nstep = seq // CHUNK
    full = lambda shape: pl.BlockSpec(shape, lambda s, _n=len(shape): (0,) * _n)
    in_specs = [
        pl.BlockSpec((nb, CHUNK, D_MODEL), lambda s: (0, s, 0)),
        _layer_spec((nb, 1, 3 * D_MODEL), l),
        _layer_spec((1, D_MODEL), l), _layer_spec((1, D_MODEL), l),
        _layer_spec((D_MAIN, D_MODEL), l), _layer_spec((D_MODEL, LANES), l), _layer_spec((SUBLANES, D_MODEL), l),
        _layer_spec((CONV_K, D_QKV), l), _layer_spec((N_POOL, POOL_GC, POOL_GC), l), _layer_spec((1, D_A), l),
        _layer_spec((1, LANES), l), _layer_spec((1, LANES), l),
        _layer_spec((SUBLANES, rows), l), _layer_spec((SUBLANES, rows), l),
        _layer_spec((1, HEAD_DIM), l), _layer_spec((D_MODEL, D_MODEL), l),
        full((N_POOL, CHUNK, 2 * CHUNK)), full((CHUNK, 2 * CHUNK)),
    ] + [pl.BlockSpec(memory_space=pl.ANY)] * len(prev)
    out_specs = [
        pl.BlockSpec((nb, CHUNK, D_MODEL), lambda s: (0, s, 0)),
        _layer_spec((nb, POOL_BUF, D_A), l), _layer_spec((nb, CONV_K - 1, D_QKV), l),
        _layer_spec((nb * N_HEADS, HEAD_DIM, HEAD_DIM), l),
    ]
    out_shape = [
        jax.ShapeDtypeStruct((nb, seq, D_MODEL), F32),
        jax.ShapeDtypeStruct((depth, nb, POOL_BUF, D_A), F32),
        jax.ShapeDtypeStruct((depth, nb, CONV_K - 1, D_QKV), F32),
        jax.ShapeDtypeStruct((depth, nb * N_HEADS, HEAD_DIM, HEAD_DIM), F32),
    ]
    scratch = [
        pltpu.VMEM((QKV_TILES, nb, CHUNK + SUBLANES, LANES), F32),
        pltpu.VMEM((nb, 2 * CHUNK, D_A), F32),
        pltpu.VMEM((nb * N_HEADS, HEAD_DIM, HEAD_DIM), F32),
        pltpu.VMEM((rows, D_B), F32), pltpu.VMEM((rows, D_B), F32), pltpu.VMEM((N_HEADS, HEAD_DIM, rows), F32),
        pltpu.VMEM((rows, D_B), F32), pltpu.VMEM((rows, D_B), F32),
        pltpu.VMEM((N_HEADS, rows, LANES), F32), pltpu.VMEM((N_HEADS, rows, LANES), F32),
        pltpu.VMEM((N_HEADS, rows, LANES), F32),
        pltpu.VMEM((SUBLANES, rows), F32),
    ]
    n_in = len(in_specs) - len(prev)
    return pl.pallas_call(
        functools.partial(_prompt_kernel, n_alias=len(prev)),
        grid=(nstep,),
        in_specs=in_specs,
        out_specs=out_specs,
        out_shape=out_shape,
        scratch_shapes=scratch,
        input_output_aliases={n_in + i: 1 + i for i in range(len(prev))},
        compiler_params=pltpu.CompilerParams(dimension_semantics=("arbitrary",), vmem_limit_bytes=VMEM_LIMIT),
        name="prompt_layer",
    )(x, mod_p, lw["prew"], lw["postw"], lw["win"], lw["wba"], lw["wbat"], lw["convw"], lw["poolw"], lw["pscale"],
      lw["alogc"], lw["dtbc"], lw["alogr"], lw["dtbr"], lw["onw"], lw["wout"], consts["band"], consts["ltri"], *prev)


def _sample_kernel(x_ref, mod_ref, spool_ref, sconv_ref, ssm_ref, prew_ref, postw_ref, win_ref, wba_ref, convw_ref,
                   poolw_ref, pscale_ref, alogc_ref, dtbc_ref, onw_ref, wout_ref, *rest, n_alias):
    rest = rest[n_alias:]
    y_ref, npool_ref, nconv_ref, nssm_ref = rest[0:4]
    q_s, k_s, v_s, o_s, beta_s, eg_s, qk_s, ya_s, zb_s = rest[4:]
    step = pl.program_id(0)
    nseq = x_ref.shape[0]

    @pl.when(step == 0)
    def _():
        x = x_ref[...]
        mod = mod_ref[...]
        shift = mod[:, 0:D_MODEL]
        scale = mod[:, D_MODEL:2 * D_MODEL]
        a_mul = prew_ref[...] * (1.0 + scale)
        ms = jnp.mean(x * x, axis=-1, keepdims=True)
        hb = (x * lax.rsqrt(ms + EPS) * a_mul + shift).astype(BF16)

        ua = _proj(hb, win_ref, 0, D_A)
        ya_parts = []
        for gi, w in enumerate(POOL_WINDOWS):
            gs = slice(gi * POOL_GC, (gi + 1) * POOL_GC)
            win = ua[:, gs]
            for d in range(1, w):
                win = win + spool_ref[POOL_BUF - d, :, gs]
            cnt = float(min(PAST_LEN + 1, w))
            pooled = win / cnt - ua[:, gs]
            ya_parts.append(_dot(pooled, poolw_ref[gi]))
        za = _proj(hb, win_ref, D_A, 2 * D_A)
        ya_s[...] = jnp.concatenate(ya_parts, axis=1) * pscale_ref[...] * _silu(za)
        npool_ref[0:POOL_BUF - 1] = spool_ref[1:POOL_BUF]
        npool_ref[POOL_BUF - 1] = ua

        qkv = _proj(hb, win_ref, 2 * D_A, 2 * D_A + D_QKV)
        cw = convw_ref[...]
        acc = qkv * cw[CONV_K - 1:CONV_K]
        for j in range(CONV_K - 1):
            acc = acc + sconv_ref[j] * cw[j:j + 1]
        nconv_ref[0:CONV_K - 2] = sconv_ref[1:CONV_K - 1]
        nconv_ref[CONV_K - 2] = qkv
        qkvc = _silu(acc)
        bac = jnp.dot(hb, wba_ref[...], preferred_element_type=F32)
        sig = jax.nn.sigmoid(bac)
        eg = jnp.exp(-jnp.exp(alogc_ref[...]) * _softplus(bac + dtbc_ref[...]))
        for hh in range(N_HEADS):
            sl = slice(hh * HEAD_DIM, (hh + 1) * HEAD_DIM)
            qh = qkvc[:, hh * HEAD_DIM:(hh + 1) * HEAD_DIM]
            kh = qkvc[:, D_B + hh * HEAD_DIM:D_B + (hh + 1) * HEAD_DIM]
            qn = qh * lax.rsqrt(jnp.sum(qh * qh, axis=-1, keepdims=True) + EPS) * (HEAD_DIM ** -0.5)
            kn = kh * lax.rsqrt(jnp.sum(kh * kh, axis=-1, keepdims=True) + EPS)
            q_s[:, sl] = qn
            k_s[:, sl] = kn
            qk_s[hh] = jnp.broadcast_to(jnp.sum(qn * kn, axis=-1, keepdims=True), (nseq, LANES))
            beta_s[hh] = jnp.broadcast_to(sig[:, hh:hh + 1], (nseq, LANES))
            eg_s[hh] = jnp.broadcast_to(eg[:, N_HEADS + hh:N_HEADS + hh + 1], (nseq, LANES))
        v_s[...] = qkvc[:, 2 * D_B:3 * D_B]
        zb_s[...] = _proj(hb, win_ref, 2 * D_A + D_QKV, D_MAIN)

    r0 = pl.multiple_of(step * SAMPLE_BLOCK, SAMPLE_BLOCK)
    kblk = k_s[pl.ds(r0, SAMPLE_BLOCK), :]
    qblk = q_s[pl.ds(r0, SAMPLE_BLOCK), :]
    vblk = v_s[pl.ds(r0, SAMPLE_BLOCK), :]
    row8 = lax.broadcasted_iota(jnp.int32, (SUBLANES, HEAD_DIM), 0)
    st = []
    for i in range(SAMPLE_BLOCK):
        for hh in range(N_HEADS):
            sl = slice(hh * HEAD_DIM, (hh + 1) * HEAD_DIM)
            krow, qrow = kblk[i:i + 1, sl], qblk[i:i + 1, sl]
            lhs = jnp.where(row8 == 0, krow, jnp.where(row8 == 1, qrow, 0.0))
            r = jnp.dot(lhs, ssm_ref[i * N_HEADS + hh], preferred_element_type=F32)
            st.append(dict(i=i, hh=hh, sl=sl, krow=krow, r=r))
    for c in st:
        i, hh = c["i"], c["hh"]
        eg = eg_s[hh, pl.ds(r0 + i, 1), :]
        r = c.pop("r")
        delta = (vblk[i:i + 1, c["sl"]] - eg * r[0:1]) * beta_s[hh, pl.ds(r0 + i, 1), :]
        c["o"] = eg * r[1:2] + qk_s[hh, pl.ds(r0 + i, 1), :] * delta
        krow = c.pop("krow")
        k_hi = krow.astype(BF16).astype(F32)
        d_hi = delta.astype(BF16).astype(F32)
        kp = jnp.where(row8 == 0, k_hi, jnp.where(row8 == 1, krow - k_hi, jnp.where(row8 == 2, k_hi, 0.0)))
        dp = jnp.where(row8 == 0, d_hi, jnp.where(row8 == 1, d_hi, jnp.where(row8 == 2, delta - d_hi, 0.0)))
        c["upd"] = _dot_tn(kp, dp)
    for c in st:
        idx = c["i"] * N_HEADS + c["hh"]
        nssm_ref[idx] = ssm_ref[idx] * eg_s[c["hh"], pl.ds(r0 + c["i"], 1), :] + c.pop("upd")
    o_rows = [jnp.concatenate([c["o"] for c in st[i * N_HEADS:(i + 1) * N_HEADS]], axis=1)
              for i in range(SAMPLE_BLOCK)]
    o_s[pl.ds(r0, SAMPLE_BLOCK), :] = jnp.concatenate(o_rows, axis=0)

    @pl.when(step == pl.num_programs(0) - 1)
    def _():
        x = x_ref[...]
        gate = mod_ref[:, 2 * D_MODEL:3 * D_MODEL]
        yb = _head_rms(o_s[...], onw_ref[...]) * _silu(zb_s[...])
        ymix = jnp.concatenate([ya_s[...], yb], axis=1)
        yo = jnp.dot(ymix.astype(BF16), wout_ref[...], preferred_element_type=F32)
        yn = yo * lax.rsqrt(jnp.mean(yo * yo, axis=-1, keepdims=True) + EPS) * postw_ref[...]
        y_ref[...] = x + gate * yn


def _sample_layer(l, x, mod_s, state_pool, state_conv, state_ssm, lw, prev):
    nseq = x.shape[0]
    depth = lw["win"].shape[0]
    nstep = nseq // SAMPLE_BLOCK
    blk_states = SAMPLE_BLOCK * N_HEADS
    full = lambda shape: pl.BlockSpec(shape, lambda s, _n=len(shape): (0,) * _n)
    ssm_spec = pl.BlockSpec((None, blk_states, HEAD_DIM, HEAD_DIM), lambda s: (l, s, 0, 0))
    in_specs = [
        full((nseq, D_MODEL)), _layer_spec((nseq, 3 * D_MODEL), l),
        _layer_spec((POOL_BUF, nseq, D_A), l), _layer_spec((CONV_K - 1, nseq, D_QKV), l), ssm_spec,
        _layer_spec((1, D_MODEL), l), _layer_spec((1, D_MODEL), l),
        _layer_spec((D_MAIN, D_MODEL), l), _layer_spec((D_MODEL, LANES), l),
        _layer_spec((CONV_K, D_QKV), l), _layer_spec((N_POOL, POOL_GC, POOL_GC), l), _layer_spec((1, D_A), l),
        _layer_spec((1, LANES), l), _layer_spec((1, LANES), l), _layer_spec((1, HEAD_DIM), l),
        _layer_spec((D_MODEL, D_MODEL), l),
    ] + [pl.BlockSpec(memory_space=pl.ANY)] * len(prev)
    out_specs = [
        full((nseq, D_MODEL)), _layer_spec((POOL_BUF, nseq, D_A), l), _layer_spec((CONV_K - 1, nseq, D_QKV), l),
        ssm_spec,
    ]
    out_shape = [
        jax.ShapeDtypeStruct((nseq, D_MODEL), F32),
        jax.ShapeDtypeStruct((depth, POOL_BUF, nseq, D_A), F32),
        jax.ShapeDtypeStruct((depth, CONV_K - 1, nseq, D_QKV), F32),
        jax.ShapeDtypeStruct((depth, nseq * N_HEADS, HEAD_DIM, HEAD_DIM), F32),
    ]
    scratch = [
        pltpu.VMEM((nseq, D_B), F32), pltpu.VMEM((nseq, D_B), F32), pltpu.VMEM((nseq, D_B), F32),
        pltpu.VMEM((nseq, D_B), F32),
        pltpu.VMEM((N_HEADS, nseq, LANES), F32), pltpu.VMEM((N_HEADS, nseq, LANES), F32),
        pltpu.VMEM((N_HEADS, nseq, LANES), F32),
        pltpu.VMEM((nseq, D_A), F32), pltpu.VMEM((nseq, D_B), F32),
    ]
    n_in = len(in_specs) - len(prev)
    return pl.pallas_call(
        functools.partial(_sample_kernel, n_alias=len(prev)),
        grid=(nstep,),
        in_specs=in_specs,
        out_specs=out_specs,
        out_shape=out_shape,
        scratch_shapes=scratch,
        input_output_aliases={n_in + i: 1 + i for i in range(len(prev))},
        compiler_params=pltpu.CompilerParams(dimension_semantics=("arbitrary",), vmem_limit_bytes=VMEM_LIMIT),
        name="sample_layer",
    )(x, mod_s, state_pool, state_conv, state_ssm, lw["prew"], lw["postw"], lw["win"], lw["wba"], lw["convw"],
      lw["poolw"], lw["pscale"], lw["alogc"], lw["dtbc"], lw["onw"], lw["wout"], *prev)


def _constants():
    t = np.arange(CHUNK)[:, None]
    j = np.arange(2 * CHUNK)[None, :]
    band = np.stack([((j <= CHUNK + t) & (j > CHUNK + t - w)) for w in POOL_WINDOWS]).astype(np.float32)
    tri = (np.arange(CHUNK)[None, :] <= t).astype(np.float32)
    ltri = np.concatenate([tri, tri], axis=1)
    return {"band": jnp.asarray(band, BF16), "ltri": jnp.asarray(ltri, BF16)}


def _stacked_weights(rows_p, pre_norm_w, post_norm_w, w_in, conv_w, pool_w, pool_scale, a_log, dt_bias, o_norm_w,
                     w_out):
    depth = w_in.shape[0]
    w_in_t = jnp.swapaxes(w_in, 1, 2)
    wbat = w_in_t[:, D_MAIN:D_MAIN + 2 * N_HEADS]
    pad_c = lambda v: jnp.zeros((depth, 1, LANES), F32).at[:, 0, N_HEADS:2 * N_HEADS].set(v)
    pad_r = lambda v: jnp.broadcast_to(
        jnp.zeros((depth, SUBLANES), F32).at[:, N_HEADS:2 * N_HEADS].set(v)[:, :, None], (depth, SUBLANES, rows_p))
    return {
        "prew": pre_norm_w.reshape(depth, 1, D_MODEL),
        "postw": post_norm_w.reshape(depth, 1, D_MODEL),
        "win": w_in_t[:, 0:D_MAIN].astype(BF16),
        "wba": jnp.zeros((depth, D_MODEL, LANES), F32).at[:, :, 0:2 * N_HEADS].set(
            jnp.swapaxes(wbat, 1, 2)).astype(BF16),
        "wbat": wbat.astype(BF16),
        "convw": conv_w,
        "poolw": pool_w.astype(BF16),
        "pscale": pool_scale.reshape(depth, 1, D_A),
        "alogc": pad_c(a_log), "dtbc": pad_c(dt_bias),
        "alogr": pad_r(a_log), "dtbr": pad_r(dt_bias),
        "onw": o_norm_w.reshape(depth, 1, HEAD_DIM),
        "wout": w_out.astype(BF16),
    }


def kernel(x_prompt, x_sample, c_prompt, c_sample, state_pool, state_conv, state_ssm, w_ada, b_ada, pre_norm_w,
           post_norm_w, w_in, conv_w, pool_w, pool_scale, a_log, dt_bias, o_norm_w, w_out):
    depth = w_in.shape[0]
    nb, seq, _ = x_prompt.shape
    nseq, dec_seq, _ = x_sample.shape
    assert dec_seq == 1 and seq % CHUNK == 0 and nb % 2 == 0 and nseq % SAMPLE_BLOCK == 0
    consts = _constants()
    lw = _stacked_weights(nb * CHUNK, pre_norm_w, post_norm_w, w_in, conv_w, pool_w, pool_scale, a_log, dt_bias,
                          o_norm_w, w_out)
    mod_p, mod_s = _mod_call(c_prompt, c_sample, w_ada, b_ada)
    mod_p = mod_p.reshape(depth, nb, 1, 3 * D_MODEL)
    ssm_in = state_ssm.reshape(depth, nseq * N_HEADS, HEAD_DIM, HEAD_DIM)
    pool_in = jnp.swapaxes(state_pool, 1, 2)
    conv_in = jnp.swapaxes(state_conv, 1, 2)
    yp, ys = x_prompt, x_sample.reshape(nseq, D_MODEL)
    prev_p, prev_s = (), ()
    for l in range(depth):
        yp, *prev_p = _prompt_layer(l, yp, mod_p, lw, consts, tuple(prev_p))
        ys, *prev_s = _sample_layer(l, ys, mod_s, pool_in, conv_in, ssm_in, lw, tuple(prev_s))
    npool_p, nconv_p, nssm_p = prev_p
    npool_s, nconv_s, nssm_s = prev_s
    return (yp, ys.reshape(nseq, dec_seq, D_MODEL), npool_p, nconv_p,
            nssm_p.reshape(depth, nb, N_HEADS, HEAD_DIM, HEAD_DIM),
            jnp.swapaxes(npool_s, 1, 2), jnp.swapaxes(nconv_s, 1, 2),
            nssm_s.reshape(depth, nseq, N_HEADS, HEAD_DIM, HEAD_DIM))
```

```python
import functools

import jax
import jax.numpy as jnp
import numpy as np
from jax import lax
from jax.experimental import pallas as pl
from jax.experimental.pallas import tpu as pltpu

F32 = jnp.float32
BF16 = jnp.bfloat16

D_MODEL = 1024
D_A = 512
D_B = 512
N_POOL = 4
POOL_WINDOWS = (2, 4, 8, 16)
POOL_GC = 128
POOL_BUF = 15
HEAD_DIM = 128
N_HEADS = 4
D_QKV = 3 * D_B
CONV_K = 4
D_MAIN = 2 * D_A + D_QKV + D_B
PAST_LEN = 16384
EPS = 1e-6
CHUNK = 64
GDN_GROUP = 16
SAMPLE_BLOCK = 8
LANES = 128
SUBLANES = 8
QKV_TILES = D_QKV // LANES
VMEM_LIMIT = 56 * 1024 * 1024


def _dot(a, b):
    return jnp.dot(a.astype(BF16), b.astype(BF16), preferred_element_type=F32)


def _dot_tn(a, b):
    return lax.dot_general(a, b, (((0,), (0,)), ((), ())), preferred_element_type=F32)


def _proj(hb, wt_ref, c0, c1):
    return lax.dot_general(hb, wt_ref[c0:c1, :], (((1,), (1,)), ((), ())), preferred_element_type=F32)


def _silu(x):
    return x * jax.nn.sigmoid(x)


def _softplus(x):
    return jnp.maximum(x, 0.0) + jnp.log1p(jnp.exp(-jnp.abs(x)))


def _block_diag(a, b):
    top = jnp.concatenate([a, jnp.zeros((a.shape[0], b.shape[1]), a.dtype)], axis=1)
    bot = jnp.concatenate([jnp.zeros((b.shape[0], a.shape[1]), b.dtype), b], axis=1)
    return jnp.concatenate([top, bot], axis=0)


def _head_rms(o_all, w):
    parts = []
    for hh in range(N_HEADS):
        oh = o_all[:, hh * HEAD_DIM:(hh + 1) * HEAD_DIM]
        parts.append(oh * lax.rsqrt(jnp.mean(oh * oh, axis=-1, keepdims=True) + EPS) * w)
    return jnp.concatenate(parts, axis=1)


def _mod_kernel(cp_ref, cs_ref, w_ref, b_ref, op_ref, os_ref):
    w = w_ref[...]
    b = b_ref[...]
    op_ref[...] = jnp.dot(_silu(cp_ref[...]), w, preferred_element_type=F32) + b
    os_ref[...] = jnp.dot(_silu(cs_ref[...]), w, preferred_element_type=F32) + b


def _mod_call(c_prompt, c_sample, w_ada, b_ada):
    depth, _, n3 = w_ada.shape
    nb, nseq = c_prompt.shape[0], c_sample.shape[0]
    nblk = n3 // D_MODEL
    return pl.pallas_call(
        _mod_kernel,
        grid=(depth, nblk),
        in_specs=[
            pl.BlockSpec((nb, D_MODEL), lambda l, j: (0, 0)),
            pl.BlockSpec((nseq, D_MODEL), lambda l, j: (0, 0)),
            pl.BlockSpec((None, D_MODEL, D_MODEL), lambda l, j: (l, 0, j)),
            pl.BlockSpec((None, 1, D_MODEL), lambda l, j: (l, 0, j)),
        ],
        out_specs=[pl.BlockSpec((None, nb, D_MODEL), lambda l, j: (l, 0, j)),
                   pl.BlockSpec((None, nseq, D_MODEL), lambda l, j: (l, 0, j))],
        out_shape=[jax.ShapeDtypeStruct((depth, nb, n3), F32), jax.ShapeDtypeStruct((depth, nseq, n3), F32)],
        compiler_params=pltpu.CompilerParams(dimension_semantics=("arbitrary", "arbitrary")),
        name="adaln_mod",
    )(c_prompt, c_sample, w_ada, b_ada.reshape(depth, 1, n3))


def _prompt_kernel(x_ref, mod_ref, prew_ref, postw_ref, win_ref, wba_ref, convw_ref, poolw_ref,
                   pscale_ref, alogc_ref, dtbc_ref, alogr_ref, dtbr_ref, onw_ref, wout_ref, band_ref, ltri_ref,
                   *rest, n_alias):
    rest = rest[n_alias:]
    y_ref, npool_ref, nconv_ref, nssm_ref = rest[0:4]
    qkv_ext, ua_ext, s_ref, q_s, k_s, kt_s, v_s, o_s, beta_s, gc_s, eg_s, gcrow_s = rest[4:]
    step = pl.program_id(0)
    nb = x_ref.shape[0]
    rows = nb * CHUNK

    @pl.when(step == 0)
    def _():
        qkv_ext[:, :, 0:SUBLANES, :] = jnp.zeros((QKV_TILES, nb, SUBLANES, LANES), F32)
        ua_ext[:, 0:CHUNK, :] = jnp.zeros((nb, CHUNK, D_A), F32)
        s_ref[...] = jnp.zeros(s_ref.shape, F32)

    x = x_ref[...]
    mod = mod_ref[...]
    shift = mod[:, :, 0:D_MODEL]
    scale = mod[:, :, D_MODEL:2 * D_MODEL]
    gate = mod[:, :, 2 * D_MODEL:3 * D_MODEL]
    a_mul = prew_ref[...][None] * (1.0 + scale)
    ms = jnp.mean(x * x, axis=-1, keepdims=True)
    h = x * lax.rsqrt(ms + EPS) * a_mul + shift
    hb = h.reshape(rows, D_MODEL).astype(BF16)

    ua = _proj(hb, win_ref, 0, D_A)
    ua3 = ua.reshape(nb, CHUNK, D_A)
    ua_ext[:, CHUNK:2 * CHUNK, :] = ua3
    pos = lax.broadcasted_iota(jnp.int32, (CHUNK, POOL_GC), 0) + step * CHUNK
    pooled_groups = []
    for gi, w in enumerate(POOL_WINDOWS):
        cnt = jnp.minimum(pos + 1, w).astype(F32)
        per_b = []
        for b in range(nb):
            ext = ua_ext[b, :, gi * POOL_GC:(gi + 1) * POOL_GC]
            win = jnp.dot(band_ref[gi], ext.astype(BF16), preferred_element_type=F32)
            per_b.append(win / cnt - ua3[b, :, gi * POOL_GC:(gi + 1) * POOL_GC])
        pooled_groups.append(jnp.concatenate(per_b, axis=0))
    ya = jnp.concatenate([_dot(jnp.concatenate(pooled_groups[2 * g2:2 * g2 + 2], axis=1), poolw_ref[g2])
                          for g2 in range(N_POOL // 2)], axis=1)
    za = _proj(hb, win_ref, D_A, 2 * D_A)
    ya = ya * pscale_ref[...] * _silu(za)

    qkv = _proj(hb, win_ref, 2 * D_A, 2 * D_A + D_QKV)
    qkv_t = jnp.stack([qkv[:, c * LANES:(c + 1) * LANES] for c in range(QKV_TILES)], axis=0)
    qkv_t = qkv_t.reshape(QKV_TILES, nb, CHUNK, LANES)
    qkv_ext[:, :, SUBLANES:SUBLANES + CHUNK, :] = qkv_t
    cw = convw_ref[...]
    cw_t = [jnp.stack([cw[j:j + 1, c * LANES:(c + 1) * LANES] for c in range(QKV_TILES)], axis=0)[:, None]
            for j in range(CONV_K)]
    acc = qkv_t * cw_t[CONV_K - 1]
    for j in range(CONV_K - 1):
        acc = acc + qkv_ext[:, :, pl.ds(SUBLANES - (CONV_K - 1) + j, CHUNK), :] * cw_t[j]
    act = _silu(acc).reshape(QKV_TILES, rows, LANES)
    for hh in range(N_HEADS):
        sl = slice(hh * HEAD_DIM, (hh + 1) * HEAD_DIM)
        qh, kh = act[hh], act[N_HEADS + hh]
        q_s[:, sl] = qh * lax.rsqrt(jnp.sum(qh * qh, axis=-1, keepdims=True) + EPS) * (HEAD_DIM ** -0.5)
        kn = kh * lax.rsqrt(jnp.sum(kh * kh, axis=-1, keepdims=True) + EPS)
        k_s[:, sl] = kn
        kt_s[hh] = kn.T
        v_s[:, sl] = act[2 * N_HEADS + hh]

    bac = jnp.dot(hb, wba_ref[...], preferred_element_type=F32)
    bar = bac.T[0:SUBLANES]
    sig = jax.nn.sigmoid(bac)
    gcol = -jnp.exp(alogc_ref[...]) * _softplus(bac + dtbc_ref[...])
    g_hi = gcol.astype(BF16)
    g_lo = (gcol - g_hi.astype(F32)).astype(BF16)
    gc_parts = []
    for b in range(nb):
        rs = slice(b * CHUNK, (b + 1) * CHUNK)
        gc_parts.append(jnp.dot(ltri_ref[...], jnp.concatenate([g_hi[rs], g_lo[rs]], axis=0),
                                preferred_element_type=F32))
    gc = jnp.concatenate(gc_parts, axis=0)
    for hh in range(N_HEADS):
        beta_s[hh] = jnp.broadcast_to(sig[:, hh:hh + 1], (rows, LANES))
        gcb = jnp.broadcast_to(gc[:, N_HEADS + hh:N_HEADS + hh + 1], (rows, LANES))
        gc_s[hh] = gcb
        eg_s[hh] = jnp.exp(gcb)
    grow = -jnp.exp(alogr_ref[...]) * _softplus(bar + dtbr_ref[...])
    lane_in_chunk = lax.broadcasted_iota(jnp.int32, grow.shape, 1) % CHUNK
    sh = 1
    while sh < CHUNK:
        grow = grow + jnp.where(lane_in_chunk >= sh, pltpu.roll(grow, sh, 1), 0.0)
        sh *= 2
    gcrow_s[...] = grow

    ri = lax.broadcasted_iota(jnp.int32, (CHUNK, LANES), 0)
    li = lax.broadcasted_iota(jnp.int32, (CHUNK, LANES), 1)
    lj = li % CHUNK
    lo_half = li < CHUNK
    strict = ri > lj
    causal = ri >= lj
    eye2 = (ri == lj).astype(F32)
    lo_half_hd = lax.broadcasted_iota(jnp.int32, (HEAD_DIM, LANES), 1) < CHUNK

    def pair_mul(xp, yp):
        ybd = jnp.concatenate([jnp.where(lo_half, yp, 0.0), jnp.where(lo_half, 0.0, yp)], axis=0)
        return _dot(xp, ybd)

    chains = [(p, hh) for p in range(nb // 2) for hh in range(N_HEADS)]
    for g0 in range(0, len(chains), GDN_GROUP):
        st = []
        for p, hh in chains[g0:g0 + GDN_GROUP]:
            ra = slice(2 * p * CHUNK, (2 * p + 1) * CHUNK)
            rb = slice((2 * p + 1) * CHUNK, (2 * p + 2) * CHUNK)
            sl = slice(hh * HEAD_DIM, (hh + 1) * HEAD_DIM)
            kba, kbb = k_s[ra, sl] * beta_s[hh, ra, :], k_s[rb, sl] * beta_s[hh, rb, :]
            lhs1 = jnp.concatenate([jnp.concatenate([kba, kbb], axis=1),
                                    jnp.concatenate([q_s[ra, sl], q_s[rb, sl]], axis=1)], axis=0)
            kt = kt_s[hh, :, 2 * p * CHUNK:(2 * p + 2) * CHUNK]
            ktbd = jnp.concatenate([jnp.where(lo_half_hd, kt, 0.0), jnp.where(lo_half_hd, 0.0, kt)], axis=0)
            kkqk = _dot(lhs1, ktbd)
            st.append(dict(ra=ra, rb=rb, sl=sl, hh=hh, p=p, kkqk=kkqk))
        for c in st:
            hh, p = c["hh"], c["p"]
            gcol_p = jnp.where(lo_half, gc_s[hh, c["ra"], :], gc_s[hh, c["rb"], :])
            grow_p = jnp.broadcast_to(gcrow_s[N_HEADS + hh:N_HEADS + hh + 1, 2 * p * CHUNK:(2 * p + 2) * CHUNK],
                                      (CHUNK, LANES))
            dec = jnp.exp(jnp.minimum(gcol_p - grow_p, 0.0))
            kkqk = c.pop("kkqk")
            c["npow"] = jnp.where(strict, kkqk[0:CHUNK] * dec, 0.0)
            c["qkm"] = jnp.where(causal, kkqk[CHUNK:2 * CHUNK] * dec, 0.0)
            c["t"] = eye2 - c["npow"]
        for c in st:
            c["npow"] = pair_mul(c["npow"], c["npow"])
        for _ in range(4):
            for c in st:
                both = pair_mul(jnp.concatenate([c["t"], c["npow"]], axis=0), c["npow"])
                c["t"] = c["t"] + both[0:CHUNK]
                c["npow"] = both[CHUNK:2 * CHUNK]
        for c in st:
            c["t"] = c["t"] + pair_mul(c["t"], c["npow"])
        for c in st:
            ra, rb, sl, hh = c["ra"], c["rb"], c["sl"], c["hh"]
            ba, bb = beta_s[hh, ra, :], beta_s[hh, rb, :]
            kba, kbb = k_s[ra, sl] * ba, k_s[rb, sl] * bb
            rhs = jnp.concatenate([_block_diag(v_s[ra, sl] * ba, v_s[rb, sl] * bb),
                                   _block_diag(kba * eg_s[hh, ra, :], kbb * eg_s[hh, rb, :])], axis=1)
            c["uw"] = _dot(c.pop("t"), rhs)
        for c in st:
            ra, rb, sl, hh, p = c["ra"], c["rb"], c["sl"], c["hh"], c["p"]
            c["ia"], c["ib"] = 2 * p * N_HEADS + hh, (2 * p + 1) * N_HEADS + hh
            lhs3 = jnp.concatenate([c["uw"][:, 2 * HEAD_DIM:4 * HEAD_DIM],
                                    jnp.concatenate([q_s[ra, sl] * eg_s[hh, ra, :], q_s[rb, sl] * eg_s[hh, rb, :]],
                                                    axis=1)], axis=0)
            c["r3"] = _dot(lhs3, _block_diag(s_ref[c["ia"]], s_ref[c["ib"]]))
        for c in st:
            ra, rb, sl, hh, p = c["ra"], c["rb"], c["sl"], c["hh"], c["p"]
            r3 = c.pop("r3")
            vn = c.pop("uw")[:, 0:2 * HEAD_DIM] - r3[0:CHUNK]
            o = r3[CHUNK:2 * CHUNK] + _dot(c.pop("qkm"), _block_diag(vn[:, 0:HEAD_DIM], vn[:, HEAD_DIM:2 * HEAD_DIM]))
            o_s[ra, sl] = o[:, 0:HEAD_DIM]
            o_s[rb, sl] = o[:, HEAD_DIM:2 * HEAD_DIM]
            gla = gc_s[hh, (2 * p + 1) * CHUNK - 1:(2 * p + 1) * CHUNK, :]
            glb = gc_s[hh, (2 * p + 2) * CHUNK - 1:(2 * p + 2) * CHUNK, :]
            kg = jnp.concatenate([k_s[ra, sl] * jnp.exp(gla - gc_s[hh, ra, :]),
                                  k_s[rb, sl] * jnp.exp(glb - gc_s[hh, rb, :])], axis=1)
            upd = _dot_tn(kg, vn)
            s_ref[c["ia"]] = s_ref[c["ia"]] * jnp.exp(gla) + upd[0:HEAD_DIM, 0:HEAD_DIM]
            s_ref[c["ib"]] = s_ref[c["ib"]] * jnp.exp(glb) + upd[HEAD_DIM:2 * HEAD_DIM, HEAD_DIM:2 * HEAD_DIM]

    zb = _proj(hb, win_ref, 2 * D_A + D_QKV, D_MAIN)
    yb = _head_rms(o_s[...], onw_ref[...]) * _silu(zb)
    ymix = jnp.concatenate([ya, yb], axis=1)
    yo = jnp.dot(ymix.astype(BF16), wout_ref[...], preferred_element_type=F32)
    yn = yo * lax.rsqrt(jnp.mean(yo * yo, axis=-1, keepdims=True) + EPS) * postw_ref[...]
    y_ref[...] = x + gate * yn.reshape(nb, CHUNK, D_MODEL)

    qkv_ext[:, :, 0:SUBLANES, :] = qkv_ext[:, :, CHUNK:CHUNK + SUBLANES, :]
    ua_ext[:, 0:CHUNK, :] = ua3

    @pl.when(step == pl.num_programs(0) - 1)
    def _():
        npool_ref[...] = ua3[:, CHUNK - POOL_BUF:CHUNK, :]
        nconv_ref[...] = qkv.reshape(nb, CHUNK, D_QKV)[:, CHUNK - (CONV_K - 1):CHUNK, :]
        nssm_ref[...] = s_ref[...]


def _layer_spec(shape, l):
    return pl.BlockSpec((None,) + tuple(shape), lambda s, _n=len(shape): (l,) + (0,) * _n)


def _prompt_layer(l, x, mod_p, lw, consts, prev):
    nb, seq, _ = x.shape
    depth = lw["win"].shape[0]
    rows = nb * CHUNK
    nstep = seq // CHUNK
    full = lambda shape: pl.BlockSpec(shape, lambda s, _n=len(shape): (0,) * _n)
    in_specs = [
        pl.BlockSpec((nb, CHUNK, D_MODEL), lambda s: (0, s, 0)),
        _layer_spec((nb, 1, 3 * D_MODEL), l),
        _layer_spec((1, D_MODEL), l), _layer_spec((1, D_MODEL), l),
        _layer_spec((D_MAIN, D_MODEL), l), _layer_spec((D_MODEL, LANES), l),
        _layer_spec((CONV_K, D_QKV), l), _layer_spec((N_POOL // 2, 2 * POOL_GC, 2 * POOL_GC), l),
        _layer_spec((1, D_A), l),
        _layer_spec((1, LANES), l), _layer_spec((1, LANES), l),
        _layer_spec((SUBLANES, rows), l), _layer_spec((SUBLANES, rows), l),
        _layer_spec((1, HEAD_DIM), l), _layer_spec((D_MODEL, D_MODEL), l),
        full((N_POOL, CHUNK, 2 * CHUNK)), full((CHUNK, 2 * CHUNK)),
    ] + [pl.BlockSpec(memory_space=pl.ANY)] * len(prev)
    out_specs = [
        pl.BlockSpec((nb, CHUNK, D_MODEL), lambda s: (0, s, 0)),
        _layer_spec((nb, POOL_BUF, D_A), l), _layer_spec((nb, CONV_K - 1, D_QKV), l),
        _layer_spec((nb * N_HEADS, HEAD_DIM, HEAD_DIM), l),
    ]
    out_shape = [
        jax.ShapeDtypeStruct((nb, seq, D_MODEL), F32),
        jax.ShapeDtypeStruct((depth, nb, POOL_BUF, D_A), F32),
        jax.ShapeDtypeStruct((depth, nb, CONV_K - 1, D_QKV), F32),
        jax.ShapeDtypeStruct((depth, nb * N_HEADS, HEAD_DIM, HEAD_DIM), F32),
    ]
    scratch = [
        pltpu.VMEM((QKV_TILES, nb, CHUNK + SUBLANES, LANES), F32),
        pltpu.VMEM((nb, 2 * CHUNK, D_A), F32),
        pltpu.VMEM((nb * N_HEADS, HEAD_DIM, HEAD_DIM), F32),
        pltpu.VMEM((rows, D_B), F32), pltpu.VMEM((rows, D_B), F32), pltpu.VMEM((N_HEADS, HEAD_DIM, rows), F32),
        pltpu.VMEM((rows, D_B), F32), pltpu.VMEM((rows, D_B), F32),
        pltpu.VMEM((N_HEADS, rows, LANES), F32), pltpu.VMEM((N_HEADS, rows, LANES), F32),
        pltpu.VMEM((N_HEADS, rows, LANES), F32),
        pltpu.VMEM((SUBLANES, rows), F32),
    ]
    n_in = len(in_specs) - len(prev)
    return pl.pallas_call(
        functools.partial(_prompt_kernel, n_alias=len(prev)),
        grid=(nstep,),
        in_specs=in_specs,
        out_specs=out_specs,
        out_shape=out_shape,
        scratch_shapes=scratch,
        input_output_aliases={n_in + i: 1 + i for i in range(len(prev))},
        compiler_params=pltpu.CompilerParams(dimension_semantics=("arbitrary",), vmem_limit_bytes=VMEM_LIMIT),
        name="prompt_layer",
    )(x, mod_p, lw["prew"], lw["postw"], lw["win"], lw["wba"], lw["convw"], lw["poolw2"], lw["pscale"],
      lw["alogc"], lw["dtbc"], lw["alogr"], lw["dtbr"], lw["onw"], lw["wout"], consts["band"], consts["ltri"], *prev)


def _sample_kernel(x_ref, mod_ref, spool_ref, sconv_ref, ssm_ref, prew_ref, postw_ref, win_ref, wba_ref, convw_ref,
                   poolw_ref, pscale_ref, alogc_ref, dtbc_ref, onw_ref, wout_ref, *rest, n_alias):
    rest = rest[n_alias:]
    y_ref, npool_ref, nconv_ref, nssm_ref = rest[0:4]
    q_s, k_s, v_s, o_s, beta_s, eg_s, qk_s, ya_s, zb_s = rest[4:]
    step = pl.program_id(0)
    nseq = x_ref.shape[0]

    @pl.when(step == 0)
    def _():
        x = x_ref[...]
        mod = mod_ref[...]
        shift = mod[:, 0:D_MODEL]
        scale = mod[:, D_MODEL:2 * D_MODEL]
        a_mul = prew_ref[...] * (1.0 + scale)
        ms = jnp.mean(x * x, axis=-1, keepdims=True)
        hb = (x * lax.rsqrt(ms + EPS) * a_mul + shift).astype(BF16)

        ua = _proj(hb, win_ref, 0, D_A)
        ya_parts = []
        for gi, w in enumerate(POOL_WINDOWS):
            gs = slice(gi * POOL_GC, (gi + 1) * POOL_GC)
            win = ua[:, gs]
            for d in range(1, w):
                win = win + spool_ref[POOL_BUF - d, :, gs]
            cnt = float(min(PAST_LEN + 1, w))
            pooled = win / cnt - ua[:, gs]
            ya_parts.append(_dot(pooled, poolw_ref[gi]))
        za = _proj(hb, win_ref, D_A, 2 * D_A)
        ya_s[...] = jnp.concatenate(ya_parts, axis=1) * pscale_ref[...] * _silu(za)
        npool_ref[0:POOL_BUF - 1] = spool_ref[1:POOL_BUF]
        npool_ref[POOL_BUF - 1] = ua

        qkv = _proj(hb, win_ref, 2 * D_A, 2 * D_A + D_QKV)
        cw = convw_ref[...]
        acc = qkv * cw[CONV_K - 1:CONV_K]
        for j in range(CONV_K - 1):
            acc = acc + sconv_ref[j] * cw[j:j + 1]
        nconv_ref[0:CONV_K - 2] = sconv_ref[1:CONV_K - 1]
        nconv_ref[CONV_K - 2] = qkv
        qkvc = _silu(acc)
        bac = jnp.dot(hb, wba_ref[...], preferred_element_type=F32)
        sig = jax.nn.sigmoid(bac)
        eg = jnp.exp(-jnp.exp(alogc_ref[...]) * _softplus(bac + dtbc_ref[...]))
        for hh in range(N_HEADS):
            sl = slice(hh * HEAD_DIM, (hh + 1) * HEAD_DIM)
            qh = qkvc[:, hh * HEAD_DIM:(hh + 1) * HEAD_DIM]
            kh = qkvc[:, D_B + hh * HEAD_DIM:D_B + (hh + 1) * HEAD_DIM]
            qn = qh * lax.rsqrt(jnp.sum(qh * qh, axis=-1, keepdims=True) + EPS) * (HEAD_DIM ** -0.5)
            kn = kh * lax.rsqrt(jnp.sum(kh * kh, axis=-1, keepdims=True) + EPS)
            q_s[:, sl] = qn
            k_s[:, sl] = kn
            qk_s[hh] = jnp.broadcast_to(jnp.sum(qn * kn, axis=-1, keepdims=True), (nseq, LANES))
            beta_s[hh] = jnp.broadcast_to(sig[:, hh:hh + 1], (nseq, LANES))
            eg_s[hh] = jnp.broadcast_to(eg[:, N_HEADS + hh:N_HEADS + hh + 1], (nseq, LANES))
        v_s[...] = qkvc[:, 2 * D_B:3 * D_B]
        zb_s[...] = _proj(hb, win_ref, 2 * D_A + D_QKV, D_MAIN)

    r0 = pl.multiple_of(step * SAMPLE_BLOCK, SAMPLE_BLOCK)
    kblk = k_s[pl.ds(r0, SAMPLE_BLOCK), :]
    qblk = q_s[pl.ds(r0, SAMPLE_BLOCK), :]
    vblk = v_s[pl.ds(r0, SAMPLE_BLOCK), :]
    row8 = lax.broadcasted_iota(jnp.int32, (SUBLANES, HEAD_DIM), 0)
    st = []
    for i in range(SAMPLE_BLOCK):
        for hh in range(N_HEADS):
            sl = slice(hh * HEAD_DIM, (hh + 1) * HEAD_DIM)
            krow, qrow = kblk[i:i + 1, sl], qblk[i:i + 1, sl]
            lhs = jnp.where(row8 == 0, krow, jnp.where(row8 == 1, qrow, 0.0))
            r = jnp.dot(lhs, ssm_ref[i * N_HEADS + hh], preferred_element_type=F32)
            st.append(dict(i=i, hh=hh, sl=sl, krow=krow, r=r))
    for c in st:
        i, hh = c["i"], c["hh"]
        eg = eg_s[hh, pl.ds(r0 + i, 1), :]
        r = c.pop("r")
        delta = (vblk[i:i + 1, c["sl"]] - eg * r[0:1]) * beta_s[hh, pl.ds(r0 + i, 1), :]
        c["o"] = eg * r[1:2] + qk_s[hh, pl.ds(r0 + i, 1), :] * delta
        krow = c.pop("krow")
        k_hi = krow.astype(BF16).astype(F32)
        d_hi = delta.astype(BF16).astype(F32)
        kp = jnp.where(row8 == 0, k_hi, jnp.where(row8 == 1, krow - k_hi, jnp.where(row8 == 2, k_hi, 0.0)))
        dp = jnp.where(row8 == 0, d_hi, jnp.where(row8 == 1, d_hi, jnp.where(row8 == 2, delta - d_hi, 0.0)))
        c["upd"] = _dot_tn(kp, dp)
    for c in st:
        idx = c["i"] * N_HEADS + c["hh"]
        nssm_ref[idx] = ssm_ref[idx] * eg_s[c["hh"], pl.ds(r0 + c["i"], 1), :] + c.pop("upd")
    o_rows = [jnp.concatenate([c["o"] for c in st[i * N_HEADS:(i + 1) * N_HEADS]], axis=1)
              for i in range(SAMPLE_BLOCK)]
    o_s[pl.ds(r0, SAMPLE_BLOCK), :] = jnp.concatenate(o_rows, axis=0)

    @pl.when(step == pl.num_programs(0) - 1)
    def _():
        x = x_ref[...]
        gate = mod_ref[:, 2 * D_MODEL:3 * D_MODEL]
        yb = _head_rms(o_s[...], onw_ref[...]) * _silu(zb_s[...])
        ymix = jnp.concatenate([ya_s[...], yb], axis=1)
        yo = jnp.dot(ymix.astype(BF16), wout_ref[...], preferred_element_type=F32)
        yn = yo * lax.rsqrt(jnp.mean(yo * yo, axis=-1, keepdims=True) + EPS) * postw_ref[...]
        y_ref[...] = x + gate * yn


def _sample_layer(l, x, mod_s, state_pool, state_conv, state_ssm, lw, prev):
    nseq = x.shape[0]
    depth = lw["win"].shape[0]
    nstep = nseq // SAMPLE_BLOCK
    blk_states = SAMPLE_BLOCK * N_HEADS
    full = lambda shape: pl.BlockSpec(shape, lambda s, _n=len(shape): (0,) * _n)
    ssm_spec = pl.BlockSpec((None, blk_states, HEAD_DIM, HEAD_DIM), lambda s: (l, s, 0, 0))
    in_specs = [
        full((nseq, D_MODEL)), _layer_spec((nseq, 3 * D_MODEL), l),
        _layer_spec((POOL_BUF, nseq, D_A), l), _layer_spec((CONV_K - 1, nseq, D_QKV), l), ssm_spec,
        _layer_spec((1, D_MODEL), l), _layer_spec((1, D_MODEL), l),
        _layer_spec((D_MAIN, D_MODEL), l), _layer_spec((D_MODEL, LANES), l),
        _layer_spec((CONV_K, D_QKV), l), _layer_spec((N_POOL, POOL_GC, POOL_GC), l), _layer_spec((1, D_A), l),
        _layer_spec((1, LANES), l), _layer_spec((1, LANES), l), _layer_spec((1, HEAD_DIM), l),
        _layer_spec((D_MODEL, D_MODEL), l),
    ] + [pl.BlockSpec(memory_space=pl.ANY)] * len(prev)
    out_specs = [
        full((nseq, D_MODEL)), _layer_spec((POOL_BUF, nseq, D_A), l), _layer_spec((CONV_K - 1, nseq, D_QKV), l),
        ssm_spec,
    ]
    out_shape = [
        jax.ShapeDtypeStruct((nseq, D_MODEL), F32),
        jax.ShapeDtypeStruct((depth, POOL_BUF, nseq, D_A), F32),
        jax.ShapeDtypeStruct((depth, CONV_K - 1, nseq, D_QKV), F32),
        jax.ShapeDtypeStruct((depth, nseq * N_HEADS, HEAD_DIM, HEAD_DIM), F32),
    ]
    scratch = [
        pltpu.VMEM((nseq, D_B), F32), pltpu.VMEM((nseq, D_B), F32), pltpu.VMEM((nseq, D_B), F32),
        pltpu.VMEM((nseq, D_B), F32),
        pltpu.VMEM((N_HEADS, nseq, LANES), F32), pltpu.VMEM((N_HEADS, nseq, LANES), F32),
        pltpu.VMEM((N_HEADS, nseq, LANES), F32),
        pltpu.VMEM((nseq, D_A), F32), pltpu.VMEM((nseq, D_B), F32),
    ]
    n_in = len(in_specs) - len(prev)
    return pl.pallas_call(
        functools.partial(_sample_kernel, n_alias=len(prev)),
        grid=(nstep,),
        in_specs=in_specs,
        out_specs=out_specs,
        out_shape=out_shape,
        scratch_shapes=scratch,
        input_output_aliases={n_in + i: 1 + i for i in range(len(prev))},
        compiler_params=pltpu.CompilerParams(dimension_semantics=("arbitrary",), vmem_limit_bytes=VMEM_LIMIT),
        name="sample_layer",
    )(x, mod_s, state_pool, state_conv, state_ssm, lw["prew"], lw["postw"], lw["win"], lw["wba"], lw["convw"],
      lw["poolw"], lw["pscale"], lw["alogc"], lw["dtbc"], lw["onw"], lw["wout"], *prev)


def _constants():
    t = np.arange(CHUNK)[:, None]
    j = np.arange(2 * CHUNK)[None, :]
    band = np.stack([((j <= CHUNK + t) & (j > CHUNK + t - w)) for w in POOL_WINDOWS]).astype(np.float32)
    tri = (np.arange(CHUNK)[None, :] <= t).astype(np.float32)
    ltri = np.concatenate([tri, tri], axis=1)
    return {"band": jnp.asarray(band, BF16), "ltri": jnp.asarray(ltri, BF16)}


def _stacked_weights(rows_p, pre_norm_w, post_norm_w, w_in, conv_w, pool_w, pool_scale, a_log, dt_bias, o_norm_w,
                     w_out):
    depth = w_in.shape[0]
    w_in_t = jnp.swapaxes(w_in, 1, 2)
    wbat = w_in_t[:, D_MAIN:D_MAIN + 2 * N_HEADS]
    pw2 = jnp.zeros((depth, N_POOL // 2, 2 * POOL_GC, 2 * POOL_GC), F32)
    pw2 = pw2.at[:, :, 0:POOL_GC, 0:POOL_GC].set(pool_w[:, 0::2])
    pw2 = pw2.at[:, :, POOL_GC:2 * POOL_GC, POOL_GC:2 * POOL_GC].set(pool_w[:, 1::2])
    pad_c = lambda v: jnp.zeros((depth, 1, LANES), F32).at[:, 0, N_HEADS:2 * N_HEADS].set(v)
    pad_r = lambda v: jnp.broadcast_to(
        jnp.zeros((depth, SUBLANES), F32).at[:, N_HEADS:2 * N_HEADS].set(v)[:, :, None], (depth, SUBLANES, rows_p))
    return {
        "prew": pre_norm_w.reshape(depth, 1, D_MODEL),
        "postw": post_norm_w.reshape(depth, 1, D_MODEL),
        "win": w_in_t[:, 0:D_MAIN].astype(BF16),
        "wba": jnp.zeros((depth, D_MODEL, LANES), F32).at[:, :, 0:2 * N_HEADS].set(
            jnp.swapaxes(wbat, 1, 2)).astype(BF16),
        "convw": conv_w,
        "poolw": pool_w.astype(BF16),
        "poolw2": pw2.astype(BF16),
        "pscale": pool_scale.reshape(depth, 1, D_A),
        "alogc": pad_c(a_log), "dtbc": pad_c(dt_bias),
        "alogr": pad_r(a_log), "dtbr": pad_r(dt_bias),
        "onw": o_norm_w.reshape(depth, 1, HEAD_DIM),
        "wout": w_out.astype(BF16),
    }


def kernel(x_prompt, x_sample, c_prompt, c_sample, state_pool, state_conv, state_ssm, w_ada, b_ada, pre_norm_w,
           post_norm_w, w_in, conv_w, pool_w, pool_scale, a_log, dt_bias, o_norm_w, w_out):
    depth = w_in.shape[0]
    nb, seq, _ = x_prompt.shape
    nseq, dec_seq, _ = x_sample.shape
    assert dec_seq == 1 and seq % CHUNK == 0 and nb % 2 == 0 and nseq % SAMPLE_BLOCK == 0
    consts = _constants()
    lw = _stacked_weights(nb * CHUNK, pre_norm_w, post_norm_w, w_in, conv_w, pool_w, pool_scale, a_log, dt_bias,
                          o_norm_w, w_out)
    mod_p, mod_s = _mod_call(c_prompt, c_sample, w_ada, b_ada)
    mod_p = mod_p.reshape(depth, nb, 1, 3 * D_MODEL)
    ssm_in = state_ssm.reshape(depth, nseq * N_HEADS, HEAD_DIM, HEAD_DIM)
    pool_in = jnp.swapaxes(state_pool, 1, 2)
    conv_in = jnp.swapaxes(state_conv, 1, 2)
    yp, ys = x_prompt, x_sample.reshape(nseq, D_MODEL)
    prev_p, prev_s = (), ()
    for l in range(depth):
        yp, *prev_p = _prompt_layer(l, yp, mod_p, lw, consts, tuple(prev_p))
        ys, *prev_s = _sample_layer(l, ys, mod_s, pool_in, conv_in, ssm_in, lw, tuple(prev_s))
    npool_p, nconv_p, nssm_p = prev_p
    npool_s, nconv_s, nssm_s = prev_s
    return (yp, ys.reshape(nseq, dec_seq, D_MODEL), npool_p, nconv_p,
            nssm_p.reshape(depth, nb, N_HEADS, HEAD_DIM, HEAD_DIM),
            jnp.swapaxes(npool_s, 1, 2), jnp.swapaxes(nconv_s, 1, 2),
            nssm_s.reshape(depth, nseq, N_HEADS, HEAD_DIM, HEAD_DIM))
```

```python
import functools

import jax
import jax.numpy as jnp
import numpy as np
from jax import lax
from jax.experimental import pallas as pl
from jax.experimental.pallas import tpu as pltpu

F32 = jnp.float32
BF16 = jnp.bfloat16

D_MODEL = 1024
D_A = 512
D_B = 512
N_POOL = 4
POOL_WINDOWS = (2, 4, 8, 16)
POOL_GC = 128
POOL_BUF = 15
HEAD_DIM = 128
N_HEADS = 4
D_QKV = 3 * D_B
CONV_K = 4
D_MAIN = 2 * D_A + D_QKV + D_B
PAST_LEN = 16384
EPS = 1e-6
CHUNK = 64
GDN_GROUP = 16
SAMPLE_BLOCK = 8
LANES = 128
SUBLANES = 8
QKV_TILES = D_QKV // LANES
VMEM_LIMIT = 56 * 1024 * 1024


def _dot(a, b):
    return jnp.dot(a.astype(BF16), b.astype(BF16), preferred_element_type=F32)


def _dot_tn(a, b):
    return lax.dot_general(a, b, (((0,), (0,)), ((), ())), preferred_element_type=F32)


def _proj(hb, wt_ref, c0, c1):
    return lax.dot_general(hb, wt_ref[c0:c1, :], (((1,), (1,)), ((), ())), preferred_element_type=F32)


def _gate_logits(hb, wba_ref):
    w = jnp.concatenate([wba_ref[...], jnp.zeros((LANES - SUBLANES, wba_ref.shape[1]), wba_ref.dtype)], axis=0)
    return lax.dot_general(hb, w, (((1,), (1,)), ((), ())), preferred_element_type=F32)


def _silu(x):
    return x * jax.nn.sigmoid(x)


def _softplus(x):
    return jnp.maximum(x, 0.0) + jnp.log1p(jnp.exp(-jnp.abs(x)))


def _block_diag(a, b):
    top = jnp.concatenate([a, jnp.zeros((a.shape[0], b.shape[1]), a.dtype)], axis=1)
    bot = jnp.concatenate([jnp.zeros((b.shape[0], a.shape[1]), b.dtype), b], axis=1)
    return jnp.concatenate([top, bot], axis=0)


def _head_rms(o_all, w):
    parts = []
    for hh in range(N_HEADS):
        oh = o_all[:, hh * HEAD_DIM:(hh + 1) * HEAD_DIM]
        parts.append(oh * lax.rsqrt(jnp.mean(oh * oh, axis=-1, keepdims=True) + EPS) * w)
    return jnp.concatenate(parts, axis=1)


def _mod_kernel(cp_ref, cs_ref, w_ref, b_ref, op_ref, os_ref):
    w = w_ref[...]
    b = b_ref[...]
    op_ref[...] = jnp.dot(_silu(cp_ref[...]), w, preferred_element_type=F32) + b
    os_ref[...] = jnp.dot(_silu(cs_ref[...]), w, preferred_element_type=F32) + b


def _mod_call(c_prompt, c_sample, w_ada, b_ada):
    depth, _, n3 = w_ada.shape
    nb, nseq = c_prompt.shape[0], c_sample.shape[0]
    nblk = n3 // D_MODEL
    return pl.pallas_call(
        _mod_kernel,
        grid=(depth, nblk),
        in_specs=[
            pl.BlockSpec((nb, D_MODEL), lambda l, j: (0, 0)),
            pl.BlockSpec((nseq, D_MODEL), lambda l, j: (0, 0)),
            pl.BlockSpec((None, D_MODEL, D_MODEL), lambda l, j: (l, 0, j)),
            pl.BlockSpec((None, 1, D_MODEL), lambda l, j: (l, 0, j)),
        ],
        out_specs=[pl.BlockSpec((None, nb, D_MODEL), lambda l, j: (l, 0, j)),
                   pl.BlockSpec((None, nseq, D_MODEL), lambda l, j: (l, 0, j))],
        out_shape=[jax.ShapeDtypeStruct((depth, nb, n3), F32), jax.ShapeDtypeStruct((depth, nseq, n3), F32)],
        compiler_params=pltpu.CompilerParams(dimension_semantics=("arbitrary", "arbitrary")),
        name="adaln_mod",
    )(c_prompt, c_sample, w_ada, b_ada.reshape(depth, 1, n3))


def _prompt_kernel(x_ref, mod_ref, prew_ref, postw_ref, win_ref, wba_ref, convw_ref, poolw_ref,
                   pscale_ref, alogc_ref, dtbc_ref, alogr_ref, dtbr_ref, onw_ref, wout_ref, band_ref, ltri_ref,
                   *rest, n_alias):
    rest = rest[n_alias:]
    y_ref, npool_ref, nconv_ref, nssm_ref = rest[0:4]
    qkv_ext, ua_ext, s_ref, q_s, k_s, kt_s, v_s, o_s, beta_s, gc_s, eg_s, gcrow_s = rest[4:]
    step = pl.program_id(0)
    nb = x_ref.shape[0]
    rows = nb * CHUNK

    @pl.when(step == 0)
    def _():
        qkv_ext[:, :, 0:SUBLANES, :] = jnp.zeros((QKV_TILES, nb, SUBLANES, LANES), F32)
        ua_ext[:, 0:CHUNK, :] = jnp.zeros((nb, CHUNK, D_A), F32)
        s_ref[...] = jnp.zeros(s_ref.shape, F32)

    x = x_ref[...]
    mod = mod_ref[...]
    shift = mod[:, :, 0:D_MODEL]
    scale = mod[:, :, D_MODEL:2 * D_MODEL]
    gate = mod[:, :, 2 * D_MODEL:3 * D_MODEL]
    a_mul = prew_ref[...][None] * (1.0 + scale)
    ms = jnp.mean(x * x, axis=-1, keepdims=True)
    h = x * lax.rsqrt(ms + EPS) * a_mul + shift
    hb = h.reshape(rows, D_MODEL).astype(BF16)

    ua = _proj(hb, win_ref, 0, D_A)
    ua3 = ua.reshape(nb, CHUNK, D_A)
    ua_ext[:, CHUNK:2 * CHUNK, :] = ua3
    pos = lax.broadcasted_iota(jnp.int32, (CHUNK, POOL_GC), 0) + step * CHUNK
    pooled_groups = []
    for gi, w in enumerate(POOL_WINDOWS):
        cnt = jnp.minimum(pos + 1, w).astype(F32)
        per_b = []
        for b in range(nb):
            ext = ua_ext[b, :, gi * POOL_GC:(gi + 1) * POOL_GC]
            win = jnp.dot(band_ref[gi], ext.astype(BF16), preferred_element_type=F32)
            per_b.append(win / cnt - ua3[b, :, gi * POOL_GC:(gi + 1) * POOL_GC])
        pooled_groups.append(jnp.concatenate(per_b, axis=0))
    ya = jnp.concatenate([_dot(jnp.concatenate(pooled_groups[2 * g2:2 * g2 + 2], axis=1), poolw_ref[g2])
                          for g2 in range(N_POOL // 2)], axis=1)
    za = _proj(hb, win_ref, D_A, 2 * D_A)
    ya = ya * pscale_ref[...] * _silu(za)

    qkv = _proj(hb, win_ref, 2 * D_A, 2 * D_A + D_QKV)
    qkv_t = jnp.stack([qkv[:, c * LANES:(c + 1) * LANES] for c in range(QKV_TILES)], axis=0)
    qkv_t = qkv_t.reshape(QKV_TILES, nb, CHUNK, LANES)
    qkv_ext[:, :, SUBLANES:SUBLANES + CHUNK, :] = qkv_t
    cw = convw_ref[...]
    cw_t = [jnp.stack([cw[j:j + 1, c * LANES:(c + 1) * LANES] for c in range(QKV_TILES)], axis=0)[:, None]
            for j in range(CONV_K)]
    acc = qkv_t * cw_t[CONV_K - 1]
    for j in range(CONV_K - 1):
        acc = acc + qkv_ext[:, :, pl.ds(SUBLANES - (CONV_K - 1) + j, CHUNK), :] * cw_t[j]
    act = _silu(acc).reshape(QKV_TILES, rows, LANES)
    for hh in range(N_HEADS):
        sl = slice(hh * HEAD_DIM, (hh + 1) * HEAD_DIM)
        qh, kh = act[hh], act[N_HEADS + hh]
        q_s[:, sl] = qh * lax.rsqrt(jnp.sum(qh * qh, axis=-1, keepdims=True) + EPS) * (HEAD_DIM ** -0.5)
        kn = kh * lax.rsqrt(jnp.sum(kh * kh, axis=-1, keepdims=True) + EPS)
        k_s[:, sl] = kn
        kt_s[hh] = kn.T
        v_s[:, sl] = act[2 * N_HEADS + hh]

    bac = _gate_logits(hb, wba_ref)
    bar = bac.T[0:SUBLANES]
    sig = jax.nn.sigmoid(bac)
    gcol = -jnp.exp(alogc_ref[...]) * _softplus(bac + dtbc_ref[...])
    g_hi = gcol.astype(BF16)
    g_lo = (gcol - g_hi.astype(F32)).astype(BF16)
    gc_parts = []
    for b in range(nb):
        rs = slice(b * CHUNK, (b + 1) * CHUNK)
        gc_parts.append(jnp.dot(ltri_ref[...], jnp.concatenate([g_hi[rs], g_lo[rs]], axis=0),
                                preferred_element_type=F32))
    gc = jnp.concatenate(gc_parts, axis=0)
    for hh in range(N_HEADS):
        beta_s[hh] = jnp.broadcast_to(sig[:, hh:hh + 1], (rows, LANES))
        gcb = jnp.broadcast_to(gc[:, N_HEADS + hh:N_HEADS + hh + 1], (rows, LANES))
        gc_s[hh] = gcb
        eg_s[hh] = jnp.exp(gcb)
    grow = -jnp.exp(alogr_ref[...]) * _softplus(bar + dtbr_ref[...])
    lane_in_chunk = lax.broadcasted_iota(jnp.int32, grow.shape, 1) % CHUNK
    sh = 1
    while sh < CHUNK:
        grow = grow + jnp.where(lane_in_chunk >= sh, pltpu.roll(grow, sh, 1), 0.0)
        sh *= 2
    gcrow_s[...] = grow

    ri = lax.broadcasted_iota(jnp.int32, (CHUNK, LANES), 0)
    li = lax.broadcasted_iota(jnp.int32, (CHUNK, LANES), 1)
    lj = li % CHUNK
    lo_half = li < CHUNK
    strict = ri > lj
    causal = ri >= lj
    eye2 = (ri == lj).astype(F32)
    lo_half_hd = lax.broadcasted_iota(jnp.int32, (HEAD_DIM, LANES), 1) < CHUNK

    def pair_mul(xp, yp):
        ybd = jnp.concatenate([jnp.where(lo_half, yp, 0.0), jnp.where(lo_half, 0.0, yp)], axis=0)
        return _dot(xp, ybd)

    chains = [(p, hh) for p in range(nb // 2) for hh in range(N_HEADS)]
    for g0 in range(0, len(chains), GDN_GROUP):
        st = []
        for p, hh in chains[g0:g0 + GDN_GROUP]:
            ra = slice(2 * p * CHUNK, (2 * p + 1) * CHUNK)
            rb = slice((2 * p + 1) * CHUNK, (2 * p + 2) * CHUNK)
            sl = slice(hh * HEAD_DIM, (hh + 1) * HEAD_DIM)
            kba, kbb = k_s[ra, sl] * beta_s[hh, ra, :], k_s[rb, sl] * beta_s[hh, rb, :]
            lhs1 = jnp.concatenate([jnp.concatenate([kba, kbb], axis=1),
                                    jnp.concatenate([q_s[ra, sl], q_s[rb, sl]], axis=1)], axis=0)
            kt = kt_s[hh, :, 2 * p * CHUNK:(2 * p + 2) * CHUNK]
            ktbd = jnp.concatenate([jnp.where(lo_half_hd, kt, 0.0), jnp.where(lo_half_hd, 0.0, kt)], axis=0)
            kkqk = _dot(lhs1, ktbd)
            st.append(dict(ra=ra, rb=rb, sl=sl, hh=hh, p=p, kkqk=kkqk))
        for c in st:
            hh, p = c["hh"], c["p"]
            gcol_p = jnp.where(lo_half, gc_s[hh, c["ra"], :], gc_s[hh, c["rb"], :])
            grow_p = jnp.broadcast_to(gcrow_s[N_HEADS + hh:N_HEADS + hh + 1, 2 * p * CHUNK:(2 * p + 2) * CHUNK],
                                      (CHUNK, LANES))
            dec = jnp.exp(jnp.minimum(gcol_p - grow_p, 0.0))
            kkqk = c.pop("kkqk")
            c["npow"] = jnp.where(strict, kkqk[0:CHUNK] * dec, 0.0)
            c["qkm"] = jnp.where(causal, kkqk[CHUNK:2 * CHUNK] * dec, 0.0)
            c["t"] = eye2 - c["npow"]
        for c in st:
            c["npow"] = pair_mul(c["npow"], c["npow"])
        for _ in range(4):
            for c in st:
                both = pair_mul(jnp.concatenate([c["t"], c["npow"]], axis=0), c["npow"])
                c["t"] = c["t"] + both[0:CHUNK]
                c["npow"] = both[CHUNK:2 * CHUNK]
        for c in st:
            c["t"] = c["t"] + pair_mul(c["t"], c["npow"])
        for c in st:
            ra, rb, sl, hh = c["ra"], c["rb"], c["sl"], c["hh"]
            ba, bb = beta_s[hh, ra, :], beta_s[hh, rb, :]
            kba, kbb = k_s[ra, sl] * ba, k_s[rb, sl] * bb
            rhs = jnp.concatenate([_block_diag(v_s[ra, sl] * ba, v_s[rb, sl] * bb),
                                   _block_diag(kba * eg_s[hh, ra, :], kbb * eg_s[hh, rb, :])], axis=1)
            c["uw"] = _dot(c.pop("t"), rhs)
        for c in st:
            ra, rb, sl, hh, p = c["ra"], c["rb"], c["sl"], c["hh"], c["p"]
            c["ia"], c["ib"] = 2 * p * N_HEADS + hh, (2 * p + 1) * N_HEADS + hh
            lhs3 = jnp.concatenate([c["uw"][:, 2 * HEAD_DIM:4 * HEAD_DIM],
                                    jnp.concatenate([q_s[ra, sl] * eg_s[hh, ra, :], q_s[rb, sl] * eg_s[hh, rb, :]],
                                                    axis=1)], axis=0)
            c["r3"] = _dot(lhs3, _block_diag(s_ref[c["ia"]], s_ref[c["ib"]]))
        for c in st:
            ra, rb, sl, hh, p = c["ra"], c["rb"], c["sl"], c["hh"], c["p"]
            r3 = c.pop("r3")
            vn = c.pop("uw")[:, 0:2 * HEAD_DIM] - r3[0:CHUNK]
            o = r3[CHUNK:2 * CHUNK] + _dot(c.pop("qkm"), _block_diag(vn[:, 0:HEAD_DIM], vn[:, HEAD_DIM:2 * HEAD_DIM]))
            o_s[ra, sl] = o[:, 0:HEAD_DIM]
            o_s[rb, sl] = o[:, HEAD_DIM:2 * HEAD_DIM]
            gla = gc_s[hh, (2 * p + 1) * CHUNK - 1:(2 * p + 1) * CHUNK, :]
            glb = gc_s[hh, (2 * p + 2) * CHUNK - 1:(2 * p + 2) * CHUNK, :]
            kg = jnp.concatenate([k_s[ra, sl] * jnp.exp(gla - gc_s[hh, ra, :]),
                                  k_s[rb, sl] * jnp.exp(glb - gc_s[hh, rb, :])], axis=1)
            upd = _dot_tn(kg, vn)
            s_ref[c["ia"]] = s_ref[c["ia"]] * jnp.exp(gla) + upd[0:HEAD_DIM, 0:HEAD_DIM]
            s_ref[c["ib"]] = s_ref[c["ib"]] * jnp.exp(glb) + upd[HEAD_DIM:2 * HEAD_DIM, HEAD_DIM:2 * HEAD_DIM]

    zb = _proj(hb, win_ref, 2 * D_A + D_QKV, D_MAIN)
    yb = _head_rms(o_s[...], onw_ref[...]) * _silu(zb)
    ymix = jnp.concatenate([ya, yb], axis=1)
    yo = jnp.dot(ymix.astype(BF16), wout_ref[...], preferred_element_type=F32)
    yn = yo * lax.rsqrt(jnp.mean(yo * yo, axis=-1, keepdims=True) + EPS) * postw_ref[...]
    y_ref[...] = x + gate * yn.reshape(nb, CHUNK, D_MODEL)

    qkv_ext[:, :, 0:SUBLANES, :] = qkv_ext[:, :, CHUNK:CHUNK + SUBLANES, :]
    ua_ext[:, 0:CHUNK, :] = ua3

    @pl.when(step == pl.num_programs(0) - 1)
    def _():
        npool_ref[...] = ua3[:, CHUNK - POOL_BUF:CHUNK, :]
        nconv_ref[...] = qkv.reshape(nb, CHUNK, D_QKV)[:, CHUNK - (CONV_K - 1):CHUNK, :]
        nssm_ref[...] = s_ref[...]


def _layer_spec(shape, l):
    return pl.BlockSpec((None,) + tuple(shape), lambda s, _n=len(shape): (l,) + (0,) * _n)


def _prompt_layer(l, x, mod_p, lw, consts, prev):
    nb, seq, _ = x.shape
    depth = lw["win"].shape[0]
    rows = nb * CHUNK
    nstep = seq // CHUNK
    full = lambda shape: pl.BlockSpec(shape, lambda s, _n=len(shape): (0,) * _n)
    in_specs = [
        pl.BlockSpec((nb, CHUNK, D_MODEL), lambda s: (0, s, 0)),
        _layer_spec((nb, 1, 3 * D_MODEL), l),
        _layer_spec((1, D_MODEL), l), _layer_spec((1, D_MODEL), l),
        pl.BlockSpec((None, D_MAIN, D_MODEL), lambda s: (l, 0, 0)),
        pl.BlockSpec((None, SUBLANES, D_MODEL), lambda s: (l, D_MAIN // SUBLANES, 0)),
        _layer_spec((CONV_K, D_QKV), l), _layer_spec((N_POOL // 2, 2 * POOL_GC, 2 * POOL_GC), l),
        _layer_spec((1, D_A), l),
        _layer_spec((1, LANES), l), _layer_spec((1, LANES), l),
        _layer_spec((SUBLANES, rows), l), _layer_spec((SUBLANES, rows), l),
        _layer_spec((1, HEAD_DIM), l), _layer_spec((D_MODEL, D_MODEL), l),
        full((N_POOL, CHUNK, 2 * CHUNK)), full((CHUNK, 2 * CHUNK)),
    ] + [pl.BlockSpec(memory_space=pl.ANY)] * len(prev)
    out_specs = [
        pl.BlockSpec((nb, CHUNK, D_MODEL), lambda s: (0, s, 0)),
        _layer_spec((nb, POOL_BUF, D_A), l), _layer_spec((nb, CONV_K - 1, D_QKV), l),
        _layer_spec((nb * N_HEADS, HEAD_DIM, HEAD_DIM), l),
    ]
    out_shape = [
        jax.ShapeDtypeStruct((nb, seq, D_MODEL), F32),
        jax.ShapeDtypeStruct((depth, nb, POOL_BUF, D_A), F32),
        jax.ShapeDtypeStruct((depth, nb, CONV_K - 1, D_QKV), F32),
        jax.ShapeDtypeStruct((depth, nb * N_HEADS, HEAD_DIM, HEAD_DIM), F32),
    ]
    scratch = [
        pltpu.VMEM((QKV_TILES, nb, CHUNK + SUBLANES, LANES), F32),
        pltpu.VMEM((nb, 2 * CHUNK, D_A), F32),
        pltpu.VMEM((nb * N_HEADS, HEAD_DIM, HEAD_DIM), F32),
        pltpu.VMEM((rows, D_B), F32), pltpu.VMEM((rows, D_B), F32), pltpu.VMEM((N_HEADS, HEAD_DIM, rows), F32),
        pltpu.VMEM((rows, D_B), F32), pltpu.VMEM((rows, D_B), F32),
        pltpu.VMEM((N_HEADS, rows, LANES), F32), pltpu.VMEM((N_HEADS, rows, LANES), F32),
        pltpu.VMEM((N_HEADS, rows, LANES), F32),
        pltpu.VMEM((SUBLANES, rows), F32),
    ]
    n_in = len(in_specs) - len(prev)
    return pl.pallas_call(
        functools.partial(_prompt_kernel, n_alias=len(prev)),
        grid=(nstep,),
        in_specs=in_specs,
        out_specs=out_specs,
        out_shape=out_shape,
        scratch_shapes=scratch,
        input_output_aliases={n_in + i: 1 + i for i in range(len(prev))},
        compiler_params=pltpu.CompilerParams(dimension_semantics=("arbitrary",), vmem_limit_bytes=VMEM_LIMIT),
        name="prompt_layer",
    )(x, mod_p, lw["prew"], lw["postw"], lw["win"], lw["win"], lw["convw"], lw["poolw2"], lw["pscale"],
      lw["alogc"], lw["dtbc"], lw["alogr"], lw["dtbr"], lw["onw"], lw["wout"], consts["band"], consts["ltri"], *prev)


def _sample_kernel(x_ref, mod_ref, spool_ref, sconv_ref, ssm_ref, prew_ref, postw_ref, win_ref, wba_ref, convw_ref,
                   poolw_ref, pscale_ref, alogc_ref, dtbc_ref, onw_ref, wout_ref, *rest, n_alias):
    rest = rest[n_alias:]
    y_ref, npool_ref, nconv_ref, nssm_ref = rest[0:4]
    q_s, k_s, v_s, o_s, beta_s, eg_s, qk_s, ya_s, zb_s = rest[4:]
    step = pl.program_id(0)
    nseq = x_ref.shape[0]

    @pl.when(step == 0)
    def _():
        x = x_ref[...]
        mod = mod_ref[...]
        shift = mod[:, 0:D_MODEL]
        scale = mod[:, D_MODEL:2 * D_MODEL]
        a_mul = prew_ref[...] * (1.0 + scale)
        ms = jnp.mean(x * x, axis=-1, keepdims=True)
        hb = (x * lax.rsqrt(ms + EPS) * a_mul + shift).astype(BF16)

        ua = _proj(hb, win_ref, 0, D_A)
        ya_parts = []
        for gi, w in enumerate(POOL_WINDOWS):
            gs = slice(gi * POOL_GC, (gi + 1) * POOL_GC)
            win = ua[:, gs]
            for d in range(1, w):
                win = win + spool_ref[POOL_BUF - d, :, gs]
            cnt = float(min(PAST_LEN + 1, w))
            pooled = win / cnt - ua[:, gs]
            ya_parts.append(_dot(pooled, poolw_ref[gi]))
        za = _proj(hb, win_ref, D_A, 2 * D_A)
        ya_s[...] = jnp.concatenate(ya_parts, axis=1) * pscale_ref[...] * _silu(za)
        npool_ref[0:POOL_BUF - 1] = spool_ref[1:POOL_BUF]
        npool_ref[POOL_BUF - 1] = ua

        qkv = _proj(hb, win_ref, 2 * D_A, 2 * D_A + D_QKV)
        cw = convw_ref[...]
        acc = qkv * cw[CONV_K - 1:CONV_K]
        for j in range(CONV_K - 1):
            acc = acc + sconv_ref[j] * cw[j:j + 1]
        nconv_ref[0:CONV_K - 2] = sconv_ref[1:CONV_K - 1]
        nconv_ref[CONV_K - 2] = qkv
        qkvc = _silu(acc)
        bac = _gate_logits(hb, wba_ref)
        sig = jax.nn.sigmoid(bac)
        eg = jnp.exp(-jnp.exp(alogc_ref[...]) * _softplus(bac + dtbc_ref[...]))
        for hh in range(N_HEADS):
            sl = slice(hh * HEAD_DIM, (hh + 1) * HEAD_DIM)
            qh = qkvc[:, hh * HEAD_DIM:(hh + 1) * HEAD_DIM]
            kh = qkvc[:, D_B + hh * HEAD_DIM:D_B + (hh + 1) * HEAD_DIM]
            qn = qh * lax.rsqrt(jnp.sum(qh * qh, axis=-1, keepdims=True) + EPS) * (HEAD_DIM ** -0.5)
            kn = kh * lax.rsqrt(jnp.sum(kh * kh, axis=-1, keepdims=True) + EPS)
            q_s[:, sl] = qn
            k_s[:, sl] = kn
            qk_s[hh] = jnp.broadcast_to(jnp.sum(qn * kn, axis=-1, keepdims=True), (nseq, LANES))
            beta_s[hh] = jnp.broadcast_to(sig[:, hh:hh + 1], (nseq, LANES))
            eg_s[hh] = jnp.broadcast_to(eg[:, N_HEADS + hh:N_HEADS + hh + 1], (nseq, LANES))
        v_s[...] = qkvc[:, 2 * D_B:3 * D_B]
        zb_s[...] = _proj(hb, win_ref, 2 * D_A + D_QKV, D_MAIN)

    r0 = pl.multiple_of(step * SAMPLE_BLOCK, SAMPLE_BLOCK)
    kblk = k_s[pl.ds(r0, SAMPLE_BLOCK), :]
    qblk = q_s[pl.ds(r0, SAMPLE_BLOCK), :]
    vblk = v_s[pl.ds(r0, SAMPLE_BLOCK), :]
    row8 = lax.broadcasted_iota(jnp.int32, (SUBLANES, HEAD_DIM), 0)
    st = []
    for i in range(SAMPLE_BLOCK):
        for hh in range(N_HEADS):
            sl = slice(hh * HEAD_DIM, (hh + 1) * HEAD_DIM)
            krow, qrow = kblk[i:i + 1, sl], qblk[i:i + 1, sl]
            lhs = jnp.where(row8 == 0, krow, jnp.where(row8 == 1, qrow, 0.0))
            r = jnp.dot(lhs, ssm_ref[i * N_HEADS + hh], preferred_element_type=F32)
            st.append(dict(i=i, hh=hh, sl=sl, krow=krow, r=r))
    for c in st:
        i, hh = c["i"], c["hh"]
        eg = eg_s[hh, pl.ds(r0 + i, 1), :]
        r = c.pop("r")
        delta = (vblk[i:i + 1, c["sl"]] - eg * r[0:1]) * beta_s[hh, pl.ds(r0 + i, 1), :]
        c["o"] = eg * r[1:2] + qk_s[hh, pl.ds(r0 + i, 1), :] * delta
        krow = c.pop("krow")
        k_hi = krow.astype(BF16).astype(F32)
        d_hi = delta.astype(BF16).astype(F32)
        kp = jnp.where(row8 == 0, k_hi, jnp.where(row8 == 1, krow - k_hi, jnp.where(row8 == 2, k_hi, 0.0)))
        dp = jnp.where(row8 == 0, d_hi, jnp.where(row8 == 1, d_hi, jnp.where(row8 == 2, delta - d_hi, 0.0)))
        c["upd"] = _dot_tn(kp, dp)
    for c in st:
        idx = c["i"] * N_HEADS + c["hh"]
        nssm_ref[idx] = ssm_ref[idx] * eg_s[c["hh"], pl.ds(r0 + c["i"], 1), :] + c.pop("upd")
    o_rows = [jnp.concatenate([c["o"] for c in st[i * N_HEADS:(i + 1) * N_HEADS]], axis=1)
              for i in range(SAMPLE_BLOCK)]
    o_s[pl.ds(r0, SAMPLE_BLOCK), :] = jnp.concatenate(o_rows, axis=0)

    @pl.when(step == pl.num_programs(0) - 1)
    def _():
        x = x_ref[...]
        gate = mod_ref[:, 2 * D_MODEL:3 * D_MODEL]
        yb = _head_rms(o_s[...], onw_ref[...]) * _silu(zb_s[...])
        ymix = jnp.concatenate([ya_s[...], yb], axis=1)
        yo = jnp.dot(ymix.astype(BF16), wout_ref[...], preferred_element_type=F32)
        yn = yo * lax.rsqrt(jnp.mean(yo * yo, axis=-1, keepdims=True) + EPS) * postw_ref[...]
        y_ref[...] = x + gate * yn


def _sample_layer(l, x, mod_s, state_pool, state_conv, state_ssm, lw, prev):
    nseq = x.shape[0]
    depth = lw["win"].shape[0]
    nstep = nseq // SAMPLE_BLOCK
    blk_states = SAMPLE_BLOCK * N_HEADS
    full = lambda shape: pl.BlockSpec(shape, lambda s, _n=len(shape): (0,) * _n)
    ssm_spec = pl.BlockSpec((None, blk_states, HEAD_DIM, HEAD_DIM), lambda s: (l, s, 0, 0))
    in_specs = [
        full((nseq, D_MODEL)), _layer_spec((nseq, 3 * D_MODEL), l),
        _layer_spec((POOL_BUF, nseq, D_A), l), _layer_spec((CONV_K - 1, nseq, D_QKV), l), ssm_spec,
        _layer_spec((1, D_MODEL), l), _layer_spec((1, D_MODEL), l),
        pl.BlockSpec((None, D_MAIN, D_MODEL), lambda s: (l, 0, 0)),
        pl.BlockSpec((None, SUBLANES, D_MODEL), lambda s: (l, D_MAIN // SUBLANES, 0)),
        _layer_spec((CONV_K, D_QKV), l), _layer_spec((N_POOL, POOL_GC, POOL_GC), l), _layer_spec((1, D_A), l),
        _layer_spec((1, LANES), l), _layer_spec((1, LANES), l), _layer_spec((1, HEAD_DIM), l),
        _layer_spec((D_MODEL, D_MODEL), l),
    ] + [pl.BlockSpec(memory_space=pl.ANY)] * len(prev)
    out_specs = [
        full((nseq, D_MODEL)), _layer_spec((POOL_BUF, nseq, D_A), l), _layer_spec((CONV_K - 1, nseq, D_QKV), l),
        ssm_spec,
    ]
    out_shape = [
        jax.ShapeDtypeStruct((nseq, D_MODEL), F32),
        jax.ShapeDtypeStruct((depth, POOL_BUF, nseq, D_A), F32),
        jax.ShapeDtypeStruct((depth, CONV_K - 1, nseq, D_QKV), F32),
        jax.ShapeDtypeStruct((depth, nseq * N_HEADS, HEAD_DIM, HEAD_DIM), F32),
    ]
    scratch = [
        pltpu.VMEM((nseq, D_B), F32), pltpu.VMEM((nseq, D_B), F32), pltpu.VMEM((nseq, D_B), F32),
        pltpu.VMEM((nseq, D_B), F32),
        pltpu.VMEM((N_HEADS, nseq, LANES), F32), pltpu.VMEM((N_HEADS, nseq, LANES), F32),
        pltpu.VMEM((N_HEADS, nseq, LANES), F32),
        pltpu.VMEM((nseq, D_A), F32), pltpu.VMEM((nseq, D_B), F32),
    ]
    n_in = len(in_specs) - len(prev)
    return pl.pallas_call(
        functools.partial(_sample_kernel, n_alias=len(prev)),
        grid=(nstep,),
        in_specs=in_specs,
        out_specs=out_specs,
        out_shape=out_shape,
        scratch_shapes=scratch,
        input_output_aliases={n_in + i: 1 + i for i in range(len(prev))},
        compiler_params=pltpu.CompilerParams(dimension_semantics=("arbitrary",), vmem_limit_bytes=VMEM_LIMIT),
        name="sample_layer",
    )(x, mod_s, state_pool, state_conv, state_ssm, lw["prew"], lw["postw"], lw["win"], lw["win"], lw["convw"],
      lw["poolw"], lw["pscale"], lw["alogc"], lw["dtbc"], lw["onw"], lw["wout"], *prev)


def _constants():
    t = np.arange(CHUNK)[:, None]
    j = np.arange(2 * CHUNK)[None, :]
    band = np.stack([((j <= CHUNK + t) & (j > CHUNK + t - w)) for w in POOL_WINDOWS]).astype(np.float32)
    tri = (np.arange(CHUNK)[None, :] <= t).astype(np.float32)
    ltri = np.concatenate([tri, tri], axis=1)
    return {"band": jnp.asarray(band, BF16), "ltri": jnp.asarray(ltri, BF16)}


def _stacked_weights(rows_p, pre_norm_w, post_norm_w, w_in, conv_w, pool_w, pool_scale, a_log, dt_bias, o_norm_w,
                     w_out):
    depth = w_in.shape[0]
    w_in_t = jnp.swapaxes(w_in, 1, 2)
    pw2 = jnp.zeros((depth, N_POOL // 2, 2 * POOL_GC, 2 * POOL_GC), F32)
    pw2 = pw2.at[:, :, 0:POOL_GC, 0:POOL_GC].set(pool_w[:, 0::2])
    pw2 = pw2.at[:, :, POOL_GC:2 * POOL_GC, POOL_GC:2 * POOL_GC].set(pool_w[:, 1::2])
    pad_c = lambda v: jnp.zeros((depth, 1, LANES), F32).at[:, 0, N_HEADS:2 * N_HEADS].set(v)
    pad_r = lambda v: jnp.broadcast_to(
        jnp.zeros((depth, SUBLANES), F32).at[:, N_HEADS:2 * N_HEADS].set(v)[:, :, None], (depth, SUBLANES, rows_p))
    return {
        "prew": pre_norm_w.reshape(depth, 1, D_MODEL),
        "postw": post_norm_w.reshape(depth, 1, D_MODEL),
        "win": w_in_t.astype(BF16),
        "convw": conv_w,
        "poolw": pool_w.astype(BF16),
        "poolw2": pw2.astype(BF16),
        "pscale": pool_scale.reshape(depth, 1, D_A),
        "alogc": pad_c(a_log), "dtbc": pad_c(dt_bias),
        "alogr": pad_r(a_log), "dtbr": pad_r(dt_bias),
        "onw": o_norm_w.reshape(depth, 1, HEAD_DIM),
        "wout": w_out.astype(BF16),
    }


def kernel(x_prompt, x_sample, c_prompt, c_sample, state_pool, state_conv, state_ssm, w_ada, b_ada, pre_norm_w,
           post_norm_w, w_in, conv_w, pool_w, pool_scale, a_log, dt_bias, o_norm_w, w_out):
    depth = w_in.shape[0]
    nb, seq, _ = x_prompt.shape
    nseq, dec_seq, _ = x_sample.shape
    assert dec_seq == 1 and seq % CHUNK == 0 and nb % 2 == 0 and nseq % SAMPLE_BLOCK == 0
    consts = _constants()
    lw = _stacked_weights(nb * CHUNK, pre_norm_w, post_norm_w, w_in, conv_w, pool_w, pool_scale, a_log, dt_bias,
                          o_norm_w, w_out)
    mod_p, mod_s = _mod_call(c_prompt, c_sample, w_ada, b_ada)
    mod_p = mod_p.reshape(depth, nb, 1, 3 * D_MODEL)
    ssm_in = state_ssm.reshape(depth, nseq * N_HEADS, HEAD_DIM, HEAD_DIM)
    pool_in = jnp.swapaxes(state_pool, 1, 2)
    conv_in = jnp.swapaxes(state_conv, 1, 2)
    yp, ys = x_prompt, x_sample.reshape(nseq, D_MODEL)
    prev_p, prev_s = (), ()
    for l in range(depth):
        yp, *prev_p = _prompt_layer(l, yp, mod_p, lw, consts, tuple(prev_p))
        ys, *prev_s = _sample_layer(l, ys, mod_s, pool_in, conv_in, ssm_in, lw, tuple(prev_s))
    npool_p, nconv_p, nssm_p = prev_p
    npool_s, nconv_s, nssm_s = prev_s
    return (yp, ys.reshape(nseq, dec_seq, D_MODEL), npool_p, nconv_p,
            nssm_p.reshape(depth, nb, N_HEADS, HEAD_DIM, HEAD_DIM),
            jnp.swapaxes(npool_s, 1, 2), jnp.swapaxes(nconv_s, 1, 2),
            nssm_s.reshape(depth, nseq, N_HEADS, HEAD_DIM, HEAD_DIM))
```

```python
import functools

import jax
import jax.numpy as jnp
import numpy as np
from jax import lax
from jax.experimental import pallas as pl
from jax.experimental.pallas import tpu as pltpu

F32 = jnp.float32
BF16 = jnp.bfloat16

D_MODEL = 1024
D_A = 512
D_B = 512
N_POOL = 4
POOL_WINDOWS = (2, 4, 8, 16)
POOL_GC = 128
POOL_BUF = 15
HEAD_DIM = 128
N_HEADS = 4
D_QKV = 3 * D_B
CONV_K = 4
D_MAIN = 2 * D_A + D_QKV + D_B
PAST_LEN = 16384
EPS = 1e-6
CHUNK = 64
GDN_GROUP = 16
SAMPLE_BLOCK = 16
LANES = 128
SUBLANES = 8
QKV_TILES = D_QKV // LANES
VMEM_LIMIT = 56 * 1024 * 1024


def _dot(a, b):
    return jnp.dot(a.astype(BF16), b.astype(BF16), preferred_element_type=F32)


def _dot_tn(a, b):
    return lax.dot_general(a, b, (((0,), (0,)), ((), ())), preferred_element_type=F32)


def _proj(hb, wt_ref, c0, c1):
    return lax.dot_general(hb, wt_ref[c0:c1, :], (((1,), (1,)), ((), ())), preferred_element_type=F32)


def _gate_logits(hb, wba_ref):
    w = jnp.concatenate([wba_ref[...], jnp.zeros((LANES - SUBLANES, wba_ref.shape[1]), wba_ref.dtype)], axis=0)
    return lax.dot_general(hb, w, (((1,), (1,)), ((), ())), preferred_element_type=F32)


def _silu(x):
    return x * jax.nn.sigmoid(x)


def _softplus(x):
    return jnp.maximum(x, 0.0) + jnp.log1p(jnp.exp(-jnp.abs(x)))


def _block_diag(a, b):
    top = jnp.concatenate([a, jnp.zeros((a.shape[0], b.shape[1]), a.dtype)], axis=1)
    bot = jnp.concatenate([jnp.zeros((b.shape[0], a.shape[1]), b.dtype), b], axis=1)
    return jnp.concatenate([top, bot], axis=0)


def _head_rms(o_all, w):
    parts = []
    for hh in range(N_HEADS):
        oh = o_all[:, hh * HEAD_DIM:(hh + 1) * HEAD_DIM]
        parts.append(oh * lax.rsqrt(jnp.mean(oh * oh, axis=-1, keepdims=True) + EPS) * w)
    return jnp.concatenate(parts, axis=1)


def _mod_kernel(cp_ref, cs_ref, w_ref, b_ref, op_ref, os_ref):
    w = w_ref[...]
    b = b_ref[...]
    op_ref[...] = jnp.dot(_silu(cp_ref[...]), w, preferred_element_type=F32) + b
    os_ref[...] = jnp.dot(_silu(cs_ref[...]), w, preferred_element_type=F32) + b


def _mod_call(c_prompt, c_sample, w_ada, b_ada):
    depth, _, n3 = w_ada.shape
    nb, nseq = c_prompt.shape[0], c_sample.shape[0]
    nblk = n3 // D_MODEL
    return pl.pallas_call(
        _mod_kernel,
        grid=(depth, nblk),
        in_specs=[
            pl.BlockSpec((nb, D_MODEL), lambda l, j: (0, 0)),
            pl.BlockSpec((nseq, D_MODEL), lambda l, j: (0, 0)),
            pl.BlockSpec((None, D_MODEL, D_MODEL), lambda l, j: (l, 0, j)),
            pl.BlockSpec((None, 1, D_MODEL), lambda l, j: (l, 0, j)),
        ],
        out_specs=[pl.BlockSpec((None, nb, D_MODEL), lambda l, j: (l, 0, j)),
                   pl.BlockSpec((None, nseq, D_MODEL), lambda l, j: (l, 0, j))],
        out_shape=[jax.ShapeDtypeStruct((depth, nb, n3), F32), jax.ShapeDtypeStruct((depth, nseq, n3), F32)],
        compiler_params=pltpu.CompilerParams(dimension_semantics=("arbitrary", "arbitrary")),
        name="adaln_mod",
    )(c_prompt, c_sample, w_ada, b_ada.reshape(depth, 1, n3))


def _prompt_kernel(x_ref, mod_ref, prew_ref, postw_ref, win_ref, wba_ref, convw_ref, poolw_ref,
                   pscale_ref, alogc_ref, dtbc_ref, alogr_ref, dtbr_ref, onw_ref, wout_ref, band_ref, ltri_ref,
                   *rest, n_alias):
    rest = rest[n_alias:]
    y_ref, npool_ref, nconv_ref, nssm_ref = rest[0:4]
    qkv_ext, ua_ext, s_ref, q_s, k_s, kt_s, v_s, o_s, beta_s, gc_s, eg_s, gcrow_s = rest[4:]
    step = pl.program_id(0)
    nb = x_ref.shape[0]
    rows = nb * CHUNK

    @pl.when(step == 0)
    def _():
        qkv_ext[:, :, 0:SUBLANES, :] = jnp.zeros((QKV_TILES, nb, SUBLANES, LANES), F32)
        ua_ext[:, 0:CHUNK, :] = jnp.zeros((nb, CHUNK, D_A), F32)
        s_ref[...] = jnp.zeros(s_ref.shape, F32)

    x = x_ref[...]
    mod = mod_ref[...]
    shift = mod[:, :, 0:D_MODEL]
    scale = mod[:, :, D_MODEL:2 * D_MODEL]
    gate = mod[:, :, 2 * D_MODEL:3 * D_MODEL]
    a_mul = prew_ref[...][None] * (1.0 + scale)
    ms = jnp.mean(x * x, axis=-1, keepdims=True)
    h = x * lax.rsqrt(ms + EPS) * a_mul + shift
    hb = h.reshape(rows, D_MODEL).astype(BF16)

    ua = _proj(hb, win_ref, 0, D_A)
    ua3 = ua.reshape(nb, CHUNK, D_A)
    ua_ext[:, CHUNK:2 * CHUNK, :] = ua3
    pos = lax.broadcasted_iota(jnp.int32, (CHUNK, POOL_GC), 0) + step * CHUNK
    pooled_groups = []
    for gi, w in enumerate(POOL_WINDOWS):
        cnt = jnp.minimum(pos + 1, w).astype(F32)
        per_b = []
        for b in range(nb):
            ext = ua_ext[b, :, gi * POOL_GC:(gi + 1) * POOL_GC]
            win = jnp.dot(band_ref[gi], ext.astype(BF16), preferred_element_type=F32)
            per_b.append(win / cnt - ua3[b, :, gi * POOL_GC:(gi + 1) * POOL_GC])
        pooled_groups.append(jnp.concatenate(per_b, axis=0))
    ya = jnp.concatenate([_dot(jnp.concatenate(pooled_groups[2 * g2:2 * g2 + 2], axis=1), poolw_ref[g2])
                          for g2 in range(N_POOL // 2)], axis=1)
    za = _proj(hb, win_ref, D_A, 2 * D_A)
    ya = ya * pscale_ref[...] * _silu(za)

    qkv = _proj(hb, win_ref, 2 * D_A, 2 * D_A + D_QKV)
    qkv_t = jnp.stack([qkv[:, c * LANES:(c + 1) * LANES] for c in range(QKV_TILES)], axis=0)
    qkv_t = qkv_t.reshape(QKV_TILES, nb, CHUNK, LANES)
    qkv_ext[:, :, SUBLANES:SUBLANES + CHUNK, :] = qkv_t
    cw = convw_ref[...]
    cw_t = [jnp.stack([cw[j:j + 1, c * LANES:(c + 1) * LANES] for c in range(QKV_TILES)], axis=0)[:, None]
            for j in range(CONV_K)]
    acc = qkv_t * cw_t[CONV_K - 1]
    for j in range(CONV_K - 1):
        acc = acc + qkv_ext[:, :, pl.ds(SUBLANES - (CONV_K - 1) + j, CHUNK), :] * cw_t[j]
    act = _silu(acc).reshape(QKV_TILES, rows, LANES)
    for hh in range(N_HEADS):
        sl = slice(hh * HEAD_DIM, (hh + 1) * HEAD_DIM)
        qh, kh = act[hh], act[N_HEADS + hh]
        q_s[:, sl] = qh * lax.rsqrt(jnp.sum(qh * qh, axis=-1, keepdims=True) + EPS) * (HEAD_DIM ** -0.5)
        kn = kh * lax.rsqrt(jnp.sum(kh * kh, axis=-1, keepdims=True) + EPS)
        k_s[:, sl] = kn
        kt_s[hh] = kn.T
        v_s[:, sl] = act[2 * N_HEADS + hh]

    bac = _gate_logits(hb, wba_ref)
    bar = bac.T[0:SUBLANES]
    sig = jax.nn.sigmoid(bac)
    gcol = -jnp.exp(alogc_ref[...]) * _softplus(bac + dtbc_ref[...])
    g_hi = gcol.astype(BF16)
    g_lo = (gcol - g_hi.astype(F32)).astype(BF16)
    gc_parts = []
    for b in range(nb):
        rs = slice(b * CHUNK, (b + 1) * CHUNK)
        gc_parts.append(jnp.dot(ltri_ref[...], jnp.concatenate([g_hi[rs], g_lo[rs]], axis=0),
                                preferred_element_type=F32))
    gc = jnp.concatenate(gc_parts, axis=0)
    for hh in range(N_HEADS):
        beta_s[hh] = jnp.broadcast_to(sig[:, hh:hh + 1], (rows, LANES))
        gcb = jnp.broadcast_to(gc[:, N_HEADS + hh:N_HEADS + hh + 1], (rows, LANES))
        gc_s[hh] = gcb
        eg_s[hh] = jnp.exp(gcb)
    grow = -jnp.exp(alogr_ref[...]) * _softplus(bar + dtbr_ref[...])
    lane_in_chunk = lax.broadcasted_iota(jnp.int32, grow.shape, 1) % CHUNK
    sh = 1
    while sh < CHUNK:
        grow = grow + jnp.where(lane_in_chunk >= sh, pltpu.roll(grow, sh, 1), 0.0)
        sh *= 2
    gcrow_s[...] = grow

    ri = lax.broadcasted_iota(jnp.int32, (CHUNK, LANES), 0)
    li = lax.broadcasted_iota(jnp.int32, (CHUNK, LANES), 1)
    lj = li % CHUNK
    lo_half = li < CHUNK
    strict = ri > lj
    causal = ri >= lj
    eye2 = (ri == lj).astype(F32)
    lo_half_hd = lax.broadcasted_iota(jnp.int32, (HEAD_DIM, LANES), 1) < CHUNK

    def pair_mul(xp, yp):
        ybd = jnp.concatenate([jnp.where(lo_half, yp, 0.0), jnp.where(lo_half, 0.0, yp)], axis=0)
        return _dot(xp, ybd)

    chains = [(p, hh) for p in range(nb // 2) for hh in range(N_HEADS)]
    for g0 in range(0, len(chains), GDN_GROUP):
        st = []
        for p, hh in chains[g0:g0 + GDN_GROUP]:
            ra = slice(2 * p * CHUNK, (2 * p + 1) * CHUNK)
            rb = slice((2 * p + 1) * CHUNK, (2 * p + 2) * CHUNK)
            sl = slice(hh * HEAD_DIM, (hh + 1) * HEAD_DIM)
            kba, kbb = k_s[ra, sl] * beta_s[hh, ra, :], k_s[rb, sl] * beta_s[hh, rb, :]
            lhs1 = jnp.concatenate([jnp.concatenate([kba, kbb], axis=1),
                                    jnp.concatenate([q_s[ra, sl], q_s[rb, sl]], axis=1)], axis=0)
            kt = kt_s[hh, :, 2 * p * CHUNK:(2 * p + 2) * CHUNK]
            ktbd = jnp.concatenate([jnp.where(lo_half_hd, kt, 0.0), jnp.where(lo_half_hd, 0.0, kt)], axis=0)
            kkqk = _dot(lhs1, ktbd)
            st.append(dict(ra=ra, rb=rb, sl=sl, hh=hh, p=p, kkqk=kkqk))
        for c in st:
            hh, p = c["hh"], c["p"]
            gcol_p = jnp.where(lo_half, gc_s[hh, c["ra"], :], gc_s[hh, c["rb"], :])
            grow_p = jnp.broadcast_to(gcrow_s[N_HEADS + hh:N_HEADS + hh + 1, 2 * p * CHUNK:(2 * p + 2) * CHUNK],
                                      (CHUNK, LANES))
            dec = jnp.exp(jnp.minimum(gcol_p - grow_p, 0.0))
            kkqk = c.pop("kkqk")
            c["npow"] = jnp.where(strict, kkqk[0:CHUNK] * dec, 0.0)
            c["qkm"] = jnp.where(causal, kkqk[CHUNK:2 * CHUNK] * dec, 0.0)
            c["t"] = eye2 - c["npow"]
        for c in st:
            c["npow"] = pair_mul(c["npow"], c["npow"])
        for _ in range(4):
            for c in st:
                both = pair_mul(jnp.concatenate([c["t"], c["npow"]], axis=0), c["npow"])
                c["t"] = c["t"] + both[0:CHUNK]
                c["npow"] = both[CHUNK:2 * CHUNK]
        for c in st:
            c["t"] = c["t"] + pair_mul(c["t"], c["npow"])
        for c in st:
            ra, rb, sl, hh = c["ra"], c["rb"], c["sl"], c["hh"]
            ba, bb = beta_s[hh, ra, :], beta_s[hh, rb, :]
            kba, kbb = k_s[ra, sl] * ba, k_s[rb, sl] * bb
            rhs = jnp.concatenate([_block_diag(v_s[ra, sl] * ba, v_s[rb, sl] * bb),
                                   _block_diag(kba * eg_s[hh, ra, :], kbb * eg_s[hh, rb, :])], axis=1)
            c["uw"] = _dot(c.pop("t"), rhs)
        for c in st:
            ra, rb, sl, hh, p = c["ra"], c["rb"], c["sl"], c["hh"], c["p"]
            c["ia"], c["ib"] = 2 * p * N_HEADS + hh, (2 * p + 1) * N_HEADS + hh
            lhs3 = jnp.concatenate([c["uw"][:, 2 * HEAD_DIM:4 * HEAD_DIM],
                                    jnp.concatenate([q_s[ra, sl] * eg_s[hh, ra, :], q_s[rb, sl] * eg_s[hh, rb, :]],
                                                    axis=1)], axis=0)
            c["r3"] = _dot(lhs3, _block_diag(s_ref[c["ia"]], s_ref[c["ib"]]))
        for c in st:
            ra, rb, sl, hh, p = c["ra"], c["rb"], c["sl"], c["hh"], c["p"]
            r3 = c.pop("r3")
            vn = c.pop("uw")[:, 0:2 * HEAD_DIM] - r3[0:CHUNK]
            o = r3[CHUNK:2 * CHUNK] + _dot(c.pop("qkm"), _block_diag(vn[:, 0:HEAD_DIM], vn[:, HEAD_DIM:2 * HEAD_DIM]))
            o_s[ra, sl] = o[:, 0:HEAD_DIM]
            o_s[rb, sl] = o[:, HEAD_DIM:2 * HEAD_DIM]
            gla = gc_s[hh, (2 * p + 1) * CHUNK - 1:(2 * p + 1) * CHUNK, :]
            glb = gc_s[hh, (2 * p + 2) * CHUNK - 1:(2 * p + 2) * CHUNK, :]
            kg = jnp.concatenate([k_s[ra, sl] * jnp.exp(gla - gc_s[hh, ra, :]),
                                  k_s[rb, sl] * jnp.exp(glb - gc_s[hh, rb, :])], axis=1)
            upd = _dot_tn(kg, vn)
            s_ref[c["ia"]] = s_ref[c["ia"]] * jnp.exp(gla) + upd[0:HEAD_DIM, 0:HEAD_DIM]
            s_ref[c["ib"]] = s_ref[c["ib"]] * jnp.exp(glb) + upd[HEAD_DIM:2 * HEAD_DIM, HEAD_DIM:2 * HEAD_DIM]

    zb = _proj(hb, win_ref, 2 * D_A + D_QKV, D_MAIN)
    yb = _head_rms(o_s[...], onw_ref[...]) * _silu(zb)
    ymix = jnp.concatenate([ya, yb], axis=1)
    yo = jnp.dot(ymix.astype(BF16), wout_ref[...], preferred_element_type=F32)
    yn = yo * lax.rsqrt(jnp.mean(yo * yo, axis=-1, keepdims=True) + EPS)
    y_ref[...] = x + yn.reshape(nb, CHUNK, D_MODEL) * (gate * postw_ref[...][None])

    qkv_ext[:, :, 0:SUBLANES, :] = qkv_ext[:, :, CHUNK:CHUNK + SUBLANES, :]
    ua_ext[:, 0:CHUNK, :] = ua3

    @pl.when(step == pl.num_programs(0) - 1)
    def _():
        npool_ref[...] = ua3[:, CHUNK - POOL_BUF:CHUNK, :]
        nconv_ref[...] = qkv.reshape(nb, CHUNK, D_QKV)[:, CHUNK - (CONV_K - 1):CHUNK, :]
        nssm_ref[...] = s_ref[...]


def _layer_spec(shape, l):
    return pl.BlockSpec((None,) + tuple(shape), lambda s, _n=len(shape): (l,) + (0,) * _n)


def _prompt_layer(l, x, mod_p, lw, consts, prev):
    nb, seq, _ = x.shape
    depth = lw["win"].shape[0]
    rows = nb * CHUNK
    nstep = seq // CHUNK
    full = lambda shape: pl.BlockSpec(shape, lambda s, _n=len(shape): (0,) * _n)
    in_specs = [
        pl.BlockSpec((nb, CHUNK, D_MODEL), lambda s: (0, s, 0)),
        _layer_spec((nb, 1, 3 * D_MODEL), l),
        _layer_spec((1, D_MODEL), l), _layer_spec((1, D_MODEL), l),
        pl.BlockSpec((None, D_MAIN, D_MODEL), lambda s: (l, 0, 0)),
        pl.BlockSpec((None, SUBLANES, D_MODEL), lambda s: (l, D_MAIN // SUBLANES, 0)),
        _layer_spec((CONV_K, D_QKV), l), _layer_spec((N_POOL // 2, 2 * POOL_GC, 2 * POOL_GC), l),
        _layer_spec((1, D_A), l),
        _layer_spec((1, LANES), l), _layer_spec((1, LANES), l),
        _layer_spec((SUBLANES, rows), l), _layer_spec((SUBLANES, rows), l),
        _layer_spec((1, HEAD_DIM), l), _layer_spec((D_MODEL, D_MODEL), l),
        full((N_POOL, CHUNK, 2 * CHUNK)), full((CHUNK, 2 * CHUNK)),
    ] + [pl.BlockSpec(memory_space=pl.ANY)] * len(prev)
    out_specs = [
        pl.BlockSpec((nb, CHUNK, D_MODEL), lambda s: (0, s, 0)),
        _layer_spec((nb, POOL_BUF, D_A), l), _layer_spec((nb, CONV_K - 1, D_QKV), l),
        _layer_spec((nb * N_HEADS, HEAD_DIM, HEAD_DIM), l),
    ]
    out_shape = [
        jax.ShapeDtypeStruct((nb, seq, D_MODEL), F32),
        jax.ShapeDtypeStruct((depth, nb, POOL_BUF, D_A), F32),
        jax.ShapeDtypeStruct((depth, nb, CONV_K - 1, D_QKV), F32),
        jax.ShapeDtypeStruct((depth, nb * N_HEADS, HEAD_DIM, HEAD_DIM), F32),
    ]
    scratch = [
        pltpu.VMEM((QKV_TILES, nb, CHUNK + SUBLANES, LANES), F32),
        pltpu.VMEM((nb, 2 * CHUNK, D_A), F32),
        pltpu.VMEM((nb * N_HEADS, HEAD_DIM, HEAD_DIM), F32),
        pltpu.VMEM((rows, D_B), F32), pltpu.VMEM((rows, D_B), F32), pltpu.VMEM((N_HEADS, HEAD_DIM, rows), F32),
        pltpu.VMEM((rows, D_B), F32), pltpu.VMEM((rows, D_B), F32),
        pltpu.VMEM((N_HEADS, rows, LANES), F32), pltpu.VMEM((N_HEADS, rows, LANES), F32),
        pltpu.VMEM((N_HEADS, rows, LANES), F32),
        pltpu.VMEM((SUBLANES, rows), F32),
    ]
    n_in = len(in_specs) - len(prev)
    return pl.pallas_call(
        functools.partial(_prompt_kernel, n_alias=len(prev)),
        grid=(nstep,),
        in_specs=in_specs,
        out_specs=out_specs,
        out_shape=out_shape,
        scratch_shapes=scratch,
        input_output_aliases={n_in + i: 1 + i for i in range(len(prev))},
        compiler_params=pltpu.CompilerParams(dimension_semantics=("arbitrary",), vmem_limit_bytes=VMEM_LIMIT),
        name="prompt_layer",
    )(x, mod_p, lw["prew"], lw["postw"], lw["win"], lw["win"], lw["convw"], lw["poolw2"], lw["pscale"],
      lw["alogc"], lw["dtbc"], lw["alogr"], lw["dtbr"], lw["onw"], lw["wout"], consts["band"], consts["ltri"], *prev)


def _sample_kernel(x_ref, mod_ref, spool_ref, sconv_ref, ssm_ref, prew_ref, postw_ref, win_ref, wba_ref, convw_ref,
                   poolw_ref, pscale_ref, alogc_ref, dtbc_ref, onw_ref, wout_ref, *rest, n_alias):
    rest = rest[n_alias:]
    y_ref, npool_ref, nconv_ref, nssm_ref = rest[0:4]
    q_s, k_s, v_s, o_s, beta_s, eg_s, qk_s, ya_s, zb_s = rest[4:]
    step = pl.program_id(0)
    nseq = x_ref.shape[0]

    @pl.when(step == 0)
    def _():
        x = x_ref[...]
        mod = mod_ref[...]
        shift = mod[:, 0:D_MODEL]
        scale = mod[:, D_MODEL:2 * D_MODEL]
        a_mul = prew_ref[...] * (1.0 + scale)
        ms = jnp.mean(x * x, axis=-1, keepdims=True)
        hb = (x * lax.rsqrt(ms + EPS) * a_mul + shift).astype(BF16)

        ua = _proj(hb, win_ref, 0, D_A)
        ya_parts = []
        for gi, w in enumerate(POOL_WINDOWS):
            gs = slice(gi * POOL_GC, (gi + 1) * POOL_GC)
            win = ua[:, gs]
            for d in range(1, w):
                win = win + spool_ref[POOL_BUF - d, :, gs]
            cnt = float(min(PAST_LEN + 1, w))
            pooled = win / cnt - ua[:, gs]
            ya_parts.append(_dot(pooled, poolw_ref[gi]))
        za = _proj(hb, win_ref, D_A, 2 * D_A)
        ya_s[...] = jnp.concatenate(ya_parts, axis=1) * pscale_ref[...] * _silu(za)
        npool_ref[0:POOL_BUF - 1] = spool_ref[1:POOL_BUF]
        npool_ref[POOL_BUF - 1] = ua

        qkv = _proj(hb, win_ref, 2 * D_A, 2 * D_A + D_QKV)
        cw = convw_ref[...]
        acc = qkv * cw[CONV_K - 1:CONV_K]
        for j in range(CONV_K - 1):
            acc = acc + sconv_ref[j] * cw[j:j + 1]
        nconv_ref[0:CONV_K - 2] = sconv_ref[1:CONV_K - 1]
        nconv_ref[CONV_K - 2] = qkv
        qkvc = _silu(acc)
        bac = _gate_logits(hb, wba_ref)
        sig = jax.nn.sigmoid(bac)
        eg = jnp.exp(-jnp.exp(alogc_ref[...]) * _softplus(bac + dtbc_ref[...]))
        for hh in range(N_HEADS):
            sl = slice(hh * HEAD_DIM, (hh + 1) * HEAD_DIM)
            qh = qkvc[:, hh * HEAD_DIM:(hh + 1) * HEAD_DIM]
            kh = qkvc[:, D_B + hh * HEAD_DIM:D_B + (hh + 1) * HEAD_DIM]
            qn = qh * lax.rsqrt(jnp.sum(qh * qh, axis=-1, keepdims=True) + EPS) * (HEAD_DIM ** -0.5)
            kn = kh * lax.rsqrt(jnp.sum(kh * kh, axis=-1, keepdims=True) + EPS)
            q_s[:, sl] = qn
            k_s[:, sl] = kn
            qk_s[hh] = jnp.broadcast_to(jnp.sum(qn * kn, axis=-1, keepdims=True), (nseq, LANES))
            beta_s[hh] = jnp.broadcast_to(sig[:, hh:hh + 1], (nseq, LANES))
            eg_s[hh] = jnp.broadcast_to(eg[:, N_HEADS + hh:N_HEADS + hh + 1], (nseq, LANES))
        v_s[...] = qkvc[:, 2 * D_B:3 * D_B]
        zb_s[...] = _proj(hb, win_ref, 2 * D_A + D_QKV, D_MAIN)

    r0 = pl.multiple_of(step * SAMPLE_BLOCK, SAMPLE_BLOCK)
    kblk = k_s[pl.ds(r0, SAMPLE_BLOCK), :]
    qblk = q_s[pl.ds(r0, SAMPLE_BLOCK), :]
    vblk = v_s[pl.ds(r0, SAMPLE_BLOCK), :]
    row8 = lax.broadcasted_iota(jnp.int32, (SUBLANES, HEAD_DIM), 0)
    st = []
    for i in range(SAMPLE_BLOCK):
        for hh in range(N_HEADS):
            sl = slice(hh * HEAD_DIM, (hh + 1) * HEAD_DIM)
            krow, qrow = kblk[i:i + 1, sl], qblk[i:i + 1, sl]
            lhs = jnp.where(row8 == 0, krow, jnp.where(row8 == 1, qrow, 0.0))
            r = jnp.dot(lhs, ssm_ref[i * N_HEADS + hh], preferred_element_type=F32)
            st.append(dict(i=i, hh=hh, sl=sl, krow=krow, r=r))
    for c in st:
        i, hh = c["i"], c["hh"]
        eg = eg_s[hh, pl.ds(r0 + i, 1), :]
        r = c.pop("r")
        delta = (vblk[i:i + 1, c["sl"]] - eg * r[0:1]) * beta_s[hh, pl.ds(r0 + i, 1), :]
        c["o"] = eg * r[1:2] + qk_s[hh, pl.ds(r0 + i, 1), :] * delta
        krow = c.pop("krow")
        k_hi = krow.astype(BF16).astype(F32)
        d_hi = delta.astype(BF16).astype(F32)
        kp = jnp.where(row8 == 0, k_hi, jnp.where(row8 == 1, krow - k_hi, jnp.where(row8 == 2, k_hi, 0.0)))
        dp = jnp.where(row8 == 0, d_hi, jnp.where(row8 == 1, d_hi, jnp.where(row8 == 2, delta - d_hi, 0.0)))
        c["upd"] = _dot_tn(kp, dp)
    for c in st:
        idx = c["i"] * N_HEADS + c["hh"]
        nssm_ref[idx] = ssm_ref[idx] * eg_s[c["hh"], pl.ds(r0 + c["i"], 1), :] + c.pop("upd")
    o_rows = [jnp.concatenate([c["o"] for c in st[i * N_HEADS:(i + 1) * N_HEADS]], axis=1)
              for i in range(SAMPLE_BLOCK)]
    o_s[pl.ds(r0, SAMPLE_BLOCK), :] = jnp.concatenate(o_rows, axis=0)

    @pl.when(step == pl.num_programs(0) - 1)
    def _():
        x = x_ref[...]
        gate = mod_ref[:, 2 * D_MODEL:3 * D_MODEL]
        yb = _head_rms(o_s[...], onw_ref[...]) * _silu(zb_s[...])
        ymix = jnp.concatenate([ya_s[...], yb], axis=1)
        yo = jnp.dot(ymix.astype(BF16), wout_ref[...], preferred_element_type=F32)
        yn = yo * lax.rsqrt(jnp.mean(yo * yo, axis=-1, keepdims=True) + EPS) * postw_ref[...]
        y_ref[...] = x + gate * yn


def _sample_layer(l, x, mod_s, state_pool, state_conv, state_ssm, lw, prev):
    nseq = x.shape[0]
    depth = lw["win"].shape[0]
    nstep = nseq // SAMPLE_BLOCK
    blk_states = SAMPLE_BLOCK * N_HEADS
    full = lambda shape: pl.BlockSpec(shape, lambda s, _n=len(shape): (0,) * _n)
    ssm_spec = pl.BlockSpec((None, blk_states, HEAD_DIM, HEAD_DIM), lambda s: (l, s, 0, 0))
    in_specs = [
        full((nseq, D_MODEL)), _layer_spec((nseq, 3 * D_MODEL), l),
        _layer_spec((POOL_BUF, nseq, D_A), l), _layer_spec((CONV_K - 1, nseq, D_QKV), l), ssm_spec,
        _layer_spec((1, D_MODEL), l), _layer_spec((1, D_MODEL), l),
        pl.BlockSpec((None, D_MAIN, D_MODEL), lambda s: (l, 0, 0)),
        pl.BlockSpec((None, SUBLANES, D_MODEL), lambda s: (l, D_MAIN // SUBLANES, 0)),
        _layer_spec((CONV_K, D_QKV), l), _layer_spec((N_POOL, POOL_GC, POOL_GC), l), _layer_spec((1, D_A), l),
        _layer_spec((1, LANES), l), _layer_spec((1, LANES), l), _layer_spec((1, HEAD_DIM), l),
        _layer_spec((D_MODEL, D_MODEL), l),
    ] + [pl.BlockSpec(memory_space=pl.ANY)] * len(prev)
    out_specs = [
        full((nseq, D_MODEL)), _layer_spec((POOL_BUF, nseq, D_A), l), _layer_spec((CONV_K - 1, nseq, D_QKV), l),
        ssm_spec,
    ]
    out_shape = [
        jax.ShapeDtypeStruct((nseq, D_MODEL), F32),
        jax.ShapeDtypeStruct((depth, POOL_BUF, nseq, D_A), F32),
        jax.ShapeDtypeStruct((depth, CONV_K - 1, nseq, D_QKV), F32),
        jax.ShapeDtypeStruct((depth, nseq * N_HEADS, HEAD_DIM, HEAD_DIM), F32),
    ]
    scratch = [
        pltpu.VMEM((nseq, D_B), F32), pltpu.VMEM((nseq, D_B), F32), pltpu.VMEM((nseq, D_B), F32),
        pltpu.VMEM((nseq, D_B), F32),
        pltpu.VMEM((N_HEADS, nseq, LANES), F32), pltpu.VMEM((N_HEADS, nseq, LANES), F32),
        pltpu.VMEM((N_HEADS, nseq, LANES), F32),
        pltpu.VMEM((nseq, D_A), F32), pltpu.VMEM((nseq, D_B), F32),
    ]
    n_in = len(in_specs) - len(prev)
    return pl.pallas_call(
        functools.partial(_sample_kernel, n_alias=len(prev)),
        grid=(nstep,),
        in_specs=in_specs,
        out_specs=out_specs,
        out_shape=out_shape,
        scratch_shapes=scratch,
        input_output_aliases={n_in + i: 1 + i for i in range(len(prev))},
        compiler_params=pltpu.CompilerParams(dimension_semantics=("arbitrary",), vmem_limit_bytes=VMEM_LIMIT),
        name="sample_layer",
    )(x, mod_s, state_pool, state_conv, state_ssm, lw["prew"], lw["postw"], lw["win"], lw["win"], lw["convw"],
      lw["poolw"], lw["pscale"], lw["alogc"], lw["dtbc"], lw["onw"], lw["wout"], *prev)


def _constants():
    t = np.arange(CHUNK)[:, None]
    j = np.arange(2 * CHUNK)[None, :]
    band = np.stack([((j <= CHUNK + t) & (j > CHUNK + t - w)) for w in POOL_WINDOWS]).astype(np.float32)
    tri = (np.arange(CHUNK)[None, :] <= t).astype(np.float32)
    ltri = np.concatenate([tri, tri], axis=1)
    return {"band": jnp.asarray(band, BF16), "ltri": jnp.asarray(ltri, BF16)}


def _stacked_weights(rows_p, pre_norm_w, post_norm_w, w_in, conv_w, pool_w, pool_scale, a_log, dt_bias, o_norm_w,
                     w_out):
    depth = w_in.shape[0]
    w_in_t = jnp.swapaxes(w_in, 1, 2)
    pw2 = jnp.zeros((depth, N_POOL // 2, 2 * POOL_GC, 2 * POOL_GC), F32)
    pw2 = pw2.at[:, :, 0:POOL_GC, 0:POOL_GC].set(pool_w[:, 0::2])
    pw2 = pw2.at[:, :, POOL_GC:2 * POOL_GC, POOL_GC:2 * POOL_GC].set(pool_w[:, 1::2])
    pad_c = lambda v: jnp.zeros((depth, 1, LANES), F32).at[:, 0, N_HEADS:2 * N_HEADS].set(v)
    pad_r = lambda v: jnp.broadcast_to(
        jnp.zeros((depth, SUBLANES), F32).at[:, N_HEADS:2 * N_HEADS].set(v)[:, :, None], (depth, SUBLANES, rows_p))
    return {
        "prew": pre_norm_w.reshape(depth, 1, D_MODEL),
        "postw": post_norm_w.reshape(depth, 1, D_MODEL),
        "win": w_in_t.astype(BF16),
        "convw": conv_w,
        "poolw": pool_w.astype(BF16),
        "poolw2": pw2.astype(BF16),
        "pscale": pool_scale.reshape(depth, 1, D_A),
        "alogc": pad_c(a_log), "dtbc": pad_c(dt_bias),
        "alogr": pad_r(a_log), "dtbr": pad_r(dt_bias),
        "onw": o_norm_w.reshape(depth, 1, HEAD_DIM),
        "wout": w_out.astype(BF16),
    }


def kernel(x_prompt, x_sample, c_prompt, c_sample, state_pool, state_conv, state_ssm, w_ada, b_ada, pre_norm_w,
           post_norm_w, w_in, conv_w, pool_w, pool_scale, a_log, dt_bias, o_norm_w, w_out):
    depth = w_in.shape[0]
    nb, seq, _ = x_prompt.shape
    nseq, dec_seq, _ = x_sample.shape
    assert dec_seq == 1 and seq % CHUNK == 0 and nb % 2 == 0 and nseq % SAMPLE_BLOCK == 0
    consts = _constants()
    lw = _stacked_weights(nb * CHUNK, pre_norm_w, post_norm_w, w_in, conv_w, pool_w, pool_scale, a_log, dt_bias,
                          o_norm_w, w_out)
    mod_p, mod_s = _mod_call(c_prompt, c_sample, w_ada, b_ada)
    mod_p = mod_p.reshape(depth, nb, 1, 3 * D_MODEL)
    ssm_in = state_ssm.reshape(depth, nseq * N_HEADS, HEAD_DIM, HEAD_DIM)
    pool_in = jnp.swapaxes(state_pool, 1, 2)
    conv_in = jnp.swapaxes(state_conv, 1, 2)
    yp, ys = x_prompt, x_sample.reshape(nseq, D_MODEL)
    prev_p, prev_s = (), ()
    for l in range(depth):
        yp, *prev_p = _prompt_layer(l, yp, mod_p, lw, consts, tuple(prev_p))
        ys, *prev_s = _sample_layer(l, ys, mod_s, pool_in, conv_in, ssm_in, lw, tuple(prev_s))
    npool_p, nconv_p, nssm_p = prev_p
    npool_s, nconv_s, nssm_s = prev_s
    return (yp, ys.reshape(nseq, dec_seq, D_MODEL), npool_p, nconv_p,
            nssm_p.reshape(depth, nb, N_HEADS, HEAD_DIM, HEAD_DIM),
            jnp.swapaxes(npool_s, 1, 2), jnp.swapaxes(nconv_s, 1, 2),
            nssm_s.reshape(depth, nseq, N_HEADS, HEAD_DIM, HEAD_DIM))
```

```python
import functools

import jax
import jax.numpy as jnp
import numpy as np
from jax import lax
from jax.experimental import pallas as pl
from jax.experimental.pallas import tpu as pltpu

F32 = jnp.float32
BF16 = jnp.bfloat16

D_MODEL = 1024
D_A = 512
D_B = 512
N_POOL = 4
POOL_WINDOWS = (2, 4, 8, 16)
POOL_GC = 128
POOL_BUF = 15
HEAD_DIM = 128
N_HEADS = 4
D_QKV = 3 * D_B
CONV_K = 4
D_MAIN = 2 * D_A + D_QKV + D_B
PAST_LEN = 16384
EPS = 1e-6
CHUNK = 64
GDN_GROUP = 16
SAMPLE_BLOCK = 16
LANES = 128
SUBLANES = 8
QKV_TILES = D_QKV // LANES
VMEM_LIMIT = 56 * 1024 * 1024


def _dot(a, b):
    return jnp.dot(a.astype(BF16), b.astype(BF16), preferred_element_type=F32)


def _dot_tn(a, b):
    return lax.dot_general(a, b, (((0,), (0,)), ((), ())), preferred_element_type=F32)


def _proj(hb, wt_ref, c0, c1):
    return lax.dot_general(hb, wt_ref[c0:c1, :], (((1,), (1,)), ((), ())), preferred_element_type=F32)


def _gate_logits(hb, wba_ref):
    w = jnp.concatenate([wba_ref[...], jnp.zeros((LANES - SUBLANES, wba_ref.shape[1]), wba_ref.dtype)], axis=0)
    return lax.dot_general(hb, w, (((1,), (1,)), ((), ())), preferred_element_type=F32)


def _silu(x):
    return x * jax.nn.sigmoid(x)


def _softplus(x):
    return jnp.maximum(x, 0.0) + jnp.log1p(jnp.exp(-jnp.abs(x)))


def _block_diag(a, b):
    top = jnp.concatenate([a, jnp.zeros((a.shape[0], b.shape[1]), a.dtype)], axis=1)
    bot = jnp.concatenate([jnp.zeros((b.shape[0], a.shape[1]), b.dtype), b], axis=1)
    return jnp.concatenate([top, bot], axis=0)


def _head_rms(o_all, w):
    parts = []
    for hh in range(N_HEADS):
        oh = o_all[:, hh * HEAD_DIM:(hh + 1) * HEAD_DIM]
        parts.append(oh * lax.rsqrt(jnp.mean(oh * oh, axis=-1, keepdims=True) + EPS) * w)
    return jnp.concatenate(parts, axis=1)


def _mod_kernel(cp_ref, cs_ref, w_ref, b_ref, op_ref, os_ref):
    w = w_ref[...]
    b = b_ref[...]
    op_ref[...] = jnp.dot(_silu(cp_ref[...]), w, preferred_element_type=F32) + b
    os_ref[...] = jnp.dot(_silu(cs_ref[...]), w, preferred_element_type=F32) + b


def _mod_call(c_prompt, c_sample, w_ada, b_ada):
    depth, _, n3 = w_ada.shape
    nb, nseq = c_prompt.shape[0], c_sample.shape[0]
    nblk = n3 // D_MODEL
    return pl.pallas_call(
        _mod_kernel,
        grid=(depth, nblk),
        in_specs=[
            pl.BlockSpec((nb, D_MODEL), lambda l, j: (0, 0)),
            pl.BlockSpec((nseq, D_MODEL), lambda l, j: (0, 0)),
            pl.BlockSpec((None, D_MODEL, D_MODEL), lambda l, j: (l, 0, j)),
            pl.BlockSpec((None, 1, D_MODEL), lambda l, j: (l, 0, j)),
        ],
        out_specs=[pl.BlockSpec((None, nb, D_MODEL), lambda l, j: (l, 0, j)),
                   pl.BlockSpec((None, nseq, D_MODEL), lambda l, j: (l, 0, j))],
        out_shape=[jax.ShapeDtypeStruct((depth, nb, n3), F32), jax.ShapeDtypeStruct((depth, nseq, n3), F32)],
        compiler_params=pltpu.CompilerParams(dimension_semantics=("arbitrary", "arbitrary")),
        name="adaln_mod",
    )(c_prompt, c_sample, w_ada, b_ada.reshape(depth, 1, n3))


def _prompt_kernel(x_ref, mod_ref, prew_ref, postw_ref, win_ref, wba_ref, convw_ref, poolw_ref,
                   pscale_ref, alogc_ref, dtbc_ref, alogr_ref, dtbr_ref, onw_ref, wout_ref, band_ref, ltri_ref,
                   *rest, n_alias):
    rest = rest[n_alias:]
    y_ref, npool_ref, nconv_ref, nssm_ref = rest[0:4]
    qkv_ext, ua_ext, s_ref, q_s, k_s, kt_s, v_s, o_s, beta_s, gc_s, eg_s, gcrow_s = rest[4:]
    step = pl.program_id(0)
    nb = x_ref.shape[0]
    rows = nb * CHUNK

    @pl.when(step == 0)
    def _():
        qkv_ext[:, :, 0:SUBLANES, :] = jnp.zeros((QKV_TILES, nb, SUBLANES, LANES), F32)
        ua_ext[:, 0:CHUNK, :] = jnp.zeros((nb, CHUNK, D_A), F32)
        s_ref[...] = jnp.zeros(s_ref.shape, F32)

    x = x_ref[...]
    mod = mod_ref[...]
    shift = mod[:, :, 0:D_MODEL]
    scale = mod[:, :, D_MODEL:2 * D_MODEL]
    gate = mod[:, :, 2 * D_MODEL:3 * D_MODEL]
    a_mul = prew_ref[...][None] * (1.0 + scale)
    ms = jnp.mean(x * x, axis=-1, keepdims=True)
    h = x * lax.rsqrt(ms + EPS) * a_mul + shift
    hb = h.reshape(rows, D_MODEL).astype(BF16)

    ua = _proj(hb, win_ref, 0, D_A)
    ua3 = ua.reshape(nb, CHUNK, D_A)
    ua_ext[:, CHUNK:2 * CHUNK, :] = ua3
    pos = lax.broadcasted_iota(jnp.int32, (CHUNK, POOL_GC), 0) + step * CHUNK
    pooled_groups = []
    for gi, w in enumerate(POOL_WINDOWS):
        cnt = jnp.minimum(pos + 1, w).astype(F32)
        per_b = []
        for b in range(nb):
            ext = ua_ext[b, :, gi * POOL_GC:(gi + 1) * POOL_GC]
            win = jnp.dot(band_ref[gi], ext.astype(BF16), preferred_element_type=F32)
            per_b.append(win / cnt - ua3[b, :, gi * POOL_GC:(gi + 1) * POOL_GC])
        pooled_groups.append(jnp.concatenate(per_b, axis=0))
    ya = jnp.concatenate([_dot(jnp.concatenate(pooled_groups[2 * g2:2 * g2 + 2], axis=1), poolw_ref[g2])
                          for g2 in range(N_POOL // 2)], axis=1)
    za = _proj(hb, win_ref, D_A, 2 * D_A)
    ya = ya * pscale_ref[...] * _silu(za)

    qkv = _proj(hb, win_ref, 2 * D_A, 2 * D_A + D_QKV)
    qkv_t = jnp.stack([qkv[:, c * LANES:(c + 1) * LANES] for c in range(QKV_TILES)], axis=0)
    qkv_t = qkv_t.reshape(QKV_TILES, nb, CHUNK, LANES)
    qkv_ext[:, :, SUBLANES:SUBLANES + CHUNK, :] = qkv_t
    cw = convw_ref[...]
    cw_t = [jnp.stack([cw[j:j + 1, c * LANES:(c + 1) * LANES] for c in range(QKV_TILES)], axis=0)[:, None]
            for j in range(CONV_K)]
    acc = qkv_t * cw_t[CONV_K - 1]
    for j in range(CONV_K - 1):
        acc = acc + qkv_ext[:, :, pl.ds(SUBLANES - (CONV_K - 1) + j, CHUNK), :] * cw_t[j]
    act = _silu(acc).reshape(QKV_TILES, rows, LANES)
    for hh in range(N_HEADS):
        sl = slice(hh * HEAD_DIM, (hh + 1) * HEAD_DIM)
        qh, kh = act[hh], act[N_HEADS + hh]
        q_s[:, sl] = qh * lax.rsqrt(jnp.sum(qh * qh, axis=-1, keepdims=True) + EPS) * (HEAD_DIM ** -0.5)
        kn = kh * lax.rsqrt(jnp.sum(kh * kh, axis=-1, keepdims=True) + EPS)
        k_s[:, sl] = kn
        kt_s[hh] = kn.T
        v_s[:, sl] = act[2 * N_HEADS + hh]

    bac = _gate_logits(hb, wba_ref)
    bar = bac.T[0:SUBLANES]
    sig = jax.nn.sigmoid(bac)
    gcol = -jnp.exp(alogc_ref[...]) * _softplus(bac + dtbc_ref[...])
    g_hi = gcol.astype(BF16)
    g_lo = (gcol - g_hi.astype(F32)).astype(BF16)
    gc_parts = []
    for b in range(nb):
        rs = slice(b * CHUNK, (b + 1) * CHUNK)
        gc_parts.append(jnp.dot(ltri_ref[...], jnp.concatenate([g_hi[rs], g_lo[rs]], axis=0),
                                preferred_element_type=F32))
    gc = jnp.concatenate(gc_parts, axis=0)
    for hh in range(N_HEADS):
        beta_s[hh] = jnp.broadcast_to(sig[:, hh:hh + 1], (rows, LANES))
        gcb = jnp.broadcast_to(gc[:, N_HEADS + hh:N_HEADS + hh + 1], (rows, LANES))
        gc_s[hh] = gcb
        eg_s[hh] = jnp.exp(gcb)
    grow = -jnp.exp(alogr_ref[...]) * _softplus(bar + dtbr_ref[...])
    lane_in_chunk = lax.broadcasted_iota(jnp.int32, grow.shape, 1) % CHUNK
    sh = 1
    while sh < CHUNK:
        grow = grow + jnp.where(lane_in_chunk >= sh, pltpu.roll(grow, sh, 1), 0.0)
        sh *= 2
    gcrow_s[...] = grow

    ri = lax.broadcasted_iota(jnp.int32, (CHUNK, LANES), 0)
    li = lax.broadcasted_iota(jnp.int32, (CHUNK, LANES), 1)
    lj = li % CHUNK
    lo_half = li < CHUNK
    strict = ri > lj
    causal = ri >= lj
    eye2 = (ri == lj).astype(F32)
    lo_half_hd = lax.broadcasted_iota(jnp.int32, (HEAD_DIM, LANES), 1) < CHUNK

    def pair_mul(xp, yp):
        ybd = jnp.concatenate([jnp.where(lo_half, yp, 0.0), jnp.where(lo_half, 0.0, yp)], axis=0)
        return _dot(xp, ybd)

    chains = [(p, hh) for p in range(nb // 2) for hh in range(N_HEADS)]
    for g0 in range(0, len(chains), GDN_GROUP):
        st = []
        for p, hh in chains[g0:g0 + GDN_GROUP]:
            ra = slice(2 * p * CHUNK, (2 * p + 1) * CHUNK)
            rb = slice((2 * p + 1) * CHUNK, (2 * p + 2) * CHUNK)
            sl = slice(hh * HEAD_DIM, (hh + 1) * HEAD_DIM)
            kba, kbb = k_s[ra, sl] * beta_s[hh, ra, :], k_s[rb, sl] * beta_s[hh, rb, :]
            lhs1 = jnp.concatenate([jnp.concatenate([kba, kbb], axis=1),
                                    jnp.concatenate([q_s[ra, sl], q_s[rb, sl]], axis=1)], axis=0)
            kt = kt_s[hh, :, 2 * p * CHUNK:(2 * p + 2) * CHUNK]
            ktbd = jnp.concatenate([jnp.where(lo_half_hd, kt, 0.0), jnp.where(lo_half_hd, 0.0, kt)], axis=0)
            kkqk = _dot(lhs1, ktbd)
            st.append(dict(ra=ra, rb=rb, sl=sl, hh=hh, p=p, kkqk=kkqk))
        for c in st:
            hh, p = c["hh"], c["p"]
            gcol_p = jnp.where(lo_half, gc_s[hh, c["ra"], :], gc_s[hh, c["rb"], :])
            grow_p = jnp.broadcast_to(gcrow_s[N_HEADS + hh:N_HEADS + hh + 1, 2 * p * CHUNK:(2 * p + 2) * CHUNK],
                                      (CHUNK, LANES))
            dec = jnp.exp(jnp.minimum(gcol_p - grow_p, 0.0))
            kkqk = c.pop("kkqk")
            c["npow"] = jnp.where(strict, kkqk[0:CHUNK] * dec, 0.0)
            c["qkm"] = jnp.where(causal, kkqk[CHUNK:2 * CHUNK] * dec, 0.0)
            c["t"] = eye2 - c["npow"]
        for c in st:
            c["npow"] = pair_mul(c["npow"], c["npow"])
        for _ in range(4):
            for c in st:
                both = pair_mul(jnp.concatenate([c["t"], c["npow"]], axis=0), c["npow"])
                c["t"] = c["t"] + both[0:CHUNK]
                c["npow"] = both[CHUNK:2 * CHUNK]
        for c in st:
            c["t"] = c["t"] + pair_mul(c["t"], c["npow"])
        for c in st:
            ra, rb, sl, hh = c["ra"], c["rb"], c["sl"], c["hh"]
            ba, bb = beta_s[hh, ra, :], beta_s[hh, rb, :]
            kba, kbb = k_s[ra, sl] * ba, k_s[rb, sl] * bb
            rhs = jnp.concatenate([_block_diag(v_s[ra, sl] * ba, v_s[rb, sl] * bb),
                                   _block_diag(kba * eg_s[hh, ra, :], kbb * eg_s[hh, rb, :])], axis=1)
            c["uw"] = _dot(c.pop("t"), rhs)
        for c in st:
            ra, rb, sl, hh, p = c["ra"], c["rb"], c["sl"], c["hh"], c["p"]
            c["ia"], c["ib"] = 2 * p * N_HEADS + hh, (2 * p + 1) * N_HEADS + hh
            lhs3 = jnp.concatenate([c["uw"][:, 2 * HEAD_DIM:4 * HEAD_DIM],
                                    jnp.concatenate([q_s[ra, sl] * eg_s[hh, ra, :], q_s[rb, sl] * eg_s[hh, rb, :]],
                                                    axis=1)], axis=0)
            c["r3"] = _dot(lhs3, _block_diag(s_ref[c["ia"]], s_ref[c["ib"]]))
        for c in st:
            ra, rb, sl, hh, p = c["ra"], c["rb"], c["sl"], c["hh"], c["p"]
            r3 = c.pop("r3")
            vn = c.pop("uw")[:, 0:2 * HEAD_DIM] - r3[0:CHUNK]
            gla = gc_s[hh, (2 * p + 1) * CHUNK - 1:(2 * p + 1) * CHUNK, :]
            glb = gc_s[hh, (2 * p + 2) * CHUNK - 1:(2 * p + 2) * CHUNK, :]
            pair_lanes = slice(2 * p * CHUNK, (2 * p + 2) * CHUNK)
            to_end = jnp.exp(jnp.where(lo_half[0:1], gla, glb) - gcrow_s[N_HEADS + hh:N_HEADS + hh + 1, pair_lanes])
            kgt = kt_s[hh, :, pair_lanes] * to_end
            both = _dot(jnp.concatenate([c.pop("qkm"), kgt], axis=0),
                        _block_diag(vn[:, 0:HEAD_DIM], vn[:, HEAD_DIM:2 * HEAD_DIM]))
            o = r3[CHUNK:2 * CHUNK] + both[0:CHUNK]
            o_s[ra, sl] = o[:, 0:HEAD_DIM]
            o_s[rb, sl] = o[:, HEAD_DIM:2 * HEAD_DIM]
            upd = both[CHUNK:CHUNK + HEAD_DIM]
            s_ref[c["ia"]] = s_ref[c["ia"]] * jnp.exp(gla) + upd[:, 0:HEAD_DIM]
            s_ref[c["ib"]] = s_ref[c["ib"]] * jnp.exp(glb) + upd[:, HEAD_DIM:2 * HEAD_DIM]

    zb = _proj(hb, win_ref, 2 * D_A + D_QKV, D_MAIN)
    yb = _head_rms(o_s[...], onw_ref[...]) * _silu(zb)
    ymix = jnp.concatenate([ya, yb], axis=1)
    yo = jnp.dot(ymix.astype(BF16), wout_ref[...], preferred_element_type=F32)
    yn = yo * lax.rsqrt(jnp.mean(yo * yo, axis=-1, keepdims=True) + EPS)
    y_ref[...] = x + yn.reshape(nb, CHUNK, D_MODEL) * (gate * postw_ref[...][None])

    qkv_ext[:, :, 0:SUBLANES, :] = qkv_ext[:, :, CHUNK:CHUNK + SUBLANES, :]
    ua_ext[:, 0:CHUNK, :] = ua3

    @pl.when(step == pl.num_programs(0) - 1)
    def _():
        npool_ref[...] = ua3[:, CHUNK - POOL_BUF:CHUNK, :]
        nconv_ref[...] = qkv.reshape(nb, CHUNK, D_QKV)[:, CHUNK - (CONV_K - 1):CHUNK, :]
        nssm_ref[...] = s_ref[...]


def _layer_spec(shape, l):
    return pl.BlockSpec((None,) + tuple(shape), lambda s, _n=len(shape): (l,) + (0,) * _n)


def _prompt_layer(l, x, mod_p, lw, consts, prev):
    nb, seq, _ = x.shape
    depth = lw["win"].shape[0]
    rows = nb * CHUNK
    nstep = seq // CHUNK
    full = lambda shape: pl.BlockSpec(shape, lambda s, _n=len(shape): (0,) * _n)
    in_specs = [
        pl.BlockSpec((nb, CHUNK, D_MODEL), lambda s: (0, s, 0)),
        _layer_spec((nb, 1, 3 * D_MODEL), l),
        _layer_spec((1, D_MODEL), l), _layer_spec((1, D_MODEL), l),
        pl.BlockSpec((None, D_MAIN, D_MODEL), lambda s: (l, 0, 0)),
        pl.BlockSpec((None, SUBLANES, D_MODEL), lambda s: (l, D_MAIN // SUBLANES, 0)),
        _layer_spec((CONV_K, D_QKV), l), _layer_spec((N_POOL // 2, 2 * POOL_GC, 2 * POOL_GC), l),
        _layer_spec((1, D_A), l),
        _layer_spec((1, LANES), l), _layer_spec((1, LANES), l),
        _layer_spec((SUBLANES, rows), l), _layer_spec((SUBLANES, rows), l),
        _layer_spec((1, HEAD_DIM), l), _layer_spec((D_MODEL, D_MODEL), l),
        full((N_POOL, CHUNK, 2 * CHUNK)), full((CHUNK, 2 * CHUNK)),
    ] + [pl.BlockSpec(memory_space=pl.ANY)] * len(prev)
    out_specs = [
        pl.BlockSpec((nb, CHUNK, D_MODEL), lambda s: (0, s, 0)),
        _layer_spec((nb, POOL_BUF, D_A), l), _layer_spec((nb, CONV_K - 1, D_QKV), l),
        _layer_spec((nb * N_HEADS, HEAD_DIM, HEAD_DIM), l),
    ]
    out_shape = [
        jax.ShapeDtypeStruct((nb, seq, D_MODEL), F32),
        jax.ShapeDtypeStruct((depth, nb, POOL_BUF, D_A), F32),
        jax.ShapeDtypeStruct((depth, nb, CONV_K - 1, D_QKV), F32),
        jax.ShapeDtypeStruct((depth, nb * N_HEADS, HEAD_DIM, HEAD_DIM), F32),
    ]
    scratch = [
        pltpu.VMEM((QKV_TILES, nb, CHUNK + SUBLANES, LANES), F32),
        pltpu.VMEM((nb, 2 * CHUNK, D_A), F32),
        pltpu.VMEM((nb * N_HEADS, HEAD_DIM, HEAD_DIM), F32),
        pltpu.VMEM((rows, D_B), F32), pltpu.VMEM((rows, D_B), F32), pltpu.VMEM((N_HEADS, HEAD_DIM, rows), F32),
        pltpu.VMEM((rows, D_B), F32), pltpu.VMEM((rows, D_B), F32),
        pltpu.VMEM((N_HEADS, rows, LANES), F32), pltpu.VMEM((N_HEADS, rows, LANES), F32),
        pltpu.VMEM((N_HEADS, rows, LANES), F32),
        pltpu.VMEM((SUBLANES, rows), F32),
    ]
    n_in = len(in_specs) - len(prev)
    return pl.pallas_call(
        functools.partial(_prompt_kernel, n_alias=len(prev)),
        grid=(nstep,),
        in_specs=in_specs,
        out_specs=out_specs,
        out_shape=out_shape,
        scratch_shapes=scratch,
        input_output_aliases={n_in + i: 1 + i for i in range(len(prev))},
        compiler_params=pltpu.CompilerParams(dimension_semantics=("arbitrary",), vmem_limit_bytes=VMEM_LIMIT),
        name="prompt_layer",
    )(x, mod_p, lw["prew"], lw["postw"], lw["win"], lw["win"], lw["convw"], lw["poolw2"], lw["pscale"],
      lw["alogc"], lw["dtbc"], lw["alogr"], lw["dtbr"], lw["onw"], lw["wout"], consts["band"], consts["ltri"], *prev)


def _sample_kernel(x_ref, mod_ref, spool_ref, sconv_ref, ssm_ref, prew_ref, postw_ref, win_ref, wba_ref, convw_ref,
                   poolw_ref, pscale_ref, alogc_ref, dtbc_ref, onw_ref, wout_ref, *rest, n_alias):
    rest = rest[n_alias:]
    y_ref, npool_ref, nconv_ref, nssm_ref = rest[0:4]
    q_s, k_s, v_s, o_s, beta_s, eg_s, qk_s, ya_s, zb_s = rest[4:]
    step = pl.program_id(0)
    nseq = x_ref.shape[0]

    @pl.when(step == 0)
    def _():
        x = x_ref[...]
        mod = mod_ref[...]
        shift = mod[:, 0:D_MODEL]
        scale = mod[:, D_MODEL:2 * D_MODEL]
        a_mul = prew_ref[...] * (1.0 + scale)
        ms = jnp.mean(x * x, axis=-1, keepdims=True)
        hb = (x * lax.rsqrt(ms + EPS) * a_mul + shift).astype(BF16)

        ua = _proj(hb, win_ref, 0, D_A)
        ya_parts = []
        for gi, w in enumerate(POOL_WINDOWS):
            gs = slice(gi * POOL_GC, (gi + 1) * POOL_GC)
            win = ua[:, gs]
            for d in range(1, w):
                win = win + spool_ref[POOL_BUF - d, :, gs]
            cnt = float(min(PAST_LEN + 1, w))
            pooled = win / cnt - ua[:, gs]
            ya_parts.append(_dot(pooled, poolw_ref[gi]))
        za = _proj(hb, win_ref, D_A, 2 * D_A)
        ya_s[...] = jnp.concatenate(ya_parts, axis=1) * pscale_ref[...] * _silu(za)
        npool_ref[0:POOL_BUF - 1] = spool_ref[1:POOL_BUF]
        npool_ref[POOL_BUF - 1] = ua

        qkv = _proj(hb, win_ref, 2 * D_A, 2 * D_A + D_QKV)
        cw = convw_ref[...]
        acc = qkv * cw[CONV_K - 1:CONV_K]
        for j in range(CONV_K - 1):
            acc = acc + sconv_ref[j] * cw[j:j + 1]
        nconv_ref[0:CONV_K - 2] = sconv_ref[1:CONV_K - 1]
        nconv_ref[CONV_K - 2] = qkv
        qkvc = _silu(acc)
        bac = _gate_logits(hb, wba_ref)
        sig = jax.nn.sigmoid(bac)
        eg = jnp.exp(-jnp.exp(alogc_ref[...]) * _softplus(bac + dtbc_ref[...]))
        for hh in range(N_HEADS):
            sl = slice(hh * HEAD_DIM, (hh + 1) * HEAD_DIM)
            qh = qkvc[:, hh * HEAD_DIM:(hh + 1) * HEAD_DIM]
            kh = qkvc[:, D_B + hh * HEAD_DIM:D_B + (hh + 1) * HEAD_DIM]
            qn = qh * lax.rsqrt(jnp.sum(qh * qh, axis=-1, keepdims=True) + EPS) * (HEAD_DIM ** -0.5)
            kn = kh * lax.rsqrt(jnp.sum(kh * kh, axis=-1, keepdims=True) + EPS)
            q_s[:, sl] = qn
            k_s[:, sl] = kn
            qk_s[hh] = jnp.broadcast_to(jnp.sum(qn * kn, axis=-1, keepdims=True), (nseq, LANES))
            beta_s[hh] = jnp.broadcast_to(sig[:, hh:hh + 1], (nseq, LANES))
            eg_s[hh] = jnp.broadcast_to(eg[:, N_HEADS + hh:N_HEADS + hh + 1], (nseq, LANES))
        v_s[...] = qkvc[:, 2 * D_B:3 * D_B]
        zb_s[...] = _proj(hb, win_ref, 2 * D_A + D_QKV, D_MAIN)

    r0 = pl.multiple_of(step * SAMPLE_BLOCK, SAMPLE_BLOCK)
    kblk = k_s[pl.ds(r0, SAMPLE_BLOCK), :]
    qblk = q_s[pl.ds(r0, SAMPLE_BLOCK), :]
    vblk = v_s[pl.ds(r0, SAMPLE_BLOCK), :]
    row8 = lax.broadcasted_iota(jnp.int32, (SUBLANES, HEAD_DIM), 0)
    st = []
    for i in range(SAMPLE_BLOCK):
        for hh in range(N_HEADS):
            sl = slice(hh * HEAD_DIM, (hh + 1) * HEAD_DIM)
            krow, qrow = kblk[i:i + 1, sl], qblk[i:i + 1, sl]
            lhs = jnp.where(row8 == 0, krow, jnp.where(row8 == 1, qrow, 0.0))
            r = jnp.dot(lhs, ssm_ref[i * N_HEADS + hh], preferred_element_type=F32)
            st.append(dict(i=i, hh=hh, sl=sl, krow=krow, r=r))
    for c in st:
        i, hh = c["i"], c["hh"]
        eg = eg_s[hh, pl.ds(r0 + i, 1), :]
        r = c.pop("r")
        delta = (vblk[i:i + 1, c["sl"]] - eg * r[0:1]) * beta_s[hh, pl.ds(r0 + i, 1), :]
        c["o"] = eg * r[1:2] + qk_s[hh, pl.ds(r0 + i, 1), :] * delta
        krow = c.pop("krow")
        k_hi = krow.astype(BF16).astype(F32)
        d_hi = delta.astype(BF16).astype(F32)
        kp = jnp.where(row8 == 0, k_hi, jnp.where(row8 == 1, krow - k_hi, jnp.where(row8 == 2, k_hi, 0.0)))
        dp = jnp.where(row8 == 0, d_hi, jnp.where(row8 == 1, d_hi, jnp.where(row8 == 2, delta - d_hi, 0.0)))
        c["upd"] = _dot_tn(kp, dp)
    for c in st:
        idx = c["i"] * N_HEADS + c["hh"]
        nssm_ref[idx] = ssm_ref[idx] * eg_s[c["hh"], pl.ds(r0 + c["i"], 1), :] + c.pop("upd")
    o_rows = [jnp.concatenate([c["o"] for c in st[i * N_HEADS:(i + 1) * N_HEADS]], axis=1)
              for i in range(SAMPLE_BLOCK)]
    o_s[pl.ds(r0, SAMPLE_BLOCK), :] = jnp.concatenate(o_rows, axis=0)

    @pl.when(step == pl.num_programs(0) - 1)
    def _():
        x = x_ref[...]
        gate = mod_ref[:, 2 * D_MODEL:3 * D_MODEL]
        yb = _head_rms(o_s[...], onw_ref[...]) * _silu(zb_s[...])
        ymix = jnp.concatenate([ya_s[...], yb], axis=1)
        yo = jnp.dot(ymix.astype(BF16), wout_ref[...], preferred_element_type=F32)
        yn = yo * lax.rsqrt(jnp.mean(yo * yo, axis=-1, keepdims=True) + EPS) * postw_ref[...]
        y_ref[...] = x + gate * yn


def _sample_layer(l, x, mod_s, state_pool, state_conv, state_ssm, lw, prev):
    nseq = x.shape[0]
    depth = lw["win"].shape[0]
    nstep = nseq // SAMPLE_BLOCK
    blk_states = SAMPLE_BLOCK * N_HEADS
    full = lambda shape: pl.BlockSpec(shape, lambda s, _n=len(shape): (0,) * _n)
    ssm_spec = pl.BlockSpec((None, blk_states, HEAD_DIM, HEAD_DIM), lambda s: (l, s, 0, 0))
    in_specs = [
        full((nseq, D_MODEL)), _layer_spec((nseq, 3 * D_MODEL), l),
        _layer_spec((POOL_BUF, nseq, D_A), l), _layer_spec((CONV_K - 1, nseq, D_QKV), l), ssm_spec,
        _layer_spec((1, D_MODEL), l), _layer_spec((1, D_MODEL), l),
        pl.BlockSpec((None, D_MAIN, D_MODEL), lambda s: (l, 0, 0)),
        pl.BlockSpec((None, SUBLANES, D_MODEL), lambda s: (l, D_MAIN // SUBLANES, 0)),
        _layer_spec((CONV_K, D_QKV), l), _layer_spec((N_POOL, POOL_GC, POOL_GC), l), _layer_spec((1, D_A), l),
        _layer_spec((1, LANES), l), _layer_spec((1, LANES), l), _layer_spec((1, HEAD_DIM), l),
        _layer_spec((D_MODEL, D_MODEL), l),
    ] + [pl.BlockSpec(memory_space=pl.ANY)] * len(prev)
    out_specs = [
        full((nseq, D_MODEL)), _layer_spec((POOL_BUF, nseq, D_A), l), _layer_spec((CONV_K - 1, nseq, D_QKV), l),
        ssm_spec,
    ]
    out_shape = [
        jax.ShapeDtypeStruct((nseq, D_MODEL), F32),
        jax.ShapeDtypeStruct((depth, POOL_BUF, nseq, D_A), F32),
        jax.ShapeDtypeStruct((depth, CONV_K - 1, nseq, D_QKV), F32),
        jax.ShapeDtypeStruct((depth, nseq * N_HEADS, HEAD_DIM, HEAD_DIM), F32),
    ]
    scratch = [
        pltpu.VMEM((nseq, D_B), F32), pltpu.VMEM((nseq, D_B), F32), pltpu.VMEM((nseq, D_B), F32),
        pltpu.VMEM((nseq, D_B), F32),
        pltpu.VMEM((N_HEADS, nseq, LANES), F32), pltpu.VMEM((N_HEADS, nseq, LANES), F32),
        pltpu.VMEM((N_HEADS, nseq, LANES), F32),
        pltpu.VMEM((nseq, D_A), F32), pltpu.VMEM((nseq, D_B), F32),
    ]
    n_in = len(in_specs) - len(prev)
    return pl.pallas_call(
        functools.partial(_sample_kernel, n_alias=len(prev)),
        grid=(nstep,),
        in_specs=in_specs,
        out_specs=out_specs,
        out_shape=out_shape,
        scratch_shapes=scratch,
        input_output_aliases={n_in + i: 1 + i for i in range(len(prev))},
        compiler_params=pltpu.CompilerParams(dimension_semantics=("arbitrary",), vmem_limit_bytes=VMEM_LIMIT),
        name="sample_layer",
    )(x, mod_s, state_pool, state_conv, state_ssm, lw["prew"], lw["postw"], lw["win"], lw["win"], lw["convw"],
      lw["poolw"], lw["pscale"], lw["alogc"], lw["dtbc"], lw["onw"], lw["wout"], *prev)


def _constants():
    t = np.arange(CHUNK)[:, None]
    j = np.arange(2 * CHUNK)[None, :]
    band = np.stack([((j <= CHUNK + t) & (j > CHUNK + t - w)) for w in POOL_WINDOWS]).astype(np.float32)
    tri = (np.arange(CHUNK)[None, :] <= t).astype(np.float32)
    ltri = np.concatenate([tri, tri], axis=1)
    return {"band": jnp.asarray(band, BF16), "ltri": jnp.asarray(ltri, BF16)}


def _stacked_weights(rows_p, pre_norm_w, post_norm_w, w_in, conv_w, pool_w, pool_scale, a_log, dt_bias, o_norm_w,
                     w_out):
    depth = w_in.shape[0]
    w_in_t = jnp.swapaxes(w_in, 1, 2)
    pw2 = jnp.zeros((depth, N_POOL // 2, 2 * POOL_GC, 2 * POOL_GC), F32)
    pw2 = pw2.at[:, :, 0:POOL_GC, 0:POOL_GC].set(pool_w[:, 0::2])
    pw2 = pw2.at[:, :, POOL_GC:2 * POOL_GC, POOL_GC:2 * POOL_GC].set(pool_w[:, 1::2])
    pad_c = lambda v: jnp.zeros((depth, 1, LANES), F32).at[:, 0, N_HEADS:2 * N_HEADS].set(v)
    pad_r = lambda v: jnp.broadcast_to(
        jnp.zeros((depth, SUBLANES), F32).at[:, N_HEADS:2 * N_HEADS].set(v)[:, :, None], (depth, SUBLANES, rows_p))
    return {
        "prew": pre_norm_w.reshape(depth, 1, D_MODEL),
        "postw": post_norm_w.reshape(depth, 1, D_MODEL),
        "win": w_in_t.astype(BF16),
        "convw": conv_w,
        "poolw": pool_w.astype(BF16),
        "poolw2": pw2.astype(BF16),
        "pscale": pool_scale.reshape(depth, 1, D_A),
        "alogc": pad_c(a_log), "dtbc": pad_c(dt_bias),
        "alogr": pad_r(a_log), "dtbr": pad_r(dt_bias),
        "onw": o_norm_w.reshape(depth, 1, HEAD_DIM),
        "wout": w_out.astype(BF16),
    }


def kernel(x_prompt, x_sample, c_prompt, c_sample, state_pool, state_conv, state_ssm, w_ada, b_ada, pre_norm_w,
           post_norm_w, w_in, conv_w, pool_w, pool_scale, a_log, dt_bias, o_norm_w, w_out):
    depth = w_in.shape[0]
    nb, seq, _ = x_prompt.shape
    nseq, dec_seq, _ = x_sample.shape
    assert dec_seq == 1 and seq % CHUNK == 0 and nb % 2 == 0 and nseq % SAMPLE_BLOCK == 0
    consts = _constants()
    lw = _stacked_weights(nb * CHUNK, pre_norm_w, post_norm_w, w_in, conv_w, pool_w, pool_scale, a_log, dt_bias,
                          o_norm_w, w_out)
    mod_p, mod_s = _mod_call(c_prompt, c_sample, w_ada, b_ada)
    mod_p = mod_p.reshape(depth, nb, 1, 3 * D_MODEL)
    ssm_in = state_ssm.reshape(depth, nseq * N_HEADS, HEAD_DIM, HEAD_DIM)
    pool_in = jnp.swapaxes(state_pool, 1, 2)
    conv_in = jnp.swapaxes(state_conv, 1, 2)
    yp, ys = x_prompt, x_sample.reshape(nseq, D_MODEL)
    prev_p, prev_s = (), ()
    for l in range(depth):
        yp, *prev_p = _prompt_layer(l, yp, mod_p, lw, consts, tuple(prev_p))
        ys, *prev_s = _sample_layer(l, ys, mod_s, pool_in, conv_in, ssm_in, lw, tuple(prev_s))
    npool_p, nconv_p, nssm_p = prev_p
    npool_s, nconv_s, nssm_s = prev_s
    return (yp, ys.reshape(nseq, dec_seq, D_MODEL), npool_p, nconv_p,
            nssm_p.reshape(depth, nb, N_HEADS, HEAD_DIM, HEAD_DIM),
            jnp.swapaxes(npool_s, 1, 2), jnp.swapaxes(nconv_s, 1, 2),
            nssm_s.reshape(depth, nseq, N_HEADS, HEAD_DIM, HEAD_DIM))
```

```python
import functools

import jax
import jax.numpy as jnp
import numpy as np
from jax import lax
from jax.experimental import pallas as pl
from jax.experimental.pallas import tpu as pltpu

F32 = jnp.float32
BF16 = jnp.bfloat16

D_MODEL = 1024
D_A = 512
D_B = 512
N_POOL = 4
POOL_WINDOWS = (2, 4, 8, 16)
POOL_GC = 128
POOL_BUF = 15
HEAD_DIM = 128
N_HEADS = 4
D_QKV = 3 * D_B
CONV_K = 4
D_MAIN = 2 * D_A + D_QKV + D_B
PAST_LEN = 16384
EPS = 1e-6
CHUNK = 64
GDN_GROUP = 16
LANES = 128
SUBLANES = 8
QKV_TILES = D_QKV // LANES
VMEM_LIMIT = 56 * 1024 * 1024


def _dot(a, b):
    return jnp.dot(a.astype(BF16), b.astype(BF16), preferred_element_type=F32)


def _dot_tn(a, b):
    return lax.dot_general(a, b, (((0,), (0,)), ((), ())), preferred_element_type=F32)


def _proj(hb, wt_ref, c0, c1):
    return lax.dot_general(hb, wt_ref[c0:c1, :], (((1,), (1,)), ((), ())), preferred_element_type=F32)


def _gate_logits(hb, wba_ref):
    w = jnp.concatenate([wba_ref[...], jnp.zeros((LANES - SUBLANES, wba_ref.shape[1]), wba_ref.dtype)], axis=0)
    return lax.dot_general(hb, w, (((1,), (1,)), ((), ())), preferred_element_type=F32)


def _silu(x):
    return x * jax.nn.sigmoid(x)


def _softplus(x):
    return jnp.maximum(x, 0.0) + jnp.log1p(jnp.exp(-jnp.abs(x)))


def _block_diag(a, b):
    top = jnp.concatenate([a, jnp.zeros((a.shape[0], b.shape[1]), a.dtype)], axis=1)
    bot = jnp.concatenate([jnp.zeros((b.shape[0], a.shape[1]), b.dtype), b], axis=1)
    return jnp.concatenate([top, bot], axis=0)


def _head_rms(o_all, w):
    parts = []
    for hh in range(N_HEADS):
        oh = o_all[:, hh * HEAD_DIM:(hh + 1) * HEAD_DIM]
        parts.append(oh * lax.rsqrt(jnp.mean(oh * oh, axis=-1, keepdims=True) + EPS) * w)
    return jnp.concatenate(parts, axis=1)


def _mod_kernel(cp_ref, cs_ref, w_ref, b_ref, op_ref, os_ref):
    w = w_ref[...]
    b = b_ref[...]
    op_ref[...] = jnp.dot(_silu(cp_ref[...]), w, preferred_element_type=F32) + b
    os_ref[...] = jnp.dot(_silu(cs_ref[...]), w, preferred_element_type=F32) + b


def _mod_call(c_prompt, c_sample, w_ada, b_ada):
    depth, _, n3 = w_ada.shape
    nb, nseq = c_prompt.shape[0], c_sample.shape[0]
    nblk = n3 // D_MODEL
    return pl.pallas_call(
        _mod_kernel,
        grid=(depth, nblk),
        in_specs=[
            pl.BlockSpec((nb, D_MODEL), lambda l, j: (0, 0)),
            pl.BlockSpec((nseq, D_MODEL), lambda l, j: (0, 0)),
            pl.BlockSpec((None, D_MODEL, D_MODEL), lambda l, j: (l, 0, j)),
            pl.BlockSpec((None, 1, D_MODEL), lambda l, j: (l, 0, j)),
        ],
        out_specs=[pl.BlockSpec((None, nb, D_MODEL), lambda l, j: (l, 0, j)),
                   pl.BlockSpec((None, nseq, D_MODEL), lambda l, j: (l, 0, j))],
        out_shape=[jax.ShapeDtypeStruct((depth, nb, n3), F32), jax.ShapeDtypeStruct((depth, nseq, n3), F32)],
        compiler_params=pltpu.CompilerParams(dimension_semantics=("arbitrary", "arbitrary")),
        name="adaln_mod",
    )(c_prompt, c_sample, w_ada, b_ada.reshape(depth, 1, n3))


def _prompt_kernel(x_ref, mod_ref, prew_ref, postw_ref, win_ref, wba_ref, convw_ref, poolw_ref,
                   pscale_ref, alogc_ref, dtbc_ref, alogr_ref, dtbr_ref, onw_ref, wout_ref, band_ref, ltri_ref,
                   xs_ref, sgate_ref, sq_ref, sk_ref, sv_ref, sbeta_ref, seg_ref, sqk_ref, sya_ref, szb_ref, sssm_ref,
                   *rest, n_alias):
    rest = rest[n_alias:]
    y_ref, npool_ref, nconv_ref, nssm_ref, nsssm_ref, ys_ref = rest[0:6]
    qkv_ext, ua_ext, s_ref, q_s, k_s, kt_s, v_s, o_s, beta_s, gc_s, eg_s, gcrow_s, so_s = rest[6:]
    step = pl.program_id(0)
    nb = x_ref.shape[0]
    rows = nb * CHUNK

    @pl.when(step == 0)
    def _():
        qkv_ext[:, :, 0:SUBLANES, :] = jnp.zeros((QKV_TILES, nb, SUBLANES, LANES), F32)
        ua_ext[:, 0:CHUNK, :] = jnp.zeros((nb, CHUNK, D_A), F32)
        s_ref[...] = jnp.zeros(s_ref.shape, F32)
        so_s[...] = jnp.zeros(so_s.shape, F32)

    x = x_ref[...]
    mod = mod_ref[...]
    shift = mod[:, :, 0:D_MODEL]
    scale = mod[:, :, D_MODEL:2 * D_MODEL]
    gate = mod[:, :, 2 * D_MODEL:3 * D_MODEL]
    a_mul = prew_ref[...][None] * (1.0 + scale)
    ms = jnp.mean(x * x, axis=-1, keepdims=True)
    h = x * lax.rsqrt(ms + EPS) * a_mul + shift
    hb = h.reshape(rows, D_MODEL).astype(BF16)

    ua = _proj(hb, win_ref, 0, D_A)
    ua3 = ua.reshape(nb, CHUNK, D_A)
    ua_ext[:, CHUNK:2 * CHUNK, :] = ua3
    pos = lax.broadcasted_iota(jnp.int32, (CHUNK, POOL_GC), 0) + step * CHUNK
    pooled_groups = []
    for gi, w in enumerate(POOL_WINDOWS):
        cnt = jnp.minimum(pos + 1, w).astype(F32)
        per_b = []
        for b in range(nb):
            ext = ua_ext[b, :, gi * POOL_GC:(gi + 1) * POOL_GC]
            win = jnp.dot(band_ref[gi], ext.astype(BF16), preferred_element_type=F32)
            per_b.append(win / cnt - ua3[b, :, gi * POOL_GC:(gi + 1) * POOL_GC])
        pooled_groups.append(jnp.concatenate(per_b, axis=0))
    ya = jnp.concatenate([_dot(jnp.concatenate(pooled_groups[2 * g2:2 * g2 + 2], axis=1), poolw_ref[g2])
                          for g2 in range(N_POOL // 2)], axis=1)
    za = _proj(hb, win_ref, D_A, 2 * D_A)
    ya = ya * pscale_ref[...] * _silu(za)

    qkv = _proj(hb, win_ref, 2 * D_A, 2 * D_A + D_QKV)
    qkv_t = jnp.stack([qkv[:, c * LANES:(c + 1) * LANES] for c in range(QKV_TILES)], axis=0)
    qkv_t = qkv_t.reshape(QKV_TILES, nb, CHUNK, LANES)
    qkv_ext[:, :, SUBLANES:SUBLANES + CHUNK, :] = qkv_t
    cw = convw_ref[...]
    cw_t = [jnp.stack([cw[j:j + 1, c * LANES:(c + 1) * LANES] for c in range(QKV_TILES)], axis=0)[:, None]
            for j in range(CONV_K)]
    acc = qkv_t * cw_t[CONV_K - 1]
    for j in range(CONV_K - 1):
        acc = acc + qkv_ext[:, :, pl.ds(SUBLANES - (CONV_K - 1) + j, CHUNK), :] * cw_t[j]
    act = _silu(acc).reshape(QKV_TILES, rows, LANES)
    for hh in range(N_HEADS):
        sl = slice(hh * HEAD_DIM, (hh + 1) * HEAD_DIM)
        qh, kh = act[hh], act[N_HEADS + hh]
        q_s[:, sl] = qh * lax.rsqrt(jnp.sum(qh * qh, axis=-1, keepdims=True) + EPS) * (HEAD_DIM ** -0.5)
        kn = kh * lax.rsqrt(jnp.sum(kh * kh, axis=-1, keepdims=True) + EPS)
        k_s[:, sl] = kn
        kt_s[hh] = kn.T
        v_s[:, sl] = act[2 * N_HEADS + hh]

    bac = _gate_logits(hb, wba_ref)
    bar = bac.T[0:SUBLANES]
    sig = jax.nn.sigmoid(bac)
    gcol = -jnp.exp(alogc_ref[...]) * _softplus(bac + dtbc_ref[...])
    g_hi = gcol.astype(BF16)
    g_lo = (gcol - g_hi.astype(F32)).astype(BF16)
    gc_parts = []
    for b in range(nb):
        rs = slice(b * CHUNK, (b + 1) * CHUNK)
        gc_parts.append(jnp.dot(ltri_ref[...], jnp.concatenate([g_hi[rs], g_lo[rs]], axis=0),
                                preferred_element_type=F32))
    gc = jnp.concatenate(gc_parts, axis=0)
    for hh in range(N_HEADS):
        beta_s[hh] = jnp.broadcast_to(sig[:, hh:hh + 1], (rows, LANES))
        gcb = jnp.broadcast_to(gc[:, N_HEADS + hh:N_HEADS + hh + 1], (rows, LANES))
        gc_s[hh] = gcb
        eg_s[hh] = jnp.exp(gcb)
    grow = -jnp.exp(alogr_ref[...]) * _softplus(bar + dtbr_ref[...])
    lane_in_chunk = lax.broadcasted_iota(jnp.int32, grow.shape, 1) % CHUNK
    sh = 1
    while sh < CHUNK:
        grow = grow + jnp.where(lane_in_chunk >= sh, pltpu.roll(grow, sh, 1), 0.0)
        sh *= 2
    gcrow_s[...] = grow

    _decode_state_step(step, sq_ref, sk_ref, sv_ref, sbeta_ref, seg_ref, sqk_ref, sssm_ref, nsssm_ref, so_s)

    ri = lax.broadcasted_iota(jnp.int32, (CHUNK, LANES), 0)
    li = lax.broadcasted_iota(jnp.int32, (CHUNK, LANES), 1)
    lj = li % CHUNK
    lo_half = li < CHUNK
    strict = ri > lj
    causal = ri >= lj
    eye2 = (ri == lj).astype(F32)
    lo_half_hd = lax.broadcasted_iota(jnp.int32, (HEAD_DIM, LANES), 1) < CHUNK

    def pair_mul(xp, yp):
        ybd = jnp.concatenate([jnp.where(lo_half, yp, 0.0), jnp.where(lo_half, 0.0, yp)], axis=0)
        return _dot(xp, ybd)

    chains = [(p, hh) for p in range(nb // 2) for hh in range(N_HEADS)]
    for g0 in range(0, len(chains), GDN_GROUP):
        st = []
        for p, hh in chains[g0:g0 + GDN_GROUP]:
            ra = slice(2 * p * CHUNK, (2 * p + 1) * CHUNK)
            rb = slice((2 * p + 1) * CHUNK, (2 * p + 2) * CHUNK)
            sl = slice(hh * HEAD_DIM, (hh + 1) * HEAD_DIM)
            kba, kbb = k_s[ra, sl] * beta_s[hh, ra, :], k_s[rb, sl] * beta_s[hh, rb, :]
            lhs1 = jnp.concatenate([jnp.concatenate([kba, kbb], axis=1),
                                    jnp.concatenate([q_s[ra, sl], q_s[rb, sl]], axis=1)], axis=0)
            kt = kt_s[hh, :, 2 * p * CHUNK:(2 * p + 2) * CHUNK]
            ktbd = jnp.concatenate([jnp.where(lo_half_hd, kt, 0.0), jnp.where(lo_half_hd, 0.0, kt)], axis=0)
            kkqk = _dot(lhs1, ktbd)
            st.append(dict(ra=ra, rb=rb, sl=sl, hh=hh, p=p, kkqk=kkqk))
        for c in st:
            hh, p = c["hh"], c["p"]
            gcol_p = jnp.where(lo_half, gc_s[hh, c["ra"], :], gc_s[hh, c["rb"], :])
            grow_p = jnp.broadcast_to(gcrow_s[N_HEADS + hh:N_HEADS + hh + 1, 2 * p * CHUNK:(2 * p + 2) * CHUNK],
                                      (CHUNK, LANES))
            dec = jnp.exp(jnp.minimum(gcol_p - grow_p, 0.0))
            kkqk = c.pop("kkqk")
            c["npow"] = jnp.where(strict, kkqk[0:CHUNK] * dec, 0.0)
            c["qkm"] = jnp.where(causal, kkqk[CHUNK:2 * CHUNK] * dec, 0.0)
            c["t"] = eye2 - c["npow"]
        for c in st:
            c["npow"] = pair_mul(c["npow"], c["npow"])
        for _ in range(4):
            for c in st:
                both = pair_mul(jnp.concatenate([c["t"], c["npow"]], axis=0), c["npow"])
                c["t"] = c["t"] + both[0:CHUNK]
                c["npow"] = both[CHUNK:2 * CHUNK]
        for c in st:
            c["t"] = c["t"] + pair_mul(c["t"], c["npow"])
        for c in st:
            ra, rb, sl, hh = c["ra"], c["rb"], c["sl"], c["hh"]
            ba, bb = beta_s[hh, ra, :], beta_s[hh, rb, :]
            kba, kbb = k_s[ra, sl] * ba, k_s[rb, sl] * bb
            rhs = jnp.concatenate([_block_diag(v_s[ra, sl] * ba, v_s[rb, sl] * bb),
                                   _block_diag(kba * eg_s[hh, ra, :], kbb * eg_s[hh, rb, :])], axis=1)
            c["uw"] = _dot(c.pop("t"), rhs)
        for c in st:
            ra, rb, sl, hh, p = c["ra"], c["rb"], c["sl"], c["hh"], c["p"]
            c["ia"], c["ib"] = 2 * p * N_HEADS + hh, (2 * p + 1) * N_HEADS + hh
            lhs3 = jnp.concatenate([c["uw"][:, 2 * HEAD_DIM:4 * HEAD_DIM],
                                    jnp.concatenate([q_s[ra, sl] * eg_s[hh, ra, :], q_s[rb, sl] * eg_s[hh, rb, :]],
                                                    axis=1)], axis=0)
            c["r3"] = _dot(lhs3, _block_diag(s_ref[c["ia"]], s_ref[c["ib"]]))
        for c in st:
            ra, rb, sl, hh, p = c["ra"], c["rb"], c["sl"], c["hh"], c["p"]
            r3 = c.pop("r3")
            vn = c.pop("uw")[:, 0:2 * HEAD_DIM] - r3[0:CHUNK]
            gla = gc_s[hh, (2 * p + 1) * CHUNK - 1:(2 * p + 1) * CHUNK, :]
            glb = gc_s[hh, (2 * p + 2) * CHUNK - 1:(2 * p + 2) * CHUNK, :]
            pair_lanes = slice(2 * p * CHUNK, (2 * p + 2) * CHUNK)
            to_end = jnp.exp(jnp.where(lo_half[0:1], gla, glb) - gcrow_s[N_HEADS + hh:N_HEADS + hh + 1, pair_lanes])
            kgt = kt_s[hh, :, pair_lanes] * to_end
            both = _dot(jnp.concatenate([c.pop("qkm"), kgt], axis=0),
                        _block_diag(vn[:, 0:HEAD_DIM], vn[:, HEAD_DIM:2 * HEAD_DIM]))
            o = r3[CHUNK:2 * CHUNK] + both[0:CHUNK]
            o_s[ra, sl] = o[:, 0:HEAD_DIM]
            o_s[rb, sl] = o[:, HEAD_DIM:2 * HEAD_DIM]
            upd = both[CHUNK:CHUNK + HEAD_DIM]
            s_ref[c["ia"]] = s_ref[c["ia"]] * jnp.exp(gla) + upd[:, 0:HEAD_DIM]
            s_ref[c["ib"]] = s_ref[c["ib"]] * jnp.exp(glb) + upd[:, HEAD_DIM:2 * HEAD_DIM]

    zb = _proj(hb, win_ref, 2 * D_A + D_QKV, D_MAIN)
    yb = _head_rms(o_s[...], onw_ref[...]) * _silu(zb)
    ymix = jnp.concatenate([ya, yb], axis=1)
    yo = jnp.dot(ymix.astype(BF16), wout_ref[...], preferred_element_type=F32)
    yn = yo * lax.rsqrt(jnp.mean(yo * yo, axis=-1, keepdims=True) + EPS)
    y_ref[...] = x + yn.reshape(nb, CHUNK, D_MODEL) * (gate * postw_ref[...][None])

    qkv_ext[:, :, 0:SUBLANES, :] = qkv_ext[:, :, CHUNK:CHUNK + SUBLANES, :]
    ua_ext[:, 0:CHUNK, :] = ua3

    @pl.when(step == pl.num_programs(0) - 1)
    def _():
        npool_ref[...] = ua3[:, CHUNK - POOL_BUF:CHUNK, :]
        nconv_ref[...] = qkv.reshape(nb, CHUNK, D_QKV)[:, CHUNK - (CONV_K - 1):CHUNK, :]
        nssm_ref[...] = s_ref[...]
        yb_s = _head_rms(so_s[...], onw_ref[...]) * _silu(szb_ref[...])
        ymix_s = jnp.concatenate([sya_ref[...], yb_s], axis=1)
        yo_s = jnp.dot(ymix_s.astype(BF16), wout_ref[...], preferred_element_type=F32)
        yn_s = yo_s * lax.rsqrt(jnp.mean(yo_s * yo_s, axis=-1, keepdims=True) + EPS) * postw_ref[...]
        ys_ref[...] = xs_ref[...] + sgate_ref[...] * yn_s


def _layer_spec(shape, l):
    return pl.BlockSpec((None,) + tuple(shape), lambda s, _n=len(shape): (l,) + (0,) * _n)


def _prompt_layer(l, x, mod_p, lw, consts, dec, prev):
    nb, seq, _ = x.shape
    depth = lw["win"].shape[0]
    rows = nb * CHUNK
    nstep = seq // CHUNK
    nseq = dec["x"].shape[0]
    blk_states = nseq * N_HEADS // nstep
    assert blk_states % N_HEADS == 0 and SUBLANES % (blk_states // N_HEADS) == 0
    full = lambda shape: pl.BlockSpec(shape, lambda s, _n=len(shape): (0,) * _n)
    sssm_spec = pl.BlockSpec((None, blk_states, HEAD_DIM, HEAD_DIM), lambda s: (l, s, 0, 0))
    dec_specs = [
        full((nseq, D_MODEL)), pl.BlockSpec((None, nseq, D_MODEL), lambda s: (l, 0, 2)),
        full((nseq, D_B)), full((nseq, D_B)), full((nseq, D_B)),
        full((N_HEADS, nseq, LANES)), full((N_HEADS, nseq, LANES)), full((N_HEADS, nseq, LANES)),
        full((nseq, D_A)), full((nseq, D_B)), sssm_spec,
    ]
    in_specs = [
        pl.BlockSpec((nb, CHUNK, D_MODEL), lambda s: (0, s, 0)),
        _layer_spec((nb, 1, 3 * D_MODEL), l),
        _layer_spec((1, D_MODEL), l), _layer_spec((1, D_MODEL), l),
        pl.BlockSpec((None, D_MAIN, D_MODEL), lambda s: (l, 0, 0)),
        pl.BlockSpec((None, SUBLANES, D_MODEL), lambda s: (l, D_MAIN // SUBLANES, 0)),
        _layer_spec((CONV_K, D_QKV), l), _layer_spec((N_POOL // 2, 2 * POOL_GC, 2 * POOL_GC), l),
        _layer_spec((1, D_A), l),
        _layer_spec((1, LANES), l), _layer_spec((1, LANES), l),
        _layer_spec((SUBLANES, rows), l), _layer_spec((SUBLANES, rows), l),
        _layer_spec((1, HEAD_DIM), l), _layer_spec((D_MODEL, D_MODEL), l),
        full((N_POOL, CHUNK, 2 * CHUNK)), full((CHUNK, 2 * CHUNK)),
    ] + dec_specs + [pl.BlockSpec(memory_space=pl.ANY)] * len(prev)
    out_specs = [
        pl.BlockSpec((nb, CHUNK, D_MODEL), lambda s: (0, s, 0)),
        _layer_spec((nb, POOL_BUF, D_A), l), _layer_spec((nb, CONV_K - 1, D_QKV), l),
        _layer_spec((nb * N_HEADS, HEAD_DIM, HEAD_DIM), l),
        sssm_spec, full((nseq, D_MODEL)),
    ]
    out_shape = [
        jax.ShapeDtypeStruct((nb, seq, D_MODEL), F32),
        jax.ShapeDtypeStruct((depth, nb, POOL_BUF, D_A), F32),
        jax.ShapeDtypeStruct((depth, nb, CONV_K - 1, D_QKV), F32),
        jax.ShapeDtypeStruct((depth, nb * N_HEADS, HEAD_DIM, HEAD_DIM), F32),
        jax.ShapeDtypeStruct((depth, nseq * N_HEADS, HEAD_DIM, HEAD_DIM), F32),
        jax.ShapeDtypeStruct((nseq, D_MODEL), F32),
    ]
    scratch = [
        pltpu.VMEM((QKV_TILES, nb, CHUNK + SUBLANES, LANES), F32),
        pltpu.VMEM((nb, 2 * CHUNK, D_A), F32),
        pltpu.VMEM((nb * N_HEADS, HEAD_DIM, HEAD_DIM), F32),
        pltpu.VMEM((rows, D_B), F32), pltpu.VMEM((rows, D_B), F32), pltpu.VMEM((N_HEADS, HEAD_DIM, rows), F32),
        pltpu.VMEM((rows, D_B), F32), pltpu.VMEM((rows, D_B), F32),
        pltpu.VMEM((N_HEADS, rows, LANES), F32), pltpu.VMEM((N_HEADS, rows, LANES), F32),
        pltpu.VMEM((N_HEADS, rows, LANES), F32),
        pltpu.VMEM((SUBLANES, rows), F32),
        pltpu.VMEM((nseq, D_B), F32),
    ]
    n_in = len(in_specs) - len(prev)
    return pl.pallas_call(
        functools.partial(_prompt_kernel, n_alias=len(prev)),
        grid=(nstep,),
        in_specs=in_specs,
        out_specs=out_specs,
        out_shape=out_shape,
        scratch_shapes=scratch,
        input_output_aliases={n_in + i: 1 + i for i in range(len(prev))},
        compiler_params=pltpu.CompilerParams(dimension_semantics=("arbitrary",), vmem_limit_bytes=VMEM_LIMIT),
        name="prompt_layer",
    )(x, mod_p, lw["prew"], lw["postw"], lw["win"], lw["win"], lw["convw"], lw["poolw2"], lw["pscale"],
      lw["alogc"], lw["dtbc"], lw["alogr"], lw["dtbr"], lw["onw"], lw["wout"], consts["band"], consts["ltri"],
      dec["x"], dec["mod"], dec["q"], dec["k"], dec["v"], dec["beta"], dec["eg"], dec["qk"], dec["ya"], dec["zb"],
      dec["ssm"], *prev)


def _decode_front_kernel(x_ref, mod_ref, spool_ref, sconv_ref, prew_ref, win_ref, wba_ref, convw_ref, poolw_ref,
                         pscale_ref, alogc_ref, dtbc_ref, *rest, n_alias):
    rest = rest[n_alias:]
    npool_ref, nconv_ref, q_ref, k_ref, v_ref, beta_ref, eg_ref, qk_ref, ya_ref, zb_ref = rest
    nseq = x_ref.shape[0]
    x = x_ref[...]
    mod = mod_ref[...]
    shift = mod[:, 0:D_MODEL]
    scale = mod[:, D_MODEL:2 * D_MODEL]
    a_mul = prew_ref[...] * (1.0 + scale)
    ms = jnp.mean(x * x, axis=-1, keepdims=True)
    hb = (x * lax.rsqrt(ms + EPS) * a_mul + shift).astype(BF16)

    ua = _proj(hb, win_ref, 0, D_A)
    ya_parts = []
    for gi, w in enumerate(POOL_WINDOWS):
        gs = slice(gi * POOL_GC, (gi + 1) * POOL_GC)
        win = ua[:, gs]
        for d in range(1, w):
            win = win + spool_ref[POOL_BUF - d, :, gs]
        cnt = float(min(PAST_LEN + 1, w))
        pooled = win / cnt - ua[:, gs]
        ya_parts.append(_dot(pooled, poolw_ref[gi]))
    za = _proj(hb, win_ref, D_A, 2 * D_A)
    ya_ref[...] = jnp.concatenate(ya_parts, axis=1) * pscale_ref[...] * _silu(za)
    npool_ref[0:POOL_BUF - 1] = spool_ref[1:POOL_BUF]
    npool_ref[POOL_BUF - 1] = ua

    qkv = _proj(hb, win_ref, 2 * D_A, 2 * D_A + D_QKV)
    cw = convw_ref[...]
    acc = qkv * cw[CONV_K - 1:CONV_K]
    for j in range(CONV_K - 1):
        acc = acc + sconv_ref[j] * cw[j:j + 1]
    nconv_ref[0:CONV_K - 2] = sconv_ref[1:CONV_K - 1]
    nconv_ref[CONV_K - 2] = qkv
    qkvc = _silu(acc)
    bac = _gate_logits(hb, wba_ref)
    sig = jax.nn.sigmoid(bac)
    eg = jnp.exp(-jnp.exp(alogc_ref[...]) * _softplus(bac + dtbc_ref[...]))
    for hh in range(N_HEADS):
        sl = slice(hh * HEAD_DIM, (hh + 1) * HEAD_DIM)
        qh = qkvc[:, hh * HEAD_DIM:(hh + 1) * HEAD_DIM]
        kh = qkvc[:, D_B + hh * HEAD_DIM:D_B + (hh + 1) * HEAD_DIM]
        qn = qh * lax.rsqrt(jnp.sum(qh * qh, axis=-1, keepdims=True) + EPS) * (HEAD_DIM ** -0.5)
        kn = kh * lax.rsqrt(jnp.sum(kh * kh, axis=-1, keepdims=True) + EPS)
        q_ref[:, sl] = qn
        k_ref[:, sl] = kn
        qk_ref[hh] = jnp.broadcast_to(jnp.sum(qn * kn, axis=-1, keepdims=True), (nseq, LANES))
        beta_ref[hh] = jnp.broadcast_to(sig[:, hh:hh + 1], (nseq, LANES))
        eg_ref[hh] = jnp.broadcast_to(eg[:, N_HEADS + hh:N_HEADS + hh + 1], (nseq, LANES))
    v_ref[...] = qkvc[:, 2 * D_B:3 * D_B]
    zb_ref[...] = _proj(hb, win_ref, 2 * D_A + D_QKV, D_MAIN)


def _decode_state_step(step, q_ref, k_ref, v_ref, beta_ref, eg_ref, qk_ref, ssm_ref, nssm_ref, o_s):
    nrow = ssm_ref.shape[0] // N_HEADS
    per_tile = SUBLANES // nrow
    tile0 = pl.multiple_of((step // per_tile) * SUBLANES, SUBLANES)
    sub = step % per_tile

    def my_rows(tile):
        out = tile[0:nrow]
        for j in range(1, per_tile):
            out = jnp.where(sub == j, tile[j * nrow:(j + 1) * nrow], out)
        return out

    kblk = my_rows(k_ref[pl.ds(tile0, SUBLANES), :])
    qblk = my_rows(q_ref[pl.ds(tile0, SUBLANES), :])
    vblk = my_rows(v_ref[pl.ds(tile0, SUBLANES), :])
    beta = [my_rows(beta_ref[hh, pl.ds(tile0, SUBLANES), :]) for hh in range(N_HEADS)]
    egs = [my_rows(eg_ref[hh, pl.ds(tile0, SUBLANES), :]) for hh in range(N_HEADS)]
    qks = [my_rows(qk_ref[hh, pl.ds(tile0, SUBLANES), :]) for hh in range(N_HEADS)]
    row8 = lax.broadcasted_iota(jnp.int32, (SUBLANES, HEAD_DIM), 0)
    st = []
    for i in range(nrow):
        for hh in range(N_HEADS):
            sl = slice(hh * HEAD_DIM, (hh + 1) * HEAD_DIM)
            krow, qrow = kblk[i:i + 1, sl], qblk[i:i + 1, sl]
            lhs = jnp.where(row8 == 0, krow, jnp.where(row8 == 1, qrow, 0.0))
            r = jnp.dot(lhs, ssm_ref[i * N_HEADS + hh], preferred_element_type=F32)
            st.append(dict(i=i, hh=hh, sl=sl, krow=krow, r=r))
    for c in st:
        i, hh = c["i"], c["hh"]
        eg = egs[hh][i:i + 1]
        r = c.pop("r")
        delta = (vblk[i:i + 1, c["sl"]] - eg * r[0:1]) * beta[hh][i:i + 1]
        c["o"] = eg * r[1:2] + qks[hh][i:i + 1] * delta
        krow = c.pop("krow")
        k_hi = krow.astype(BF16).astype(F32)
        d_hi = delta.astype(BF16).astype(F32)
        kp = jnp.where(row8 == 0, k_hi, jnp.where(row8 == 1, krow - k_hi, jnp.where(row8 == 2, k_hi, 0.0)))
        dp = jnp.where(row8 == 0, d_hi, jnp.where(row8 == 1, d_hi, jnp.where(row8 == 2, delta - d_hi, 0.0)))
        c["upd"] = _dot_tn(kp, dp)
    for c in st:
        idx = c["i"] * N_HEADS + c["hh"]
        nssm_ref[idx] = ssm_ref[idx] * egs[c["hh"]][c["i"]:c["i"] + 1] + c.pop("upd")
    o_rows = [jnp.concatenate([c["o"] for c in st[i * N_HEADS:(i + 1) * N_HEADS]], axis=1) for i in range(nrow)]
    tile = o_s[pl.ds(tile0, SUBLANES), :]
    placed = jnp.concatenate(o_rows * per_tile, axis=0)
    row_group = lax.broadcasted_iota(jnp.int32, tile.shape, 0) // nrow
    o_s[pl.ds(tile0, SUBLANES), :] = jnp.where(row_group == sub, placed, tile)


def _decode_front(l, x, mod_s, state_pool, state_conv, lw, prev):
    nseq = x.shape[0]
    depth = lw["win"].shape[0]
    full = lambda shape: pl.BlockSpec(shape, lambda s, _n=len(shape): (0,) * _n)
    in_specs = [
        full((nseq, D_MODEL)), _layer_spec((nseq, 3 * D_MODEL), l),
        _layer_spec((POOL_BUF, nseq, D_A), l), _layer_spec((CONV_K - 1, nseq, D_QKV), l),
        _layer_spec((1, D_MODEL), l),
        pl.BlockSpec((None, D_MAIN, D_MODEL), lambda s: (l, 0, 0)),
        pl.BlockSpec((None, SUBLANES, D_MODEL), lambda s: (l, D_MAIN // SUBLANES, 0)),
        _layer_spec((CONV_K, D_QKV), l), _layer_spec((N_POOL, POOL_GC, POOL_GC), l), _layer_spec((1, D_A), l),
        _layer_spec((1, LANES), l), _layer_spec((1, LANES), l),
    ] + [pl.BlockSpec(memory_space=pl.ANY)] * len(prev)
    row_out = lambda width: (full((nseq, width)), jax.ShapeDtypeStruct((nseq, width), F32))
    head_out = (full((N_HEADS, nseq, LANES)), jax.ShapeDtypeStruct((N_HEADS, nseq, LANES), F32))
    outs = [
        (_layer_spec((POOL_BUF, nseq, D_A), l), jax.ShapeDtypeStruct((depth, POOL_BUF, nseq, D_A), F32)),
        (_layer_spec((CONV_K - 1, nseq, D_QKV), l), jax.ShapeDtypeStruct((depth, CONV_K - 1, nseq, D_QKV), F32)),
        row_out(D_B), row_out(D_B), row_out(D_B), head_out, head_out, head_out, row_out(D_A), row_out(D_B),
    ]
    n_in = len(in_specs) - len(prev)
    npool, nconv, q, k, v, beta, eg, qk, ya, zb = pl.pallas_call(
        functools.partial(_decode_front_kernel, n_alias=len(prev)),
        grid=(1,),
        in_specs=in_specs,
        out_specs=[o[0] for o in outs],
        out_shape=[o[1] for o in outs],
        input_output_aliases={n_in + i: i for i in range(len(prev))},
        compiler_params=pltpu.CompilerParams(dimension_semantics=("arbitrary",), vmem_limit_bytes=VMEM_LIMIT),
        name="decode_front",
    )(x, mod_s, state_pool, state_conv, lw["prew"], lw["win"], lw["win"], lw["convw"], lw["poolw"], lw["pscale"],
      lw["alogc"], lw["dtbc"], *prev)
    return (npool, nconv), dict(x=x, mod=mod_s, q=q, k=k, v=v, beta=beta, eg=eg, qk=qk, ya=ya, zb=zb)


def _constants():
    t = np.arange(CHUNK)[:, None]
    j = np.arange(2 * CHUNK)[None, :]
    band = np.stack([((j <= CHUNK + t) & (j > CHUNK + t - w)) for w in POOL_WINDOWS]).astype(np.float32)
    tri = (np.arange(CHUNK)[None, :] <= t).astype(np.float32)
    ltri = np.concatenate([tri, tri], axis=1)
    return {"band": jnp.asarray(band, BF16), "ltri": jnp.asarray(ltri, BF16)}


def _stacked_weights(rows_p, pre_norm_w, post_norm_w, w_in, conv_w, pool_w, pool_scale, a_log, dt_bias, o_norm_w,
                     w_out):
    depth = w_in.shape[0]
    w_in_t = jnp.swapaxes(w_in, 1, 2)
    pw2 = jnp.zeros((depth, N_POOL // 2, 2 * POOL_GC, 2 * POOL_GC), F32)
    pw2 = pw2.at[:, :, 0:POOL_GC, 0:POOL_GC].set(pool_w[:, 0::2])
    pw2 = pw2.at[:, :, POOL_GC:2 * POOL_GC, POOL_GC:2 * POOL_GC].set(pool_w[:, 1::2])
    pad_c = lambda v: jnp.zeros((depth, 1, LANES), F32).at[:, 0, N_HEADS:2 * N_HEADS].set(v)
    pad_r = lambda v: jnp.broadcast_to(
        jnp.zeros((depth, SUBLANES), F32).at[:, N_HEADS:2 * N_HEADS].set(v)[:, :, None], (depth, SUBLANES, rows_p))
    return {
        "prew": pre_norm_w.reshape(depth, 1, D_MODEL),
        "postw": post_norm_w.reshape(depth, 1, D_MODEL),
        "win": w_in_t.astype(BF16),
        "convw": conv_w,
        "poolw": pool_w.astype(BF16),
        "poolw2": pw2.astype(BF16),
        "pscale": pool_scale.reshape(depth, 1, D_A),
        "alogc": pad_c(a_log), "dtbc": pad_c(dt_bias),
        "alogr": pad_r(a_log), "dtbr": pad_r(dt_bias),
        "onw": o_norm_w.reshape(depth, 1, HEAD_DIM),
        "wout": w_out.astype(BF16),
    }


def kernel(x_prompt, x_sample, c_prompt, c_sample, state_pool, state_conv, state_ssm, w_ada, b_ada, pre_norm_w,
           post_norm_w, w_in, conv_w, pool_w, pool_scale, a_log, dt_bias, o_norm_w, w_out):
    depth = w_in.shape[0]
    nb, seq, _ = x_prompt.shape
    nseq, dec_seq, _ = x_sample.shape
    assert dec_seq == 1 and seq % CHUNK == 0 and nb % 2 == 0 and nseq % (seq // CHUNK) == 0
    consts = _constants()
    lw = _stacked_weights(nb * CHUNK, pre_norm_w, post_norm_w, w_in, conv_w, pool_w, pool_scale, a_log, dt_bias,
                          o_norm_w, w_out)
    mod_p, mod_s = _mod_call(c_prompt, c_sample, w_ada, b_ada)
    mod_p = mod_p.reshape(depth, nb, 1, 3 * D_MODEL)
    ssm_in = state_ssm.reshape(depth, nseq * N_HEADS, HEAD_DIM, HEAD_DIM)
    pool_in = jnp.swapaxes(state_pool, 1, 2)
    conv_in = jnp.swapaxes(state_conv, 1, 2)
    yp, ys = x_prompt, x_sample.reshape(nseq, D_MODEL)
    prev_p, prev_s = (), ()
    for l in range(depth):
        prev_s, dec = _decode_front(l, ys, mod_s, pool_in, conv_in, lw, tuple(prev_s))
        yp, *prev_p, ys = _prompt_layer(l, yp, mod_p, lw, consts, dict(dec, ssm=ssm_in), tuple(prev_p))
    npool_p, nconv_p, nssm_p, nssm_s = prev_p
    npool_s, nconv_s = prev_s
    return (yp, ys.reshape(nseq, dec_seq, D_MODEL), npool_p, nconv_p,
            nssm_p.reshape(depth, nb, N_HEADS, HEAD_DIM, HEAD_DIM),
            jnp.swapaxes(npool_s, 1, 2), jnp.swapaxes(nconv_s, 1, 2),
            nssm_s.reshape(depth, nseq, N_HEADS, HEAD_DIM, HEAD_DIM))
```

```python
import functools

import jax
import jax.numpy as jnp
import numpy as np
from jax import lax
from jax.experimental import pallas as pl
from jax.experimental.pallas import tpu as pltpu

F32 = jnp.float32
BF16 = jnp.bfloat16

D_MODEL = 1024
D_A = 512
D_B = 512
N_POOL = 4
POOL_WINDOWS = (2, 4, 8, 16)
POOL_GC = 128
POOL_BUF = 15
HEAD_DIM = 128
N_HEADS = 4
D_QKV = 3 * D_B
CONV_K = 4
D_MAIN = 2 * D_A + D_QKV + D_B
PAST_LEN = 16384
EPS = 1e-6
CHUNK = 64
GDN_GROUP = 16
LANES = 128
SUBLANES = 8
QKV_TILES = D_QKV // LANES
MOD_SPLIT = 4
VMEM_LIMIT = 56 * 1024 * 1024


def _dot(a, b):
    return jnp.dot(a.astype(BF16), b.astype(BF16), preferred_element_type=F32)


def _dot_tn(a, b):
    return lax.dot_general(a, b, (((0,), (0,)), ((), ())), preferred_element_type=F32)


def _proj(hb, wt_ref, c0, c1):
    return lax.dot_general(hb, wt_ref[c0:c1, :], (((1,), (1,)), ((), ())), preferred_element_type=F32)


def _gate_logits(hb, wba_ref):
    w = jnp.concatenate([wba_ref[...], jnp.zeros((LANES - SUBLANES, wba_ref.shape[1]), wba_ref.dtype)], axis=0)
    return lax.dot_general(hb, w, (((1,), (1,)), ((), ())), preferred_element_type=F32)


def _silu(x):
    return x * jax.nn.sigmoid(x)


def _softplus(x):
    return jnp.maximum(x, 0.0) + jnp.log1p(jnp.exp(-jnp.abs(x)))


def _block_diag(a, b):
    top = jnp.concatenate([a, jnp.zeros((a.shape[0], b.shape[1]), a.dtype)], axis=1)
    bot = jnp.concatenate([jnp.zeros((b.shape[0], a.shape[1]), b.dtype), b], axis=1)
    return jnp.concatenate([top, bot], axis=0)


def _head_rms(o_all, w):
    parts = []
    for hh in range(N_HEADS):
        oh = o_all[:, hh * HEAD_DIM:(hh + 1) * HEAD_DIM]
        parts.append(oh * lax.rsqrt(jnp.mean(oh * oh, axis=-1, keepdims=True) + EPS) * w)
    return jnp.concatenate(parts, axis=1)


def _mod_kernel(cp_ref, cs_ref, *rest):
    w_refs, (b_ref, op_ref, os_ref) = rest[:-3], rest[-3:]
    kb = w_refs[0].shape[0]
    sp, ss = _silu(cp_ref[...]), _silu(cs_ref[...])
    accp = accs = b_ref[pl.ds(pl.program_id(0), 1), :]
    for i, w_ref in enumerate(w_refs):
        w = w_ref[...]
        accp = accp + jnp.dot(sp[:, i * kb:(i + 1) * kb], w, preferred_element_type=F32)
        accs = accs + jnp.dot(ss[:, i * kb:(i + 1) * kb], w, preferred_element_type=F32)
    op_ref[...] = accp
    os_ref[...] = accs


def _mod_call(c_prompt, c_sample, w_ada, b_ada):
    depth, _, n3 = w_ada.shape
    nb, nseq = c_prompt.shape[0], c_sample.shape[0]
    nblk = n3 // D_MODEL
    w_specs = [pl.BlockSpec((None, D_MODEL // MOD_SPLIT, D_MODEL), lambda l, j, i=i: (l, i, j))
               for i in range(MOD_SPLIT)]
    return pl.pallas_call(
        _mod_kernel,
        grid=(depth, nblk),
        in_specs=[
            pl.BlockSpec((nb, D_MODEL), lambda l, j: (0, 0)),
            pl.BlockSpec((nseq, D_MODEL), lambda l, j: (0, 0)),
            *w_specs,
            pl.BlockSpec((depth, D_MODEL), lambda l, j: (0, j)),
        ],
        out_specs=[pl.BlockSpec((None, nb, D_MODEL), lambda l, j: (l, 0, j)),
                   pl.BlockSpec((None, nseq, D_MODEL), lambda l, j: (l, 0, j))],
        out_shape=[jax.ShapeDtypeStruct((depth, nb, n3), F32), jax.ShapeDtypeStruct((depth, nseq, n3), F32)],
        compiler_params=pltpu.CompilerParams(dimension_semantics=("arbitrary", "arbitrary")),
        name="adaln_mod",
    )(c_prompt, c_sample, *([w_ada] * MOD_SPLIT), b_ada)


def _head_scalars(ref, layer, shape, axis):
    pos = lax.broadcasted_iota(jnp.int32, shape, axis)
    out = jnp.zeros(shape, F32)
    for hh in range(N_HEADS):
        out = jnp.where(pos == N_HEADS + hh, ref[layer, hh], out)
    return out


def _prompt_kernel(alog_ref, dtb_ref, x_ref, mod_ref, prew_ref, postw_ref, win_ref, wba_ref, convw_ref, poolw_ref,
                   pscale_ref, onw_ref, wout_ref, band_ref, ltri_ref,
                   xs_ref, sgate_ref, sq_ref, sk_ref, sv_ref, sbeta_ref, seg_ref, sqk_ref, sya_ref, szb_ref, sssm_ref,
                   *rest, n_alias, layer):
    rest = rest[n_alias:]
    y_ref, npool_ref, nconv_ref, nssm_ref, nsssm_ref, ys_ref = rest[0:6]
    qkv_ext, ua_ext, s_ref, q_s, k_s, kt_s, v_s, o_s, beta_s, gc_s, eg_s, gcrow_s, so_s = rest[6:]
    step = pl.program_id(0)
    nb = x_ref.shape[0]
    rows = nb * CHUNK

    @pl.when(step == 0)
    def _():
        qkv_ext[:, :, 0:SUBLANES, :] = jnp.zeros((QKV_TILES, nb, SUBLANES, LANES), F32)
        ua_ext[:, 0:CHUNK, :] = jnp.zeros((nb, CHUNK, D_A), F32)
        s_ref[...] = jnp.zeros(s_ref.shape, F32)
        so_s[...] = jnp.zeros(so_s.shape, F32)

    x = x_ref[...]
    mod = mod_ref[...][:, None, :]
    shift = mod[:, :, 0:D_MODEL]
    scale = mod[:, :, D_MODEL:2 * D_MODEL]
    gate = mod[:, :, 2 * D_MODEL:3 * D_MODEL]
    prew, postw = prew_ref[layer:layer + 1, :], postw_ref[layer:layer + 1, :]
    pscale, onw = pscale_ref[layer:layer + 1, :], onw_ref[layer:layer + 1, :]
    a_mul = prew[None] * (1.0 + scale)
    ms = jnp.mean(x * x, axis=-1, keepdims=True)
    h = x * lax.rsqrt(ms + EPS) * a_mul + shift
    hb = h.reshape(rows, D_MODEL).astype(BF16)

    ua = _proj(hb, win_ref, 0, D_A)
    ua3 = ua.reshape(nb, CHUNK, D_A)
    ua_ext[:, CHUNK:2 * CHUNK, :] = ua3
    pos = lax.broadcasted_iota(jnp.int32, (CHUNK, POOL_GC), 0) + step * CHUNK
    pooled_groups = []
    for gi, w in enumerate(POOL_WINDOWS):
        cnt = jnp.minimum(pos + 1, w).astype(F32)
        per_b = []
        for b in range(nb):
            ext = ua_ext[b, :, gi * POOL_GC:(gi + 1) * POOL_GC]
            win = jnp.dot(band_ref[gi], ext.astype(BF16), preferred_element_type=F32)
            per_b.append(win / cnt - ua3[b, :, gi * POOL_GC:(gi + 1) * POOL_GC])
        pooled_groups.append(jnp.concatenate(per_b, axis=0))
    ya = jnp.concatenate([_dot(jnp.concatenate(pooled_groups[2 * g2:2 * g2 + 2], axis=1),
                               _block_diag(poolw_ref[2 * g2], poolw_ref[2 * g2 + 1]))
                          for g2 in range(N_POOL // 2)], axis=1)
    za = _proj(hb, win_ref, D_A, 2 * D_A)
    ya = ya * pscale * _silu(za)

    qkv = _proj(hb, win_ref, 2 * D_A, 2 * D_A + D_QKV)
    qkv_t = jnp.stack([qkv[:, c * LANES:(c + 1) * LANES] for c in range(QKV_TILES)], axis=0)
    qkv_t = qkv_t.reshape(QKV_TILES, nb, CHUNK, LANES)
    qkv_ext[:, :, SUBLANES:SUBLANES + CHUNK, :] = qkv_t
    cw = convw_ref[...]
    cw_t = [jnp.stack([cw[j:j + 1, c * LANES:(c + 1) * LANES] for c in range(QKV_TILES)], axis=0)[:, None]
            for j in range(CONV_K)]
    acc = qkv_t * cw_t[CONV_K - 1]
    for j in range(CONV_K - 1):
        acc = acc + qkv_ext[:, :, pl.ds(SUBLANES - (CONV_K - 1) + j, CHUNK), :] * cw_t[j]
    act = _silu(acc).reshape(QKV_TILES, rows, LANES)
    for hh in range(N_HEADS):
        sl = slice(hh * HEAD_DIM, (hh + 1) * HEAD_DIM)
        qh, kh = act[hh], act[N_HEADS + hh]
        q_s[:, sl] = qh * lax.rsqrt(jnp.sum(qh * qh, axis=-1, keepdims=True) + EPS) * (HEAD_DIM ** -0.5)
        kn = kh * lax.rsqrt(jnp.sum(kh * kh, axis=-1, keepdims=True) + EPS)
        k_s[:, sl] = kn
        kt_s[hh] = kn.T
        v_s[:, sl] = act[2 * N_HEADS + hh]

    bac = _gate_logits(hb, wba_ref)
    bar = bac.T[0:SUBLANES]
    sig = jax.nn.sigmoid(bac)
    gcol = (-jnp.exp(_head_scalars(alog_ref, layer, (1, LANES), 1))
            * _softplus(bac + _head_scalars(dtb_ref, layer, (1, LANES), 1)))
    g_hi = gcol.astype(BF16)
    g_lo = (gcol - g_hi.astype(F32)).astype(BF16)
    gc_parts = []
    for b in range(nb):
        rs = slice(b * CHUNK, (b + 1) * CHUNK)
        gc_parts.append(jnp.dot(ltri_ref[...], jnp.concatenate([g_hi[rs], g_lo[rs]], axis=0),
                                preferred_element_type=F32))
    gc = jnp.concatenate(gc_parts, axis=0)
    for hh in range(N_HEADS):
        beta_s[hh] = jnp.broadcast_to(sig[:, hh:hh + 1], (rows, LANES))
        gcb = jnp.broadcast_to(gc[:, N_HEADS + hh:N_HEADS + hh + 1], (rows, LANES))
        gc_s[hh] = gcb
        eg_s[hh] = jnp.exp(gcb)
    grow = (-jnp.exp(_head_scalars(alog_ref, layer, (SUBLANES, rows), 0))
            * _softplus(bar + _head_scalars(dtb_ref, layer, (SUBLANES, rows), 0)))
    lane_in_chunk = lax.broadcasted_iota(jnp.int32, grow.shape, 1) % CHUNK
    sh = 1
    while sh < CHUNK:
        grow = grow + jnp.where(lane_in_chunk >= sh, pltpu.roll(grow, sh, 1), 0.0)
        sh *= 2
    gcrow_s[...] = grow

    _decode_state_step(step, sq_ref, sk_ref, sv_ref, sbeta_ref, seg_ref, sqk_ref, sssm_ref, nsssm_ref, so_s)

    ri = lax.broadcasted_iota(jnp.int32, (CHUNK, LANES), 0)
    li = lax.broadcasted_iota(jnp.int32, (CHUNK, LANES), 1)
    lj = li % CHUNK
    lo_half = li < CHUNK
    strict = ri > lj
    causal = ri >= lj
    eye2 = (ri == lj).astype(F32)
    lo_half_hd = lax.broadcasted_iota(jnp.int32, (HEAD_DIM, LANES), 1) < CHUNK

    def pair_mul(xp, yp):
        ybd = jnp.concatenate([jnp.where(lo_half, yp, 0.0), jnp.where(lo_half, 0.0, yp)], axis=0)
        return _dot(xp, ybd)

    chains = [(p, hh) for p in range(nb // 2) for hh in range(N_HEADS)]
    for g0 in range(0, len(chains), GDN_GROUP):
        st = []
        for p, hh in chains[g0:g0 + GDN_GROUP]:
            ra = slice(2 * p * CHUNK, (2 * p + 1) * CHUNK)
            rb = slice((2 * p + 1) * CHUNK, (2 * p + 2) * CHUNK)
            sl = slice(hh * HEAD_DIM, (hh + 1) * HEAD_DIM)
            kba, kbb = k_s[ra, sl] * beta_s[hh, ra, :], k_s[rb, sl] * beta_s[hh, rb, :]
            lhs1 = jnp.concatenate([jnp.concatenate([kba, kbb], axis=1),
                                    jnp.concatenate([q_s[ra, sl], q_s[rb, sl]], axis=1)], axis=0)
            kt = kt_s[hh, :, 2 * p * CHUNK:(2 * p + 2) * CHUNK]
            ktbd = jnp.concatenate([jnp.where(lo_half_hd, kt, 0.0), jnp.where(lo_half_hd, 0.0, kt)], axis=0)
            kkqk = _dot(lhs1, ktbd)
            st.append(dict(ra=ra, rb=rb, sl=sl, hh=hh, p=p, kkqk=kkqk))
        for c in st:
            hh, p = c["hh"], c["p"]
            gcol_p = jnp.where(lo_half, gc_s[hh, c["ra"], :], gc_s[hh, c["rb"], :])
            grow_p = jnp.broadcast_to(gcrow_s[N_HEADS + hh:N_HEADS + hh + 1, 2 * p * CHUNK:(2 * p + 2) * CHUNK],
                                      (CHUNK, LANES))
            dec = jnp.exp(jnp.minimum(gcol_p - grow_p, 0.0))
            kkqk = c.pop("kkqk")
            c["npow"] = jnp.where(strict, kkqk[0:CHUNK] * dec, 0.0)
            c["qkm"] = jnp.where(causal, kkqk[CHUNK:2 * CHUNK] * dec, 0.0)
            c["t"] = eye2 - c["npow"]
        for c in st:
            c["npow"] = pair_mul(c["npow"], c["npow"])
        for _ in range(4):
            for c in st:
                both = pair_mul(jnp.concatenate([c["t"], c["npow"]], axis=0), c["npow"])
                c["t"] = c["t"] + both[0:CHUNK]
                c["npow"] = both[CHUNK:2 * CHUNK]
        for c in st:
            c["t"] = c["t"] + pair_mul(c["t"], c["npow"])
        for c in st:
            ra, rb, sl, hh = c["ra"], c["rb"], c["sl"], c["hh"]
            ba, bb = beta_s[hh, ra, :], beta_s[hh, rb, :]
            kba, kbb = k_s[ra, sl] * ba, k_s[rb, sl] * bb
            rhs = jnp.concatenate([_block_diag(v_s[ra, sl] * ba, v_s[rb, sl] * bb),
                                   _block_diag(kba * eg_s[hh, ra, :], kbb * eg_s[hh, rb, :])], axis=1)
            c["uw"] = _dot(c.pop("t"), rhs)
        for c in st:
            ra, rb, sl, hh, p = c["ra"], c["rb"], c["sl"], c["hh"], c["p"]
            c["ia"], c["ib"] = 2 * p * N_HEADS + hh, (2 * p + 1) * N_HEADS + hh
            lhs3 = jnp.concatenate([c["uw"][:, 2 * HEAD_DIM:4 * HEAD_DIM],
                                    jnp.concatenate([q_s[ra, sl] * eg_s[hh, ra, :], q_s[rb, sl] * eg_s[hh, rb, :]],
                                                    axis=1)], axis=0)
            c["r3"] = _dot(lhs3, _block_diag(s_ref[c["ia"]], s_ref[c["ib"]]))
        for c in st:
            ra, rb, sl, hh, p = c["ra"], c["rb"], c["sl"], c["hh"], c["p"]
            r3 = c.pop("r3")
            vn = c.pop("uw")[:, 0:2 * HEAD_DIM] - r3[0:CHUNK]
            gla = gc_s[hh, (2 * p + 1) * CHUNK - 1:(2 * p + 1) * CHUNK, :]
            glb = gc_s[hh, (2 * p + 2) * CHUNK - 1:(2 * p + 2) * CHUNK, :]
            pair_lanes = slice(2 * p * CHUNK, (2 * p + 2) * CHUNK)
            to_end = jnp.exp(jnp.where(lo_half[0:1], gla, glb) - gcrow_s[N_HEADS + hh:N_HEADS + hh + 1, pair_lanes])
            kgt = kt_s[hh, :, pair_lanes] * to_end
            both = _dot(jnp.concatenate([c.pop("qkm"), kgt], axis=0),
                        _block_diag(vn[:, 0:HEAD_DIM], vn[:, HEAD_DIM:2 * HEAD_DIM]))
            o = r3[CHUNK:2 * CHUNK] + both[0:CHUNK]
            o_s[ra, sl] = o[:, 0:HEAD_DIM]
            o_s[rb, sl] = o[:, HEAD_DIM:2 * HEAD_DIM]
            upd = both[CHUNK:CHUNK + HEAD_DIM]
            s_ref[c["ia"]] = s_ref[c["ia"]] * jnp.exp(gla) + upd[:, 0:HEAD_DIM]
            s_ref[c["ib"]] = s_ref[c["ib"]] * jnp.exp(glb) + upd[:, HEAD_DIM:2 * HEAD_DIM]

    zb = _proj(hb, win_ref, 2 * D_A + D_QKV, D_MAIN)
    yb = _head_rms(o_s[...], onw) * _silu(zb)
    ymix = jnp.concatenate([ya, yb], axis=1)
    yo = jnp.dot(ymix.astype(BF16), wout_ref[...], preferred_element_type=F32)
    yn = yo * lax.rsqrt(jnp.mean(yo * yo, axis=-1, keepdims=True) + EPS)
    y_ref[...] = x + yn.reshape(nb, CHUNK, D_MODEL) * (gate * postw[None])

    qkv_ext[:, :, 0:SUBLANES, :] = qkv_ext[:, :, CHUNK:CHUNK + SUBLANES, :]
    ua_ext[:, 0:CHUNK, :] = ua3

    @pl.when(step == pl.num_programs(0) - 1)
    def _():
        qkv3 = qkv.reshape(nb, CHUNK, D_QKV)
        for b in range(nb):
            npool_ref[:, b, :] = ua3[b, CHUNK - POOL_BUF:CHUNK, :]
            nconv_ref[:, b, :] = qkv3[b, CHUNK - (CONV_K - 1):CHUNK, :]
        nssm_ref[...] = s_ref[...]
        yb_s = _head_rms(so_s[...], onw) * _silu(szb_ref[...])
        ymix_s = jnp.concatenate([sya_ref[...], yb_s], axis=1)
        yo_s = jnp.dot(ymix_s.astype(BF16), wout_ref[...], preferred_element_type=F32)
        yn_s = yo_s * lax.rsqrt(jnp.mean(yo_s * yo_s, axis=-1, keepdims=True) + EPS) * postw
        ys_ref[...] = xs_ref[...] + sgate_ref[...] * yn_s


def _layer_spec(shape, l):
    return pl.BlockSpec((None,) + tuple(shape), lambda s, _n=len(shape): (l,) + (0,) * _n)


def _prompt_layer(l, x, mod_p, lw, consts, dec, prev):
    nb, seq, _ = x.shape
    depth = lw["win"].shape[0]
    rows = nb * CHUNK
    nstep = seq // CHUNK
    nseq = dec["x"].shape[0]
    blk_states = nseq * N_HEADS // nstep
    assert blk_states % N_HEADS == 0 and SUBLANES % (blk_states // N_HEADS) == 0
    full = lambda shape: pl.BlockSpec(shape, lambda s, _n=len(shape): (0,) * _n)
    sssm_spec = pl.BlockSpec((None, blk_states, HEAD_DIM, HEAD_DIM), lambda s: (l, s, 0, 0))
    dec_specs = [
        full((nseq, D_MODEL)), pl.BlockSpec((None, nseq, D_MODEL), lambda s: (l, 0, 2)),
        full((nseq, D_B)), full((nseq, D_B)), full((nseq, D_B)),
        full((N_HEADS, nseq, LANES)), full((N_HEADS, nseq, LANES)), full((N_HEADS, nseq, LANES)),
        full((nseq, D_A)), full((nseq, D_B)), sssm_spec,
    ]
    smem = pl.BlockSpec(memory_space=pltpu.SMEM)
    in_specs = [
        smem, smem,
        pl.BlockSpec((nb, CHUNK, D_MODEL), lambda s: (0, s, 0)),
        _layer_spec((nb, 3 * D_MODEL), l),
        full((depth, D_MODEL)), full((depth, D_MODEL)),
        pl.BlockSpec((None, D_MAIN, D_MODEL), lambda s: (l, 0, 0)),
        pl.BlockSpec((None, SUBLANES, D_MODEL), lambda s: (l, D_MAIN // SUBLANES, 0)),
        _layer_spec((CONV_K, D_QKV), l), _layer_spec((N_POOL, POOL_GC, POOL_GC), l),
        full((depth, D_A)), full((depth, HEAD_DIM)), _layer_spec((D_MODEL, D_MODEL), l),
        full((N_POOL, CHUNK, 2 * CHUNK)), full((CHUNK, 2 * CHUNK)),
    ] + dec_specs + [pl.BlockSpec(memory_space=pl.ANY)] * len(prev)
    out_specs = [
        pl.BlockSpec((nb, CHUNK, D_MODEL), lambda s: (0, s, 0)),
        _layer_spec((POOL_BUF, nb, D_A), l), _layer_spec((CONV_K - 1, nb, D_QKV), l),
        _layer_spec((nb * N_HEADS, HEAD_DIM, HEAD_DIM), l),
        sssm_spec, full((nseq, D_MODEL)),
    ]
    out_shape = [
        jax.ShapeDtypeStruct((nb, seq, D_MODEL), F32),
        jax.ShapeDtypeStruct((depth, POOL_BUF, nb, D_A), F32),
        jax.ShapeDtypeStruct((depth, CONV_K - 1, nb, D_QKV), F32),
        jax.ShapeDtypeStruct((depth, nb * N_HEADS, HEAD_DIM, HEAD_DIM), F32),
        jax.ShapeDtypeStruct((depth, nseq * N_HEADS, HEAD_DIM, HEAD_DIM), F32),
        jax.ShapeDtypeStruct((nseq, D_MODEL), F32),
    ]
    scratch = [
        pltpu.VMEM((QKV_TILES, nb, CHUNK + SUBLANES, LANES), F32),
        pltpu.VMEM((nb, 2 * CHUNK, D_A), F32),
        pltpu.VMEM((nb * N_HEADS, HEAD_DIM, HEAD_DIM), F32),
        pltpu.VMEM((rows, D_B), F32), pltpu.VMEM((rows, D_B), F32), pltpu.VMEM((N_HEADS, HEAD_DIM, rows), F32),
        pltpu.VMEM((rows, D_B), F32), pltpu.VMEM((rows, D_B), F32),
        pltpu.VMEM((N_HEADS, rows, LANES), F32), pltpu.VMEM((N_HEADS, rows, LANES), F32),
        pltpu.VMEM((N_HEADS, rows, LANES), F32),
        pltpu.VMEM((SUBLANES, rows), F32),
        pltpu.VMEM((nseq, D_B), F32),
    ]
    n_in = len(in_specs) - len(prev)
    return pl.pallas_call(
        functools.partial(_prompt_kernel, n_alias=len(prev), layer=l),
        grid=(nstep,),
        in_specs=in_specs,
        out_specs=out_specs,
        out_shape=out_shape,
        scratch_shapes=scratch,
        input_output_aliases={n_in + i: 1 + i for i in range(len(prev))},
        compiler_params=pltpu.CompilerParams(dimension_semantics=("arbitrary",), vmem_limit_bytes=VMEM_LIMIT),
        name="prompt_layer",
    )(lw["alog"], lw["dtb"], x, mod_p, lw["prew"], lw["postw"], lw["win"], lw["win"], lw["convw"], lw["poolw"],
      lw["pscale"], lw["onw"], lw["wout"], consts["band"], consts["ltri"],
      dec["x"], dec["mod"], dec["q"], dec["k"], dec["v"], dec["beta"], dec["eg"], dec["qk"], dec["ya"], dec["zb"],
      dec["ssm"], *prev)


def _decode_front_kernel(alog_ref, dtb_ref, x_ref, mod_ref, spool_ref, sconv_ref, prew_ref, win_ref, wba_ref,
                         convw_ref, poolw_ref, pscale_ref, *rest, n_alias, layer):
    rest = rest[n_alias:]
    npool_ref, nconv_ref, q_ref, k_ref, v_ref, beta_ref, eg_ref, qk_ref, ya_ref, zb_ref = rest
    nseq = x_ref.shape[0]
    x = x_ref[...]
    mod = mod_ref[...]
    shift = mod[:, 0:D_MODEL]
    scale = mod[:, D_MODEL:2 * D_MODEL]
    a_mul = prew_ref[layer:layer + 1, :] * (1.0 + scale)
    ms = jnp.mean(x * x, axis=-1, keepdims=True)
    hb = (x * lax.rsqrt(ms + EPS) * a_mul + shift).astype(BF16)

    ua = _proj(hb, win_ref, 0, D_A)
    ya_parts = []
    for gi, w in enumerate(POOL_WINDOWS):
        gs = slice(gi * POOL_GC, (gi + 1) * POOL_GC)
        win = ua[:, gs]
        for d in range(1, w):
            win = win + spool_ref[POOL_BUF - d, :, gs]
        cnt = float(min(PAST_LEN + 1, w))
        pooled = win / cnt - ua[:, gs]
        ya_parts.append(_dot(pooled, poolw_ref[gi]))
    za = _proj(hb, win_ref, D_A, 2 * D_A)
    ya_ref[...] = jnp.concatenate(ya_parts, axis=1) * pscale_ref[layer:layer + 1, :] * _silu(za)
    npool_ref[0:POOL_BUF - 1] = spool_ref[1:POOL_BUF]
    npool_ref[POOL_BUF - 1] = ua

    qkv = _proj(hb, win_ref, 2 * D_A, 2 * D_A + D_QKV)
    cw = convw_ref[...]
    acc = qkv * cw[CONV_K - 1:CONV_K]
    for j in range(CONV_K - 1):
        acc = acc + sconv_ref[j] * cw[j:j + 1]
    nconv_ref[0:CONV_K - 2] = sconv_ref[1:CONV_K - 1]
    nconv_ref[CONV_K - 2] = qkv
    qkvc = _silu(acc)
    bac = _gate_logits(hb, wba_ref)
    sig = jax.nn.sigmoid(bac)
    eg = jnp.exp(-jnp.exp(_head_scalars(alog_ref, layer, (1, LANES), 1))
                 * _softplus(bac + _head_scalars(dtb_ref, layer, (1, LANES), 1)))
    for hh in range(N_HEADS):
        sl = slice(hh * HEAD_DIM, (hh + 1) * HEAD_DIM)
        qh = qkvc[:, hh * HEAD_DIM:(hh + 1) * HEAD_DIM]
        kh = qkvc[:, D_B + hh * HEAD_DIM:D_B + (hh + 1) * HEAD_DIM]
        qn = qh * lax.rsqrt(jnp.sum(qh * qh, axis=-1, keepdims=True) + EPS) * (HEAD_DIM ** -0.5)
        kn = kh * lax.rsqrt(jnp.sum(kh * kh, axis=-1, keepdims=True) + EPS)
        q_ref[:, sl] = qn
        k_ref[:, sl] = kn
        qk_ref[hh] = jnp.broadcast_to(jnp.sum(qn * kn, axis=-1, keepdims=True), (nseq, LANES))
        beta_ref[hh] = jnp.broadcast_to(sig[:, hh:hh + 1], (nseq, LANES))
        eg_ref[hh] = jnp.broadcast_to(eg[:, N_HEADS + hh:N_HEADS + hh + 1], (nseq, LANES))
    v_ref[...] = qkvc[:, 2 * D_B:3 * D_B]
    zb_ref[...] = _proj(hb, win_ref, 2 * D_A + D_QKV, D_MAIN)


def _decode_state_step(step, q_ref, k_ref, v_ref, beta_ref, eg_ref, qk_ref, ssm_ref, nssm_ref, o_s):
    nrow = ssm_ref.shape[0] // N_HEADS
    per_tile = SUBLANES // nrow
    tile0 = pl.multiple_of((step // per_tile) * SUBLANES, SUBLANES)
    sub = step % per_tile

    def my_rows(tile):
        out = tile[0:nrow]
        for j in range(1, per_tile):
            out = jnp.where(sub == j, tile[j * nrow:(j + 1) * nrow], out)
        return out

    kblk = my_rows(k_ref[pl.ds(tile0, SUBLANES), :])
    qblk = my_rows(q_ref[pl.ds(tile0, SUBLANES), :])
    vblk = my_rows(v_ref[pl.ds(tile0, SUBLANES), :])
    beta = [my_rows(beta_ref[hh, pl.ds(tile0, SUBLANES), :]) for hh in range(N_HEADS)]
    egs = [my_rows(eg_ref[hh, pl.ds(tile0, SUBLANES), :]) for hh in range(N_HEADS)]
    qks = [my_rows(qk_ref[hh, pl.ds(tile0, SUBLANES), :]) for hh in range(N_HEADS)]
    row8 = lax.broadcasted_iota(jnp.int32, (SUBLANES, HEAD_DIM), 0)
    st = []
    for i in range(nrow):
        for hh in range(N_HEADS):
            sl = slice(hh * HEAD_DIM, (hh + 1) * HEAD_DIM)
            krow, qrow = kblk[i:i + 1, sl], qblk[i:i + 1, sl]
            lhs = jnp.where(row8 == 0, krow, jnp.where(row8 == 1, qrow, 0.0))
            r = jnp.dot(lhs, ssm_ref[i * N_HEADS + hh], preferred_element_type=F32)
            st.append(dict(i=i, hh=hh, sl=sl, krow=krow, r=r))
    for c in st:
        i, hh = c["i"], c["hh"]
        eg = egs[hh][i:i + 1]
        r = c.pop("r")
        delta = (vblk[i:i + 1, c["sl"]] - eg * r[0:1]) * beta[hh][i:i + 1]
        c["o"] = eg * r[1:2] + qks[hh][i:i + 1] * delta
        krow = c.pop("krow")
        k_hi = krow.astype(BF16).astype(F32)
        d_hi = delta.astype(BF16).astype(F32)
        kp = jnp.where(row8 == 0, k_hi, jnp.where(row8 == 1, krow - k_hi, jnp.where(row8 == 2, k_hi, 0.0)))
        dp = jnp.where(row8 == 0, d_hi, jnp.where(row8 == 1, d_hi, jnp.where(row8 == 2, delta - d_hi, 0.0)))
        c["upd"] = _dot_tn(kp, dp)
    for c in st:
        idx = c["i"] * N_HEADS + c["hh"]
        nssm_ref[idx] = ssm_ref[idx] * egs[c["hh"]][c["i"]:c["i"] + 1] + c.pop("upd")
    o_rows = [jnp.concatenate([c["o"] for c in st[i * N_HEADS:(i + 1) * N_HEADS]], axis=1) for i in range(nrow)]
    tile = o_s[pl.ds(tile0, SUBLANES), :]
    placed = jnp.concatenate(o_rows * per_tile, axis=0)
    row_group = lax.broadcasted_iota(jnp.int32, tile.shape, 0) // nrow
    o_s[pl.ds(tile0, SUBLANES), :] = jnp.where(row_group == sub, placed, tile)


def _decode_front(l, x, mod_s, state_pool, state_conv, lw, prev):
    nseq = x.shape[0]
    depth = lw["win"].shape[0]
    full = lambda shape: pl.BlockSpec(shape, lambda s, _n=len(shape): (0,) * _n)
    smem = pl.BlockSpec(memory_space=pltpu.SMEM)
    in_specs = [
        smem, smem,
        full((nseq, D_MODEL)), _layer_spec((nseq, 3 * D_MODEL), l),
        _layer_spec((POOL_BUF, nseq, D_A), l), _layer_spec((CONV_K - 1, nseq, D_QKV), l),
        full((depth, D_MODEL)),
        pl.BlockSpec((None, D_MAIN, D_MODEL), lambda s: (l, 0, 0)),
        pl.BlockSpec((None, SUBLANES, D_MODEL), lambda s: (l, D_MAIN // SUBLANES, 0)),
        _layer_spec((CONV_K, D_QKV), l), _layer_spec((N_POOL, POOL_GC, POOL_GC), l), full((depth, D_A)),
    ] + [pl.BlockSpec(memory_space=pl.ANY)] * len(prev)
    row_out = lambda width: (full((nseq, width)), jax.ShapeDtypeStruct((nseq, width), F32))
    head_out = (full((N_HEADS, nseq, LANES)), jax.ShapeDtypeStruct((N_HEADS, nseq, LANES), F32))
    outs = [
        (_layer_spec((POOL_BUF, nseq, D_A), l), jax.ShapeDtypeStruct((depth, POOL_BUF, nseq, D_A), F32)),
        (_layer_spec((CONV_K - 1, nseq, D_QKV), l), jax.ShapeDtypeStruct((depth, CONV_K - 1, nseq, D_QKV), F32)),
        row_out(D_B), row_out(D_B), row_out(D_B), head_out, head_out, head_out, row_out(D_A), row_out(D_B),
    ]
    n_in = len(in_specs) - len(prev)
    npool, nconv, q, k, v, beta, eg, qk, ya, zb = pl.pallas_call(
        functools.partial(_decode_front_kernel, n_alias=len(prev), layer=l),
        grid=(1,),
        in_specs=in_specs,
        out_specs=[o[0] for o in outs],
        out_shape=[o[1] for o in outs],
        input_output_aliases={n_in + i: i for i in range(len(prev))},
        compiler_params=pltpu.CompilerParams(dimension_semantics=("arbitrary",), vmem_limit_bytes=VMEM_LIMIT),
        name="decode_front",
    )(lw["alog"], lw["dtb"], x, mod_s, state_pool, state_conv, lw["prew"], lw["win"], lw["win"], lw["convw"],
      lw["poolw"], lw["pscale"], *prev)
    return (npool, nconv), dict(x=x, mod=mod_s, q=q, k=k, v=v, beta=beta, eg=eg, qk=qk, ya=ya, zb=zb)


def _constants():
    t = np.arange(CHUNK)[:, None]
    j = np.arange(2 * CHUNK)[None, :]
    band = np.stack([((j <= CHUNK + t) & (j > CHUNK + t - w)) for w in POOL_WINDOWS]).astype(np.float32)
    tri = (np.arange(CHUNK)[None, :] <= t).astype(np.float32)
    ltri = np.concatenate([tri, tri], axis=1)
    return {"band": jnp.asarray(band, BF16), "ltri": jnp.asarray(ltri, BF16)}


def _stacked_weights(pre_norm_w, post_norm_w, w_in, conv_w, pool_w, pool_scale, a_log, dt_bias, o_norm_w, w_out):
    w_in_t = jnp.swapaxes(w_in, 1, 2)
    return {
        "prew": pre_norm_w, "postw": post_norm_w, "win": w_in_t.astype(BF16), "convw": conv_w, "poolw": pool_w,
        "pscale": pool_scale, "alog": a_log, "dtb": dt_bias, "onw": o_norm_w, "wout": w_out.astype(BF16),
    }


def kernel(x_prompt, x_sample, c_prompt, c_sample, state_pool, state_conv, state_ssm, w_ada, b_ada, pre_norm_w,
           post_norm_w, w_in, conv_w, pool_w, pool_scale, a_log, dt_bias, o_norm_w, w_out):
    depth = w_in.shape[0]
    nb, seq, _ = x_prompt.shape
    nseq, dec_seq, _ = x_sample.shape
    assert dec_seq == 1 and seq % CHUNK == 0 and nb % 2 == 0 and nseq % (seq // CHUNK) == 0
    consts = _constants()
    lw = _stacked_weights(pre_norm_w, post_norm_w, w_in, conv_w, pool_w, pool_scale, a_log, dt_bias, o_norm_w, w_out)
    mod_p, mod_s = _mod_call(c_prompt, c_sample, w_ada, b_ada)
    ssm_in = state_ssm.reshape(depth, nseq * N_HEADS, HEAD_DIM, HEAD_DIM)
    pool_in = jnp.swapaxes(state_pool, 1, 2)
    conv_in = jnp.swapaxes(state_conv, 1, 2)
    yp, ys = x_prompt, x_sample.reshape(nseq, D_MODEL)
    prev_p, prev_s = (), ()
    for l in range(depth):
        prev_s, dec = _decode_front(l, ys, mod_s, pool_in, conv_in, lw, tuple(prev_s))
        yp, *prev_p, ys = _prompt_layer(l, yp, mod_p, lw, consts, dict(dec, ssm=ssm_in), tuple(prev_p))
    npool_p, nconv_p, nssm_p, nssm_s = prev_p
    npool_s, nconv_s = prev_s
    return (yp, ys.reshape(nseq, dec_seq, D_MODEL), jnp.swapaxes(npool_p, 1, 2), jnp.swapaxes(nconv_p, 1, 2),
            nssm_p.reshape(depth, nb, N_HEADS, HEAD_DIM, HEAD_DIM),
            jnp.swapaxes(npool_s, 1, 2), jnp.swapaxes(nconv_s, 1, 2),
            nssm_s.reshape(depth, nseq, N_HEADS, HEAD_DIM, HEAD_DIM))
```

```python
import functools

import jax
import jax.numpy as jnp
import numpy as np
from jax import lax
from jax.experimental import pallas as pl
from jax.experimental.pallas import tpu as pltpu

F32 = jnp.float32
BF16 = jnp.bfloat16

D_MODEL = 1024
D_A = 512
D_B = 512
N_POOL = 4
POOL_WINDOWS = (2, 4, 8, 16)
POOL_GC = 128
POOL_BUF = 15
HEAD_DIM = 128
N_HEADS = 4
D_QKV = 3 * D_B
CONV_K = 4
D_MAIN = 2 * D_A + D_QKV + D_B
PAST_LEN = 16384
EPS = 1e-6
CHUNK = 64
GDN_GROUP = 16
LANES = 128
SUBLANES = 8
QKV_TILES = D_QKV // LANES
MOD_SPLIT = 4
VMEM_LIMIT = 56 * 1024 * 1024


def _dot(a, b):
    return jnp.dot(a.astype(BF16), b.astype(BF16), preferred_element_type=F32)


def _dot_tn(a, b):
    return lax.dot_general(a, b, (((0,), (0,)), ((), ())), preferred_element_type=F32)


def _proj(hb, wt_ref, c0, c1):
    return lax.dot_general(hb, wt_ref[c0:c1, :], (((1,), (1,)), ((), ())), preferred_element_type=F32)


def _gate_logits(hb, wba_ref):
    w = jnp.concatenate([wba_ref[...], jnp.zeros((LANES - SUBLANES, wba_ref.shape[1]), wba_ref.dtype)], axis=0)
    return lax.dot_general(hb, w, (((1,), (1,)), ((), ())), preferred_element_type=F32)


def _silu(x):
    return x * jax.nn.sigmoid(x)


def _softplus(x):
    return jnp.maximum(x, 0.0) + jnp.log1p(jnp.exp(-jnp.abs(x)))


def _block_diag(a, b):
    top = jnp.concatenate([a, jnp.zeros((a.shape[0], b.shape[1]), a.dtype)], axis=1)
    bot = jnp.concatenate([jnp.zeros((b.shape[0], a.shape[1]), b.dtype), b], axis=1)
    return jnp.concatenate([top, bot], axis=0)


def _head_rms(o_all, w):
    parts = []
    for hh in range(N_HEADS):
        oh = o_all[:, hh * HEAD_DIM:(hh + 1) * HEAD_DIM]
        parts.append(oh * lax.rsqrt(jnp.mean(oh * oh, axis=-1, keepdims=True) + EPS) * w)
    return jnp.concatenate(parts, axis=1)


def _mod_kernel(cp_ref, cs_ref, w_ref, b_ref, op_ref, os_ref):
    @pl.when(pl.program_id(1) == 0)
    def _():
        b = b_ref[pl.ds(pl.program_id(0), 1), :]
        op_ref[...] = jnp.broadcast_to(b, op_ref.shape)
        os_ref[...] = jnp.broadcast_to(b, os_ref.shape)

    w = w_ref[...]
    op_ref[...] += jnp.dot(_silu(cp_ref[...]), w, preferred_element_type=F32)
    os_ref[...] += jnp.dot(_silu(cs_ref[...]), w, preferred_element_type=F32)


def _mod_call(c_prompt, c_sample, w_ada, b_ada):
    depth, _, n3 = w_ada.shape
    nb, nseq = c_prompt.shape[0], c_sample.shape[0]
    kb = D_MODEL // MOD_SPLIT
    return pl.pallas_call(
        _mod_kernel,
        grid=(depth, MOD_SPLIT),
        in_specs=[
            pl.BlockSpec((nb, kb), lambda l, k: (0, k)),
            pl.BlockSpec((nseq, kb), lambda l, k: (0, k)),
            pl.BlockSpec((None, kb, n3), lambda l, k: (l, k, 0)),
            pl.BlockSpec((depth, n3), lambda l, k: (0, 0)),
        ],
        out_specs=[pl.BlockSpec((None, nb, n3), lambda l, k: (l, 0, 0)),
                   pl.BlockSpec((None, nseq, n3), lambda l, k: (l, 0, 0))],
        out_shape=[jax.ShapeDtypeStruct((depth, nb, n3), F32), jax.ShapeDtypeStruct((depth, nseq, n3), F32)],
        compiler_params=pltpu.CompilerParams(dimension_semantics=("arbitrary", "arbitrary")),
        name="adaln_mod",
    )(c_prompt, c_sample, w_ada, b_ada)


def _head_scalars(ref, layer, shape, axis):
    pos = lax.broadcasted_iota(jnp.int32, shape, axis)
    out = jnp.zeros(shape, F32)
    for hh in range(N_HEADS):
        out = jnp.where(pos == N_HEADS + hh, ref[layer, hh], out)
    return out


def _prompt_kernel(alog_ref, dtb_ref, x_ref, mod_ref, prew_ref, postw_ref, win_ref, wba_ref, convw_ref, poolw_ref,
                   pscale_ref, onw_ref, wout_ref, band_ref, ltri_ref,
                   xs_ref, sgate_ref, sq_ref, sk_ref, sv_ref, sbeta_ref, seg_ref, sqk_ref, sya_ref, szb_ref, sssm_ref,
                   *rest, n_alias, layer):
    rest = rest[n_alias:]
    y_ref, npool_ref, nconv_ref, nssm_ref, nsssm_ref, ys_ref = rest[0:6]
    qkv_ext, ua_ext, s_ref, q_s, k_s, kt_s, v_s, o_s, beta_s, gc_s, eg_s, gcrow_s, so_s = rest[6:]
    step = pl.program_id(0)
    nb = x_ref.shape[0]
    rows = nb * CHUNK

    @pl.when(step == 0)
    def _():
        qkv_ext[:, :, 0:SUBLANES, :] = jnp.zeros((QKV_TILES, nb, SUBLANES, LANES), F32)
        ua_ext[:, 0:CHUNK, :] = jnp.zeros((nb, CHUNK, D_A), F32)
        s_ref[...] = jnp.zeros(s_ref.shape, F32)
        so_s[...] = jnp.zeros(so_s.shape, F32)

    x = x_ref[...]
    mod = mod_ref[...][:, None, :]
    shift = mod[:, :, 0:D_MODEL]
    scale = mod[:, :, D_MODEL:2 * D_MODEL]
    gate = mod[:, :, 2 * D_MODEL:3 * D_MODEL]
    prew, postw = prew_ref[layer:layer + 1, :], postw_ref[layer:layer + 1, :]
    pscale, onw = pscale_ref[layer:layer + 1, :], onw_ref[layer:layer + 1, :]
    a_mul = prew[None] * (1.0 + scale)
    ms = jnp.mean(x * x, axis=-1, keepdims=True)
    h = x * lax.rsqrt(ms + EPS) * a_mul + shift
    hb = h.reshape(rows, D_MODEL).astype(BF16)

    ua = _proj(hb, win_ref, 0, D_A)
    ua3 = ua.reshape(nb, CHUNK, D_A)
    ua_ext[:, CHUNK:2 * CHUNK, :] = ua3
    pos = lax.broadcasted_iota(jnp.int32, (CHUNK, POOL_GC), 0) + step * CHUNK
    pooled_groups = []
    for gi, w in enumerate(POOL_WINDOWS):
        cnt = jnp.minimum(pos + 1, w).astype(F32)
        gsl = slice(gi * POOL_GC, (gi + 1) * POOL_GC)
        per_b = []
        for b in range(0, nb, 2):
            ext = jnp.concatenate([ua_ext[b, :, gsl], ua_ext[b + 1, :, gsl]], axis=1)
            win = jnp.dot(band_ref[gi], ext.astype(BF16), preferred_element_type=F32)
            per_b.append(win[:, 0:POOL_GC] / cnt - ua3[b, :, gsl])
            per_b.append(win[:, POOL_GC:2 * POOL_GC] / cnt - ua3[b + 1, :, gsl])
        pooled_groups.append(jnp.concatenate(per_b, axis=0))
    ya = jnp.concatenate([_dot(jnp.concatenate(pooled_groups[2 * g2:2 * g2 + 2], axis=1),
                               _block_diag(poolw_ref[2 * g2], poolw_ref[2 * g2 + 1]))
                          for g2 in range(N_POOL // 2)], axis=1)
    za = _proj(hb, win_ref, D_A, 2 * D_A)
    ya = ya * pscale * _silu(za)

    qkv = _proj(hb, win_ref, 2 * D_A, 2 * D_A + D_QKV)
    qkv_t = jnp.stack([qkv[:, c * LANES:(c + 1) * LANES] for c in range(QKV_TILES)], axis=0)
    qkv_t = qkv_t.reshape(QKV_TILES, nb, CHUNK, LANES)
    qkv_ext[:, :, SUBLANES:SUBLANES + CHUNK, :] = qkv_t
    cw = convw_ref[...]
    cw_t = [jnp.stack([cw[j:j + 1, c * LANES:(c + 1) * LANES] for c in range(QKV_TILES)], axis=0)[:, None]
            for j in range(CONV_K)]
    acc = qkv_t * cw_t[CONV_K - 1]
    for j in range(CONV_K - 1):
        acc = acc + qkv_ext[:, :, pl.ds(SUBLANES - (CONV_K - 1) + j, CHUNK), :] * cw_t[j]
    act = _silu(acc).reshape(QKV_TILES, rows, LANES)
    for hh in range(N_HEADS):
        sl = slice(hh * HEAD_DIM, (hh + 1) * HEAD_DIM)
        qh, kh = act[hh], act[N_HEADS + hh]
        q_s[:, sl] = qh * lax.rsqrt(jnp.sum(qh * qh, axis=-1, keepdims=True) + EPS) * (HEAD_DIM ** -0.5)
        kn = kh * lax.rsqrt(jnp.sum(kh * kh, axis=-1, keepdims=True) + EPS)
        k_s[:, sl] = kn
        kt_s[hh] = kn.T
        v_s[:, sl] = act[2 * N_HEADS + hh]

    bac = _gate_logits(hb, wba_ref)
    bar = bac.T[0:SUBLANES]
    sig = jax.nn.sigmoid(bac)
    gcol = (-jnp.exp(_head_scalars(alog_ref, layer, (1, LANES), 1))
            * _softplus(bac + _head_scalars(dtb_ref, layer, (1, LANES), 1)))
    g_hi = gcol.astype(BF16)
    g_lo = (gcol - g_hi.astype(F32)).astype(BF16)
    gc_parts = []
    for b in range(0, nb, 2):
        ra, rb = slice(b * CHUNK, (b + 1) * CHUNK), slice((b + 1) * CHUNK, (b + 2) * CHUNK)
        hi_lo = jnp.concatenate([jnp.concatenate([g_hi[ra], g_lo[ra]], axis=0),
                                 jnp.concatenate([g_hi[rb], g_lo[rb]], axis=0)], axis=1)
        both = jnp.dot(ltri_ref[...], hi_lo, preferred_element_type=F32)
        gc_parts += [both[:, 0:LANES], both[:, LANES:2 * LANES]]
    gc = jnp.concatenate(gc_parts, axis=0)
    for hh in range(N_HEADS):
        beta_s[hh] = jnp.broadcast_to(sig[:, hh:hh + 1], (rows, LANES))
        gcb = jnp.broadcast_to(gc[:, N_HEADS + hh:N_HEADS + hh + 1], (rows, LANES))
        gc_s[hh] = gcb
        eg_s[hh] = jnp.exp(gcb)
    grow = (-jnp.exp(_head_scalars(alog_ref, layer, (SUBLANES, rows), 0))
            * _softplus(bar + _head_scalars(dtb_ref, layer, (SUBLANES, rows), 0)))
    lane_in_chunk = lax.broadcasted_iota(jnp.int32, grow.shape, 1) % CHUNK
    sh = 1
    while sh < CHUNK:
        grow = grow + jnp.where(lane_in_chunk >= sh, pltpu.roll(grow, sh, 1), 0.0)
        sh *= 2
    gcrow_s[...] = grow

    _decode_state_step(step, sq_ref, sk_ref, sv_ref, sbeta_ref, seg_ref, sqk_ref, sssm_ref, nsssm_ref, so_s)

    ri = lax.broadcasted_iota(jnp.int32, (CHUNK, LANES), 0)
    li = lax.broadcasted_iota(jnp.int32, (CHUNK, LANES), 1)
    lj = li % CHUNK
    lo_half = li < CHUNK
    strict = ri > lj
    causal = ri >= lj
    eye2 = (ri == lj).astype(F32)
    lo_half_hd = lax.broadcasted_iota(jnp.int32, (HEAD_DIM, LANES), 1) < CHUNK

    def pair_mul(xp, yp):
        ybd = jnp.concatenate([jnp.where(lo_half, yp, 0.0), jnp.where(lo_half, 0.0, yp)], axis=0)
        return _dot(xp, ybd)

    chains = [(p, hh) for p in range(nb // 2) for hh in range(N_HEADS)]
    for g0 in range(0, len(chains), GDN_GROUP):
        st = []
        for p, hh in chains[g0:g0 + GDN_GROUP]:
            ra = slice(2 * p * CHUNK, (2 * p + 1) * CHUNK)
            rb = slice((2 * p + 1) * CHUNK, (2 * p + 2) * CHUNK)
            sl = slice(hh * HEAD_DIM, (hh + 1) * HEAD_DIM)
            kba, kbb = k_s[ra, sl] * beta_s[hh, ra, :], k_s[rb, sl] * beta_s[hh, rb, :]
            lhs1 = jnp.concatenate([jnp.concatenate([kba, kbb], axis=1),
                                    jnp.concatenate([q_s[ra, sl], q_s[rb, sl]], axis=1)], axis=0)
            kt = kt_s[hh, :, 2 * p * CHUNK:(2 * p + 2) * CHUNK]
            ktbd = jnp.concatenate([jnp.where(lo_half_hd, kt, 0.0), jnp.where(lo_half_hd, 0.0, kt)], axis=0)
            kkqk = _dot(lhs1, ktbd)
            st.append(dict(ra=ra, rb=rb, sl=sl, hh=hh, p=p, kkqk=kkqk))
        for c in st:
            hh, p = c["hh"], c["p"]
            gcol_p = jnp.where(lo_half, gc_s[hh, c["ra"], :], gc_s[hh, c["rb"], :])
            grow_p = jnp.broadcast_to(gcrow_s[N_HEADS + hh:N_HEADS + hh + 1, 2 * p * CHUNK:(2 * p + 2) * CHUNK],
                                      (CHUNK, LANES))
            dec = jnp.exp(jnp.minimum(gcol_p - grow_p, 0.0))
            kkqk = c.pop("kkqk")
            c["npow"] = jnp.where(strict, kkqk[0:CHUNK] * dec, 0.0)
            c["qkm"] = jnp.where(causal, kkqk[CHUNK:2 * CHUNK] * dec, 0.0)
            c["t"] = eye2 - c["npow"]
        for c in st:
            c["npow"] = pair_mul(c["npow"], c["npow"])
        for _ in range(4):
            for c in st:
                both = pair_mul(jnp.concatenate([c["t"], c["npow"]], axis=0), c["npow"])
                c["t"] = c["t"] + both[0:CHUNK]
                c["npow"] = both[CHUNK:2 * CHUNK]
        for c in st:
            c["t"] = c["t"] + pair_mul(c["t"], c["npow"])
        for c in st:
            ra, rb, sl, hh = c["ra"], c["rb"], c["sl"], c["hh"]
            ba, bb = beta_s[hh, ra, :], beta_s[hh, rb, :]
            kba, kbb = k_s[ra, sl] * ba, k_s[rb, sl] * bb
            rhs = jnp.concatenate([_block_diag(v_s[ra, sl] * ba, v_s[rb, sl] * bb),
                                   _block_diag(kba * eg_s[hh, ra, :], kbb * eg_s[hh, rb, :])], axis=1)
            c["uw"] = _dot(c.pop("t"), rhs)
        for c in st:
            ra, rb, sl, hh, p = c["ra"], c["rb"], c["sl"], c["hh"], c["p"]
            c["ia"], c["ib"] = 2 * p * N_HEADS + hh, (2 * p + 1) * N_HEADS + hh
            lhs3 = jnp.concatenate([c["uw"][:, 2 * HEAD_DIM:4 * HEAD_DIM],
                                    jnp.concatenate([q_s[ra, sl] * eg_s[hh, ra, :], q_s[rb, sl] * eg_s[hh, rb, :]],
                                                    axis=1)], axis=0)
            c["r3"] = _dot(lhs3, _block_diag(s_ref[c["ia"]], s_ref[c["ib"]]))
        for c in st:
            ra, rb, sl, hh, p = c["ra"], c["rb"], c["sl"], c["hh"], c["p"]
            r3 = c.pop("r3")
            vn = c.pop("uw")[:, 0:2 * HEAD_DIM] - r3[0:CHUNK]
            gla = gc_s[hh, (2 * p + 1) * CHUNK - 1:(2 * p + 1) * CHUNK, :]
            glb = gc_s[hh, (2 * p + 2) * CHUNK - 1:(2 * p + 2) * CHUNK, :]
            pair_lanes = slice(2 * p * CHUNK, (2 * p + 2) * CHUNK)
            to_end = jnp.exp(jnp.where(lo_half[0:1], gla, glb) - gcrow_s[N_HEADS + hh:N_HEADS + hh + 1, pair_lanes])
            kgt = kt_s[hh, :, pair_lanes] * to_end
            both = _dot(jnp.concatenate([c.pop("qkm"), kgt], axis=0),
                        _block_diag(vn[:, 0:HEAD_DIM], vn[:, HEAD_DIM:2 * HEAD_DIM]))
            o = r3[CHUNK:2 * CHUNK] + both[0:CHUNK]
            o_s[ra, sl] = o[:, 0:HEAD_DIM]
            o_s[rb, sl] = o[:, HEAD_DIM:2 * HEAD_DIM]
            upd = both[CHUNK:CHUNK + HEAD_DIM]
            s_ref[c["ia"]] = s_ref[c["ia"]] * jnp.exp(gla) + upd[:, 0:HEAD_DIM]
            s_ref[c["ib"]] = s_ref[c["ib"]] * jnp.exp(glb) + upd[:, HEAD_DIM:2 * HEAD_DIM]

    zb = _proj(hb, win_ref, 2 * D_A + D_QKV, D_MAIN)
    yb = _head_rms(o_s[...], onw) * _silu(zb)
    ymix = jnp.concatenate([ya, yb], axis=1)
    yo = jnp.dot(ymix.astype(BF16), wout_ref[...], preferred_element_type=F32)
    yn = yo * lax.rsqrt(jnp.mean(yo * yo, axis=-1, keepdims=True) + EPS)
    y_ref[...] = x + yn.reshape(nb, CHUNK, D_MODEL) * (gate * postw[None])

    qkv_ext[:, :, 0:SUBLANES, :] = qkv_ext[:, :, CHUNK:CHUNK + SUBLANES, :]
    ua_ext[:, 0:CHUNK, :] = ua3

    @pl.when(step == pl.num_programs(0) - 1)
    def _():
        qkv3 = qkv.reshape(nb, CHUNK, D_QKV)
        for b in range(nb):
            npool_ref[:, b, :] = ua3[b, CHUNK - POOL_BUF:CHUNK, :]
            nconv_ref[:, b, :] = qkv3[b, CHUNK - (CONV_K - 1):CHUNK, :]
        nssm_ref[...] = s_ref[...]
        yb_s = _head_rms(so_s[...], onw) * _silu(szb_ref[...])
        ymix_s = jnp.concatenate([sya_ref[...], yb_s], axis=1)
        yo_s = jnp.dot(ymix_s.astype(BF16), wout_ref[...], preferred_element_type=F32)
        yn_s = yo_s * lax.rsqrt(jnp.mean(yo_s * yo_s, axis=-1, keepdims=True) + EPS) * postw
        ys_ref[...] = xs_ref[...] + sgate_ref[...] * yn_s


def _layer_spec(shape, l):
    return pl.BlockSpec((None,) + tuple(shape), lambda s, _n=len(shape): (l,) + (0,) * _n)


def _prompt_layer(l, x, mod_p, lw, consts, dec, prev):
    nb, seq, _ = x.shape
    depth = lw["win"].shape[0]
    rows = nb * CHUNK
    nstep = seq // CHUNK
    nseq = dec["x"].shape[0]
    blk_states = nseq * N_HEADS // nstep
    assert blk_states % N_HEADS == 0 and SUBLANES % (blk_states // N_HEADS) == 0
    full = lambda shape: pl.BlockSpec(shape, lambda s, _n=len(shape): (0,) * _n)
    sssm_spec = pl.BlockSpec((None, blk_states, HEAD_DIM, HEAD_DIM), lambda s: (l, s, 0, 0))
    dec_specs = [
        full((nseq, D_MODEL)), pl.BlockSpec((None, nseq, D_MODEL), lambda s: (l, 0, 2)),
        full((nseq, D_B)), full((nseq, D_B)), full((nseq, D_B)),
        full((N_HEADS, nseq, LANES)), full((N_HEADS, nseq, LANES)), full((N_HEADS, nseq, LANES)),
        full((nseq, D_A)), full((nseq, D_B)), sssm_spec,
    ]
    smem = pl.BlockSpec(memory_space=pltpu.SMEM)
    in_specs = [
        smem, smem,
        pl.BlockSpec((nb, CHUNK, D_MODEL), lambda s: (0, s, 0)),
        _layer_spec((nb, 3 * D_MODEL), l),
        full((depth, D_MODEL)), full((depth, D_MODEL)),
        pl.BlockSpec((None, D_MAIN, D_MODEL), lambda s: (l, 0, 0)),
        pl.BlockSpec((None, SUBLANES, D_MODEL), lambda s: (l, D_MAIN // SUBLANES, 0)),
        _layer_spec((CONV_K, D_QKV), l), _layer_spec((N_POOL, POOL_GC, POOL_GC), l),
        full((depth, D_A)), full((depth, HEAD_DIM)), _layer_spec((D_MODEL, D_MODEL), l),
        full((N_POOL, CHUNK, 2 * CHUNK)), full((CHUNK, 2 * CHUNK)),
    ] + dec_specs + [pl.BlockSpec(memory_space=pl.ANY)] * len(prev)
    out_specs = [
        pl.BlockSpec((nb, CHUNK, D_MODEL), lambda s: (0, s, 0)),
        _layer_spec((POOL_BUF, nb, D_A), l), _layer_spec((CONV_K - 1, nb, D_QKV), l),
        _layer_spec((nb * N_HEADS, HEAD_DIM, HEAD_DIM), l),
        sssm_spec, full((nseq, D_MODEL)),
    ]
    out_shape = [
        jax.ShapeDtypeStruct((nb, seq, D_MODEL), F32),
        jax.ShapeDtypeStruct((depth, POOL_BUF, nb, D_A), F32),
        jax.ShapeDtypeStruct((depth, CONV_K - 1, nb, D_QKV), F32),
        jax.ShapeDtypeStruct((depth, nb * N_HEADS, HEAD_DIM, HEAD_DIM), F32),
        jax.ShapeDtypeStruct((depth, nseq * N_HEADS, HEAD_DIM, HEAD_DIM), F32),
        jax.ShapeDtypeStruct((nseq, D_MODEL), F32),
    ]
    scratch = [
        pltpu.VMEM((QKV_TILES, nb, CHUNK + SUBLANES, LANES), F32),
        pltpu.VMEM((nb, 2 * CHUNK, D_A), F32),
        pltpu.VMEM((nb * N_HEADS, HEAD_DIM, HEAD_DIM), F32),
        pltpu.VMEM((rows, D_B), F32), pltpu.VMEM((rows, D_B), F32), pltpu.VMEM((N_HEADS, HEAD_DIM, rows), F32),
        pltpu.VMEM((rows, D_B), F32), pltpu.VMEM((rows, D_B), F32),
        pltpu.VMEM((N_HEADS, rows, LANES), F32), pltpu.VMEM((N_HEADS, rows, LANES), F32),
        pltpu.VMEM((N_HEADS, rows, LANES), F32),
        pltpu.VMEM((SUBLANES, rows), F32),
        pltpu.VMEM((nseq, D_B), F32),
    ]
    n_in = len(in_specs) - len(prev)
    return pl.pallas_call(
        functools.partial(_prompt_kernel, n_alias=len(prev), layer=l),
        grid=(nstep,),
        in_specs=in_specs,
        out_specs=out_specs,
        out_shape=out_shape,
        scratch_shapes=scratch,
        input_output_aliases={n_in + i: 1 + i for i in range(len(prev))},
        compiler_params=pltpu.CompilerParams(dimension_semantics=("arbitrary",), vmem_limit_bytes=VMEM_LIMIT),
        name="prompt_layer",
    )(lw["alog"], lw["dtb"], x, mod_p, lw["prew"], lw["postw"], lw["win"], lw["win"], lw["convw"], lw["poolw"],
      lw["pscale"], lw["onw"], lw["wout"], consts["band"], consts["ltri"],
      dec["x"], dec["mod"], dec["q"], dec["k"], dec["v"], dec["beta"], dec["eg"], dec["qk"], dec["ya"], dec["zb"],
      dec["ssm"], *prev)


def _decode_front_kernel(alog_ref, dtb_ref, x_ref, mod_ref, spool_ref, sconv_ref, prew_ref, win_ref, wba_ref,
                         convw_ref, poolw_ref, pscale_ref, *rest, n_alias, layer):
    rest = rest[n_alias:]
    npool_ref, nconv_ref, q_ref, k_ref, v_ref, beta_ref, eg_ref, qk_ref, ya_ref, zb_ref = rest
    nseq = x_ref.shape[0]
    x = x_ref[...]
    mod = mod_ref[...]
    shift = mod[:, 0:D_MODEL]
    scale = mod[:, D_MODEL:2 * D_MODEL]
    a_mul = prew_ref[layer:layer + 1, :] * (1.0 + scale)
    ms = jnp.mean(x * x, axis=-1, keepdims=True)
    hb = (x * lax.rsqrt(ms + EPS) * a_mul + shift).astype(BF16)

    ua = _proj(hb, win_ref, 0, D_A)
    ya_parts = []
    for gi, w in enumerate(POOL_WINDOWS):
        gs = slice(gi * POOL_GC, (gi + 1) * POOL_GC)
        win = ua[:, gs]
        for d in range(1, w):
            win = win + spool_ref[POOL_BUF - d, :, gs]
        cnt = float(min(PAST_LEN + 1, w))
        pooled = win / cnt - ua[:, gs]
        ya_parts.append(_dot(pooled, poolw_ref[gi]))
    za = _proj(hb, win_ref, D_A, 2 * D_A)
    ya_ref[...] = jnp.concatenate(ya_parts, axis=1) * pscale_ref[layer:layer + 1, :] * _silu(za)
    npool_ref[0:POOL_BUF - 1] = spool_ref[1:POOL_BUF]
    npool_ref[POOL_BUF - 1] = ua

    qkv = _proj(hb, win_ref, 2 * D_A, 2 * D_A + D_QKV)
    cw = convw_ref[...]
    acc = qkv * cw[CONV_K - 1:CONV_K]
    for j in range(CONV_K - 1):
        acc = acc + sconv_ref[j] * cw[j:j + 1]
    nconv_ref[0:CONV_K - 2] = sconv_ref[1:CONV_K - 1]
    nconv_ref[CONV_K - 2] = qkv
    qkvc = _silu(acc)
    bac = _gate_logits(hb, wba_ref)
    sig = jax.nn.sigmoid(bac)
    eg = jnp.exp(-jnp.exp(_head_scalars(alog_ref, layer, (1, LANES), 1))
                 * _softplus(bac + _head_scalars(dtb_ref, layer, (1, LANES), 1)))
    for hh in range(N_HEADS):
        sl = slice(hh * HEAD_DIM, (hh + 1) * HEAD_DIM)
        qh = qkvc[:, hh * HEAD_DIM:(hh + 1) * HEAD_DIM]
        kh = qkvc[:, D_B + hh * HEAD_DIM:D_B + (hh + 1) * HEAD_DIM]
        qn = qh * lax.rsqrt(jnp.sum(qh * qh, axis=-1, keepdims=True) + EPS) * (HEAD_DIM ** -0.5)
        kn = kh * lax.rsqrt(jnp.sum(kh * kh, axis=-1, keepdims=True) + EPS)
        q_ref[:, sl] = qn
        k_ref[:, sl] = kn
        qk_ref[hh] = jnp.broadcast_to(jnp.sum(qn * kn, axis=-1, keepdims=True), (nseq, LANES))
        beta_ref[hh] = jnp.broadcast_to(sig[:, hh:hh + 1], (nseq, LANES))
        eg_ref[hh] = jnp.broadcast_to(eg[:, N_HEADS + hh:N_HEADS + hh + 1], (nseq, LANES))
    v_ref[...] = qkvc[:, 2 * D_B:3 * D_B]
    zb_ref[...] = _proj(hb, win_ref, 2 * D_A + D_QKV, D_MAIN)


def _decode_state_step(step, q_ref, k_ref, v_ref, beta_ref, eg_ref, qk_ref, ssm_ref, nssm_ref, o_s):
    nrow = ssm_ref.shape[0] // N_HEADS
    per_tile = SUBLANES // nrow
    tile0 = pl.multiple_of((step // per_tile) * SUBLANES, SUBLANES)
    sub = step % per_tile

    def my_rows(tile):
        out = tile[0:nrow]
        for j in range(1, per_tile):
            out = jnp.where(sub == j, tile[j * nrow:(j + 1) * nrow], out)
        return out

    kblk = my_rows(k_ref[pl.ds(tile0, SUBLANES), :])
    qblk = my_rows(q_ref[pl.ds(tile0, SUBLANES), :])
    vblk = my_rows(v_ref[pl.ds(tile0, SUBLANES), :])
    beta = [my_rows(beta_ref[hh, pl.ds(tile0, SUBLANES), :]) for hh in range(N_HEADS)]
    egs = [my_rows(eg_ref[hh, pl.ds(tile0, SUBLANES), :]) for hh in range(N_HEADS)]
    qks = [my_rows(qk_ref[hh, pl.ds(tile0, SUBLANES), :]) for hh in range(N_HEADS)]
    row8 = lax.broadcasted_iota(jnp.int32, (SUBLANES, HEAD_DIM), 0)
    st = []
    for i in range(nrow):
        for hh in range(N_HEADS):
            sl = slice(hh * HEAD_DIM, (hh + 1) * HEAD_DIM)
            krow, qrow = kblk[i:i + 1, sl], qblk[i:i + 1, sl]
            lhs = jnp.where(row8 == 0, krow, jnp.where(row8 == 1, qrow, 0.0))
            r = jnp.dot(lhs, ssm_ref[i * N_HEADS + hh], preferred_element_type=F32)
            st.append(dict(i=i, hh=hh, sl=sl, krow=krow, r=r))
    for c in st:
        i, hh = c["i"], c["hh"]
        eg = egs[hh][i:i + 1]
        r = c.pop("r")
        delta = (vblk[i:i + 1, c["sl"]] - eg * r[0:1]) * beta[hh][i:i + 1]
        c["o"] = eg * r[1:2] + qks[hh][i:i + 1] * delta
        krow = c.pop("krow")
        k_hi = krow.astype(BF16).astype(F32)
        d_hi = delta.astype(BF16).astype(F32)
        kp = jnp.where(row8 == 0, k_hi, jnp.where(row8 == 1, krow - k_hi, jnp.where(row8 == 2, k_hi, 0.0)))
        dp = jnp.where(row8 == 0, d_hi, jnp.where(row8 == 1, d_hi, jnp.where(row8 == 2, delta - d_hi, 0.0)))
        c["upd"] = _dot_tn(kp, dp)
    for c in st:
        idx = c["i"] * N_HEADS + c["hh"]
        nssm_ref[idx] = ssm_ref[idx] * egs[c["hh"]][c["i"]:c["i"] + 1] + c.pop("upd")
    o_rows = [jnp.concatenate([c["o"] for c in st[i * N_HEADS:(i + 1) * N_HEADS]], axis=1) for i in range(nrow)]
    tile = o_s[pl.ds(tile0, SUBLANES), :]
    placed = jnp.concatenate(o_rows * per_tile, axis=0)
    row_group = lax.broadcasted_iota(jnp.int32, tile.shape, 0) // nrow
    o_s[pl.ds(tile0, SUBLANES), :] = jnp.where(row_group == sub, placed, tile)


def _decode_front(l, x, mod_s, state_pool, state_conv, lw, prev):
    nseq = x.shape[0]
    depth = lw["win"].shape[0]
    full = lambda shape: pl.BlockSpec(shape, lambda s, _n=len(shape): (0,) * _n)
    smem = pl.BlockSpec(memory_space=pltpu.SMEM)
    in_specs = [
        smem, smem,
        full((nseq, D_MODEL)), _layer_spec((nseq, 3 * D_MODEL), l),
        _layer_spec((POOL_BUF, nseq, D_A), l), _layer_spec((CONV_K - 1, nseq, D_QKV), l),
        full((depth, D_MODEL)),
        pl.BlockSpec((None, D_MAIN, D_MODEL), lambda s: (l, 0, 0)),
        pl.BlockSpec((None, SUBLANES, D_MODEL), lambda s: (l, D_MAIN // SUBLANES, 0)),
        _layer_spec((CONV_K, D_QKV), l), _layer_spec((N_POOL, POOL_GC, POOL_GC), l), full((depth, D_A)),
    ] + [pl.BlockSpec(memory_space=pl.ANY)] * len(prev)
    row_out = lambda width: (full((nseq, width)), jax.ShapeDtypeStruct((nseq, width), F32))
    head_out = (full((N_HEADS, nseq, LANES)), jax.ShapeDtypeStruct((N_HEADS, nseq, LANES), F32))
    outs = [
        (_layer_spec((POOL_BUF, nseq, D_A), l), jax.ShapeDtypeStruct((depth, POOL_BUF, nseq, D_A), F32)),
        (_layer_spec((CONV_K - 1, nseq, D_QKV), l), jax.ShapeDtypeStruct((depth, CONV_K - 1, nseq, D_QKV), F32)),
        row_out(D_B), row_out(D_B), row_out(D_B), head_out, head_out, head_out, row_out(D_A), row_out(D_B),
    ]
    n_in = len(in_specs) - len(prev)
    npool, nconv, q, k, v, beta, eg, qk, ya, zb = pl.pallas_call(
        functools.partial(_decode_front_kernel, n_alias=len(prev), layer=l),
        grid=(1,),
        in_specs=in_specs,
        out_specs=[o[0] for o in outs],
        out_shape=[o[1] for o in outs],
        input_output_aliases={n_in + i: i for i in range(len(prev))},
        compiler_params=pltpu.CompilerParams(dimension_semantics=("arbitrary",), vmem_limit_bytes=VMEM_LIMIT),
        name="decode_front",
    )(lw["alog"], lw["dtb"], x, mod_s, state_pool, state_conv, lw["prew"], lw["win"], lw["win"], lw["convw"],
      lw["poolw"], lw["pscale"], *prev)
    return (npool, nconv), dict(x=x, mod=mod_s, q=q, k=k, v=v, beta=beta, eg=eg, qk=qk, ya=ya, zb=zb)


def _constants():
    t = np.arange(CHUNK)[:, None]
    j = np.arange(2 * CHUNK)[None, :]
    band = np.stack([((j <= CHUNK + t) & (j > CHUNK + t - w)) for w in POOL_WINDOWS]).astype(np.float32)
    tri = (np.arange(CHUNK)[None, :] <= t).astype(np.float32)
    ltri = np.concatenate([tri, tri], axis=1)
    return {"band": jnp.asarray(band, BF16), "ltri": jnp.asarray(ltri, BF16)}


def _stacked_weights(pre_norm_w, post_norm_w, w_in, conv_w, pool_w, pool_scale, a_log, dt_bias, o_norm_w, w_out):
    w_in_t = jnp.swapaxes(w_in, 1, 2)
    return {
        "prew": pre_norm_w, "postw": post_norm_w, "win": w_in_t.astype(BF16), "convw": conv_w, "poolw": pool_w,
        "pscale": pool_scale, "alog": a_log, "dtb": dt_bias, "onw": o_norm_w, "wout": w_out.astype(BF16),
    }


def kernel(x_prompt, x_sample, c_prompt, c_sample, state_pool, state_conv, state_ssm, w_ada, b_ada, pre_norm_w,
           post_norm_w, w_in, conv_w, pool_w, pool_scale, a_log, dt_bias, o_norm_w, w_out):
    depth = w_in.shape[0]
    nb, seq, _ = x_prompt.shape
    nseq, dec_seq, _ = x_sample.shape
    assert dec_seq == 1 and seq % CHUNK == 0 and nb % 2 == 0 and nseq % (seq // CHUNK) == 0
    consts = _constants()
    lw = _stacked_weights(pre_norm_w, post_norm_w, w_in, conv_w, pool_w, pool_scale, a_log, dt_bias, o_norm_w, w_out)
    mod_p, mod_s = _mod_call(c_prompt, c_sample, w_ada, b_ada)
    ssm_in = state_ssm.reshape(depth, nseq * N_HEADS, HEAD_DIM, HEAD_DIM)
    pool_in = jnp.swapaxes(state_pool, 1, 2)
    conv_in = jnp.swapaxes(state_conv, 1, 2)
    yp, ys = x_prompt, x_sample.reshape(nseq, D_MODEL)
    prev_p, prev_s = (), ()
    for l in range(depth):
        prev_s, dec = _decode_front(l, ys, mod_s, pool_in, conv_in, lw, tuple(prev_s))
        yp, *prev_p, ys = _prompt_layer(l, yp, mod_p, lw, consts, dict(dec, ssm=ssm_in), tuple(prev_p))
    npool_p, nconv_p, nssm_p, nssm_s = prev_p
    npool_s, nconv_s = prev_s
    return (yp, ys.reshape(nseq, dec_seq, D_MODEL), jnp.swapaxes(npool_p, 1, 2), jnp.swapaxes(nconv_p, 1, 2),
            nssm_p.reshape(depth, nb, N_HEADS, HEAD_DIM, HEAD_DIM),
            jnp.swapaxes(npool_s, 1, 2), jnp.swapaxes(nconv_s, 1, 2),
            nssm_s.reshape(depth, nseq, N_HEADS, HEAD_DIM, HEAD_DIM))
```

```python
import functools

import jax
import jax.numpy as jnp
import numpy as np
from jax import lax
from jax.experimental import pallas as pl
from jax.experimental.pallas import tpu as pltpu

F32 = jnp.float32
BF16 = jnp.bfloat16

D_MODEL = 1024
D_A = 512
D_B = 512
N_POOL = 4
POOL_WINDOWS = (2, 4, 8, 16)
POOL_GC = 128
POOL_BUF = 15
HEAD_DIM = 128
N_HEADS = 4
D_QKV = 3 * D_B
CONV_K = 4
D_MAIN = 2 * D_A + D_QKV + D_B
PAST_LEN = 16384
EPS = 1e-6
CHUNK = 64
GDN_GROUP = 16
LANES = 128
SUBLANES = 8
QKV_TILES = D_QKV // LANES
MOD_SPLIT = 4
VMEM_LIMIT = 56 * 1024 * 1024


def _dot(a, b):
    return jnp.dot(a.astype(BF16), b.astype(BF16), preferred_element_type=F32)


def _dot_tn(a, b):
    return lax.dot_general(a, b, (((0,), (0,)), ((), ())), preferred_element_type=F32)


def _proj(hb, wt_ref, c0, c1):
    return lax.dot_general(hb, wt_ref[c0:c1, :], (((1,), (1,)), ((), ())), preferred_element_type=F32)


def _gate_logits(hb, wba_ref):
    w = jnp.concatenate([wba_ref[...], jnp.zeros((LANES - SUBLANES, wba_ref.shape[1]), wba_ref.dtype)], axis=0)
    return lax.dot_general(hb, w, (((1,), (1,)), ((), ())), preferred_element_type=F32)


def _silu(x):
    half = 0.5 * x
    return half + half * jnp.tanh(half)


def _softplus(x):
    return jnp.maximum(x, 0.0) + jnp.log1p(jnp.exp(-jnp.abs(x)))


def _block_diag(a, b):
    top = jnp.concatenate([a, jnp.zeros((a.shape[0], b.shape[1]), a.dtype)], axis=1)
    bot = jnp.concatenate([jnp.zeros((b.shape[0], a.shape[1]), b.dtype), b], axis=1)
    return jnp.concatenate([top, bot], axis=0)


def _head_rms(o_all, w):
    parts = []
    for hh in range(N_HEADS):
        oh = o_all[:, hh * HEAD_DIM:(hh + 1) * HEAD_DIM]
        parts.append(oh * lax.rsqrt(jnp.mean(oh * oh, axis=-1, keepdims=True) + EPS) * w)
    return jnp.concatenate(parts, axis=1)


def _mod_kernel(cp_ref, cs_ref, w_ref, b_ref, op_ref, os_ref):
    @pl.when(pl.program_id(1) == 0)
    def _():
        b = b_ref[pl.ds(pl.program_id(0), 1), :]
        op_ref[...] = jnp.broadcast_to(b, op_ref.shape)
        os_ref[...] = jnp.broadcast_to(b, os_ref.shape)

    w = w_ref[...]
    op_ref[...] += jnp.dot(_silu(cp_ref[...]), w, preferred_element_type=F32)
    os_ref[...] += jnp.dot(_silu(cs_ref[...]), w, preferred_element_type=F32)


def _mod_call(c_prompt, c_sample, w_ada, b_ada):
    depth, _, n3 = w_ada.shape
    nb, nseq = c_prompt.shape[0], c_sample.shape[0]
    kb = D_MODEL // MOD_SPLIT
    return pl.pallas_call(
        _mod_kernel,
        grid=(depth, MOD_SPLIT),
        in_specs=[
            pl.BlockSpec((nb, kb), lambda l, k: (0, k)),
            pl.BlockSpec((nseq, kb), lambda l, k: (0, k)),
            pl.BlockSpec((None, kb, n3), lambda l, k: (l, k, 0)),
            pl.BlockSpec((depth, n3), lambda l, k: (0, 0)),
        ],
        out_specs=[pl.BlockSpec((None, nb, n3), lambda l, k: (l, 0, 0)),
                   pl.BlockSpec((None, nseq, n3), lambda l, k: (l, 0, 0))],
        out_shape=[jax.ShapeDtypeStruct((depth, nb, n3), F32), jax.ShapeDtypeStruct((depth, nseq, n3), F32)],
        compiler_params=pltpu.CompilerParams(dimension_semantics=("arbitrary", "arbitrary")),
        name="adaln_mod",
    )(c_prompt, c_sample, w_ada, b_ada)


def _head_scalars(ref, layer, shape, axis):
    pos = lax.broadcasted_iota(jnp.int32, shape, axis)
    out = jnp.zeros(shape, F32)
    for hh in range(N_HEADS):
        out = jnp.where(pos == N_HEADS + hh, ref[layer, hh], out)
    return out


def _prompt_kernel(alog_ref, dtb_ref, x_ref, mod_ref, prew_ref, postw_ref, win_ref, wba_ref, convw_ref, poolw_ref,
                   pscale_ref, onw_ref, wout_ref, band_ref, ltri_ref,
                   xs_ref, sgate_ref, sq_ref, sk_ref, sv_ref, sbeta_ref, seg_ref, sqk_ref, sya_ref, szb_ref, sssm_ref,
                   *rest, n_alias, layer):
    rest = rest[n_alias:]
    y_ref, npool_ref, nconv_ref, nssm_ref, nsssm_ref, ys_ref = rest[0:6]
    qkv_ext, ua_ext, s_ref, q_s, k_s, kt_s, v_s, o_s, beta_s, gc_s, eg_s, gcrow_s, so_s = rest[6:]
    step = pl.program_id(0)
    nb = x_ref.shape[0]
    rows = nb * CHUNK

    @pl.when(step == 0)
    def _():
        qkv_ext[:, :, 0:SUBLANES, :] = jnp.zeros((QKV_TILES, nb, SUBLANES, LANES), F32)
        ua_ext[:, 0:CHUNK, :] = jnp.zeros((nb, CHUNK, D_A), F32)
        s_ref[...] = jnp.zeros(s_ref.shape, F32)
        so_s[...] = jnp.zeros(so_s.shape, F32)

    x = x_ref[...]
    mod = mod_ref[...][:, None, :]
    shift = mod[:, :, 0:D_MODEL]
    scale = mod[:, :, D_MODEL:2 * D_MODEL]
    gate = mod[:, :, 2 * D_MODEL:3 * D_MODEL]
    prew, postw = prew_ref[layer:layer + 1, :], postw_ref[layer:layer + 1, :]
    pscale, onw = pscale_ref[layer:layer + 1, :], onw_ref[layer:layer + 1, :]
    a_mul = prew[None] * (1.0 + scale)
    ms = jnp.mean(x * x, axis=-1, keepdims=True)
    h = x * lax.rsqrt(ms + EPS) * a_mul + shift
    hb = h.reshape(rows, D_MODEL).astype(BF16)

    ua = _proj(hb, win_ref, 0, D_A)
    ua3 = ua.reshape(nb, CHUNK, D_A)
    ua_ext[:, CHUNK:2 * CHUNK, :] = ua3
    pos = lax.broadcasted_iota(jnp.int32, (CHUNK, POOL_GC), 0) + step * CHUNK
    pooled_groups = []
    for gi, w in enumerate(POOL_WINDOWS):
        cnt = jnp.minimum(pos + 1, w).astype(F32)
        gsl = slice(gi * POOL_GC, (gi + 1) * POOL_GC)
        per_b = []
        for b in range(0, nb, 2):
            ext = jnp.concatenate([ua_ext[b, :, gsl], ua_ext[b + 1, :, gsl]], axis=1)
            win = jnp.dot(band_ref[gi], ext.astype(BF16), preferred_element_type=F32)
            per_b.append(win[:, 0:POOL_GC] / cnt - ua3[b, :, gsl])
            per_b.append(win[:, POOL_GC:2 * POOL_GC] / cnt - ua3[b + 1, :, gsl])
        pooled_groups.append(jnp.concatenate(per_b, axis=0))
    ya = jnp.concatenate([_dot(jnp.concatenate(pooled_groups[2 * g2:2 * g2 + 2], axis=1),
                               _block_diag(poolw_ref[2 * g2], poolw_ref[2 * g2 + 1]))
                          for g2 in range(N_POOL // 2)], axis=1)
    za = _proj(hb, win_ref, D_A, 2 * D_A)
    ya = ya * pscale * _silu(za)

    qkv = _proj(hb, win_ref, 2 * D_A, 2 * D_A + D_QKV)
    qkv_t = jnp.stack([qkv[:, c * LANES:(c + 1) * LANES] for c in range(QKV_TILES)], axis=0)
    qkv_t = qkv_t.reshape(QKV_TILES, nb, CHUNK, LANES)
    qkv_ext[:, :, SUBLANES:SUBLANES + CHUNK, :] = qkv_t
    cw = convw_ref[...]
    cw_t = [jnp.stack([cw[j:j + 1, c * LANES:(c + 1) * LANES] for c in range(QKV_TILES)], axis=0)[:, None]
            for j in range(CONV_K)]
    acc = qkv_t * cw_t[CONV_K - 1]
    for j in range(CONV_K - 1):
        acc = acc + qkv_ext[:, :, pl.ds(SUBLANES - (CONV_K - 1) + j, CHUNK), :] * cw_t[j]
    act = _silu(acc).reshape(QKV_TILES, rows, LANES)
    for hh in range(N_HEADS):
        sl = slice(hh * HEAD_DIM, (hh + 1) * HEAD_DIM)
        qh, kh = act[hh], act[N_HEADS + hh]
        q_s[:, sl] = qh * lax.rsqrt(jnp.sum(qh * qh, axis=-1, keepdims=True) + EPS) * (HEAD_DIM ** -0.5)
        kn = kh * lax.rsqrt(jnp.sum(kh * kh, axis=-1, keepdims=True) + EPS)
        k_s[:, sl] = kn
        kt_s[hh] = kn.T
        v_s[:, sl] = act[2 * N_HEADS + hh]

    bac = _gate_logits(hb, wba_ref)
    bar = bac.T[0:SUBLANES]
    sig = jax.nn.sigmoid(bac)
    gcol = (-jnp.exp(_head_scalars(alog_ref, layer, (1, LANES), 1))
            * _softplus(bac + _head_scalars(dtb_ref, layer, (1, LANES), 1)))
    g_hi = gcol.astype(BF16)
    g_lo = (gcol - g_hi.astype(F32)).astype(BF16)
    gc_parts = []
    for b in range(0, nb, 2):
        ra, rb = slice(b * CHUNK, (b + 1) * CHUNK), slice((b + 1) * CHUNK, (b + 2) * CHUNK)
        hi_lo = jnp.concatenate([jnp.concatenate([g_hi[ra], g_lo[ra]], axis=0),
                                 jnp.concatenate([g_hi[rb], g_lo[rb]], axis=0)], axis=1)
        both = jnp.dot(ltri_ref[...], hi_lo, preferred_element_type=F32)
        gc_parts += [both[:, 0:LANES], both[:, LANES:2 * LANES]]
    gc = jnp.concatenate(gc_parts, axis=0)
    for hh in range(N_HEADS):
        beta_s[hh] = jnp.broadcast_to(sig[:, hh:hh + 1], (rows, LANES))
        gcb = jnp.broadcast_to(gc[:, N_HEADS + hh:N_HEADS + hh + 1], (rows, LANES))
        gc_s[hh] = gcb
        eg_s[hh] = jnp.exp(gcb)
    grow = (-jnp.exp(_head_scalars(alog_ref, layer, (SUBLANES, rows), 0))
            * _softplus(bar + _head_scalars(dtb_ref, layer, (SUBLANES, rows), 0)))
    lane_in_chunk = lax.broadcasted_iota(jnp.int32, grow.shape, 1) % CHUNK
    sh = 1
    while sh < CHUNK:
        grow = grow + jnp.where(lane_in_chunk >= sh, pltpu.roll(grow, sh, 1), 0.0)
        sh *= 2
    gcrow_s[...] = grow

    _decode_state_step(step, sq_ref, sk_ref, sv_ref, sbeta_ref, seg_ref, sqk_ref, sssm_ref, nsssm_ref, so_s)

    ri = lax.broadcasted_iota(jnp.int32, (CHUNK, LANES), 0)
    li = lax.broadcasted_iota(jnp.int32, (CHUNK, LANES), 1)
    lj = li % CHUNK
    lo_half = li < CHUNK
    strict = ri > lj
    causal = ri >= lj
    eye2 = (ri == lj).astype(F32)
    lo_half_hd = lax.broadcasted_iota(jnp.int32, (HEAD_DIM, LANES), 1) < CHUNK

    def pair_mul(xp, yp):
        ybd = jnp.concatenate([jnp.where(lo_half, yp, 0.0), jnp.where(lo_half, 0.0, yp)], axis=0)
        return _dot(xp, ybd)

    chains = [(p, hh) for p in range(nb // 2) for hh in range(N_HEADS)]
    for g0 in range(0, len(chains), GDN_GROUP):
        st = []
        for p, hh in chains[g0:g0 + GDN_GROUP]:
            ra = slice(2 * p * CHUNK, (2 * p + 1) * CHUNK)
            rb = slice((2 * p + 1) * CHUNK, (2 * p + 2) * CHUNK)
            sl = slice(hh * HEAD_DIM, (hh + 1) * HEAD_DIM)
            kba, kbb = k_s[ra, sl] * beta_s[hh, ra, :], k_s[rb, sl] * beta_s[hh, rb, :]
            lhs1 = jnp.concatenate([jnp.concatenate([kba, kbb], axis=1),
                                    jnp.concatenate([q_s[ra, sl], q_s[rb, sl]], axis=1)], axis=0)
            kt = kt_s[hh, :, 2 * p * CHUNK:(2 * p + 2) * CHUNK]
            ktbd = jnp.concatenate([jnp.where(lo_half_hd, kt, 0.0), jnp.where(lo_half_hd, 0.0, kt)], axis=0)
            kkqk = _dot(lhs1, ktbd)
            st.append(dict(ra=ra, rb=rb, sl=sl, hh=hh, p=p, kkqk=kkqk))
        for c in st:
            hh, p = c["hh"], c["p"]
            gcol_p = jnp.where(lo_half, gc_s[hh, c["ra"], :], gc_s[hh, c["rb"], :])
            grow_p = jnp.broadcast_to(gcrow_s[N_HEADS + hh:N_HEADS + hh + 1, 2 * p * CHUNK:(2 * p + 2) * CHUNK],
                                      (CHUNK, LANES))
            dec = jnp.exp(jnp.minimum(gcol_p - grow_p, 0.0))
            kkqk = c.pop("kkqk")
            c["npow"] = jnp.where(strict, kkqk[0:CHUNK] * dec, 0.0)
            c["qkm"] = jnp.where(causal, kkqk[CHUNK:2 * CHUNK] * dec, 0.0)
            c["t"] = eye2 - c["npow"]
        for c in st:
            c["npow"] = pair_mul(c["npow"], c["npow"])
        for _ in range(4):
            for c in st:
                both = pair_mul(jnp.concatenate([c["t"], c["npow"]], axis=0), c["npow"])
                c["t"] = c["t"] + both[0:CHUNK]
                c["npow"] = both[CHUNK:2 * CHUNK]
        for c in st:
            c["t"] = c["t"] + pair_mul(c["t"], c["npow"])
        for c in st:
            ra, rb, sl, hh = c["ra"], c["rb"], c["sl"], c["hh"]
            ba, bb = beta_s[hh, ra, :], beta_s[hh, rb, :]
            kba, kbb = k_s[ra, sl] * ba, k_s[rb, sl] * bb
            rhs = jnp.concatenate([_block_diag(v_s[ra, sl] * ba, v_s[rb, sl] * bb),
                                   _block_diag(kba * eg_s[hh, ra, :], kbb * eg_s[hh, rb, :])], axis=1)
            c["uw"] = _dot(c.pop("t"), rhs)
        for c in st:
            ra, rb, sl, hh, p = c["ra"], c["rb"], c["sl"], c["hh"], c["p"]
            c["ia"], c["ib"] = 2 * p * N_HEADS + hh, (2 * p + 1) * N_HEADS + hh
            lhs3 = jnp.concatenate([c["uw"][:, 2 * HEAD_DIM:4 * HEAD_DIM],
                                    jnp.concatenate([q_s[ra, sl] * eg_s[hh, ra, :], q_s[rb, sl] * eg_s[hh, rb, :]],
                                                    axis=1)], axis=0)
            c["r3"] = _dot(lhs3, _block_diag(s_ref[c["ia"]], s_ref[c["ib"]]))
        for c in st:
            ra, rb, sl, hh, p = c["ra"], c["rb"], c["sl"], c["hh"], c["p"]
            r3 = c.pop("r3")
            vn = c.pop("uw")[:, 0:2 * HEAD_DIM] - r3[0:CHUNK]
            gla = gc_s[hh, (2 * p + 1) * CHUNK - 1:(2 * p + 1) * CHUNK, :]
            glb = gc_s[hh, (2 * p + 2) * CHUNK - 1:(2 * p + 2) * CHUNK, :]
            pair_lanes = slice(2 * p * CHUNK, (2 * p + 2) * CHUNK)
            to_end = jnp.exp(jnp.where(lo_half[0:1], gla, glb) - gcrow_s[N_HEADS + hh:N_HEADS + hh + 1, pair_lanes])
            kgt = kt_s[hh, :, pair_lanes] * to_end
            both = _dot(jnp.concatenate([c.pop("qkm"), kgt], axis=0),
                        _block_diag(vn[:, 0:HEAD_DIM], vn[:, HEAD_DIM:2 * HEAD_DIM]))
            o = r3[CHUNK:2 * CHUNK] + both[0:CHUNK]
            o_s[ra, sl] = o[:, 0:HEAD_DIM]
            o_s[rb, sl] = o[:, HEAD_DIM:2 * HEAD_DIM]
            upd = both[CHUNK:CHUNK + HEAD_DIM]
            s_ref[c["ia"]] = s_ref[c["ia"]] * jnp.exp(gla) + upd[:, 0:HEAD_DIM]
            s_ref[c["ib"]] = s_ref[c["ib"]] * jnp.exp(glb) + upd[:, HEAD_DIM:2 * HEAD_DIM]

    zb = _proj(hb, win_ref, 2 * D_A + D_QKV, D_MAIN)
    yb = _head_rms(o_s[...], onw) * _silu(zb)
    ymix = jnp.concatenate([ya, yb], axis=1)
    yo = jnp.dot(ymix.astype(BF16), wout_ref[...], preferred_element_type=F32)
    yn = yo * lax.rsqrt(jnp.mean(yo * yo, axis=-1, keepdims=True) + EPS)
    y_ref[...] = x + yn.reshape(nb, CHUNK, D_MODEL) * (gate * postw[None])

    qkv_ext[:, :, 0:SUBLANES, :] = qkv_ext[:, :, CHUNK:CHUNK + SUBLANES, :]
    ua_ext[:, 0:CHUNK, :] = ua3

    @pl.when(step == pl.num_programs(0) - 1)
    def _():
        qkv3 = qkv.reshape(nb, CHUNK, D_QKV)
        for b in range(nb):
            npool_ref[:, b, :] = ua3[b, CHUNK - POOL_BUF:CHUNK, :]
            nconv_ref[:, b, :] = qkv3[b, CHUNK - (CONV_K - 1):CHUNK, :]
        nssm_ref[...] = s_ref[...]
        yb_s = _head_rms(so_s[...], onw) * _silu(szb_ref[...])
        ymix_s = jnp.concatenate([sya_ref[...], yb_s], axis=1)
        yo_s = jnp.dot(ymix_s.astype(BF16), wout_ref[...], preferred_element_type=F32)
        yn_s = yo_s * lax.rsqrt(jnp.mean(yo_s * yo_s, axis=-1, keepdims=True) + EPS) * postw
        ys_ref[...] = xs_ref[...] + sgate_ref[...] * yn_s


def _layer_spec(shape, l):
    return pl.BlockSpec((None,) + tuple(shape), lambda s, _n=len(shape): (l,) + (0,) * _n)


def _prompt_layer(l, x, mod_p, lw, consts, dec, prev):
    nb, seq, _ = x.shape
    depth = lw["win"].shape[0]
    rows = nb * CHUNK
    nstep = seq // CHUNK
    nseq = dec["x"].shape[0]
    blk_states = nseq * N_HEADS // nstep
    assert blk_states % N_HEADS == 0 and SUBLANES % (blk_states // N_HEADS) == 0
    full = lambda shape: pl.BlockSpec(shape, lambda s, _n=len(shape): (0,) * _n)
    sssm_spec = pl.BlockSpec((None, blk_states, HEAD_DIM, HEAD_DIM), lambda s: (l, s, 0, 0))
    dec_specs = [
        full((nseq, D_MODEL)), pl.BlockSpec((None, nseq, D_MODEL), lambda s: (l, 0, 2)),
        full((nseq, D_B)), full((nseq, D_B)), full((nseq, D_B)),
        full((N_HEADS, nseq, LANES)), full((N_HEADS, nseq, LANES)), full((N_HEADS, nseq, LANES)),
        full((nseq, D_A)), full((nseq, D_B)), sssm_spec,
    ]
    smem = pl.BlockSpec(memory_space=pltpu.SMEM)
    in_specs = [
        smem, smem,
        pl.BlockSpec((nb, CHUNK, D_MODEL), lambda s: (0, s, 0)),
        _layer_spec((nb, 3 * D_MODEL), l),
        full((depth, D_MODEL)), full((depth, D_MODEL)),
        pl.BlockSpec((None, D_MAIN, D_MODEL), lambda s: (l, 0, 0)),
        pl.BlockSpec((None, SUBLANES, D_MODEL), lambda s: (l, D_MAIN // SUBLANES, 0)),
        _layer_spec((CONV_K, D_QKV), l), _layer_spec((N_POOL, POOL_GC, POOL_GC), l),
        full((depth, D_A)), full((depth, HEAD_DIM)), _layer_spec((D_MODEL, D_MODEL), l),
        full((N_POOL, CHUNK, 2 * CHUNK)), full((CHUNK, 2 * CHUNK)),
    ] + dec_specs + [pl.BlockSpec(memory_space=pl.ANY)] * len(prev)
    out_specs = [
        pl.BlockSpec((nb, CHUNK, D_MODEL), lambda s: (0, s, 0)),
        _layer_spec((POOL_BUF, nb, D_A), l), _layer_spec((CONV_K - 1, nb, D_QKV), l),
        _layer_spec((nb * N_HEADS, HEAD_DIM, HEAD_DIM), l),
        sssm_spec, full((nseq, D_MODEL)),
    ]
    out_shape = [
        jax.ShapeDtypeStruct((nb, seq, D_MODEL), F32),
        jax.ShapeDtypeStruct((depth, POOL_BUF, nb, D_A), F32),
        jax.ShapeDtypeStruct((depth, CONV_K - 1, nb, D_QKV), F32),
        jax.ShapeDtypeStruct((depth, nb * N_HEADS, HEAD_DIM, HEAD_DIM), F32),
        jax.ShapeDtypeStruct((depth, nseq * N_HEADS, HEAD_DIM, HEAD_DIM), F32),
        jax.ShapeDtypeStruct((nseq, D_MODEL), F32),
    ]
    scratch = [
        pltpu.VMEM((QKV_TILES, nb, CHUNK + SUBLANES, LANES), F32),
        pltpu.VMEM((nb, 2 * CHUNK, D_A), F32),
        pltpu.VMEM((nb * N_HEADS, HEAD_DIM, HEAD_DIM), F32),
        pltpu.VMEM((rows, D_B), F32), pltpu.VMEM((rows, D_B), F32), pltpu.VMEM((N_HEADS, HEAD_DIM, rows), F32),
        pltpu.VMEM((rows, D_B), F32), pltpu.VMEM((rows, D_B), F32),
        pltpu.VMEM((N_HEADS, rows, LANES), F32), pltpu.VMEM((N_HEADS, rows, LANES), F32),
        pltpu.VMEM((N_HEADS, rows, LANES), F32),
        pltpu.VMEM((SUBLANES, rows), F32),
        pltpu.VMEM((nseq, D_B), F32),
    ]
    n_in = len(in_specs) - len(prev)
    return pl.pallas_call(
        functools.partial(_prompt_kernel, n_alias=len(prev), layer=l),
        grid=(nstep,),
        in_specs=in_specs,
        out_specs=out_specs,
        out_shape=out_shape,
        scratch_shapes=scratch,
        input_output_aliases={n_in + i: 1 + i for i in range(len(prev))},
        compiler_params=pltpu.CompilerParams(dimension_semantics=("arbitrary",), vmem_limit_bytes=VMEM_LIMIT),
        name="prompt_layer",
    )(lw["alog"], lw["dtb"], x, mod_p, lw["prew"], lw["postw"], lw["win"], lw["win"], lw["convw"], lw["poolw"],
      lw["pscale"], lw["onw"], lw["wout"], consts["band"], consts["ltri"],
      dec["x"], dec["mod"], dec["q"], dec["k"], dec["v"], dec["beta"], dec["eg"], dec["qk"], dec["ya"], dec["zb"],
      dec["ssm"], *prev)


def _decode_front_kernel(alog_ref, dtb_ref, x_ref, mod_ref, spool_ref, sconv_ref, prew_ref, win_ref, wba_ref,
                         convw_ref, poolw_ref, pscale_ref, *rest, n_alias, layer):
    rest = rest[n_alias:]
    npool_ref, nconv_ref, q_ref, k_ref, v_ref, beta_ref, eg_ref, qk_ref, ya_ref, zb_ref = rest
    nseq = x_ref.shape[0]
    x = x_ref[...]
    mod = mod_ref[...]
    shift = mod[:, 0:D_MODEL]
    scale = mod[:, D_MODEL:2 * D_MODEL]
    a_mul = prew_ref[layer:layer + 1, :] * (1.0 + scale)
    ms = jnp.mean(x * x, axis=-1, keepdims=True)
    hb = (x * lax.rsqrt(ms + EPS) * a_mul + shift).astype(BF16)

    ua = _proj(hb, win_ref, 0, D_A)
    ya_parts = []
    for gi, w in enumerate(POOL_WINDOWS):
        gs = slice(gi * POOL_GC, (gi + 1) * POOL_GC)
        win = ua[:, gs]
        for d in range(1, w):
            win = win + spool_ref[POOL_BUF - d, :, gs]
        cnt = float(min(PAST_LEN + 1, w))
        pooled = win / cnt - ua[:, gs]
        ya_parts.append(_dot(pooled, poolw_ref[gi]))
    za = _proj(hb, win_ref, D_A, 2 * D_A)
    ya_ref[...] = jnp.concatenate(ya_parts, axis=1) * pscale_ref[layer:layer + 1, :] * _silu(za)
    npool_ref[0:POOL_BUF - 1] = spool_ref[1:POOL_BUF]
    npool_ref[POOL_BUF - 1] = ua

    qkv = _proj(hb, win_ref, 2 * D_A, 2 * D_A + D_QKV)
    cw = convw_ref[...]
    acc = qkv * cw[CONV_K - 1:CONV_K]
    for j in range(CONV_K - 1):
        acc = acc + sconv_ref[j] * cw[j:j + 1]
    nconv_ref[0:CONV_K - 2] = sconv_ref[1:CONV_K - 1]
    nconv_ref[CONV_K - 2] = qkv
    qkvc = _silu(acc)
    bac = _gate_logits(hb, wba_ref)
    sig = jax.nn.sigmoid(bac)
    eg = jnp.exp(-jnp.exp(_head_scalars(alog_ref, layer, (1, LANES), 1))
                 * _softplus(bac + _head_scalars(dtb_ref, layer, (1, LANES), 1)))
    for hh in range(N_HEADS):
        sl = slice(hh * HEAD_DIM, (hh + 1) * HEAD_DIM)
        qh = qkvc[:, hh * HEAD_DIM:(hh + 1) * HEAD_DIM]
        kh = qkvc[:, D_B + hh * HEAD_DIM:D_B + (hh + 1) * HEAD_DIM]
        qn = qh * lax.rsqrt(jnp.sum(qh * qh, axis=-1, keepdims=True) + EPS) * (HEAD_DIM ** -0.5)
        kn = kh * lax.rsqrt(jnp.sum(kh * kh, axis=-1, keepdims=True) + EPS)
        q_ref[:, sl] = qn
        k_ref[:, sl] = kn
        qk_ref[hh] = jnp.broadcast_to(jnp.sum(qn * kn, axis=-1, keepdims=True), (nseq, LANES))
        beta_ref[hh] = jnp.broadcast_to(sig[:, hh:hh + 1], (nseq, LANES))
        eg_ref[hh] = jnp.broadcast_to(eg[:, N_HEADS + hh:N_HEADS + hh + 1], (nseq, LANES))
    v_ref[...] = qkvc[:, 2 * D_B:3 * D_B]
    zb_ref[...] = _proj(hb, win_ref, 2 * D_A + D_QKV, D_MAIN)


def _decode_state_step(step, q_ref, k_ref, v_ref, beta_ref, eg_ref, qk_ref, ssm_ref, nssm_ref, o_s):
    nrow = ssm_ref.shape[0] // N_HEADS
    per_tile = SUBLANES // nrow
    tile0 = pl.multiple_of((step // per_tile) * SUBLANES, SUBLANES)
    sub = step % per_tile

    def my_rows(tile):
        out = tile[0:nrow]
        for j in range(1, per_tile):
            out = jnp.where(sub == j, tile[j * nrow:(j + 1) * nrow], out)
        return out

    kblk = my_rows(k_ref[pl.ds(tile0, SUBLANES), :])
    qblk = my_rows(q_ref[pl.ds(tile0, SUBLANES), :])
    vblk = my_rows(v_ref[pl.ds(tile0, SUBLANES), :])
    beta = [my_rows(beta_ref[hh, pl.ds(tile0, SUBLANES), :]) for hh in range(N_HEADS)]
    egs = [my_rows(eg_ref[hh, pl.ds(tile0, SUBLANES), :]) for hh in range(N_HEADS)]
    qks = [my_rows(qk_ref[hh, pl.ds(tile0, SUBLANES), :]) for hh in range(N_HEADS)]
    row8 = lax.broadcasted_iota(jnp.int32, (SUBLANES, HEAD_DIM), 0)
    st = []
    for i in range(nrow):
        for hh in range(N_HEADS):
            sl = slice(hh * HEAD_DIM, (hh + 1) * HEAD_DIM)
            krow, qrow = kblk[i:i + 1, sl], qblk[i:i + 1, sl]
            lhs = jnp.where(row8 == 0, krow, jnp.where(row8 == 1, qrow, 0.0))
            r = jnp.dot(lhs, ssm_ref[i * N_HEADS + hh], preferred_element_type=F32)
            st.append(dict(i=i, hh=hh, sl=sl, krow=krow, r=r))
    for c in st:
        i, hh = c["i"], c["hh"]
        eg = egs[hh][i:i + 1]
        r = c.pop("r")
        delta = (vblk[i:i + 1, c["sl"]] - eg * r[0:1]) * beta[hh][i:i + 1]
        c["o"] = eg * r[1:2] + qks[hh][i:i + 1] * delta
        krow = c.pop("krow")
        k_hi = krow.astype(BF16).astype(F32)
        d_hi = delta.astype(BF16).astype(F32)
        kp = jnp.where(row8 == 0, k_hi, jnp.where(row8 == 1, krow - k_hi, jnp.where(row8 == 2, k_hi, 0.0)))
        dp = jnp.where(row8 == 0, d_hi, jnp.where(row8 == 1, d_hi, jnp.where(row8 == 2, delta - d_hi, 0.0)))
        c["upd"] = _dot_tn(kp, dp)
    for c in st:
        idx = c["i"] * N_HEADS + c["hh"]
        nssm_ref[idx] = ssm_ref[idx] * egs[c["hh"]][c["i"]:c["i"] + 1] + c.pop("upd")
    o_rows = [jnp.concatenate([c["o"] for c in st[i * N_HEADS:(i + 1) * N_HEADS]], axis=1) for i in range(nrow)]
    tile = o_s[pl.ds(tile0, SUBLANES), :]
    placed = jnp.concatenate(o_rows * per_tile, axis=0)
    row_group = lax.broadcasted_iota(jnp.int32, tile.shape, 0) // nrow
    o_s[pl.ds(tile0, SUBLANES), :] = jnp.where(row_group == sub, placed, tile)


def _decode_front(l, x, mod_s, state_pool, state_conv, lw, prev):
    nseq = x.shape[0]
    depth = lw["win"].shape[0]
    full = lambda shape: pl.BlockSpec(shape, lambda s, _n=len(shape): (0,) * _n)
    smem = pl.BlockSpec(memory_space=pltpu.SMEM)
    in_specs = [
        smem, smem,
        full((nseq, D_MODEL)), _layer_spec((nseq, 3 * D_MODEL), l),
        _layer_spec((POOL_BUF, nseq, D_A), l), _layer_spec((CONV_K - 1, nseq, D_QKV), l),
        full((depth, D_MODEL)),
        pl.BlockSpec((None, D_MAIN, D_MODEL), lambda s: (l, 0, 0)),
        pl.BlockSpec((None, SUBLANES, D_MODEL), lambda s: (l, D_MAIN // SUBLANES, 0)),
        _layer_spec((CONV_K, D_QKV), l), _layer_spec((N_POOL, POOL_GC, POOL_GC), l), full((depth, D_A)),
    ] + [pl.BlockSpec(memory_space=pl.ANY)] * len(prev)
    row_out = lambda width: (full((nseq, width)), jax.ShapeDtypeStruct((nseq, width), F32))
    head_out = (full((N_HEADS, nseq, LANES)), jax.ShapeDtypeStruct((N_HEADS, nseq, LANES), F32))
    outs = [
        (_layer_spec((POOL_BUF, nseq, D_A), l), jax.ShapeDtypeStruct((depth, POOL_BUF, nseq, D_A), F32)),
        (_layer_spec((CONV_K - 1, nseq, D_QKV), l), jax.ShapeDtypeStruct((depth, CONV_K - 1, nseq, D_QKV), F32)),
        row_out(D_B), row_out(D_B), row_out(D_B), head_out, head_out, head_out, row_out(D_A), row_out(D_B),
    ]
    n_in = len(in_specs) - len(prev)
    npool, nconv, q, k, v, beta, eg, qk, ya, zb = pl.pallas_call(
        functools.partial(_decode_front_kernel, n_alias=len(prev), layer=l),
        grid=(1,),
        in_specs=in_specs,
        out_specs=[o[0] for o in outs],
        out_shape=[o[1] for o in outs],
        input_output_aliases={n_in + i: i for i in range(len(prev))},
        compiler_params=pltpu.CompilerParams(dimension_semantics=("arbitrary",), vmem_limit_bytes=VMEM_LIMIT),
        name="decode_front",
    )(lw["alog"], lw["dtb"], x, mod_s, state_pool, state_conv, lw["prew"], lw["win"], lw["win"], lw["convw"],
      lw["poolw"], lw["pscale"], *prev)
    return (npool, nconv), dict(x=x, mod=mod_s, q=q, k=k, v=v, beta=beta, eg=eg, qk=qk, ya=ya, zb=zb)


def _constants():
    t = np.arange(CHUNK)[:, None]
    j = np.arange(2 * CHUNK)[None, :]
    band = np.stack([((j <= CHUNK + t) & (j > CHUNK + t - w)) for w in POOL_WINDOWS]).astype(np.float32)
    tri = (np.arange(CHUNK)[None, :] <= t).astype(np.float32)
    ltri = np.concatenate([tri, tri], axis=1)
    return {"band": jnp.asarray(band, BF16), "ltri": jnp.asarray(ltri, BF16)}


def _stacked_weights(pre_norm_w, post_norm_w, w_in, conv_w, pool_w, pool_scale, a_log, dt_bias, o_norm_w, w_out):
    w_in_t = jnp.swapaxes(w_in, 1, 2)
    return {
        "prew": pre_norm_w, "postw": post_norm_w, "win": w_in_t.astype(BF16), "convw": conv_w, "poolw": pool_w,
        "pscale": pool_scale, "alog": a_log, "dtb": dt_bias, "onw": o_norm_w, "wout": w_out.astype(BF16),
    }


def kernel(x_prompt, x_sample, c_prompt, c_sample, state_pool, state_conv, state_ssm, w_ada, b_ada, pre_norm_w,
           post_norm_w, w_in, conv_w, pool_w, pool_scale, a_log, dt_bias, o_norm_w, w_out):
    depth = w_in.shape[0]
    nb, seq, _ = x_prompt.shape
    nseq, dec_seq, _ = x_sample.shape
    assert dec_seq == 1 and seq % CHUNK == 0 and nb % 2 == 0 and nseq % (seq // CHUNK) == 0
    consts = _constants()
    lw = _stacked_weights(pre_norm_w, post_norm_w, w_in, conv_w, pool_w, pool_scale, a_log, dt_bias, o_norm_w, w_out)
    mod_p, mod_s = _mod_call(c_prompt, c_sample, w_ada, b_ada)
    ssm_in = state_ssm.reshape(depth, nseq * N_HEADS, HEAD_DIM, HEAD_DIM)
    pool_in = jnp.swapaxes(state_pool, 1, 2)
    conv_in = jnp.swapaxes(state_conv, 1, 2)
    yp, ys = x_prompt, x_sample.reshape(nseq, D_MODEL)
    prev_p, prev_s = (), ()
    for l in range(depth):
        prev_s, dec = _decode_front(l, ys, mod_s, pool_in, conv_in, lw, tuple(prev_s))
        yp, *prev_p, ys = _prompt_layer(l, yp, mod_p, lw, consts, dict(dec, ssm=ssm_in), tuple(prev_p))
    npool_p, nconv_p, nssm_p, nssm_s = prev_p
    npool_s, nconv_s = prev_s
    return (yp, ys.reshape(nseq, dec_seq, D_MODEL), jnp.swapaxes(npool_p, 1, 2), jnp.swapaxes(nconv_p, 1, 2),
            nssm_p.reshape(depth, nb, N_HEADS, HEAD_DIM, HEAD_DIM),
            jnp.swapaxes(npool_s, 1, 2), jnp.swapaxes(nconv_s, 1, 2),
            nssm_s.reshape(depth, nseq, N_HEADS, HEAD_DIM, HEAD_DIM))
```

```python
import functools

import jax
import jax.numpy as jnp
import numpy as np
from jax import lax
from jax.experimental import pallas as pl
from jax.experimental.pallas import tpu as pltpu

F32 = jnp.float32
BF16 = jnp.bfloat16

D_MODEL = 1024
D_A = 512
D_B = 512
N_POOL = 4
POOL_WINDOWS = (2, 4, 8, 16)
POOL_GC = 128
POOL_BUF = 15
HEAD_DIM = 128
N_HEADS = 4
D_QKV = 3 * D_B
CONV_K = 4
D_MAIN = 2 * D_A + D_QKV + D_B
PAST_LEN = 16384
EPS = 1e-6
CHUNK = 64
GDN_GROUP = 16
LANES = 128
SUBLANES = 8
QKV_TILES = D_QKV // LANES
MOD_SPLIT = 4
VMEM_LIMIT = 56 * 1024 * 1024


def _dot(a, b):
    return jnp.dot(a.astype(BF16), b.astype(BF16), preferred_element_type=F32)


def _dot_tn(a, b):
    return lax.dot_general(a, b, (((0,), (0,)), ((), ())), preferred_element_type=F32)


def _proj(hb, wt_ref, c0, c1):
    return lax.dot_general(hb, wt_ref[c0:c1, :], (((1,), (1,)), ((), ())), preferred_element_type=F32)


def _gate_logits(hb, wba_ref):
    w = jnp.concatenate([wba_ref[...], jnp.zeros((LANES - SUBLANES, wba_ref.shape[1]), wba_ref.dtype)], axis=0)
    return lax.dot_general(hb, w, (((1,), (1,)), ((), ())), preferred_element_type=F32)


def _silu(x):
    half = 0.5 * x
    return half + half * jnp.tanh(half)


def _softplus(x):
    return jnp.maximum(x, 0.0) + jnp.log1p(jnp.exp(-jnp.abs(x)))


def _block_diag(a, b):
    top = jnp.concatenate([a, jnp.zeros((a.shape[0], b.shape[1]), a.dtype)], axis=1)
    bot = jnp.concatenate([jnp.zeros((b.shape[0], a.shape[1]), b.dtype), b], axis=1)
    return jnp.concatenate([top, bot], axis=0)


def _head_rms(o_all, w):
    parts = []
    for hh in range(N_HEADS):
        oh = o_all[:, hh * HEAD_DIM:(hh + 1) * HEAD_DIM]
        parts.append(oh * lax.rsqrt(jnp.mean(oh * oh, axis=-1, keepdims=True) + EPS) * w)
    return jnp.concatenate(parts, axis=1)


def _mod_kernel(cp_ref, cs_ref, w_ref, b_ref, op_ref, os_ref):
    @pl.when(pl.program_id(1) == 0)
    def _():
        b = b_ref[pl.ds(pl.program_id(0), 1), :]
        op_ref[...] = jnp.broadcast_to(b, op_ref.shape)
        os_ref[...] = jnp.broadcast_to(b, os_ref.shape)

    nb = cp_ref.shape[0]
    c = jnp.concatenate([cp_ref[...], cs_ref[...]], axis=0)
    r = jnp.dot(_silu(c), w_ref[...], preferred_element_type=F32)
    op_ref[...] += r[0:nb]
    os_ref[...] += r[nb:]


def _mod_call(c_prompt, c_sample, w_ada, b_ada):
    depth, _, n3 = w_ada.shape
    nb, nseq = c_prompt.shape[0], c_sample.shape[0]
    kb = D_MODEL // MOD_SPLIT
    return pl.pallas_call(
        _mod_kernel,
        grid=(depth, MOD_SPLIT),
        in_specs=[
            pl.BlockSpec((nb, kb), lambda l, k: (0, k)),
            pl.BlockSpec((nseq, kb), lambda l, k: (0, k)),
            pl.BlockSpec((None, kb, n3), lambda l, k: (l, k, 0)),
            pl.BlockSpec((depth, n3), lambda l, k: (0, 0)),
        ],
        out_specs=[pl.BlockSpec((None, nb, n3), lambda l, k: (l, 0, 0)),
                   pl.BlockSpec((None, nseq, n3), lambda l, k: (l, 0, 0))],
        out_shape=[jax.ShapeDtypeStruct((depth, nb, n3), F32), jax.ShapeDtypeStruct((depth, nseq, n3), F32)],
        compiler_params=pltpu.CompilerParams(dimension_semantics=("arbitrary", "arbitrary")),
        name="adaln_mod",
    )(c_prompt, c_sample, w_ada, b_ada)


def _head_scalars(ref, layer, shape, axis):
    pos = lax.broadcasted_iota(jnp.int32, shape, axis)
    out = jnp.zeros(shape, F32)
    for hh in range(N_HEADS):
        out = jnp.where(pos == N_HEADS + hh, ref[layer, hh], out)
    return out


def _prompt_kernel(alog_ref, dtb_ref, x_ref, mod_ref, prew_ref, postw_ref, win_ref, wba_ref, convw_ref, poolw_ref,
                   pscale_ref, onw_ref, wout_ref, band_ref, ltri_ref,
                   xs_ref, sgate_ref, sq_ref, sk_ref, sv_ref, sbeta_ref, seg_ref, sqk_ref, sya_ref, szb_ref, sssm_ref,
                   *rest, n_alias, layer):
    rest = rest[n_alias:]
    y_ref, npool_ref, nconv_ref, nssm_ref, nsssm_ref, ys_ref = rest[0:6]
    qkv_ext, ua_ext, s_ref, q_s, k_s, kt_s, v_s, o_s, beta_s, gc_s, eg_s, gcrow_s, so_s = rest[6:]
    step = pl.program_id(0)
    nb = x_ref.shape[0]
    rows = nb * CHUNK

    @pl.when(step == 0)
    def _():
        qkv_ext[:, :, 0:SUBLANES, :] = jnp.zeros((QKV_TILES, nb, SUBLANES, LANES), F32)
        ua_ext[:, 0:CHUNK, :] = jnp.zeros((nb, CHUNK, D_A), F32)
        s_ref[...] = jnp.zeros(s_ref.shape, F32)
        so_s[...] = jnp.zeros(so_s.shape, F32)

    x = x_ref[...]
    mod = mod_ref[...][:, None, :]
    shift = mod[:, :, 0:D_MODEL]
    scale = mod[:, :, D_MODEL:2 * D_MODEL]
    gate = mod[:, :, 2 * D_MODEL:3 * D_MODEL]
    prew, postw = prew_ref[layer:layer + 1, :], postw_ref[layer:layer + 1, :]
    pscale, onw = pscale_ref[layer:layer + 1, :], onw_ref[layer:layer + 1, :]
    a_mul = prew[None] * (1.0 + scale)
    ms = jnp.mean(x * x, axis=-1, keepdims=True)
    h = x * lax.rsqrt(ms + EPS) * a_mul + shift
    hb = h.reshape(rows, D_MODEL).astype(BF16)

    ua = _proj(hb, win_ref, 0, D_A)
    ua3 = ua.reshape(nb, CHUNK, D_A)
    ua_ext[:, CHUNK:2 * CHUNK, :] = ua3
    pos = lax.broadcasted_iota(jnp.int32, (CHUNK, POOL_GC), 0) + step * CHUNK
    pooled_groups = []
    for gi, w in enumerate(POOL_WINDOWS):
        cnt = jnp.minimum(pos + 1, w).astype(F32)
        gsl = slice(gi * POOL_GC, (gi + 1) * POOL_GC)
        per_b = []
        for b in range(0, nb, 2):
            ext = jnp.concatenate([ua_ext[b, :, gsl], ua_ext[b + 1, :, gsl]], axis=1)
            win = jnp.dot(band_ref[gi], ext.astype(BF16), preferred_element_type=F32)
            per_b.append(win[:, 0:POOL_GC] / cnt - ua3[b, :, gsl])
            per_b.append(win[:, POOL_GC:2 * POOL_GC] / cnt - ua3[b + 1, :, gsl])
        pooled_groups.append(jnp.concatenate(per_b, axis=0))
    ya = jnp.concatenate([_dot(jnp.concatenate(pooled_groups[2 * g2:2 * g2 + 2], axis=1),
                               _block_diag(poolw_ref[2 * g2], poolw_ref[2 * g2 + 1]))
                          for g2 in range(N_POOL // 2)], axis=1)
    za = _proj(hb, win_ref, D_A, 2 * D_A)
    ya = ya * pscale * _silu(za)

    qkv = _proj(hb, win_ref, 2 * D_A, 2 * D_A + D_QKV)
    qkv_t = jnp.stack([qkv[:, c * LANES:(c + 1) * LANES] for c in range(QKV_TILES)], axis=0)
    qkv_t = qkv_t.reshape(QKV_TILES, nb, CHUNK, LANES)
    qkv_ext[:, :, SUBLANES:SUBLANES + CHUNK, :] = qkv_t
    cw = convw_ref[...]
    cw_t = [jnp.stack([cw[j:j + 1, c * LANES:(c + 1) * LANES] for c in range(QKV_TILES)], axis=0)[:, None]
            for j in range(CONV_K)]
    acc = qkv_t * cw_t[CONV_K - 1]
    for j in range(CONV_K - 1):
        acc = acc + qkv_ext[:, :, pl.ds(SUBLANES - (CONV_K - 1) + j, CHUNK), :] * cw_t[j]
    act = _silu(acc).reshape(QKV_TILES, rows, LANES)
    for hh in range(N_HEADS):
        sl = slice(hh * HEAD_DIM, (hh + 1) * HEAD_DIM)
        qh, kh = act[hh], act[N_HEADS + hh]
        q_s[:, sl] = qh * lax.rsqrt(jnp.sum(qh * qh, axis=-1, keepdims=True) + EPS) * (HEAD_DIM ** -0.5)
        kn = kh * lax.rsqrt(jnp.sum(kh * kh, axis=-1, keepdims=True) + EPS)
        k_s[:, sl] = kn
        kt_s[hh] = kn.T
        v_s[:, sl] = act[2 * N_HEADS + hh]

    bac = _gate_logits(hb, wba_ref)
    bar = bac.T[0:SUBLANES]
    sig = jax.nn.sigmoid(bac)
    gcol = (-jnp.exp(_head_scalars(alog_ref, layer, (1, LANES), 1))
            * _softplus(bac + _head_scalars(dtb_ref, layer, (1, LANES), 1)))
    g_hi = gcol.astype(BF16)
    g_lo = (gcol - g_hi.astype(F32)).astype(BF16)
    gc_parts = []
    for b in range(0, nb, 2):
        ra, rb = slice(b * CHUNK, (b + 1) * CHUNK), slice((b + 1) * CHUNK, (b + 2) * CHUNK)
        hi_lo = jnp.concatenate([jnp.concatenate([g_hi[ra], g_lo[ra]], axis=0),
                                 jnp.concatenate([g_hi[rb], g_lo[rb]], axis=0)], axis=1)
        both = jnp.dot(ltri_ref[...], hi_lo, preferred_element_type=F32)
        gc_parts += [both[:, 0:LANES], both[:, LANES:2 * LANES]]
    gc = jnp.concatenate(gc_parts, axis=0)
    for hh in range(N_HEADS):
        beta_s[hh] = jnp.broadcast_to(sig[:, hh:hh + 1], (rows, LANES))
        gcb = jnp.broadcast_to(gc[:, N_HEADS + hh:N_HEADS + hh + 1], (rows, LANES))
        gc_s[hh] = gcb
        eg_s[hh] = jnp.exp(gcb)
    grow = (-jnp.exp(_head_scalars(alog_ref, layer, (SUBLANES, rows), 0))
            * _softplus(bar + _head_scalars(dtb_ref, layer, (SUBLANES, rows), 0)))
    lane_in_chunk = lax.broadcasted_iota(jnp.int32, grow.shape, 1) % CHUNK
    sh = 1
    while sh < CHUNK:
        grow = grow + jnp.where(lane_in_chunk >= sh, pltpu.roll(grow, sh, 1), 0.0)
        sh *= 2
    gcrow_s[...] = grow

    _decode_state_step(step, sq_ref, sk_ref, sv_ref, sbeta_ref, seg_ref, sqk_ref, sssm_ref, nsssm_ref, so_s)

    ri = lax.broadcasted_iota(jnp.int32, (CHUNK, LANES), 0)
    li = lax.broadcasted_iota(jnp.int32, (CHUNK, LANES), 1)
    lj = li % CHUNK
    lo_half = li < CHUNK
    strict = ri > lj
    causal = ri >= lj
    eye2 = (ri == lj).astype(F32)
    lo_half_hd = lax.broadcasted_iota(jnp.int32, (HEAD_DIM, LANES), 1) < CHUNK

    def pair_mul(xp, yp):
        ybd = jnp.concatenate([jnp.where(lo_half, yp, 0.0), jnp.where(lo_half, 0.0, yp)], axis=0)
        return _dot(xp, ybd)

    chains = [(p, hh) for p in range(nb // 2) for hh in range(N_HEADS)]
    for g0 in range(0, len(chains), GDN_GROUP):
        st = []
        for p, hh in chains[g0:g0 + GDN_GROUP]:
            ra = slice(2 * p * CHUNK, (2 * p + 1) * CHUNK)
            rb = slice((2 * p + 1) * CHUNK, (2 * p + 2) * CHUNK)
            sl = slice(hh * HEAD_DIM, (hh + 1) * HEAD_DIM)
            kba, kbb = k_s[ra, sl] * beta_s[hh, ra, :], k_s[rb, sl] * beta_s[hh, rb, :]
            lhs1 = jnp.concatenate([jnp.concatenate([kba, kbb], axis=1),
                                    jnp.concatenate([q_s[ra, sl], q_s[rb, sl]], axis=1)], axis=0)
            kt = kt_s[hh, :, 2 * p * CHUNK:(2 * p + 2) * CHUNK]
            ktbd = jnp.concatenate([jnp.where(lo_half_hd, kt, 0.0), jnp.where(lo_half_hd, 0.0, kt)], axis=0)
            kkqk = _dot(lhs1, ktbd)
            st.append(dict(ra=ra, rb=rb, sl=sl, hh=hh, p=p, kkqk=kkqk))
        for c in st:
            hh, p = c["hh"], c["p"]
            gcol_p = jnp.where(lo_half, gc_s[hh, c["ra"], :], gc_s[hh, c["rb"], :])
            grow_p = jnp.broadcast_to(gcrow_s[N_HEADS + hh:N_HEADS + hh + 1, 2 * p * CHUNK:(2 * p + 2) * CHUNK],
                                      (CHUNK, LANES))
            dec = jnp.exp(jnp.minimum(gcol_p - grow_p, 0.0))
            kkqk = c.pop("kkqk")
            c["npow"] = jnp.where(strict, kkqk[0:CHUNK] * dec, 0.0)
            c["qkm"] = jnp.where(causal, kkqk[CHUNK:2 * CHUNK] * dec, 0.0)
            c["t"] = eye2 - c["npow"]
        for c in st:
            c["npow"] = pair_mul(c["npow"], c["npow"])
        for _ in range(4):
            for c in st:
                both = pair_mul(jnp.concatenate([c["t"], c["npow"]], axis=0), c["npow"])
                c["t"] = c["t"] + both[0:CHUNK]
                c["npow"] = both[CHUNK:2 * CHUNK]
        for c in st:
            c["t"] = c["t"] + pair_mul(c["t"], c["npow"])
        for c in st:
            ra, rb, sl, hh = c["ra"], c["rb"], c["sl"], c["hh"]
            ba, bb = beta_s[hh, ra, :], beta_s[hh, rb, :]
            kba, kbb = k_s[ra, sl] * ba, k_s[rb, sl] * bb
            rhs = jnp.concatenate([_block_diag(v_s[ra, sl] * ba, v_s[rb, sl] * bb),
                                   _block_diag(kba * eg_s[hh, ra, :], kbb * eg_s[hh, rb, :])], axis=1)
            c["uw"] = _dot(c.pop("t"), rhs)
        for c in st:
            ra, rb, sl, hh, p = c["ra"], c["rb"], c["sl"], c["hh"], c["p"]
            c["ia"], c["ib"] = 2 * p * N_HEADS + hh, (2 * p + 1) * N_HEADS + hh
            lhs3 = jnp.concatenate([c["uw"][:, 2 * HEAD_DIM:4 * HEAD_DIM],
                                    jnp.concatenate([q_s[ra, sl] * eg_s[hh, ra, :], q_s[rb, sl] * eg_s[hh, rb, :]],
                                                    axis=1)], axis=0)
            c["r3"] = _dot(lhs3, _block_diag(s_ref[c["ia"]], s_ref[c["ib"]]))
        for c in st:
            ra, rb, sl, hh, p = c["ra"], c["rb"], c["sl"], c["hh"], c["p"]
            r3 = c.pop("r3")
            vn = c.pop("uw")[:, 0:2 * HEAD_DIM] - r3[0:CHUNK]
            gla = gc_s[hh, (2 * p + 1) * CHUNK - 1:(2 * p + 1) * CHUNK, :]
            glb = gc_s[hh, (2 * p + 2) * CHUNK - 1:(2 * p + 2) * CHUNK, :]
            pair_lanes = slice(2 * p * CHUNK, (2 * p + 2) * CHUNK)
            to_end = jnp.exp(jnp.where(lo_half[0:1], gla, glb) - gcrow_s[N_HEADS + hh:N_HEADS + hh + 1, pair_lanes])
            kgt = kt_s[hh, :, pair_lanes] * to_end
            both = _dot(jnp.concatenate([c.pop("qkm"), kgt], axis=0),
                        _block_diag(vn[:, 0:HEAD_DIM], vn[:, HEAD_DIM:2 * HEAD_DIM]))
            o = r3[CHUNK:2 * CHUNK] + both[0:CHUNK]
            o_s[ra, sl] = o[:, 0:HEAD_DIM]
            o_s[rb, sl] = o[:, HEAD_DIM:2 * HEAD_DIM]
            upd = both[CHUNK:CHUNK + HEAD_DIM]
            s_ref[c["ia"]] = s_ref[c["ia"]] * jnp.exp(gla) + upd[:, 0:HEAD_DIM]
            s_ref[c["ib"]] = s_ref[c["ib"]] * jnp.exp(glb) + upd[:, HEAD_DIM:2 * HEAD_DIM]

    zb = _proj(hb, win_ref, 2 * D_A + D_QKV, D_MAIN)
    yb = _head_rms(o_s[...], onw) * _silu(zb)
    ymix = jnp.concatenate([ya, yb], axis=1)
    yo = jnp.dot(ymix.astype(BF16), wout_ref[...], preferred_element_type=F32)
    yn = yo * lax.rsqrt(jnp.mean(yo * yo, axis=-1, keepdims=True) + EPS)
    y_ref[...] = x + yn.reshape(nb, CHUNK, D_MODEL) * (gate * postw[None])

    qkv_ext[:, :, 0:SUBLANES, :] = qkv_ext[:, :, CHUNK:CHUNK + SUBLANES, :]
    ua_ext[:, 0:CHUNK, :] = ua3

    @pl.when(step == pl.num_programs(0) - 1)
    def _():
        qkv3 = qkv.reshape(nb, CHUNK, D_QKV)
        for b in range(nb):
            npool_ref[:, b, :] = ua3[b, CHUNK - POOL_BUF:CHUNK, :]
            nconv_ref[:, b, :] = qkv3[b, CHUNK - (CONV_K - 1):CHUNK, :]
        nssm_ref[...] = s_ref[...]
        yb_s = _head_rms(so_s[...], onw) * _silu(szb_ref[...])
        ymix_s = jnp.concatenate([sya_ref[...], yb_s], axis=1)
        yo_s = jnp.dot(ymix_s.astype(BF16), wout_ref[...], preferred_element_type=F32)
        yn_s = yo_s * lax.rsqrt(jnp.mean(yo_s * yo_s, axis=-1, keepdims=True) + EPS) * postw
        ys_ref[:, 0, :] = xs_ref[:, 0, :] + sgate_ref[...] * yn_s


def _layer_spec(shape, l):
    return pl.BlockSpec((None,) + tuple(shape), lambda s, _n=len(shape): (l,) + (0,) * _n)


def _prompt_layer(l, x, mod_p, lw, consts, dec, prev):
    nb, seq, _ = x.shape
    depth = lw["win"].shape[0]
    rows = nb * CHUNK
    nstep = seq // CHUNK
    nseq = dec["x"].shape[0]
    blk_states = nseq * N_HEADS // nstep
    assert blk_states % N_HEADS == 0 and SUBLANES % (blk_states // N_HEADS) == 0
    full = lambda shape: pl.BlockSpec(shape, lambda s, _n=len(shape): (0,) * _n)
    sssm_spec = pl.BlockSpec((None, blk_states, HEAD_DIM, HEAD_DIM), lambda s: (l, s, 0, 0))
    dec_specs = [
        full((nseq, 1, D_MODEL)), pl.BlockSpec((None, nseq, D_MODEL), lambda s: (l, 0, 2)),
        full((nseq, D_B)), full((nseq, D_B)), full((nseq, D_B)),
        full((N_HEADS, nseq, LANES)), full((N_HEADS, nseq, LANES)), full((N_HEADS, nseq, LANES)),
        full((nseq, D_A)), full((nseq, D_B)), sssm_spec,
    ]
    smem = pl.BlockSpec(memory_space=pltpu.SMEM)
    in_specs = [
        smem, smem,
        pl.BlockSpec((nb, CHUNK, D_MODEL), lambda s: (0, s, 0)),
        _layer_spec((nb, 3 * D_MODEL), l),
        full((depth, D_MODEL)), full((depth, D_MODEL)),
        pl.BlockSpec((None, D_MAIN, D_MODEL), lambda s: (l, 0, 0)),
        pl.BlockSpec((None, SUBLANES, D_MODEL), lambda s: (l, D_MAIN // SUBLANES, 0)),
        _layer_spec((CONV_K, D_QKV), l), _layer_spec((N_POOL, POOL_GC, POOL_GC), l),
        full((depth, D_A)), full((depth, HEAD_DIM)), _layer_spec((D_MODEL, D_MODEL), l),
        full((N_POOL, CHUNK, 2 * CHUNK)), full((CHUNK, 2 * CHUNK)),
    ] + dec_specs + [pl.BlockSpec(memory_space=pl.ANY)] * len(prev)
    out_specs = [
        pl.BlockSpec((nb, CHUNK, D_MODEL), lambda s: (0, s, 0)),
        _layer_spec((POOL_BUF, nb, D_A), l), _layer_spec((CONV_K - 1, nb, D_QKV), l),
        _layer_spec((nb * N_HEADS, HEAD_DIM, HEAD_DIM), l),
        sssm_spec, full((nseq, 1, D_MODEL)),
    ]
    out_shape = [
        jax.ShapeDtypeStruct((nb, seq, D_MODEL), F32),
        jax.ShapeDtypeStruct((depth, POOL_BUF, nb, D_A), F32),
        jax.ShapeDtypeStruct((depth, CONV_K - 1, nb, D_QKV), F32),
        jax.ShapeDtypeStruct((depth, nb * N_HEADS, HEAD_DIM, HEAD_DIM), F32),
        jax.ShapeDtypeStruct((depth, nseq * N_HEADS, HEAD_DIM, HEAD_DIM), F32),
        jax.ShapeDtypeStruct((nseq, 1, D_MODEL), F32),
    ]
    scratch = [
        pltpu.VMEM((QKV_TILES, nb, CHUNK + SUBLANES, LANES), F32),
        pltpu.VMEM((nb, 2 * CHUNK, D_A), F32),
        pltpu.VMEM((nb * N_HEADS, HEAD_DIM, HEAD_DIM), F32),
        pltpu.VMEM((rows, D_B), F32), pltpu.VMEM((rows, D_B), F32), pltpu.VMEM((N_HEADS, HEAD_DIM, rows), F32),
        pltpu.VMEM((rows, D_B), F32), pltpu.VMEM((rows, D_B), F32),
        pltpu.VMEM((N_HEADS, rows, LANES), F32), pltpu.VMEM((N_HEADS, rows, LANES), F32),
        pltpu.VMEM((N_HEADS, rows, LANES), F32),
        pltpu.VMEM((SUBLANES, rows), F32),
        pltpu.VMEM((nseq, D_B), F32),
    ]
    n_in = len(in_specs) - len(prev)
    return pl.pallas_call(
        functools.partial(_prompt_kernel, n_alias=len(prev), layer=l),
        grid=(nstep,),
        in_specs=in_specs,
        out_specs=out_specs,
        out_shape=out_shape,
        scratch_shapes=scratch,
        input_output_aliases={n_in + i: 1 + i for i in range(len(prev))},
        compiler_params=pltpu.CompilerParams(dimension_semantics=("arbitrary",), vmem_limit_bytes=VMEM_LIMIT),
        name="prompt_layer",
    )(lw["alog"], lw["dtb"], x, mod_p, lw["prew"], lw["postw"], lw["win"], lw["win"], lw["convw"], lw["poolw"],
      lw["pscale"], lw["onw"], lw["wout"], consts["band"], consts["ltri"],
      dec["x"], dec["mod"], dec["q"], dec["k"], dec["v"], dec["beta"], dec["eg"], dec["qk"], dec["ya"], dec["zb"],
      dec["ssm"], *prev)


def _decode_front_kernel(alog_ref, dtb_ref, x_ref, mod_ref, spool_ref, sconv_ref, prew_ref, win_ref, wba_ref,
                         convw_ref, poolw_ref, pscale_ref, *rest, n_alias, layer):
    rest = rest[n_alias:]
    npool_ref, nconv_ref, q_ref, k_ref, v_ref, beta_ref, eg_ref, qk_ref, ya_ref, zb_ref = rest
    nseq = x_ref.shape[0]
    x = x_ref[:, 0, :]
    mod = mod_ref[...]
    shift = mod[:, 0:D_MODEL]
    scale = mod[:, D_MODEL:2 * D_MODEL]
    a_mul = prew_ref[layer:layer + 1, :] * (1.0 + scale)
    ms = jnp.mean(x * x, axis=-1, keepdims=True)
    hb = (x * lax.rsqrt(ms + EPS) * a_mul + shift).astype(BF16)

    ua = _proj(hb, win_ref, 0, D_A)
    ya_parts = []
    for gi, w in enumerate(POOL_WINDOWS):
        gs = slice(gi * POOL_GC, (gi + 1) * POOL_GC)
        win = ua[:, gs]
        for d in range(1, w):
            win = win + spool_ref[POOL_BUF - d, :, gs]
        cnt = float(min(PAST_LEN + 1, w))
        pooled = win / cnt - ua[:, gs]
        ya_parts.append(_dot(pooled, poolw_ref[gi]))
    za = _proj(hb, win_ref, D_A, 2 * D_A)
    ya_ref[...] = jnp.concatenate(ya_parts, axis=1) * pscale_ref[layer:layer + 1, :] * _silu(za)
    npool_ref[0:POOL_BUF - 1] = spool_ref[1:POOL_BUF]
    npool_ref[POOL_BUF - 1] = ua

    qkv = _proj(hb, win_ref, 2 * D_A, 2 * D_A + D_QKV)
    cw = convw_ref[...]
    acc = qkv * cw[CONV_K - 1:CONV_K]
    for j in range(CONV_K - 1):
        acc = acc + sconv_ref[j] * cw[j:j + 1]
    nconv_ref[0:CONV_K - 2] = sconv_ref[1:CONV_K - 1]
    nconv_ref[CONV_K - 2] = qkv
    qkvc = _silu(acc)
    bac = _gate_logits(hb, wba_ref)
    sig = jax.nn.sigmoid(bac)
    eg = jnp.exp(-jnp.exp(_head_scalars(alog_ref, layer, (1, LANES), 1))
                 * _softplus(bac + _head_scalars(dtb_ref, layer, (1, LANES), 1)))
    for hh in range(N_HEADS):
        sl = slice(hh * HEAD_DIM, (hh + 1) * HEAD_DIM)
        qh = qkvc[:, hh * HEAD_DIM:(hh + 1) * HEAD_DIM]
        kh = qkvc[:, D_B + hh * HEAD_DIM:D_B + (hh + 1) * HEAD_DIM]
        qn = qh * lax.rsqrt(jnp.sum(qh * qh, axis=-1, keepdims=True) + EPS) * (HEAD_DIM ** -0.5)
        kn = kh * lax.rsqrt(jnp.sum(kh * kh, axis=-1, keepdims=True) + EPS)
        q_ref[:, sl] = qn
        k_ref[:, sl] = kn
        qk_ref[hh] = jnp.broadcast_to(jnp.sum(qn * kn, axis=-1, keepdims=True), (nseq, LANES))
        beta_ref[hh] = jnp.broadcast_to(sig[:, hh:hh + 1], (nseq, LANES))
        eg_ref[hh] = jnp.broadcast_to(eg[:, N_HEADS + hh:N_HEADS + hh + 1], (nseq, LANES))
    v_ref[...] = qkvc[:, 2 * D_B:3 * D_B]
    zb_ref[...] = _proj(hb, win_ref, 2 * D_A + D_QKV, D_MAIN)


def _decode_state_step(step, q_ref, k_ref, v_ref, beta_ref, eg_ref, qk_ref, ssm_ref, nssm_ref, o_s):
    nrow = ssm_ref.shape[0] // N_HEADS
    per_tile = SUBLANES // nrow
    tile0 = pl.multiple_of((step // per_tile) * SUBLANES, SUBLANES)
    sub = step % per_tile

    def my_rows(tile):
        out = tile[0:nrow]
        for j in range(1, per_tile):
            out = jnp.where(sub == j, tile[j * nrow:(j + 1) * nrow], out)
        return out

    kblk = my_rows(k_ref[pl.ds(tile0, SUBLANES), :])
    qblk = my_rows(q_ref[pl.ds(tile0, SUBLANES), :])
    vblk = my_rows(v_ref[pl.ds(tile0, SUBLANES), :])
    beta = [my_rows(beta_ref[hh, pl.ds(tile0, SUBLANES), :]) for hh in range(N_HEADS)]
    egs = [my_rows(eg_ref[hh, pl.ds(tile0, SUBLANES), :]) for hh in range(N_HEADS)]
    qks = [my_rows(qk_ref[hh, pl.ds(tile0, SUBLANES), :]) for hh in range(N_HEADS)]
    row8 = lax.broadcasted_iota(jnp.int32, (SUBLANES, HEAD_DIM), 0)
    st = []
    for i in range(nrow):
        for hh in range(N_HEADS):
            sl = slice(hh * HEAD_DIM, (hh + 1) * HEAD_DIM)
            krow, qrow = kblk[i:i + 1, sl], qblk[i:i + 1, sl]
            lhs = jnp.where(row8 == 0, krow, jnp.where(row8 == 1, qrow, 0.0))
            r = jnp.dot(lhs, ssm_ref[i * N_HEADS + hh], preferred_element_type=F32)
            st.append(dict(i=i, hh=hh, sl=sl, krow=krow, r=r))
    for c in st:
        i, hh = c["i"], c["hh"]
        eg = egs[hh][i:i + 1]
        r = c.pop("r")
        delta = (vblk[i:i + 1, c["sl"]] - eg * r[0:1]) * beta[hh][i:i + 1]
        c["o"] = eg * r[1:2] + qks[hh][i:i + 1] * delta
        krow = c.pop("krow")
        k_hi = krow.astype(BF16).astype(F32)
        d_hi = delta.astype(BF16).astype(F32)
        kp = jnp.where(row8 == 0, k_hi, jnp.where(row8 == 1, krow - k_hi, jnp.where(row8 == 2, k_hi, 0.0)))
        dp = jnp.where(row8 == 0, d_hi, jnp.where(row8 == 1, d_hi, jnp.where(row8 == 2, delta - d_hi, 0.0)))
        c["upd"] = _dot_tn(kp, dp)
    for c in st:
        idx = c["i"] * N_HEADS + c["hh"]
        nssm_ref[idx] = ssm_ref[idx] * egs[c["hh"]][c["i"]:c["i"] + 1] + c.pop("upd")
    o_rows = [jnp.concatenate([c["o"] for c in st[i * N_HEADS:(i + 1) * N_HEADS]], axis=1) for i in range(nrow)]
    tile = o_s[pl.ds(tile0, SUBLANES), :]
    placed = jnp.concatenate(o_rows * per_tile, axis=0)
    row_group = lax.broadcasted_iota(jnp.int32, tile.shape, 0) // nrow
    o_s[pl.ds(tile0, SUBLANES), :] = jnp.where(row_group == sub, placed, tile)


def _decode_front(l, x, mod_s, state_pool, state_conv, lw, prev):
    nseq = x.shape[0]
    depth = lw["win"].shape[0]
    full = lambda shape: pl.BlockSpec(shape, lambda s, _n=len(shape): (0,) * _n)
    smem = pl.BlockSpec(memory_space=pltpu.SMEM)
    in_specs = [
        smem, smem,
        full((nseq, 1, D_MODEL)), _layer_spec((nseq, 3 * D_MODEL), l),
        _layer_spec((POOL_BUF, nseq, D_A), l), _layer_spec((CONV_K - 1, nseq, D_QKV), l),
        full((depth, D_MODEL)),
        pl.BlockSpec((None, D_MAIN, D_MODEL), lambda s: (l, 0, 0)),
        pl.BlockSpec((None, SUBLANES, D_MODEL), lambda s: (l, D_MAIN // SUBLANES, 0)),
        _layer_spec((CONV_K, D_QKV), l), _layer_spec((N_POOL, POOL_GC, POOL_GC), l), full((depth, D_A)),
    ] + [pl.BlockSpec(memory_space=pl.ANY)] * len(prev)
    row_out = lambda width: (full((nseq, width)), jax.ShapeDtypeStruct((nseq, width), F32))
    head_out = (full((N_HEADS, nseq, LANES)), jax.ShapeDtypeStruct((N_HEADS, nseq, LANES), F32))
    outs = [
        (_layer_spec((POOL_BUF, nseq, D_A), l), jax.ShapeDtypeStruct((depth, POOL_BUF, nseq, D_A), F32)),
        (_layer_spec((CONV_K - 1, nseq, D_QKV), l), jax.ShapeDtypeStruct((depth, CONV_K - 1, nseq, D_QKV), F32)),
        row_out(D_B), row_out(D_B), row_out(D_B), head_out, head_out, head_out, row_out(D_A), row_out(D_B),
    ]
    n_in = len(in_specs) - len(prev)
    npool, nconv, q, k, v, beta, eg, qk, ya, zb = pl.pallas_call(
        functools.partial(_decode_front_kernel, n_alias=len(prev), layer=l),
        grid=(1,),
        in_specs=in_specs,
        out_specs=[o[0] for o in outs],
        out_shape=[o[1] for o in outs],
        input_output_aliases={n_in + i: i for i in range(len(prev))},
        compiler_params=pltpu.CompilerParams(dimension_semantics=("arbitrary",), vmem_limit_bytes=VMEM_LIMIT),
        name="decode_front",
    )(lw["alog"], lw["dtb"], x, mod_s, state_pool, state_conv, lw["prew"], lw["win"], lw["win"], lw["convw"],
      lw["poolw"], lw["pscale"], *prev)
    return (npool, nconv), dict(x=x, mod=mod_s, q=q, k=k, v=v, beta=beta, eg=eg, qk=qk, ya=ya, zb=zb)


def _constants():
    t = np.arange(CHUNK)[:, None]
    j = np.arange(2 * CHUNK)[None, :]
    band = np.stack([((j <= CHUNK + t) & (j > CHUNK + t - w)) for w in POOL_WINDOWS]).astype(np.float32)
    tri = (np.arange(CHUNK)[None, :] <= t).astype(np.float32)
    ltri = np.concatenate([tri, tri], axis=1)
    return {"band": jnp.asarray(band, BF16), "ltri": jnp.asarray(ltri, BF16)}


def _stacked_weights(pre_norm_w, post_norm_w, w_in, conv_w, pool_w, pool_scale, a_log, dt_bias, o_norm_w, w_out):
    w_in_t = jnp.swapaxes(w_in, 1, 2)
    return {
        "prew": pre_norm_w, "postw": post_norm_w, "win": w_in_t.astype(BF16), "convw": conv_w, "poolw": pool_w,
        "pscale": pool_scale, "alog": a_log, "dtb": dt_bias, "onw": o_norm_w, "wout": w_out.astype(BF16),
    }


def kernel(x_prompt, x_sample, c_prompt, c_sample, state_pool, state_conv, state_ssm, w_ada, b_ada, pre_norm_w,
           post_norm_w, w_in, conv_w, pool_w, pool_scale, a_log, dt_bias, o_norm_w, w_out):
    depth = w_in.shape[0]
    nb, seq, _ = x_prompt.shape
    nseq, dec_seq, _ = x_sample.shape
    assert dec_seq == 1 and seq % CHUNK == 0 and nb % 2 == 0 and nseq % (seq // CHUNK) == 0
    consts = _constants()
    lw = _stacked_weights(pre_norm_w, post_norm_w, w_in, conv_w, pool_w, pool_scale, a_log, dt_bias, o_norm_w, w_out)
    mod_p, mod_s = _mod_call(c_prompt, c_sample, w_ada, b_ada)
    ssm_in = state_ssm.reshape(depth, nseq * N_HEADS, HEAD_DIM, HEAD_DIM)
    pool_in = jnp.swapaxes(state_pool, 1, 2)
    conv_in = jnp.swapaxes(state_conv, 1, 2)
    yp, ys = x_prompt, x_sample
    prev_p, prev_s = (), ()
    for l in range(depth):
        prev_s, dec = _decode_front(l, ys, mod_s, pool_in, conv_in, lw, tuple(prev_s))
        yp, *prev_p, ys = _prompt_layer(l, yp, mod_p, lw, consts, dict(dec, ssm=ssm_in), tuple(prev_p))
    npool_p, nconv_p, nssm_p, nssm_s = prev_p
    npool_s, nconv_s = prev_s
    return (yp, ys, jnp.swapaxes(npool_p, 1, 2), jnp.swapaxes(nconv_p, 1, 2),
            nssm_p.reshape(depth, nb, N_HEADS, HEAD_DIM, HEAD_DIM),
            jnp.swapaxes(npool_s, 1, 2), jnp.swapaxes(nconv_s, 1, 2),
            nssm_s.reshape(depth, nseq, N_HEADS, HEAD_DIM, HEAD_DIM))
```

```python
import functools

import jax
import jax.numpy as jnp
import numpy as np
from jax import lax
from jax.experimental import pallas as pl
from jax.experimental.pallas import tpu as pltpu

F32 = jnp.float32
BF16 = jnp.bfloat16

D_MODEL = 1024
D_A = 512
D_B = 512
N_POOL = 4
POOL_WINDOWS = (2, 4, 8, 16)
POOL_GC = 128
POOL_BUF = 15
HEAD_DIM = 128
N_HEADS = 4
D_QKV = 3 * D_B
CONV_K = 4
D_MAIN = 2 * D_A + D_QKV + D_B
PAST_LEN = 16384
EPS = 1e-6
CHUNK = 64
GDN_GROUP = 16
LANES = 128
SUBLANES = 8
QKV_TILES = D_QKV // LANES
MOD_SPLIT = 4
VMEM_LIMIT = 56 * 1024 * 1024


def _dot(a, b):
    return jnp.dot(a.astype(BF16), b.astype(BF16), preferred_element_type=F32)


def _dot_tn(a, b):
    return lax.dot_general(a, b, (((0,), (0,)), ((), ())), preferred_element_type=F32)


def _proj(hb, wt_ref, c0, c1):
    return lax.dot_general(hb, wt_ref[c0:c1, :], (((1,), (1,)), ((), ())), preferred_element_type=F32)


def _gate_logits(hb, wba_ref):
    w = jnp.concatenate([wba_ref[...], jnp.zeros((LANES - SUBLANES, wba_ref.shape[1]), wba_ref.dtype)], axis=0)
    return lax.dot_general(hb, w, (((1,), (1,)), ((), ())), preferred_element_type=F32)


def _silu(x):
    half = 0.5 * x
    return half + half * jnp.tanh(half)


def _softplus(x):
    return jnp.maximum(x, 0.0) + jnp.log1p(jnp.exp(-jnp.abs(x)))


def _block_diag(a, b):
    top = jnp.concatenate([a, jnp.zeros((a.shape[0], b.shape[1]), a.dtype)], axis=1)
    bot = jnp.concatenate([jnp.zeros((b.shape[0], a.shape[1]), b.dtype), b], axis=1)
    return jnp.concatenate([top, bot], axis=0)


def _head_rms(o_all, w):
    parts = []
    for hh in range(N_HEADS):
        oh = o_all[:, hh * HEAD_DIM:(hh + 1) * HEAD_DIM]
        parts.append(oh * lax.rsqrt(jnp.mean(oh * oh, axis=-1, keepdims=True) + EPS) * w)
    return jnp.concatenate(parts, axis=1)


def _mod_kernel(cp_ref, cs_ref, w_ref, b_ref, op_ref, os_ref):
    @pl.when(pl.program_id(1) == 0)
    def _():
        b = b_ref[pl.ds(pl.program_id(0), 1), :]
        op_ref[...] = jnp.broadcast_to(b, op_ref.shape)
        os_ref[...] = jnp.broadcast_to(b, os_ref.shape)

    nb = cp_ref.shape[0]
    c = jnp.concatenate([cp_ref[...], cs_ref[...]], axis=0)
    r = jnp.dot(_silu(c), w_ref[...], preferred_element_type=F32)
    op_ref[...] += r[0:nb]
    os_ref[...] += r[nb:]


def _mod_call(c_prompt, c_sample, w_ada, b_ada):
    depth, _, n3 = w_ada.shape
    nb, nseq = c_prompt.shape[0], c_sample.shape[0]
    kb = D_MODEL // MOD_SPLIT
    return pl.pallas_call(
        _mod_kernel,
        grid=(depth, MOD_SPLIT),
        in_specs=[
            pl.BlockSpec((nb, kb), lambda l, k: (0, k)),
            pl.BlockSpec((nseq, kb), lambda l, k: (0, k)),
            pl.BlockSpec((None, kb, n3), lambda l, k: (l, k, 0)),
            pl.BlockSpec((depth, n3), lambda l, k: (0, 0)),
        ],
        out_specs=[pl.BlockSpec((None, nb, n3), lambda l, k: (l, 0, 0)),
                   pl.BlockSpec((None, nseq, n3), lambda l, k: (l, 0, 0))],
        out_shape=[jax.ShapeDtypeStruct((depth, nb, n3), F32), jax.ShapeDtypeStruct((depth, nseq, n3), F32)],
        compiler_params=pltpu.CompilerParams(dimension_semantics=("arbitrary", "arbitrary")),
        name="adaln_mod",
    )(c_prompt, c_sample, w_ada, b_ada)


def _head_scalars(ref, layer, shape, axis):
    pos = lax.broadcasted_iota(jnp.int32, shape, axis)
    out = jnp.zeros(shape, F32)
    for hh in range(N_HEADS):
        out = jnp.where(pos == N_HEADS + hh, ref[layer, hh], out)
    return out


def _prompt_kernel(alog_ref, dtb_ref, x_ref, mod_ref, prew_ref, postw_ref, win_ref, wba_ref, convw_ref, poolw_ref,
                   pscale_ref, onw_ref, wout_ref, band_ref, ltri_ref,
                   xs_ref, sgate_ref, sq_ref, sk_ref, sv_ref, sbeta_ref, seg_ref, sqk_ref, sya_ref, szb_ref, sssm_ref,
                   *rest, n_alias, layer):
    rest = rest[n_alias:]
    y_ref, npool_ref, nconv_ref, nssm_ref, nsssm_ref, ys_ref = rest[0:6]
    qkv_ext, ua_ext, s_ref, q_s, k_s, kt_s, v_s, o_s, beta_s, gc_s, eg_s, gcrow_s, so_s = rest[6:]
    step = pl.program_id(0)
    nb = x_ref.shape[0]
    rows = nb * CHUNK

    @pl.when(step == 0)
    def _():
        qkv_ext[:, :, 0:SUBLANES, :] = jnp.zeros((QKV_TILES, nb, SUBLANES, LANES), F32)
        ua_ext[:, 0:CHUNK, :] = jnp.zeros((nb, CHUNK, D_A), F32)
        s_ref[...] = jnp.zeros(s_ref.shape, F32)
        so_s[...] = jnp.zeros(so_s.shape, F32)

    x = x_ref[...]
    mod = mod_ref[...][:, None, :]
    shift = mod[:, :, 0:D_MODEL]
    scale = mod[:, :, D_MODEL:2 * D_MODEL]
    gate = mod[:, :, 2 * D_MODEL:3 * D_MODEL]
    prew, postw = prew_ref[layer:layer + 1, :], postw_ref[layer:layer + 1, :]
    pscale, onw = pscale_ref[layer:layer + 1, :], onw_ref[layer:layer + 1, :]
    a_mul = prew[None] * (1.0 + scale)
    ms = jnp.mean(x * x, axis=-1, keepdims=True)
    h = x * lax.rsqrt(ms + EPS) * a_mul + shift
    hb = h.reshape(rows, D_MODEL).astype(BF16)

    ua = _proj(hb, win_ref, 0, D_A)
    ua3 = ua.reshape(nb, CHUNK, D_A)
    ua_ext[:, CHUNK:2 * CHUNK, :] = ua3
    pos = lax.broadcasted_iota(jnp.int32, (CHUNK, POOL_GC), 0) + step * CHUNK
    pooled_groups = []
    for gi, w in enumerate(POOL_WINDOWS):
        cnt = jnp.minimum(pos + 1, w).astype(F32)
        gsl = slice(gi * POOL_GC, (gi + 1) * POOL_GC)
        per_b = []
        for b in range(0, nb, 2):
            ext = jnp.concatenate([ua_ext[b, :, gsl], ua_ext[b + 1, :, gsl]], axis=1)
            win = jnp.dot(band_ref[gi], ext.astype(BF16), preferred_element_type=F32)
            per_b.append(win[:, 0:POOL_GC] / cnt - ua3[b, :, gsl])
            per_b.append(win[:, POOL_GC:2 * POOL_GC] / cnt - ua3[b + 1, :, gsl])
        pooled_groups.append(jnp.concatenate(per_b, axis=0))
    ya = jnp.concatenate([_dot(jnp.concatenate(pooled_groups[2 * g2:2 * g2 + 2], axis=1),
                               _block_diag(poolw_ref[2 * g2], poolw_ref[2 * g2 + 1]))
                          for g2 in range(N_POOL // 2)], axis=1)
    za = _proj(hb, win_ref, D_A, 2 * D_A)
    ya = ya * pscale * _silu(za)

    qkv = _proj(hb, win_ref, 2 * D_A, 2 * D_A + D_QKV)
    qkv_t = jnp.stack([qkv[:, c * LANES:(c + 1) * LANES] for c in range(QKV_TILES)], axis=0)
    qkv_t = qkv_t.reshape(QKV_TILES, nb, CHUNK, LANES)
    qkv_ext[:, :, SUBLANES:SUBLANES + CHUNK, :] = qkv_t
    cw = convw_ref[...]
    cw_t = [jnp.stack([cw[j:j + 1, c * LANES:(c + 1) * LANES] for c in range(QKV_TILES)], axis=0)[:, None]
            for j in range(CONV_K)]
    acc = qkv_t * cw_t[CONV_K - 1]
    for j in range(CONV_K - 1):
        acc = acc + qkv_ext[:, :, pl.ds(SUBLANES - (CONV_K - 1) + j, CHUNK), :] * cw_t[j]
    act = _silu(acc).reshape(QKV_TILES, rows, LANES)
    for hh in range(N_HEADS):
        sl = slice(hh * HEAD_DIM, (hh + 1) * HEAD_DIM)
        qh, kh = act[hh], act[N_HEADS + hh]
        q_s[:, sl] = qh * lax.rsqrt(jnp.sum(qh * qh, axis=-1, keepdims=True) + EPS) * (HEAD_DIM ** -0.5)
        kn = kh * lax.rsqrt(jnp.sum(kh * kh, axis=-1, keepdims=True) + EPS)
        k_s[:, sl] = kn
        kt_s[hh] = kn.T
        v_s[:, sl] = act[2 * N_HEADS + hh]

    bac = _gate_logits(hb, wba_ref)
    bar = bac.T[0:SUBLANES]
    sig = jax.nn.sigmoid(bac)
    gcol = (-jnp.exp(_head_scalars(alog_ref, layer, (1, LANES), 1))
            * _softplus(bac + _head_scalars(dtb_ref, layer, (1, LANES), 1)))
    g_hi = gcol.astype(BF16)
    g_lo = (gcol - g_hi.astype(F32)).astype(BF16)
    gc_parts = []
    for b in range(0, nb, 2):
        ra, rb = slice(b * CHUNK, (b + 1) * CHUNK), slice((b + 1) * CHUNK, (b + 2) * CHUNK)
        hi_lo = jnp.concatenate([jnp.concatenate([g_hi[ra], g_lo[ra]], axis=0),
                                 jnp.concatenate([g_hi[rb], g_lo[rb]], axis=0)], axis=1)
        both = jnp.dot(ltri_ref[...], hi_lo, preferred_element_type=F32)
        gc_parts += [both[:, 0:LANES], both[:, LANES:2 * LANES]]
    gc = jnp.concatenate(gc_parts, axis=0)
    egc = jnp.exp(gc)
    for hh in range(N_HEADS):
        beta_s[hh] = jnp.broadcast_to(sig[:, hh:hh + 1], (rows, LANES))
        gc_s[hh] = jnp.broadcast_to(gc[:, N_HEADS + hh:N_HEADS + hh + 1], (rows, LANES))
        eg_s[hh] = jnp.broadcast_to(egc[:, N_HEADS + hh:N_HEADS + hh + 1], (rows, LANES))
    grow = (-jnp.exp(_head_scalars(alog_ref, layer, (SUBLANES, rows), 0))
            * _softplus(bar + _head_scalars(dtb_ref, layer, (SUBLANES, rows), 0)))
    lane_in_chunk = lax.broadcasted_iota(jnp.int32, grow.shape, 1) % CHUNK
    sh = 1
    while sh < CHUNK:
        grow = grow + jnp.where(lane_in_chunk >= sh, pltpu.roll(grow, sh, 1), 0.0)
        sh *= 2
    gcrow_s[...] = grow

    _decode_state_step(step, sq_ref, sk_ref, sv_ref, sbeta_ref, seg_ref, sqk_ref, sssm_ref, nsssm_ref, so_s)

    ri = lax.broadcasted_iota(jnp.int32, (CHUNK, LANES), 0)
    li = lax.broadcasted_iota(jnp.int32, (CHUNK, LANES), 1)
    lj = li % CHUNK
    lo_half = li < CHUNK
    strict = ri > lj
    causal = ri >= lj
    eye2 = (ri == lj).astype(F32)
    lo_half_hd = lax.broadcasted_iota(jnp.int32, (HEAD_DIM, LANES), 1) < CHUNK

    def pair_mul(xp, yp):
        ybd = jnp.concatenate([jnp.where(lo_half, yp, 0.0), jnp.where(lo_half, 0.0, yp)], axis=0)
        return _dot(xp, ybd)

    chains = [(p, hh) for p in range(nb // 2) for hh in range(N_HEADS)]
    for g0 in range(0, len(chains), GDN_GROUP):
        st = []
        for p, hh in chains[g0:g0 + GDN_GROUP]:
            ra = slice(2 * p * CHUNK, (2 * p + 1) * CHUNK)
            rb = slice((2 * p + 1) * CHUNK, (2 * p + 2) * CHUNK)
            sl = slice(hh * HEAD_DIM, (hh + 1) * HEAD_DIM)
            kba, kbb = k_s[ra, sl] * beta_s[hh, ra, :], k_s[rb, sl] * beta_s[hh, rb, :]
            lhs1 = jnp.concatenate([jnp.concatenate([kba, kbb], axis=1),
                                    jnp.concatenate([q_s[ra, sl], q_s[rb, sl]], axis=1)], axis=0)
            kt = kt_s[hh, :, 2 * p * CHUNK:(2 * p + 2) * CHUNK]
            ktbd = jnp.concatenate([jnp.where(lo_half_hd, kt, 0.0), jnp.where(lo_half_hd, 0.0, kt)], axis=0)
            kkqk = _dot(lhs1, ktbd)
            st.append(dict(ra=ra, rb=rb, sl=sl, hh=hh, p=p, kkqk=kkqk))
        for c in st:
            hh, p = c["hh"], c["p"]
            gcol_p = jnp.where(lo_half, gc_s[hh, c["ra"], :], gc_s[hh, c["rb"], :])
            grow_p = jnp.broadcast_to(gcrow_s[N_HEADS + hh:N_HEADS + hh + 1, 2 * p * CHUNK:(2 * p + 2) * CHUNK],
                                      (CHUNK, LANES))
            dec = jnp.exp(jnp.minimum(gcol_p - grow_p, 0.0))
            kkqk = c.pop("kkqk")
            c["npow"] = jnp.where(strict, kkqk[0:CHUNK] * dec, 0.0)
            c["qkm"] = jnp.where(causal, kkqk[CHUNK:2 * CHUNK] * dec, 0.0)
            c["t"] = eye2 - c["npow"]
        for c in st:
            c["npow"] = pair_mul(c["npow"], c["npow"])
        for _ in range(4):
            for c in st:
                both = pair_mul(jnp.concatenate([c["t"], c["npow"]], axis=0), c["npow"])
                c["t"] = c["t"] + both[0:CHUNK]
                c["npow"] = both[CHUNK:2 * CHUNK]
        for c in st:
            c["t"] = c["t"] + pair_mul(c["t"], c["npow"])
        for c in st:
            ra, rb, sl, hh = c["ra"], c["rb"], c["sl"], c["hh"]
            ba, bb = beta_s[hh, ra, :], beta_s[hh, rb, :]
            kba, kbb = k_s[ra, sl] * ba, k_s[rb, sl] * bb
            rhs = jnp.concatenate([_block_diag(v_s[ra, sl] * ba, v_s[rb, sl] * bb),
                                   _block_diag(kba * eg_s[hh, ra, :], kbb * eg_s[hh, rb, :])], axis=1)
            c["uw"] = _dot(c.pop("t"), rhs)
        for c in st:
            ra, rb, sl, hh, p = c["ra"], c["rb"], c["sl"], c["hh"], c["p"]
            c["ia"], c["ib"] = 2 * p * N_HEADS + hh, (2 * p + 1) * N_HEADS + hh
            lhs3 = jnp.concatenate([c["uw"][:, 2 * HEAD_DIM:4 * HEAD_DIM],
                                    jnp.concatenate([q_s[ra, sl] * eg_s[hh, ra, :], q_s[rb, sl] * eg_s[hh, rb, :]],
                                                    axis=1)], axis=0)
            c["r3"] = _dot(lhs3, _block_diag(s_ref[c["ia"]], s_ref[c["ib"]]))
        for c in st:
            ra, rb, sl, hh, p = c["ra"], c["rb"], c["sl"], c["hh"], c["p"]
            r3 = c.pop("r3")
            vn = c.pop("uw")[:, 0:2 * HEAD_DIM] - r3[0:CHUNK]
            gla = gc_s[hh, (2 * p + 1) * CHUNK - 1:(2 * p + 1) * CHUNK, :]
            glb = gc_s[hh, (2 * p + 2) * CHUNK - 1:(2 * p + 2) * CHUNK, :]
            pair_lanes = slice(2 * p * CHUNK, (2 * p + 2) * CHUNK)
            to_end = jnp.exp(jnp.where(lo_half[0:1], gla, glb) - gcrow_s[N_HEADS + hh:N_HEADS + hh + 1, pair_lanes])
            kgt = kt_s[hh, :, pair_lanes] * to_end
            both = _dot(jnp.concatenate([c.pop("qkm"), kgt], axis=0),
                        _block_diag(vn[:, 0:HEAD_DIM], vn[:, HEAD_DIM:2 * HEAD_DIM]))
            o = r3[CHUNK:2 * CHUNK] + both[0:CHUNK]
            o_s[ra, sl] = o[:, 0:HEAD_DIM]
            o_s[rb, sl] = o[:, HEAD_DIM:2 * HEAD_DIM]
            upd = both[CHUNK:CHUNK + HEAD_DIM]
            s_ref[c["ia"]] = s_ref[c["ia"]] * jnp.exp(gla) + upd[:, 0:HEAD_DIM]
            s_ref[c["ib"]] = s_ref[c["ib"]] * jnp.exp(glb) + upd[:, HEAD_DIM:2 * HEAD_DIM]

    zb = _proj(hb, win_ref, 2 * D_A + D_QKV, D_MAIN)
    yb = _head_rms(o_s[...], onw) * _silu(zb)
    ymix = jnp.concatenate([ya, yb], axis=1)
    yo = jnp.dot(ymix.astype(BF16), wout_ref[...], preferred_element_type=F32)
    yn = yo * lax.rsqrt(jnp.mean(yo * yo, axis=-1, keepdims=True) + EPS)
    y_ref[...] = x + yn.reshape(nb, CHUNK, D_MODEL) * (gate * postw[None])

    qkv_ext[:, :, 0:SUBLANES, :] = qkv_ext[:, :, CHUNK:CHUNK + SUBLANES, :]
    ua_ext[:, 0:CHUNK, :] = ua3

    @pl.when(step == pl.num_programs(0) - 1)
    def _():
        qkv3 = qkv.reshape(nb, CHUNK, D_QKV)
        for b in range(nb):
            npool_ref[:, b, :] = ua3[b, CHUNK - POOL_BUF:CHUNK, :]
            nconv_ref[:, b, :] = qkv3[b, CHUNK - (CONV_K - 1):CHUNK, :]
        nssm_ref[...] = s_ref[...]
        yb_s = _head_rms(so_s[...], onw) * _silu(szb_ref[...])
        ymix_s = jnp.concatenate([sya_ref[...], yb_s], axis=1)
        yo_s = jnp.dot(ymix_s.astype(BF16), wout_ref[...], preferred_element_type=F32)
        yn_s = yo_s * lax.rsqrt(jnp.mean(yo_s * yo_s, axis=-1, keepdims=True) + EPS) * postw
        ys_ref[:, 0, :] = xs_ref[:, 0, :] + sgate_ref[...] * yn_s


def _layer_spec(shape, l):
    return pl.BlockSpec((None,) + tuple(shape), lambda s, _n=len(shape): (l,) + (0,) * _n)


def _prompt_layer(l, x, mod_p, lw, consts, dec, prev):
    nb, seq, _ = x.shape
    depth = lw["win"].shape[0]
    rows = nb * CHUNK
    nstep = seq // CHUNK
    nseq = dec["x"].shape[0]
    blk_states = nseq * N_HEADS // nstep
    assert blk_states % N_HEADS == 0 and SUBLANES % (blk_states // N_HEADS) == 0
    full = lambda shape: pl.BlockSpec(shape, lambda s, _n=len(shape): (0,) * _n)
    sssm_spec = pl.BlockSpec((None, blk_states, HEAD_DIM, HEAD_DIM), lambda s: (l, s, 0, 0))
    dec_specs = [
        full((nseq, 1, D_MODEL)), pl.BlockSpec((None, nseq, D_MODEL), lambda s: (l, 0, 2)),
        full((nseq, D_B)), full((nseq, D_B)), full((nseq, D_B)),
        full((N_HEADS, nseq, LANES)), full((N_HEADS, nseq, LANES)), full((N_HEADS, nseq, LANES)),
        full((nseq, D_A)), full((nseq, D_B)), sssm_spec,
    ]
    smem = pl.BlockSpec(memory_space=pltpu.SMEM)
    in_specs = [
        smem, smem,
        pl.BlockSpec((nb, CHUNK, D_MODEL), lambda s: (0, s, 0)),
        _layer_spec((nb, 3 * D_MODEL), l),
        full((depth, D_MODEL)), full((depth, D_MODEL)),
        pl.BlockSpec((None, D_MAIN, D_MODEL), lambda s: (l, 0, 0)),
        pl.BlockSpec((None, SUBLANES, D_MODEL), lambda s: (l, D_MAIN // SUBLANES, 0)),
        _layer_spec((CONV_K, D_QKV), l), _layer_spec((N_POOL, POOL_GC, POOL_GC), l),
        full((depth, D_A)), full((depth, HEAD_DIM)), _layer_spec((D_MODEL, D_MODEL), l),
        full((N_POOL, CHUNK, 2 * CHUNK)), full((CHUNK, 2 * CHUNK)),
    ] + dec_specs + [pl.BlockSpec(memory_space=pl.ANY)] * len(prev)
    out_specs = [
        pl.BlockSpec((nb, CHUNK, D_MODEL), lambda s: (0, s, 0)),
        _layer_spec((POOL_BUF, nb, D_A), l), _layer_spec((CONV_K - 1, nb, D_QKV), l),
        _layer_spec((nb * N_HEADS, HEAD_DIM, HEAD_DIM), l),
        sssm_spec, full((nseq, 1, D_MODEL)),
    ]
    out_shape = [
        jax.ShapeDtypeStruct((nb, seq, D_MODEL), F32),
        jax.ShapeDtypeStruct((depth, POOL_BUF, nb, D_A), F32),
        jax.ShapeDtypeStruct((depth, CONV_K - 1, nb, D_QKV), F32),
        jax.ShapeDtypeStruct((depth, nb * N_HEADS, HEAD_DIM, HEAD_DIM), F32),
        jax.ShapeDtypeStruct((depth, nseq * N_HEADS, HEAD_DIM, HEAD_DIM), F32),
        jax.ShapeDtypeStruct((nseq, 1, D_MODEL), F32),
    ]
    scratch = [
        pltpu.VMEM((QKV_TILES, nb, CHUNK + SUBLANES, LANES), F32),
        pltpu.VMEM((nb, 2 * CHUNK, D_A), F32),
        pltpu.VMEM((nb * N_HEADS, HEAD_DIM, HEAD_DIM), F32),
        pltpu.VMEM((rows, D_B), F32), pltpu.VMEM((rows, D_B), F32), pltpu.VMEM((N_HEADS, HEAD_DIM, rows), F32),
        pltpu.VMEM((rows, D_B), F32), pltpu.VMEM((rows, D_B), F32),
        pltpu.VMEM((N_HEADS, rows, LANES), F32), pltpu.VMEM((N_HEADS, rows, LANES), F32),
        pltpu.VMEM((N_HEADS, rows, LANES), F32),
        pltpu.VMEM((SUBLANES, rows), F32),
        pltpu.VMEM((nseq, D_B), F32),
    ]
    n_in = len(in_specs) - len(prev)
    return pl.pallas_call(
        functools.partial(_prompt_kernel, n_alias=len(prev), layer=l),
        grid=(nstep,),
        in_specs=in_specs,
        out_specs=out_specs,
        out_shape=out_shape,
        scratch_shapes=scratch,
        input_output_aliases={n_in + i: 1 + i for i in range(len(prev))},
        compiler_params=pltpu.CompilerParams(dimension_semantics=("arbitrary",), vmem_limit_bytes=VMEM_LIMIT),
        name="prompt_layer",
    )(lw["alog"], lw["dtb"], x, mod_p, lw["prew"], lw["postw"], lw["win"], lw["win"], lw["convw"], lw["poolw"],
      lw["pscale"], lw["onw"], lw["wout"], consts["band"], consts["ltri"],
      dec["x"], dec["mod"], dec["q"], dec["k"], dec["v"], dec["beta"], dec["eg"], dec["qk"], dec["ya"], dec["zb"],
      dec["ssm"], *prev)


def _decode_front_kernel(alog_ref, dtb_ref, x_ref, mod_ref, spool_ref, sconv_ref, prew_ref, win_ref, wba_ref,
                         convw_ref, poolw_ref, pscale_ref, *rest, n_alias, layer):
    rest = rest[n_alias:]
    npool_ref, nconv_ref, q_ref, k_ref, v_ref, beta_ref, eg_ref, qk_ref, ya_ref, zb_ref = rest
    nseq = x_ref.shape[0]
    x = x_ref[:, 0, :]
    mod = mod_ref[...]
    shift = mod[:, 0:D_MODEL]
    scale = mod[:, D_MODEL:2 * D_MODEL]
    a_mul = prew_ref[layer:layer + 1, :] * (1.0 + scale)
    ms = jnp.mean(x * x, axis=-1, keepdims=True)
    hb = (x * lax.rsqrt(ms + EPS) * a_mul + shift).astype(BF16)

    ua = _proj(hb, win_ref, 0, D_A)
    ya_parts = []
    for gi, w in enumerate(POOL_WINDOWS):
        gs = slice(gi * POOL_GC, (gi + 1) * POOL_GC)
        win = ua[:, gs]
        for d in range(1, w):
            win = win + spool_ref[POOL_BUF - d, :, gs]
        cnt = float(min(PAST_LEN + 1, w))
        pooled = win / cnt - ua[:, gs]
        ya_parts.append(_dot(pooled, poolw_ref[gi]))
    za = _proj(hb, win_ref, D_A, 2 * D_A)
    ya_ref[...] = jnp.concatenate(ya_parts, axis=1) * pscale_ref[layer:layer + 1, :] * _silu(za)
    npool_ref[0:POOL_BUF - 1] = spool_ref[1:POOL_BUF]
    npool_ref[POOL_BUF - 1] = ua

    qkv = _proj(hb, win_ref, 2 * D_A, 2 * D_A + D_QKV)
    cw = convw_ref[...]
    acc = qkv * cw[CONV_K - 1:CONV_K]
    for j in range(CONV_K - 1):
        acc = acc + sconv_ref[j] * cw[j:j + 1]
    nconv_ref[0:CONV_K - 2] = sconv_ref[1:CONV_K - 1]
    nconv_ref[CONV_K - 2] = qkv
    qkvc = _silu(acc)
    bac = _gate_logits(hb, wba_ref)
    sig = jax.nn.sigmoid(bac)
    eg = jnp.exp(-jnp.exp(_head_scalars(alog_ref, layer, (1, LANES), 1))
                 * _softplus(bac + _head_scalars(dtb_ref, layer, (1, LANES), 1)))
    for hh in range(N_HEADS):
        sl = slice(hh * HEAD_DIM, (hh + 1) * HEAD_DIM)
        qh = qkvc[:, hh * HEAD_DIM:(hh + 1) * HEAD_DIM]
        kh = qkvc[:, D_B + hh * HEAD_DIM:D_B + (hh + 1) * HEAD_DIM]
        qn = qh * lax.rsqrt(jnp.sum(qh * qh, axis=-1, keepdims=True) + EPS) * (HEAD_DIM ** -0.5)
        kn = kh * lax.rsqrt(jnp.sum(kh * kh, axis=-1, keepdims=True) + EPS)
        q_ref[:, sl] = qn
        k_ref[:, sl] = kn
        qk_ref[hh] = jnp.broadcast_to(jnp.sum(qn * kn, axis=-1, keepdims=True), (nseq, LANES))
        beta_ref[hh] = jnp.broadcast_to(sig[:, hh:hh + 1], (nseq, LANES))
        eg_ref[hh] = jnp.broadcast_to(eg[:, N_HEADS + hh:N_HEADS + hh + 1], (nseq, LANES))
    v_ref[...] = qkvc[:, 2 * D_B:3 * D_B]
    zb_ref[...] = _proj(hb, win_ref, 2 * D_A + D_QKV, D_MAIN)


def _decode_state_step(step, q_ref, k_ref, v_ref, beta_ref, eg_ref, qk_ref, ssm_ref, nssm_ref, o_s):
    nrow = ssm_ref.shape[0] // N_HEADS
    per_tile = SUBLANES // nrow
    tile0 = pl.multiple_of((step // per_tile) * SUBLANES, SUBLANES)
    sub = step % per_tile

    def my_rows(tile):
        out = tile[0:nrow]
        for j in range(1, per_tile):
            out = jnp.where(sub == j, tile[j * nrow:(j + 1) * nrow], out)
        return out

    kblk = my_rows(k_ref[pl.ds(tile0, SUBLANES), :])
    qblk = my_rows(q_ref[pl.ds(tile0, SUBLANES), :])
    vblk = my_rows(v_ref[pl.ds(tile0, SUBLANES), :])
    beta = [my_rows(beta_ref[hh, pl.ds(tile0, SUBLANES), :]) for hh in range(N_HEADS)]
    egs = [my_rows(eg_ref[hh, pl.ds(tile0, SUBLANES), :]) for hh in range(N_HEADS)]
    qks = [my_rows(qk_ref[hh, pl.ds(tile0, SUBLANES), :]) for hh in range(N_HEADS)]
    row8 = lax.broadcasted_iota(jnp.int32, (SUBLANES, HEAD_DIM), 0)
    st = []
    for i in range(nrow):
        for hh in range(N_HEADS):
            sl = slice(hh * HEAD_DIM, (hh + 1) * HEAD_DIM)
            krow, qrow = kblk[i:i + 1, sl], qblk[i:i + 1, sl]
            lhs = jnp.where(row8 == 0, krow, jnp.where(row8 == 1, qrow, 0.0))
            r = jnp.dot(lhs, ssm_ref[i * N_HEADS + hh], preferred_element_type=F32)
            st.append(dict(i=i, hh=hh, sl=sl, krow=krow, r=r))
    for c in st:
        i, hh = c["i"], c["hh"]
        eg = egs[hh][i:i + 1]
        r = c.pop("r")
        delta = (vblk[i:i + 1, c["sl"]] - eg * r[0:1]) * beta[hh][i:i + 1]
        c["o"] = eg * r[1:2] + qks[hh][i:i + 1] * delta
        krow = c.pop("krow")
        k_hi = krow.astype(BF16).astype(F32)
        d_hi = delta.astype(BF16).astype(F32)
        kp = jnp.where(row8 == 0, k_hi, jnp.where(row8 == 1, krow - k_hi, jnp.where(row8 == 2, k_hi, 0.0)))
        dp = jnp.where(row8 == 0, d_hi, jnp.where(row8 == 1, d_hi, jnp.where(row8 == 2, delta - d_hi, 0.0)))
        c["upd"] = _dot_tn(kp.astype(BF16), dp.astype(BF16))
    for c in st:
        idx = c["i"] * N_HEADS + c["hh"]
        nssm_ref[idx] = ssm_ref[idx] * egs[c["hh"]][c["i"]:c["i"] + 1] + c.pop("upd")
    o_rows = [jnp.concatenate([c["o"] for c in st[i * N_HEADS:(i + 1) * N_HEADS]], axis=1) for i in range(nrow)]
    tile = o_s[pl.ds(tile0, SUBLANES), :]
    placed = jnp.concatenate(o_rows * per_tile, axis=0)
    row_group = lax.broadcasted_iota(jnp.int32, tile.shape, 0) // nrow
    o_s[pl.ds(tile0, SUBLANES), :] = jnp.where(row_group == sub, placed, tile)


def _decode_front(l, x, mod_s, state_pool, state_conv, lw, prev):
    nseq = x.shape[0]
    depth = lw["win"].shape[0]
    full = lambda shape: pl.BlockSpec(shape, lambda s, _n=len(shape): (0,) * _n)
    smem = pl.BlockSpec(memory_space=pltpu.SMEM)
    in_specs = [
        smem, smem,
        full((nseq, 1, D_MODEL)), _layer_spec((nseq, 3 * D_MODEL), l),
        _layer_spec((POOL_BUF, nseq, D_A), l), _layer_spec((CONV_K - 1, nseq, D_QKV), l),
        full((depth, D_MODEL)),
        pl.BlockSpec((None, D_MAIN, D_MODEL), lambda s: (l, 0, 0)),
        pl.BlockSpec((None, SUBLANES, D_MODEL), lambda s: (l, D_MAIN // SUBLANES, 0)),
        _layer_spec((CONV_K, D_QKV), l), _layer_spec((N_POOL, POOL_GC, POOL_GC), l), full((depth, D_A)),
    ] + [pl.BlockSpec(memory_space=pl.ANY)] * len(prev)
    row_out = lambda width: (full((nseq, width)), jax.ShapeDtypeStruct((nseq, width), F32))
    head_out = (full((N_HEADS, nseq, LANES)), jax.ShapeDtypeStruct((N_HEADS, nseq, LANES), F32))
    outs = [
        (_layer_spec((POOL_BUF, nseq, D_A), l), jax.ShapeDtypeStruct((depth, POOL_BUF, nseq, D_A), F32)),
        (_layer_spec((CONV_K - 1, nseq, D_QKV), l), jax.ShapeDtypeStruct((depth, CONV_K - 1, nseq, D_QKV), F32)),
        row_out(D_B), row_out(D_B), row_out(D_B), head_out, head_out, head_out, row_out(D_A), row_out(D_B),
    ]
    n_in = len(in_specs) - len(prev)
    npool, nconv, q, k, v, beta, eg, qk, ya, zb = pl.pallas_call(
        functools.partial(_decode_front_kernel, n_alias=len(prev), layer=l),
        grid=(1,),
        in_specs=in_specs,
        out_specs=[o[0] for o in outs],
        out_shape=[o[1] for o in outs],
        input_output_aliases={n_in + i: i for i in range(len(prev))},
        compiler_params=pltpu.CompilerParams(dimension_semantics=("arbitrary",), vmem_limit_bytes=VMEM_LIMIT),
        name="decode_front",
    )(lw["alog"], lw["dtb"], x, mod_s, state_pool, state_conv, lw["prew"], lw["win"], lw["win"], lw["convw"],
      lw["poolw"], lw["pscale"], *prev)
    return (npool, nconv), dict(x=x, mod=mod_s, q=q, k=k, v=v, beta=beta, eg=eg, qk=qk, ya=ya, zb=zb)


def _constants():
    t = np.arange(CHUNK)[:, None]
    j = np.arange(2 * CHUNK)[None, :]
    band = np.stack([((j <= CHUNK + t) & (j > CHUNK + t - w)) for w in POOL_WINDOWS]).astype(np.float32)
    tri = (np.arange(CHUNK)[None, :] <= t).astype(np.float32)
    ltri = np.concatenate([tri, tri], axis=1)
    return {"band": jnp.asarray(band, BF16), "ltri": jnp.asarray(ltri, BF16)}


def _stacked_weights(pre_norm_w, post_norm_w, w_in, conv_w, pool_w, pool_scale, a_log, dt_bias, o_norm_w, w_out):
    w_in_t = jnp.swapaxes(w_in, 1, 2)
    return {
        "prew": pre_norm_w, "postw": post_norm_w, "win": w_in_t.astype(BF16), "convw": conv_w, "poolw": pool_w,
        "pscale": pool_scale, "alog": a_log, "dtb": dt_bias, "onw": o_norm_w, "wout": w_out.astype(BF16),
    }


def kernel(x_prompt, x_sample, c_prompt, c_sample, state_pool, state_conv, state_ssm, w_ada, b_ada, pre_norm_w,
           post_norm_w, w_in, conv_w, pool_w, pool_scale, a_log, dt_bias, o_norm_w, w_out):
    depth = w_in.shape[0]
    nb, seq, _ = x_prompt.shape
    nseq, dec_seq, _ = x_sample.shape
    assert dec_seq == 1 and seq % CHUNK == 0 and nb % 2 == 0 and nseq % (seq // CHUNK) == 0
    consts = _constants()
    lw = _stacked_weights(pre_norm_w, post_norm_w, w_in, conv_w, pool_w, pool_scale, a_log, dt_bias, o_norm_w, w_out)
    mod_p, mod_s = _mod_call(c_prompt, c_sample, w_ada, b_ada)
    ssm_in = state_ssm.reshape(depth, nseq * N_HEADS, HEAD_DIM, HEAD_DIM)
    pool_in = jnp.swapaxes(state_pool, 1, 2)
    conv_in = jnp.swapaxes(state_conv, 1, 2)
    yp, ys = x_prompt, x_sample
    prev_p, prev_s = (), ()
    for l in range(depth):
        prev_s, dec = _decode_front(l, ys, mod_s, pool_in, conv_in, lw, tuple(prev_s))
        yp, *prev_p, ys = _prompt_layer(l, yp, mod_p, lw, consts, dict(dec, ssm=ssm_in), tuple(prev_p))
    npool_p, nconv_p, nssm_p, nssm_s = prev_p
    npool_s, nconv_s = prev_s
    return (yp, ys, jnp.swapaxes(npool_p, 1, 2), jnp.swapaxes(nconv_p, 1, 2),
            nssm_p.reshape(depth, nb, N_HEADS, HEAD_DIM, HEAD_DIM),
            jnp.swapaxes(npool_s, 1, 2), jnp.swapaxes(nconv_s, 1, 2),
            nssm_s.reshape(depth, nseq, N_HEADS, HEAD_DIM, HEAD_DIM))
```

```python
import functools

import jax
import jax.numpy as jnp
import numpy as np
from jax import lax
from jax.experimental import pallas as pl
from jax.experimental.pallas import tpu as pltpu

F32 = jnp.float32
BF16 = jnp.bfloat16

D_MODEL = 1024
D_A = 512
D_B = 512
N_POOL = 4
POOL_WINDOWS = (2, 4, 8, 16)
POOL_GC = 128
POOL_BUF = 15
HEAD_DIM = 128
N_HEADS = 4
D_QKV = 3 * D_B
CONV_K = 4
D_MAIN = 2 * D_A + D_QKV + D_B
PAST_LEN = 16384
EPS = 1e-6
CHUNK = 64
GDN_GROUP = 16
LANES = 128
SUBLANES = 8
QKV_TILES = D_QKV // LANES
MOD_SPLIT = 2
VMEM_LIMIT = 56 * 1024 * 1024


def _dot(a, b):
    return jnp.dot(a.astype(BF16), b.astype(BF16), preferred_element_type=F32)


def _dot_tn(a, b):
    return lax.dot_general(a, b, (((0,), (0,)), ((), ())), preferred_element_type=F32)


def _proj(hb, wt_ref, c0, c1):
    return lax.dot_general(hb, wt_ref[c0:c1, :], (((1,), (1,)), ((), ())), preferred_element_type=F32)


def _gate_logits(hb, wba_ref):
    w = jnp.concatenate([wba_ref[...], jnp.zeros((LANES - SUBLANES, wba_ref.shape[1]), wba_ref.dtype)], axis=0)
    return lax.dot_general(hb, w, (((1,), (1,)), ((), ())), preferred_element_type=F32)


def _silu(x):
    half = 0.5 * x
    return half + half * jnp.tanh(half)


def _softplus(x):
    return jnp.maximum(x, 0.0) + jnp.log1p(jnp.exp(-jnp.abs(x)))


def _block_diag(a, b):
    top = jnp.concatenate([a, jnp.zeros((a.shape[0], b.shape[1]), a.dtype)], axis=1)
    bot = jnp.concatenate([jnp.zeros((b.shape[0], a.shape[1]), b.dtype), b], axis=1)
    return jnp.concatenate([top, bot], axis=0)


def _head_rms(o_all, w):
    parts = []
    for hh in range(N_HEADS):
        oh = o_all[:, hh * HEAD_DIM:(hh + 1) * HEAD_DIM]
        parts.append(oh * lax.rsqrt(jnp.mean(oh * oh, axis=-1, keepdims=True) + EPS) * w)
    return jnp.concatenate(parts, axis=1)


def _mod_kernel(cp_ref, cs_ref, w_ref, b_ref, op_ref, os_ref):
    @pl.when(pl.program_id(1) == 0)
    def _():
        b = b_ref[pl.ds(pl.program_id(0), 1), :]
        op_ref[...] = jnp.broadcast_to(b, op_ref.shape)
        os_ref[...] = jnp.broadcast_to(b, os_ref.shape)

    nb = cp_ref.shape[0]
    c = jnp.concatenate([cp_ref[...], cs_ref[...]], axis=0)
    r = jnp.dot(_silu(c), w_ref[...], preferred_element_type=F32)
    op_ref[...] += r[0:nb]
    os_ref[...] += r[nb:]


def _mod_call(c_prompt, c_sample, w_ada, b_ada):
    depth, _, n3 = w_ada.shape
    nb, nseq = c_prompt.shape[0], c_sample.shape[0]
    kb = D_MODEL // MOD_SPLIT
    return pl.pallas_call(
        _mod_kernel,
        grid=(depth, MOD_SPLIT),
        in_specs=[
            pl.BlockSpec((nb, kb), lambda l, k: (0, k)),
            pl.BlockSpec((nseq, kb), lambda l, k: (0, k)),
            pl.BlockSpec((None, kb, n3), lambda l, k: (l, k, 0)),
            pl.BlockSpec((depth, n3), lambda l, k: (0, 0)),
        ],
        out_specs=[pl.BlockSpec((None, nb, n3), lambda l, k: (l, 0, 0)),
                   pl.BlockSpec((None, nseq, n3), lambda l, k: (l, 0, 0))],
        out_shape=[jax.ShapeDtypeStruct((depth, nb, n3), F32), jax.ShapeDtypeStruct((depth, nseq, n3), F32)],
        compiler_params=pltpu.CompilerParams(dimension_semantics=("arbitrary", "arbitrary")),
        name="adaln_mod",
    )(c_prompt, c_sample, w_ada, b_ada)


def _head_scalars(ref, layer, shape, axis):
    pos = lax.broadcasted_iota(jnp.int32, shape, axis)
    out = jnp.zeros(shape, F32)
    for hh in range(N_HEADS):
        out = jnp.where(pos == N_HEADS + hh, ref[layer, hh], out)
    return out


def _prompt_kernel(alog_ref, dtb_ref, x_ref, mod_ref, prew_ref, postw_ref, win_ref, wba_ref, convw_ref, poolw_ref,
                   pscale_ref, onw_ref, wout_ref, band_ref, ltri_ref,
                   xs_ref, sgate_ref, sq_ref, sk_ref, sv_ref, sbeta_ref, seg_ref, sqk_ref, sya_ref, szb_ref, sssm_ref,
                   *rest, n_alias, layer):
    rest = rest[n_alias:]
    y_ref, npool_ref, nconv_ref, nssm_ref, nsssm_ref, ys_ref = rest[0:6]
    qkv_ext, ua_ext, s_ref, q_s, k_s, kt_s, v_s, o_s, beta_s, gc_s, eg_s, gcrow_s, so_s, wout_s = rest[6:]
    step = pl.program_id(0)
    nb = x_ref.shape[0]
    rows = nb * CHUNK

    @pl.when(step == 0)
    def _():
        qkv_ext[:, :, 0:SUBLANES, :] = jnp.zeros((QKV_TILES, nb, SUBLANES, LANES), F32)
        ua_ext[:, 0:CHUNK, :] = jnp.zeros((nb, CHUNK, D_A), F32)
        s_ref[...] = jnp.zeros(s_ref.shape, F32)
        so_s[...] = jnp.zeros(so_s.shape, F32)
        wout_s[...] = wout_ref[...].astype(BF16)

    x = x_ref[...]
    mod = mod_ref[...][:, None, :]
    shift = mod[:, :, 0:D_MODEL]
    scale = mod[:, :, D_MODEL:2 * D_MODEL]
    gate = mod[:, :, 2 * D_MODEL:3 * D_MODEL]
    prew, postw = prew_ref[layer:layer + 1, :], postw_ref[layer:layer + 1, :]
    pscale, onw = pscale_ref[layer:layer + 1, :], onw_ref[layer:layer + 1, :]
    a_mul = prew[None] * (1.0 + scale)
    ms = jnp.mean(x * x, axis=-1, keepdims=True)
    h = x * lax.rsqrt(ms + EPS) * a_mul + shift
    hb = h.reshape(rows, D_MODEL).astype(BF16)

    ua = _proj(hb, win_ref, 0, D_A)
    ua3 = ua.reshape(nb, CHUNK, D_A)
    ua_ext[:, CHUNK:2 * CHUNK, :] = ua3
    pos = lax.broadcasted_iota(jnp.int32, (CHUNK, POOL_GC), 0) + step * CHUNK
    pooled_groups = []
    for gi, w in enumerate(POOL_WINDOWS):
        cnt = jnp.minimum(pos + 1, w).astype(F32)
        gsl = slice(gi * POOL_GC, (gi + 1) * POOL_GC)
        per_b = []
        for b in range(0, nb, 2):
            ext = jnp.concatenate([ua_ext[b, :, gsl], ua_ext[b + 1, :, gsl]], axis=1)
            win = jnp.dot(band_ref[gi], ext.astype(BF16), preferred_element_type=F32)
            per_b.append(win[:, 0:POOL_GC] / cnt - ua3[b, :, gsl])
            per_b.append(win[:, POOL_GC:2 * POOL_GC] / cnt - ua3[b + 1, :, gsl])
        pooled_groups.append(jnp.concatenate(per_b, axis=0))
    ya = jnp.concatenate([_dot(jnp.concatenate(pooled_groups[2 * g2:2 * g2 + 2], axis=1),
                               _block_diag(poolw_ref[2 * g2], poolw_ref[2 * g2 + 1]))
                          for g2 in range(N_POOL // 2)], axis=1)
    za = _proj(hb, win_ref, D_A, 2 * D_A)
    ya = ya * pscale * _silu(za)

    qkv = _proj(hb, win_ref, 2 * D_A, 2 * D_A + D_QKV)
    qkv_t = jnp.stack([qkv[:, c * LANES:(c + 1) * LANES] for c in range(QKV_TILES)], axis=0)
    qkv_t = qkv_t.reshape(QKV_TILES, nb, CHUNK, LANES)
    qkv_ext[:, :, SUBLANES:SUBLANES + CHUNK, :] = qkv_t
    cw = convw_ref[...]
    cw_t = [jnp.stack([cw[j:j + 1, c * LANES:(c + 1) * LANES] for c in range(QKV_TILES)], axis=0)[:, None]
            for j in range(CONV_K)]
    acc = qkv_t * cw_t[CONV_K - 1]
    for j in range(CONV_K - 1):
        acc = acc + qkv_ext[:, :, pl.ds(SUBLANES - (CONV_K - 1) + j, CHUNK), :] * cw_t[j]
    act = _silu(acc).reshape(QKV_TILES, rows, LANES)
    for hh in range(N_HEADS):
        sl = slice(hh * HEAD_DIM, (hh + 1) * HEAD_DIM)
        qh, kh = act[hh], act[N_HEADS + hh]
        q_s[:, sl] = qh * lax.rsqrt(jnp.sum(qh * qh, axis=-1, keepdims=True) + EPS) * (HEAD_DIM ** -0.5)
        kn = kh * lax.rsqrt(jnp.sum(kh * kh, axis=-1, keepdims=True) + EPS)
        k_s[:, sl] = kn
        kt_s[hh] = kn.T
        v_s[:, sl] = act[2 * N_HEADS + hh]

    bac = _gate_logits(hb, wba_ref)
    bar = bac.T[0:SUBLANES]
    sig = jax.nn.sigmoid(bac)
    gcol = (-jnp.exp(_head_scalars(alog_ref, layer, (1, LANES), 1))
            * _softplus(bac + _head_scalars(dtb_ref, layer, (1, LANES), 1)))
    g_hi = gcol.astype(BF16)
    g_lo = (gcol - g_hi.astype(F32)).astype(BF16)
    gc_parts = []
    for b in range(0, nb, 2):
        ra, rb = slice(b * CHUNK, (b + 1) * CHUNK), slice((b + 1) * CHUNK, (b + 2) * CHUNK)
        hi_lo = jnp.concatenate([jnp.concatenate([g_hi[ra], g_lo[ra]], axis=0),
                                 jnp.concatenate([g_hi[rb], g_lo[rb]], axis=0)], axis=1)
        both = jnp.dot(ltri_ref[...], hi_lo, preferred_element_type=F32)
        gc_parts += [both[:, 0:LANES], both[:, LANES:2 * LANES]]
    gc = jnp.concatenate(gc_parts, axis=0)
    for hh in range(N_HEADS):
        beta_s[hh] = jnp.broadcast_to(sig[:, hh:hh + 1], (rows, LANES))
        gcb = jnp.broadcast_to(gc[:, N_HEADS + hh:N_HEADS + hh + 1], (rows, LANES))
        gc_s[hh] = gcb
        eg_s[hh] = jnp.exp(gcb)
    grow = (-jnp.exp(_head_scalars(alog_ref, layer, (SUBLANES, rows), 0))
            * _softplus(bar + _head_scalars(dtb_ref, layer, (SUBLANES, rows), 0)))
    lane_in_chunk = lax.broadcasted_iota(jnp.int32, grow.shape, 1) % CHUNK
    sh = 1
    while sh < CHUNK:
        grow = grow + jnp.where(lane_in_chunk >= sh, pltpu.roll(grow, sh, 1), 0.0)
        sh *= 2
    gcrow_s[...] = grow

    _decode_state_step(step, sq_ref, sk_ref, sv_ref, sbeta_ref, seg_ref, sqk_ref, sssm_ref, nsssm_ref, so_s)

    ri = lax.broadcasted_iota(jnp.int32, (CHUNK, LANES), 0)
    li = lax.broadcasted_iota(jnp.int32, (CHUNK, LANES), 1)
    lj = li % CHUNK
    lo_half = li < CHUNK
    strict = ri > lj
    causal = ri >= lj
    eye2 = (ri == lj).astype(F32)
    lo_half_hd = lax.broadcasted_iota(jnp.int32, (HEAD_DIM, LANES), 1) < CHUNK

    def pair_mul(xp, yp):
        ybd = jnp.concatenate([jnp.where(lo_half, yp, 0.0), jnp.where(lo_half, 0.0, yp)], axis=0)
        return _dot(xp, ybd)

    chains = [(p, hh) for p in range(nb // 2) for hh in range(N_HEADS)]
    for g0 in range(0, len(chains), GDN_GROUP):
        st = []
        for p, hh in chains[g0:g0 + GDN_GROUP]:
            ra = slice(2 * p * CHUNK, (2 * p + 1) * CHUNK)
            rb = slice((2 * p + 1) * CHUNK, (2 * p + 2) * CHUNK)
            sl = slice(hh * HEAD_DIM, (hh + 1) * HEAD_DIM)
            kba, kbb = k_s[ra, sl] * beta_s[hh, ra, :], k_s[rb, sl] * beta_s[hh, rb, :]
            lhs1 = jnp.concatenate([jnp.concatenate([kba, kbb], axis=1),
                                    jnp.concatenate([q_s[ra, sl], q_s[rb, sl]], axis=1)], axis=0)
            kt = kt_s[hh, :, 2 * p * CHUNK:(2 * p + 2) * CHUNK]
            ktbd = jnp.concatenate([jnp.where(lo_half_hd, kt, 0.0), jnp.where(lo_half_hd, 0.0, kt)], axis=0)
            kkqk = _dot(lhs1, ktbd)
            st.append(dict(ra=ra, rb=rb, sl=sl, hh=hh, p=p, kkqk=kkqk))
        for c in st:
            hh, p = c["hh"], c["p"]
            gcol_p = jnp.where(lo_half, gc_s[hh, c["ra"], :], gc_s[hh, c["rb"], :])
            grow_p = jnp.broadcast_to(gcrow_s[N_HEADS + hh:N_HEADS + hh + 1, 2 * p * CHUNK:(2 * p + 2) * CHUNK],
                                      (CHUNK, LANES))
            dec = jnp.exp(jnp.minimum(gcol_p - grow_p, 0.0))
            kkqk = c.pop("kkqk")
            c["npow"] = jnp.where(strict, kkqk[0:CHUNK] * dec, 0.0)
            c["qkm"] = jnp.where(causal, kkqk[CHUNK:2 * CHUNK] * dec, 0.0)
            c["t"] = eye2 - c["npow"]
        for c in st:
            c["npow"] = pair_mul(c["npow"], c["npow"])
        for _ in range(4):
            for c in st:
                both = pair_mul(jnp.concatenate([c["t"], c["npow"]], axis=0), c["npow"])
                c["t"] = c["t"] + both[0:CHUNK]
                c["npow"] = both[CHUNK:2 * CHUNK]
        for c in st:
            c["t"] = c["t"] + pair_mul(c["t"], c["npow"])
        for c in st:
            ra, rb, sl, hh = c["ra"], c["rb"], c["sl"], c["hh"]
            ba, bb = beta_s[hh, ra, :], beta_s[hh, rb, :]
            kba, kbb = k_s[ra, sl] * ba, k_s[rb, sl] * bb
            rhs = jnp.concatenate([_block_diag(v_s[ra, sl] * ba, v_s[rb, sl] * bb),
                                   _block_diag(kba * eg_s[hh, ra, :], kbb * eg_s[hh, rb, :])], axis=1)
            c["uw"] = _dot(c.pop("t"), rhs)
        for c in st:
            ra, rb, sl, hh, p = c["ra"], c["rb"], c["sl"], c["hh"], c["p"]
            c["ia"], c["ib"] = 2 * p * N_HEADS + hh, (2 * p + 1) * N_HEADS + hh
            lhs3 = jnp.concatenate([c["uw"][:, 2 * HEAD_DIM:4 * HEAD_DIM],
                                    jnp.concatenate([q_s[ra, sl] * eg_s[hh, ra, :], q_s[rb, sl] * eg_s[hh, rb, :]],
                                                    axis=1)], axis=0)
            c["r3"] = _dot(lhs3, _block_diag(s_ref[c["ia"]], s_ref[c["ib"]]))
        for c in st:
            ra, rb, sl, hh, p = c["ra"], c["rb"], c["sl"], c["hh"], c["p"]
            r3 = c.pop("r3")
            vn = c.pop("uw")[:, 0:2 * HEAD_DIM] - r3[0:CHUNK]
            gla = gc_s[hh, (2 * p + 1) * CHUNK - 1:(2 * p + 1) * CHUNK, :]
            glb = gc_s[hh, (2 * p + 2) * CHUNK - 1:(2 * p + 2) * CHUNK, :]
            pair_lanes = slice(2 * p * CHUNK, (2 * p + 2) * CHUNK)
            to_end = jnp.exp(jnp.where(lo_half[0:1], gla, glb) - gcrow_s[N_HEADS + hh:N_HEADS + hh + 1, pair_lanes])
            kgt = kt_s[hh, :, pair_lanes] * to_end
            both = _dot(jnp.concatenate([c.pop("qkm"), kgt], axis=0),
                        _block_diag(vn[:, 0:HEAD_DIM], vn[:, HEAD_DIM:2 * HEAD_DIM]))
            o = r3[CHUNK:2 * CHUNK] + both[0:CHUNK]
            o_s[ra, sl] = o[:, 0:HEAD_DIM]
            o_s[rb, sl] = o[:, HEAD_DIM:2 * HEAD_DIM]
            upd = both[CHUNK:CHUNK + HEAD_DIM]
            s_ref[c["ia"]] = s_ref[c["ia"]] * jnp.exp(gla) + upd[:, 0:HEAD_DIM]
            s_ref[c["ib"]] = s_ref[c["ib"]] * jnp.exp(glb) + upd[:, HEAD_DIM:2 * HEAD_DIM]

    zb = _proj(hb, win_ref, 2 * D_A + D_QKV, D_MAIN)
    yb = _head_rms(o_s[...], onw) * _silu(zb)
    ymix = jnp.concatenate([ya, yb], axis=1)
    yo = jnp.dot(ymix.astype(BF16), wout_s[...], preferred_element_type=F32)
    yn = yo * lax.rsqrt(jnp.mean(yo * yo, axis=-1, keepdims=True) + EPS)
    y_ref[...] = x + yn.reshape(nb, CHUNK, D_MODEL) * (gate * postw[None])

    qkv_ext[:, :, 0:SUBLANES, :] = qkv_ext[:, :, CHUNK:CHUNK + SUBLANES, :]
    ua_ext[:, 0:CHUNK, :] = ua3

    @pl.when(step == pl.num_programs(0) - 1)
    def _():
        qkv3 = qkv.reshape(nb, CHUNK, D_QKV)
        for b in range(nb):
            npool_ref[:, b, :] = ua3[b, CHUNK - POOL_BUF:CHUNK, :]
            nconv_ref[:, b, :] = qkv3[b, CHUNK - (CONV_K - 1):CHUNK, :]
        nssm_ref[...] = s_ref[...]
        yb_s = _head_rms(so_s[...], onw) * _silu(szb_ref[...])
        ymix_s = jnp.concatenate([sya_ref[...], yb_s], axis=1)
        yo_s = jnp.dot(ymix_s.astype(BF16), wout_s[...], preferred_element_type=F32)
        yn_s = yo_s * lax.rsqrt(jnp.mean(yo_s * yo_s, axis=-1, keepdims=True) + EPS) * postw
        ys_ref[:, 0, :] = xs_ref[:, 0, :] + sgate_ref[...] * yn_s


def _layer_spec(shape, l):
    return pl.BlockSpec((None,) + tuple(shape), lambda s, _n=len(shape): (l,) + (0,) * _n)


def _prompt_layer(l, x, mod_p, lw, consts, dec, prev):
    nb, seq, _ = x.shape
    depth = lw["win"].shape[0]
    rows = nb * CHUNK
    nstep = seq // CHUNK
    nseq = dec["x"].shape[0]
    blk_states = nseq * N_HEADS // nstep
    assert blk_states % N_HEADS == 0 and SUBLANES % (blk_states // N_HEADS) == 0
    full = lambda shape: pl.BlockSpec(shape, lambda s, _n=len(shape): (0,) * _n)
    sssm_spec = pl.BlockSpec((None, blk_states, HEAD_DIM, HEAD_DIM), lambda s: (l, s, 0, 0))
    dec_specs = [
        full((nseq, 1, D_MODEL)), pl.BlockSpec((None, nseq, D_MODEL), lambda s: (l, 0, 2)),
        full((nseq, D_B)), full((nseq, D_B)), full((nseq, D_B)),
        full((N_HEADS, nseq, LANES)), full((N_HEADS, nseq, LANES)), full((N_HEADS, nseq, LANES)),
        full((nseq, D_A)), full((nseq, D_B)), sssm_spec,
    ]
    smem = pl.BlockSpec(memory_space=pltpu.SMEM)
    in_specs = [
        smem, smem,
        pl.BlockSpec((nb, CHUNK, D_MODEL), lambda s: (0, s, 0)),
        _layer_spec((nb, 3 * D_MODEL), l),
        full((depth, D_MODEL)), full((depth, D_MODEL)),
        pl.BlockSpec((None, D_MAIN, D_MODEL), lambda s: (l, 0, 0)),
        pl.BlockSpec((None, SUBLANES, D_MODEL), lambda s: (l, D_MAIN // SUBLANES, 0)),
        _layer_spec((CONV_K, D_QKV), l), _layer_spec((N_POOL, POOL_GC, POOL_GC), l),
        full((depth, D_A)), full((depth, HEAD_DIM)), _layer_spec((D_MODEL, D_MODEL), l),
        full((N_POOL, CHUNK, 2 * CHUNK)), full((CHUNK, 2 * CHUNK)),
    ] + dec_specs + [pl.BlockSpec(memory_space=pl.ANY)] * len(prev)
    out_specs = [
        pl.BlockSpec((nb, CHUNK, D_MODEL), lambda s: (0, s, 0)),
        _layer_spec((POOL_BUF, nb, D_A), l), _layer_spec((CONV_K - 1, nb, D_QKV), l),
        _layer_spec((nb * N_HEADS, HEAD_DIM, HEAD_DIM), l),
        sssm_spec, full((nseq, 1, D_MODEL)),
    ]
    out_shape = [
        jax.ShapeDtypeStruct((nb, seq, D_MODEL), F32),
        jax.ShapeDtypeStruct((depth, POOL_BUF, nb, D_A), F32),
        jax.ShapeDtypeStruct((depth, CONV_K - 1, nb, D_QKV), F32),
        jax.ShapeDtypeStruct((depth, nb * N_HEADS, HEAD_DIM, HEAD_DIM), F32),
        jax.ShapeDtypeStruct((depth, nseq * N_HEADS, HEAD_DIM, HEAD_DIM), F32),
        jax.ShapeDtypeStruct((nseq, 1, D_MODEL), F32),
    ]
    scratch = [
        pltpu.VMEM((QKV_TILES, nb, CHUNK + SUBLANES, LANES), F32),
        pltpu.VMEM((nb, 2 * CHUNK, D_A), F32),
        pltpu.VMEM((nb * N_HEADS, HEAD_DIM, HEAD_DIM), F32),
        pltpu.VMEM((rows, D_B), F32), pltpu.VMEM((rows, D_B), F32), pltpu.VMEM((N_HEADS, HEAD_DIM, rows), F32),
        pltpu.VMEM((rows, D_B), F32), pltpu.VMEM((rows, D_B), F32),
        pltpu.VMEM((N_HEADS, rows, LANES), F32), pltpu.VMEM((N_HEADS, rows, LANES), F32),
        pltpu.VMEM((N_HEADS, rows, LANES), F32),
        pltpu.VMEM((SUBLANES, rows), F32),
        pltpu.VMEM((nseq, D_B), F32),
        pltpu.VMEM((D_MODEL, D_MODEL), BF16),
    ]
    n_in = len(in_specs) - len(prev)
    return pl.pallas_call(
        functools.partial(_prompt_kernel, n_alias=len(prev), layer=l),
        grid=(nstep,),
        in_specs=in_specs,
        out_specs=out_specs,
        out_shape=out_shape,
        scratch_shapes=scratch,
        input_output_aliases={n_in + i: 1 + i for i in range(len(prev))},
        compiler_params=pltpu.CompilerParams(dimension_semantics=("arbitrary",), vmem_limit_bytes=VMEM_LIMIT),
        name="prompt_layer",
    )(lw["alog"], lw["dtb"], x, mod_p, lw["prew"], lw["postw"], lw["win"], lw["win"], lw["convw"], lw["poolw"],
      lw["pscale"], lw["onw"], lw["wout"], consts["band"], consts["ltri"],
      dec["x"], dec["mod"], dec["q"], dec["k"], dec["v"], dec["beta"], dec["eg"], dec["qk"], dec["ya"], dec["zb"],
      dec["ssm"], *prev)


def _decode_front_kernel(alog_ref, dtb_ref, x_ref, mod_ref, spool_ref, sconv_ref, prew_ref, win_ref, wba_ref,
                         convw_ref, poolw_ref, pscale_ref, *rest, n_alias, layer):
    rest = rest[n_alias:]
    npool_ref, nconv_ref, q_ref, k_ref, v_ref, beta_ref, eg_ref, qk_ref, ya_ref, zb_ref = rest
    nseq = x_ref.shape[0]
    x = x_ref[:, 0, :]
    mod = mod_ref[...]
    shift = mod[:, 0:D_MODEL]
    scale = mod[:, D_MODEL:2 * D_MODEL]
    a_mul = prew_ref[layer:layer + 1, :] * (1.0 + scale)
    ms = jnp.mean(x * x, axis=-1, keepdims=True)
    hb = (x * lax.rsqrt(ms + EPS) * a_mul + shift).astype(BF16)

    ua = _proj(hb, win_ref, 0, D_A)
    ya_parts = []
    for gi, w in enumerate(POOL_WINDOWS):
        gs = slice(gi * POOL_GC, (gi + 1) * POOL_GC)
        win = ua[:, gs]
        for d in range(1, w):
            win = win + spool_ref[POOL_BUF - d, :, gs]
        cnt = float(min(PAST_LEN + 1, w))
        pooled = win / cnt - ua[:, gs]
        ya_parts.append(_dot(pooled, poolw_ref[gi]))
    za = _proj(hb, win_ref, D_A, 2 * D_A)
    ya_ref[...] = jnp.concatenate(ya_parts, axis=1) * pscale_ref[layer:layer + 1, :] * _silu(za)
    npool_ref[0:POOL_BUF - 1] = spool_ref[1:POOL_BUF]
    npool_ref[POOL_BUF - 1] = ua

    qkv = _proj(hb, win_ref, 2 * D_A, 2 * D_A + D_QKV)
    cw = convw_ref[...]
    acc = qkv * cw[CONV_K - 1:CONV_K]
    for j in range(CONV_K - 1):
        acc = acc + sconv_ref[j] * cw[j:j + 1]
    nconv_ref[0:CONV_K - 2] = sconv_ref[1:CONV_K - 1]
    nconv_ref[CONV_K - 2] = qkv
    qkvc = _silu(acc)
    bac = _gate_logits(hb, wba_ref)
    sig = jax.nn.sigmoid(bac)
    eg = jnp.exp(-jnp.exp(_head_scalars(alog_ref, layer, (1, LANES), 1))
                 * _softplus(bac + _head_scalars(dtb_ref, layer, (1, LANES), 1)))
    for hh in range(N_HEADS):
        sl = slice(hh * HEAD_DIM, (hh + 1) * HEAD_DIM)
        qh = qkvc[:, hh * HEAD_DIM:(hh + 1) * HEAD_DIM]
        kh = qkvc[:, D_B + hh * HEAD_DIM:D_B + (hh + 1) * HEAD_DIM]
        qn = qh * lax.rsqrt(jnp.sum(qh * qh, axis=-1, keepdims=True) + EPS) * (HEAD_DIM ** -0.5)
        kn = kh * lax.rsqrt(jnp.sum(kh * kh, axis=-1, keepdims=True) + EPS)
        q_ref[:, sl] = qn
        k_ref[:, sl] = kn
        qk_ref[hh] = jnp.broadcast_to(jnp.sum(qn * kn, axis=-1, keepdims=True), (nseq, LANES))
        beta_ref[hh] = jnp.broadcast_to(sig[:, hh:hh + 1], (nseq, LANES))
        eg_ref[hh] = jnp.broadcast_to(eg[:, N_HEADS + hh:N_HEADS + hh + 1], (nseq, LANES))
    v_ref[...] = qkvc[:, 2 * D_B:3 * D_B]
    zb_ref[...] = _proj(hb, win_ref, 2 * D_A + D_QKV, D_MAIN)


def _decode_state_step(step, q_ref, k_ref, v_ref, beta_ref, eg_ref, qk_ref, ssm_ref, nssm_ref, o_s):
    nrow = ssm_ref.shape[0] // N_HEADS
    per_tile = SUBLANES // nrow
    tile0 = pl.multiple_of((step // per_tile) * SUBLANES, SUBLANES)
    sub = step % per_tile

    def my_rows(tile):
        out = tile[0:nrow]
        for j in range(1, per_tile):
            out = jnp.where(sub == j, tile[j * nrow:(j + 1) * nrow], out)
        return out

    kblk = my_rows(k_ref[pl.ds(tile0, SUBLANES), :])
    qblk = my_rows(q_ref[pl.ds(tile0, SUBLANES), :])
    vblk = my_rows(v_ref[pl.ds(tile0, SUBLANES), :])
    beta = [my_rows(beta_ref[hh, pl.ds(tile0, SUBLANES), :]) for hh in range(N_HEADS)]
    egs = [my_rows(eg_ref[hh, pl.ds(tile0, SUBLANES), :]) for hh in range(N_HEADS)]
    qks = [my_rows(qk_ref[hh, pl.ds(tile0, SUBLANES), :]) for hh in range(N_HEADS)]
    row8 = lax.broadcasted_iota(jnp.int32, (SUBLANES, HEAD_DIM), 0)
    st = []
    for i in range(nrow):
        for hh in range(N_HEADS):
            sl = slice(hh * HEAD_DIM, (hh + 1) * HEAD_DIM)
            krow, qrow = kblk[i:i + 1, sl], qblk[i:i + 1, sl]
            lhs = jnp.where(row8 == 0, krow, jnp.where(row8 == 1, qrow, 0.0))
            r = jnp.dot(lhs, ssm_ref[i * N_HEADS + hh], preferred_element_type=F32)
            st.append(dict(i=i, hh=hh, sl=sl, krow=krow, r=r))
    for c in st:
        i, hh = c["i"], c["hh"]
        eg = egs[hh][i:i + 1]
        r = c.pop("r")
        delta = (vblk[i:i + 1, c["sl"]] - eg * r[0:1]) * beta[hh][i:i + 1]
        c["o"] = eg * r[1:2] + qks[hh][i:i + 1] * delta
        krow = c.pop("krow")
        k_hi = krow.astype(BF16).astype(F32)
        d_hi = delta.astype(BF16).astype(F32)
        kp = jnp.where(row8 == 0, k_hi, jnp.where(row8 == 1, krow - k_hi, jnp.where(row8 == 2, k_hi, 0.0)))
        dp = jnp.where(row8 == 0, d_hi, jnp.where(row8 == 1, d_hi, jnp.where(row8 == 2, delta - d_hi, 0.0)))
        c["upd"] = _dot_tn(kp, dp)
    for c in st:
        idx = c["i"] * N_HEADS + c["hh"]
        nssm_ref[idx] = ssm_ref[idx] * egs[c["hh"]][c["i"]:c["i"] + 1] + c.pop("upd")
    o_rows = [jnp.concatenate([c["o"] for c in st[i * N_HEADS:(i + 1) * N_HEADS]], axis=1) for i in range(nrow)]
    tile = o_s[pl.ds(tile0, SUBLANES), :]
    placed = jnp.concatenate(o_rows * per_tile, axis=0)
    row_group = lax.broadcasted_iota(jnp.int32, tile.shape, 0) // nrow
    o_s[pl.ds(tile0, SUBLANES), :] = jnp.where(row_group == sub, placed, tile)


def _decode_front(l, x, mod_s, state_pool, state_conv, lw, prev):
    nseq = x.shape[0]
    depth = lw["win"].shape[0]
    full = lambda shape: pl.BlockSpec(shape, lambda s, _n=len(shape): (0,) * _n)
    smem = pl.BlockSpec(memory_space=pltpu.SMEM)
    in_specs = [
        smem, smem,
        full((nseq, 1, D_MODEL)), pl.BlockSpec((None, nseq, 2 * D_MODEL), lambda s: (l, 0, 0)),
        _layer_spec((POOL_BUF, nseq, D_A), l), _layer_spec((CONV_K - 1, nseq, D_QKV), l),
        full((depth, D_MODEL)),
        pl.BlockSpec((None, D_MAIN, D_MODEL), lambda s: (l, 0, 0)),
        pl.BlockSpec((None, SUBLANES, D_MODEL), lambda s: (l, D_MAIN // SUBLANES, 0)),
        _layer_spec((CONV_K, D_QKV), l), _layer_spec((N_POOL, POOL_GC, POOL_GC), l), full((depth, D_A)),
    ] + [pl.BlockSpec(memory_space=pl.ANY)] * len(prev)
    row_out = lambda width: (full((nseq, width)), jax.ShapeDtypeStruct((nseq, width), F32))
    head_out = (full((N_HEADS, nseq, LANES)), jax.ShapeDtypeStruct((N_HEADS, nseq, LANES), F32))
    outs = [
        (_layer_spec((POOL_BUF, nseq, D_A), l), jax.ShapeDtypeStruct((depth, POOL_BUF, nseq, D_A), F32)),
        (_layer_spec((CONV_K - 1, nseq, D_QKV), l), jax.ShapeDtypeStruct((depth, CONV_K - 1, nseq, D_QKV), F32)),
        row_out(D_B), row_out(D_B), row_out(D_B), head_out, head_out, head_out, row_out(D_A), row_out(D_B),
    ]
    n_in = len(in_specs) - len(prev)
    npool, nconv, q, k, v, beta, eg, qk, ya, zb = pl.pallas_call(
        functools.partial(_decode_front_kernel, n_alias=len(prev), layer=l),
        grid=(1,),
        in_specs=in_specs,
        out_specs=[o[0] for o in outs],
        out_shape=[o[1] for o in outs],
        input_output_aliases={n_in + i: i for i in range(len(prev))},
        compiler_params=pltpu.CompilerParams(dimension_semantics=("arbitrary",), vmem_limit_bytes=VMEM_LIMIT),
        name="decode_front",
    )(lw["alog"], lw["dtb"], x, mod_s, state_pool, state_conv, lw["prew"], lw["win"], lw["win"], lw["convw"],
      lw["poolw"], lw["pscale"], *prev)
    return (npool, nconv), dict(x=x, mod=mod_s, q=q, k=k, v=v, beta=beta, eg=eg, qk=qk, ya=ya, zb=zb)


def _constants():
    t = np.arange(CHUNK)[:, None]
    j = np.arange(2 * CHUNK)[None, :]
    band = np.stack([((j <= CHUNK + t) & (j > CHUNK + t - w)) for w in POOL_WINDOWS]).astype(np.float32)
    tri = (np.arange(CHUNK)[None, :] <= t).astype(np.float32)
    ltri = np.concatenate([tri, tri], axis=1)
    return {"band": jnp.asarray(band, BF16), "ltri": jnp.asarray(ltri, BF16)}


def _stacked_weights(pre_norm_w, post_norm_w, w_in, conv_w, pool_w, pool_scale, a_log, dt_bias, o_norm_w, w_out):
    w_in_t = jnp.swapaxes(w_in, 1, 2)
    return {
        "prew": pre_norm_w, "postw": post_norm_w, "win": w_in_t.astype(BF16), "convw": conv_w, "poolw": pool_w,
        "pscale": pool_scale, "alog": a_log, "dtb": dt_bias, "onw": o_norm_w, "wout": w_out,
    }


def kernel(x_prompt, x_sample, c_prompt, c_sample, state_pool, state_conv, state_ssm, w_ada, b_ada, pre_norm_w,
           post_norm_w, w_in, conv_w, pool_w, pool_scale, a_log, dt_bias, o_norm_w, w_out):
    depth = w_in.shape[0]
    nb, seq, _ = x_prompt.shape
    nseq, dec_seq, _ = x_sample.shape
    assert dec_seq == 1 and seq % CHUNK == 0 and nb % 2 == 0 and nseq % (seq // CHUNK) == 0
    consts = _constants()
    lw = _stacked_weights(pre_norm_w, post_norm_w, w_in, conv_w, pool_w, pool_scale, a_log, dt_bias, o_norm_w, w_out)
    mod_p, mod_s = _mod_call(c_prompt, c_sample, w_ada, b_ada)
    ssm_in = state_ssm.reshape(depth, nseq * N_HEADS, HEAD_DIM, HEAD_DIM)
    pool_in = jnp.swapaxes(state_pool, 1, 2)
    conv_in = jnp.swapaxes(state_conv, 1, 2)
    yp, ys = x_prompt, x_sample
    prev_p, prev_s = (), ()
    for l in range(depth):
        prev_s, dec = _decode_front(l, ys, mod_s, pool_in, conv_in, lw, tuple(prev_s))
        yp, *prev_p, ys = _prompt_layer(l, yp, mod_p, lw, consts, dict(dec, ssm=ssm_in), tuple(prev_p))
    npool_p, nconv_p, nssm_p, nssm_s = prev_p
    npool_s, nconv_s = prev_s
    return (yp, ys, jnp.swapaxes(npool_p, 1, 2), jnp.swapaxes(nconv_p, 1, 2),
            nssm_p.reshape(depth, nb, N_HEADS, HEAD_DIM, HEAD_DIM),
            jnp.swapaxes(npool_s, 1, 2), jnp.swapaxes(nconv_s, 1, 2),
            nssm_s.reshape(depth, nseq, N_HEADS, HEAD_DIM, HEAD_DIM))
```

```python
import functools

import jax
import jax.numpy as jnp
import numpy as np
from jax import lax
from jax.experimental import pallas as pl
from jax.experimental.pallas import tpu as pltpu

F32 = jnp.float32
BF16 = jnp.bfloat16

D_MODEL = 1024
D_A = 512
D_B = 512
N_POOL = 4
POOL_WINDOWS = (2, 4, 8, 16)
POOL_GC = 128
POOL_BUF = 15
HEAD_DIM = 128
N_HEADS = 4
D_QKV = 3 * D_B
CONV_K = 4
D_MAIN = 2 * D_A + D_QKV + D_B
PAST_LEN = 16384
EPS = 1e-6
CHUNK = 64
GDN_GROUP = 16
LANES = 128
SUBLANES = 8
QKV_TILES = D_QKV // LANES
PROJ_TILES = 2
MOD_SPLIT = 2
VMEM_LIMIT = 56 * 1024 * 1024


def _dot(a, b):
    return jnp.dot(a.astype(BF16), b.astype(BF16), preferred_element_type=F32)


def _dot_tn(a, b):
    return lax.dot_general(a, b, (((0,), (0,)), ((), ())), preferred_element_type=F32)


def _proj(hb, wt_ref, c0, c1):
    return lax.dot_general(hb, wt_ref[c0:c1, :], (((1,), (1,)), ((), ())), preferred_element_type=F32)


def _gate_logits(hb, wba_ref):
    w = jnp.concatenate([wba_ref[...], jnp.zeros((LANES - SUBLANES, wba_ref.shape[1]), wba_ref.dtype)], axis=0)
    return lax.dot_general(hb, w, (((1,), (1,)), ((), ())), preferred_element_type=F32)


def _silu(x):
    half = 0.5 * x
    return half + half * jnp.tanh(half)


def _softplus(x):
    return jnp.maximum(x, 0.0) + jnp.log1p(jnp.exp(-jnp.abs(x)))


def _block_diag(a, b):
    top = jnp.concatenate([a, jnp.zeros((a.shape[0], b.shape[1]), a.dtype)], axis=1)
    bot = jnp.concatenate([jnp.zeros((b.shape[0], a.shape[1]), b.dtype), b], axis=1)
    return jnp.concatenate([top, bot], axis=0)


def _head_rms(o_all, w):
    parts = []
    for hh in range(N_HEADS):
        oh = o_all[:, hh * HEAD_DIM:(hh + 1) * HEAD_DIM]
        parts.append(oh * lax.rsqrt(jnp.mean(oh * oh, axis=-1, keepdims=True) + EPS) * w)
    return jnp.concatenate(parts, axis=1)


def _mod_kernel(cp_ref, cs_ref, w_ref, b_ref, op_ref, os_ref):
    @pl.when(pl.program_id(1) == 0)
    def _():
        b = b_ref[pl.ds(pl.program_id(0), 1), :]
        op_ref[...] = jnp.broadcast_to(b, op_ref.shape)
        os_ref[...] = jnp.broadcast_to(b, os_ref.shape)

    nb = cp_ref.shape[0]
    c = jnp.concatenate([cp_ref[...], cs_ref[...]], axis=0)
    r = jnp.dot(_silu(c), w_ref[...], preferred_element_type=F32)
    op_ref[...] += r[0:nb]
    os_ref[...] += r[nb:]


def _mod_call(c_prompt, c_sample, w_ada, b_ada):
    depth, _, n3 = w_ada.shape
    nb, nseq = c_prompt.shape[0], c_sample.shape[0]
    kb = D_MODEL // MOD_SPLIT
    return pl.pallas_call(
        _mod_kernel,
        grid=(depth, MOD_SPLIT),
        in_specs=[
            pl.BlockSpec((nb, kb), lambda l, k: (0, k)),
            pl.BlockSpec((nseq, kb), lambda l, k: (0, k)),
            pl.BlockSpec((None, kb, n3), lambda l, k: (l, k, 0)),
            pl.BlockSpec((depth, n3), lambda l, k: (0, 0)),
        ],
        out_specs=[pl.BlockSpec((None, nb, n3), lambda l, k: (l, 0, 0)),
                   pl.BlockSpec((None, nseq, n3), lambda l, k: (l, 0, 0))],
        out_shape=[jax.ShapeDtypeStruct((depth, nb, n3), F32), jax.ShapeDtypeStruct((depth, nseq, n3), F32)],
        compiler_params=pltpu.CompilerParams(dimension_semantics=("arbitrary", "arbitrary")),
        name="adaln_mod",
    )(c_prompt, c_sample, w_ada, b_ada)


def _head_scalars(ref, layer, shape, axis):
    pos = lax.broadcasted_iota(jnp.int32, shape, axis)
    out = jnp.zeros(shape, F32)
    for hh in range(N_HEADS):
        out = jnp.where(pos == N_HEADS + hh, ref[layer, hh], out)
    return out


def _prompt_kernel(alog_ref, dtb_ref, x_ref, mod_ref, prew_ref, postw_ref, win_ref, wba_ref, convw_ref, poolw_ref,
                   pscale_ref, onw_ref, wout_ref, band_ref, ltri_ref,
                   xs_ref, sgate_ref, sq_ref, sk_ref, sv_ref, sbeta_ref, seg_ref, sqk_ref, sya_ref, szb_ref, sssm_ref,
                   *rest, n_alias, layer):
    rest = rest[n_alias:]
    y_ref, npool_ref, nconv_ref, nssm_ref, nsssm_ref, ys_ref = rest[0:6]
    qkv_ext, ua_ext, s_ref, q_s, k_s, kt_s, v_s, o_s, beta_s, gc_s, eg_s, gcrow_s, so_s, wout_s = rest[6:]
    step = pl.program_id(0)
    nb = x_ref.shape[0]
    rows = nb * CHUNK

    @pl.when(step == 0)
    def _():
        qkv_ext[:, :, 0:SUBLANES, :] = jnp.zeros((QKV_TILES, nb, SUBLANES, LANES), F32)
        ua_ext[:, 0:CHUNK, :] = jnp.zeros((nb, CHUNK, D_A), F32)
        s_ref[...] = jnp.zeros(s_ref.shape, F32)
        so_s[...] = jnp.zeros(so_s.shape, F32)
        wout_s[...] = wout_ref[...].astype(BF16)

    x = x_ref[...]
    mod = mod_ref[...][:, None, :]
    shift = mod[:, :, 0:D_MODEL]
    scale = mod[:, :, D_MODEL:2 * D_MODEL]
    gate = mod[:, :, 2 * D_MODEL:3 * D_MODEL]
    prew, postw = prew_ref[layer:layer + 1, :], postw_ref[layer:layer + 1, :]
    pscale, onw = pscale_ref[layer:layer + 1, :], onw_ref[layer:layer + 1, :]
    a_mul = prew[None] * (1.0 + scale)
    ms = jnp.mean(x * x, axis=-1, keepdims=True)
    h = x * lax.rsqrt(ms + EPS) * a_mul + shift
    hb = h.reshape(rows, D_MODEL).astype(BF16)

    ua = _proj(hb, win_ref, 0, D_A)
    ua3 = ua.reshape(nb, CHUNK, D_A)
    ua_ext[:, CHUNK:2 * CHUNK, :] = ua3
    pos = lax.broadcasted_iota(jnp.int32, (CHUNK, POOL_GC), 0) + step * CHUNK
    pooled_groups = []
    for gi, w in enumerate(POOL_WINDOWS):
        cnt = jnp.minimum(pos + 1, w).astype(F32)
        gsl = slice(gi * POOL_GC, (gi + 1) * POOL_GC)
        per_b = []
        for b in range(0, nb, 2):
            ext = jnp.concatenate([ua_ext[b, :, gsl], ua_ext[b + 1, :, gsl]], axis=1)
            win = jnp.dot(band_ref[gi], ext.astype(BF16), preferred_element_type=F32)
            per_b.append(win[:, 0:POOL_GC] / cnt - ua3[b, :, gsl])
            per_b.append(win[:, POOL_GC:2 * POOL_GC] / cnt - ua3[b + 1, :, gsl])
        pooled_groups.append(jnp.concatenate(per_b, axis=0))
    ya = jnp.concatenate([_dot(jnp.concatenate(pooled_groups[2 * g2:2 * g2 + 2], axis=1),
                               _block_diag(poolw_ref[2 * g2], poolw_ref[2 * g2 + 1]))
                          for g2 in range(N_POOL // 2)], axis=1)
    za = _proj(hb, win_ref, D_A, 2 * D_A)
    ya = ya * pscale * _silu(za)

    cw = convw_ref[...]
    for t0 in range(0, QKV_TILES, PROJ_TILES):
        c0 = 2 * D_A + t0 * LANES
        part = _proj(hb, win_ref, c0, c0 + PROJ_TILES * LANES)
        for i in range(PROJ_TILES):
            tl = t0 + i
            cols = slice(tl * LANES, (tl + 1) * LANES)
            xt = part[:, i * LANES:(i + 1) * LANES].reshape(nb, CHUNK, LANES)
            qkv_ext[tl, :, SUBLANES:SUBLANES + CHUNK, :] = xt
            acc = xt * cw[CONV_K - 1:CONV_K, cols][None]
            for j in range(CONV_K - 1):
                acc = acc + qkv_ext[tl, :, pl.ds(SUBLANES - (CONV_K - 1) + j, CHUNK), :] * cw[j:j + 1, cols][None]
            a = _silu(acc).reshape(rows, LANES)
            kind, hh = divmod(tl, N_HEADS)
            sl = slice(hh * HEAD_DIM, (hh + 1) * HEAD_DIM)
            if kind == 0:
                q_s[:, sl] = a * lax.rsqrt(jnp.sum(a * a, axis=-1, keepdims=True) + EPS) * (HEAD_DIM ** -0.5)
            elif kind == 1:
                kn = a * lax.rsqrt(jnp.sum(a * a, axis=-1, keepdims=True) + EPS)
                k_s[:, sl] = kn
                kt_s[hh] = kn.T
            else:
                v_s[:, sl] = a

    bac = _gate_logits(hb, wba_ref)
    bar = bac.T[0:SUBLANES]
    sig = jax.nn.sigmoid(bac)
    gcol = (-jnp.exp(_head_scalars(alog_ref, layer, (1, LANES), 1))
            * _softplus(bac + _head_scalars(dtb_ref, layer, (1, LANES), 1)))
    g_hi = gcol.astype(BF16)
    g_lo = (gcol - g_hi.astype(F32)).astype(BF16)
    gc_parts = []
    for b in range(0, nb, 2):
        ra, rb = slice(b * CHUNK, (b + 1) * CHUNK), slice((b + 1) * CHUNK, (b + 2) * CHUNK)
        hi_lo = jnp.concatenate([jnp.concatenate([g_hi[ra], g_lo[ra]], axis=0),
                                 jnp.concatenate([g_hi[rb], g_lo[rb]], axis=0)], axis=1)
        both = jnp.dot(ltri_ref[...], hi_lo, preferred_element_type=F32)
        gc_parts += [both[:, 0:LANES], both[:, LANES:2 * LANES]]
    gc = jnp.concatenate(gc_parts, axis=0)
    for hh in range(N_HEADS):
        beta_s[hh] = jnp.broadcast_to(sig[:, hh:hh + 1], (rows, LANES))
        gcb = jnp.broadcast_to(gc[:, N_HEADS + hh:N_HEADS + hh + 1], (rows, LANES))
        gc_s[hh] = gcb
        eg_s[hh] = jnp.exp(gcb)
    grow = (-jnp.exp(_head_scalars(alog_ref, layer, (SUBLANES, rows), 0))
            * _softplus(bar + _head_scalars(dtb_ref, layer, (SUBLANES, rows), 0)))
    lane_in_chunk = lax.broadcasted_iota(jnp.int32, grow.shape, 1) % CHUNK
    sh = 1
    while sh < CHUNK:
        grow = grow + jnp.where(lane_in_chunk >= sh, pltpu.roll(grow, sh, 1), 0.0)
        sh *= 2
    gcrow_s[...] = grow

    _decode_state_step(step, sq_ref, sk_ref, sv_ref, sbeta_ref, seg_ref, sqk_ref, sssm_ref, nsssm_ref, so_s)

    ri = lax.broadcasted_iota(jnp.int32, (CHUNK, LANES), 0)
    li = lax.broadcasted_iota(jnp.int32, (CHUNK, LANES), 1)
    lj = li % CHUNK
    lo_half = li < CHUNK
    strict = ri > lj
    causal = ri >= lj
    eye2 = (ri == lj).astype(F32)
    lo_half_hd = lax.broadcasted_iota(jnp.int32, (HEAD_DIM, LANES), 1) < CHUNK

    def pair_mul(xp, yp):
        ybd = jnp.concatenate([jnp.where(lo_half, yp, 0.0), jnp.where(lo_half, 0.0, yp)], axis=0)
        return _dot(xp, ybd)

    chains = [(p, hh) for p in range(nb // 2) for hh in range(N_HEADS)]
    for g0 in range(0, len(chains), GDN_GROUP):
        st = []
        for p, hh in chains[g0:g0 + GDN_GROUP]:
            ra = slice(2 * p * CHUNK, (2 * p + 1) * CHUNK)
            rb = slice((2 * p + 1) * CHUNK, (2 * p + 2) * CHUNK)
            sl = slice(hh * HEAD_DIM, (hh + 1) * HEAD_DIM)
            kba, kbb = k_s[ra, sl] * beta_s[hh, ra, :], k_s[rb, sl] * beta_s[hh, rb, :]
            lhs1 = jnp.concatenate([jnp.concatenate([kba, kbb], axis=1),
                                    jnp.concatenate([q_s[ra, sl], q_s[rb, sl]], axis=1)], axis=0)
            kt = kt_s[hh, :, 2 * p * CHUNK:(2 * p + 2) * CHUNK]
            ktbd = jnp.concatenate([jnp.where(lo_half_hd, kt, 0.0), jnp.where(lo_half_hd, 0.0, kt)], axis=0)
            kkqk = _dot(lhs1, ktbd)
            st.append(dict(ra=ra, rb=rb, sl=sl, hh=hh, p=p, kkqk=kkqk))
        for c in st:
            hh, p = c["hh"], c["p"]
            gcol_p = jnp.where(lo_half, gc_s[hh, c["ra"], :], gc_s[hh, c["rb"], :])
            grow_p = jnp.broadcast_to(gcrow_s[N_HEADS + hh:N_HEADS + hh + 1, 2 * p * CHUNK:(2 * p + 2) * CHUNK],
                                      (CHUNK, LANES))
            dec = jnp.exp(jnp.minimum(gcol_p - grow_p, 0.0))
            kkqk = c.pop("kkqk")
            c["npow"] = jnp.where(strict, kkqk[0:CHUNK] * dec, 0.0)
            c["qkm"] = jnp.where(causal, kkqk[CHUNK:2 * CHUNK] * dec, 0.0)
            c["t"] = eye2 - c["npow"]
        for c in st:
            c["npow"] = pair_mul(c["npow"], c["npow"])
        for _ in range(4):
            for c in st:
                both = pair_mul(jnp.concatenate([c["t"], c["npow"]], axis=0), c["npow"])
                c["t"] = c["t"] + both[0:CHUNK]
                c["npow"] = both[CHUNK:2 * CHUNK]
        for c in st:
            c["t"] = c["t"] + pair_mul(c["t"], c["npow"])
        for c in st:
            ra, rb, sl, hh = c["ra"], c["rb"], c["sl"], c["hh"]
            ba, bb = beta_s[hh, ra, :], beta_s[hh, rb, :]
            kba, kbb = k_s[ra, sl] * ba, k_s[rb, sl] * bb
            rhs = jnp.concatenate([_block_diag(v_s[ra, sl] * ba, v_s[rb, sl] * bb),
                                   _block_diag(kba * eg_s[hh, ra, :], kbb * eg_s[hh, rb, :])], axis=1)
            c["uw"] = _dot(c.pop("t"), rhs)
        for c in st:
            ra, rb, sl, hh, p = c["ra"], c["rb"], c["sl"], c["hh"], c["p"]
            c["ia"], c["ib"] = 2 * p * N_HEADS + hh, (2 * p + 1) * N_HEADS + hh
            lhs3 = jnp.concatenate([c["uw"][:, 2 * HEAD_DIM:4 * HEAD_DIM],
                                    jnp.concatenate([q_s[ra, sl] * eg_s[hh, ra, :], q_s[rb, sl] * eg_s[hh, rb, :]],
                                                    axis=1)], axis=0)
            c["r3"] = _dot(lhs3, _block_diag(s_ref[c["ia"]], s_ref[c["ib"]]))
        for c in st:
            ra, rb, sl, hh, p = c["ra"], c["rb"], c["sl"], c["hh"], c["p"]
            r3 = c.pop("r3")
            vn = c.pop("uw")[:, 0:2 * HEAD_DIM] - r3[0:CHUNK]
            gla = gc_s[hh, (2 * p + 1) * CHUNK - 1:(2 * p + 1) * CHUNK, :]
            glb = gc_s[hh, (2 * p + 2) * CHUNK - 1:(2 * p + 2) * CHUNK, :]
            pair_lanes = slice(2 * p * CHUNK, (2 * p + 2) * CHUNK)
            to_end = jnp.exp(jnp.where(lo_half[0:1], gla, glb) - gcrow_s[N_HEADS + hh:N_HEADS + hh + 1, pair_lanes])
            kgt = kt_s[hh, :, pair_lanes] * to_end
            both = _dot(jnp.concatenate([c.pop("qkm"), kgt], axis=0),
                        _block_diag(vn[:, 0:HEAD_DIM], vn[:, HEAD_DIM:2 * HEAD_DIM]))
            o = r3[CHUNK:2 * CHUNK] + both[0:CHUNK]
            o_s[ra, sl] = o[:, 0:HEAD_DIM]
            o_s[rb, sl] = o[:, HEAD_DIM:2 * HEAD_DIM]
            upd = both[CHUNK:CHUNK + HEAD_DIM]
            s_ref[c["ia"]] = s_ref[c["ia"]] * jnp.exp(gla) + upd[:, 0:HEAD_DIM]
            s_ref[c["ib"]] = s_ref[c["ib"]] * jnp.exp(glb) + upd[:, HEAD_DIM:2 * HEAD_DIM]

    zb = _proj(hb, win_ref, 2 * D_A + D_QKV, D_MAIN)
    yb = _head_rms(o_s[...], onw) * _silu(zb)
    ymix = jnp.concatenate([ya, yb], axis=1)
    yo = jnp.dot(ymix.astype(BF16), wout_s[...], preferred_element_type=F32)
    yn = yo * lax.rsqrt(jnp.mean(yo * yo, axis=-1, keepdims=True) + EPS)
    y_ref[...] = x + yn.reshape(nb, CHUNK, D_MODEL) * (gate * postw[None])

    qkv_ext[:, :, 0:SUBLANES, :] = qkv_ext[:, :, CHUNK:CHUNK + SUBLANES, :]
    ua_ext[:, 0:CHUNK, :] = ua3

    @pl.when(step == pl.num_programs(0) - 1)
    def _():
        for b in range(nb):
            npool_ref[:, b, :] = ua3[b, CHUNK - POOL_BUF:CHUNK, :]
            for tl in range(QKV_TILES):
                nconv_ref[:, b, tl * LANES:(tl + 1) * LANES] = qkv_ext[tl, b, SUBLANES + CHUNK - (CONV_K - 1):
                                                                       SUBLANES + CHUNK, :]
        nssm_ref[...] = s_ref[...]
        yb_s = _head_rms(so_s[...], onw) * _silu(szb_ref[...])
        ymix_s = jnp.concatenate([sya_ref[...], yb_s], axis=1)
        yo_s = jnp.dot(ymix_s.astype(BF16), wout_s[...], preferred_element_type=F32)
        yn_s = yo_s * lax.rsqrt(jnp.mean(yo_s * yo_s, axis=-1, keepdims=True) + EPS) * postw
        ys_ref[:, 0, :] = xs_ref[:, 0, :] + sgate_ref[...] * yn_s


def _layer_spec(shape, l):
    return pl.BlockSpec((None,) + tuple(shape), lambda s, _n=len(shape): (l,) + (0,) * _n)


def _prompt_layer(l, x, mod_p, lw, consts, dec, prev):
    nb, seq, _ = x.shape
    depth = lw["win"].shape[0]
    rows = nb * CHUNK
    nstep = seq // CHUNK
    nseq = dec["x"].shape[0]
    blk_states = nseq * N_HEADS // nstep
    assert blk_states % N_HEADS == 0 and SUBLANES % (blk_states // N_HEADS) == 0
    full = lambda shape: pl.BlockSpec(shape, lambda s, _n=len(shape): (0,) * _n)
    sssm_spec = pl.BlockSpec((None, blk_states, HEAD_DIM, HEAD_DIM), lambda s: (l, s, 0, 0))
    dec_specs = [
        full((nseq, 1, D_MODEL)), pl.BlockSpec((None, nseq, D_MODEL), lambda s: (l, 0, 2)),
        full((nseq, D_B)), full((nseq, D_B)), full((nseq, D_B)),
        full((N_HEADS, nseq, LANES)), full((N_HEADS, nseq, LANES)), full((N_HEADS, nseq, LANES)),
        full((nseq, D_A)), full((nseq, D_B)), sssm_spec,
    ]
    smem = pl.BlockSpec(memory_space=pltpu.SMEM)
    in_specs = [
        smem, smem,
        pl.BlockSpec((nb, CHUNK, D_MODEL), lambda s: (0, s, 0)),
        _layer_spec((nb, 3 * D_MODEL), l),
        full((depth, D_MODEL)), full((depth, D_MODEL)),
        pl.BlockSpec((None, D_MAIN, D_MODEL), lambda s: (l, 0, 0)),
        pl.BlockSpec((None, SUBLANES, D_MODEL), lambda s: (l, D_MAIN // SUBLANES, 0)),
        _layer_spec((CONV_K, D_QKV), l), _layer_spec((N_POOL, POOL_GC, POOL_GC), l),
        full((depth, D_A)), full((depth, HEAD_DIM)), _layer_spec((D_MODEL, D_MODEL), l),
        full((N_POOL, CHUNK, 2 * CHUNK)), full((CHUNK, 2 * CHUNK)),
    ] + dec_specs + [pl.BlockSpec(memory_space=pl.ANY)] * len(prev)
    out_specs = [
        pl.BlockSpec((nb, CHUNK, D_MODEL), lambda s: (0, s, 0)),
        _layer_spec((POOL_BUF, nb, D_A), l), _layer_spec((CONV_K - 1, nb, D_QKV), l),
        _layer_spec((nb * N_HEADS, HEAD_DIM, HEAD_DIM), l),
        sssm_spec, full((nseq, 1, D_MODEL)),
    ]
    out_shape = [
        jax.ShapeDtypeStruct((nb, seq, D_MODEL), F32),
        jax.ShapeDtypeStruct((depth, POOL_BUF, nb, D_A), F32),
        jax.ShapeDtypeStruct((depth, CONV_K - 1, nb, D_QKV), F32),
        jax.ShapeDtypeStruct((depth, nb * N_HEADS, HEAD_DIM, HEAD_DIM), F32),
        jax.ShapeDtypeStruct((depth, nseq * N_HEADS, HEAD_DIM, HEAD_DIM), F32),
        jax.ShapeDtypeStruct((nseq, 1, D_MODEL), F32),
    ]
    scratch = [
        pltpu.VMEM((QKV_TILES, nb, CHUNK + SUBLANES, LANES), F32),
        pltpu.VMEM((nb, 2 * CHUNK, D_A), F32),
        pltpu.VMEM((nb * N_HEADS, HEAD_DIM, HEAD_DIM), F32),
        pltpu.VMEM((rows, D_B), F32), pltpu.VMEM((rows, D_B), F32), pltpu.VMEM((N_HEADS, HEAD_DIM, rows), F32),
        pltpu.VMEM((rows, D_B), F32), pltpu.VMEM((rows, D_B), F32),
        pltpu.VMEM((N_HEADS, rows, LANES), F32), pltpu.VMEM((N_HEADS, rows, LANES), F32),
        pltpu.VMEM((N_HEADS, rows, LANES), F32),
        pltpu.VMEM((SUBLANES, rows), F32),
        pltpu.VMEM((nseq, D_B), F32),
        pltpu.VMEM((D_MODEL, D_MODEL), BF16),
    ]
    n_in = len(in_specs) - len(prev)
    return pl.pallas_call(
        functools.partial(_prompt_kernel, n_alias=len(prev), layer=l),
        grid=(nstep,),
        in_specs=in_specs,
        out_specs=out_specs,
        out_shape=out_shape,
        scratch_shapes=scratch,
        input_output_aliases={n_in + i: 1 + i for i in range(len(prev))},
        compiler_params=pltpu.CompilerParams(dimension_semantics=("arbitrary",), vmem_limit_bytes=VMEM_LIMIT),
        name="prompt_layer",
    )(lw["alog"], lw["dtb"], x, mod_p, lw["prew"], lw["postw"], lw["win"], lw["win"], lw["convw"], lw["poolw"],
      lw["pscale"], lw["onw"], lw["wout"], consts["band"], consts["ltri"],
      dec["x"], dec["mod"], dec["q"], dec["k"], dec["v"], dec["beta"], dec["eg"], dec["qk"], dec["ya"], dec["zb"],
      dec["ssm"], *prev)


def _decode_front_kernel(alog_ref, dtb_ref, x_ref, mod_ref, spool_ref, sconv_ref, prew_ref, win_ref, wba_ref,
                         convw_ref, poolw_ref, pscale_ref, *rest, n_alias, layer):
    rest = rest[n_alias:]
    npool_ref, nconv_ref, q_ref, k_ref, v_ref, beta_ref, eg_ref, qk_ref, ya_ref, zb_ref = rest
    nseq = x_ref.shape[0]
    x = x_ref[:, 0, :]
    mod = mod_ref[...]
    shift = mod[:, 0:D_MODEL]
    scale = mod[:, D_MODEL:2 * D_MODEL]
    a_mul = prew_ref[layer:layer + 1, :] * (1.0 + scale)
    ms = jnp.mean(x * x, axis=-1, keepdims=True)
    hb = (x * lax.rsqrt(ms + EPS) * a_mul + shift).astype(BF16)

    ua = _proj(hb, win_ref, 0, D_A)
    ya_parts = []
    for gi, w in enumerate(POOL_WINDOWS):
        gs = slice(gi * POOL_GC, (gi + 1) * POOL_GC)
        win = ua[:, gs]
        for d in range(1, w):
            win = win + spool_ref[POOL_BUF - d, :, gs]
        cnt = float(min(PAST_LEN + 1, w))
        pooled = win / cnt - ua[:, gs]
        ya_parts.append(_dot(pooled, poolw_ref[gi]))
    za = _proj(hb, win_ref, D_A, 2 * D_A)
    ya_ref[...] = jnp.concatenate(ya_parts, axis=1) * pscale_ref[layer:layer + 1, :] * _silu(za)
    npool_ref[0:POOL_BUF - 1] = spool_ref[1:POOL_BUF]
    npool_ref[POOL_BUF - 1] = ua

    qkv = _proj(hb, win_ref, 2 * D_A, 2 * D_A + D_QKV)
    cw = convw_ref[...]
    acc = qkv * cw[CONV_K - 1:CONV_K]
    for j in range(CONV_K - 1):
        acc = acc + sconv_ref[j] * cw[j:j + 1]
    nconv_ref[0:CONV_K - 2] = sconv_ref[1:CONV_K - 1]
    nconv_ref[CONV_K - 2] = qkv
    qkvc = _silu(acc)
    bac = _gate_logits(hb, wba_ref)
    sig = jax.nn.sigmoid(bac)
    eg = jnp.exp(-jnp.exp(_head_scalars(alog_ref, layer, (1, LANES), 1))
                 * _softplus(bac + _head_scalars(dtb_ref, layer, (1, LANES), 1)))
    for hh in range(N_HEADS):
        sl = slice(hh * HEAD_DIM, (hh + 1) * HEAD_DIM)
        qh = qkvc[:, hh * HEAD_DIM:(hh + 1) * HEAD_DIM]
        kh = qkvc[:, D_B + hh * HEAD_DIM:D_B + (hh + 1) * HEAD_DIM]
        qn = qh * lax.rsqrt(jnp.sum(qh * qh, axis=-1, keepdims=True) + EPS) * (HEAD_DIM ** -0.5)
        kn = kh * lax.rsqrt(jnp.sum(kh * kh, axis=-1, keepdims=True) + EPS)
        q_ref[:, sl] = qn
        k_ref[:, sl] = kn
        qk_ref[hh] = jnp.broadcast_to(jnp.sum(qn * kn, axis=-1, keepdims=True), (nseq, LANES))
        beta_ref[hh] = jnp.broadcast_to(sig[:, hh:hh + 1], (nseq, LANES))
        eg_ref[hh] = jnp.broadcast_to(eg[:, N_HEADS + hh:N_HEADS + hh + 1], (nseq, LANES))
    v_ref[...] = qkvc[:, 2 * D_B:3 * D_B]
    zb_ref[...] = _proj(hb, win_ref, 2 * D_A + D_QKV, D_MAIN)


def _decode_state_step(step, q_ref, k_ref, v_ref, beta_ref, eg_ref, qk_ref, ssm_ref, nssm_ref, o_s):
    nrow = ssm_ref.shape[0] // N_HEADS
    per_tile = SUBLANES // nrow
    tile0 = pl.multiple_of((step // per_tile) * SUBLANES, SUBLANES)
    sub = step % per_tile

    def my_rows(tile):
        out = tile[0:nrow]
        for j in range(1, per_tile):
            out = jnp.where(sub == j, tile[j * nrow:(j + 1) * nrow], out)
        return out

    kblk = my_rows(k_ref[pl.ds(tile0, SUBLANES), :])
    qblk = my_rows(q_ref[pl.ds(tile0, SUBLANES), :])
    vblk = my_rows(v_ref[pl.ds(tile0, SUBLANES), :])
    beta = [my_rows(beta_ref[hh, pl.ds(tile0, SUBLANES), :]) for hh in range(N_HEADS)]
    egs = [my_rows(eg_ref[hh, pl.ds(tile0, SUBLANES), :]) for hh in range(N_HEADS)]
    qks = [my_rows(qk_ref[hh, pl.ds(tile0, SUBLANES), :]) for hh in range(N_HEADS)]
    row8 = lax.broadcasted_iota(jnp.int32, (SUBLANES, HEAD_DIM), 0)
    st = []
    for i in range(nrow):
        for hh in range(N_HEADS):
            sl = slice(hh * HEAD_DIM, (hh + 1) * HEAD_DIM)
            krow, qrow = kblk[i:i + 1, sl], qblk[i:i + 1, sl]
            lhs = jnp.where(row8 == 0, krow, jnp.where(row8 == 1, qrow, 0.0))
            r = jnp.dot(lhs, ssm_ref[i * N_HEADS + hh], preferred_element_type=F32)
            st.append(dict(i=i, hh=hh, sl=sl, krow=krow, r=r))
    for c in st:
        i, hh = c["i"], c["hh"]
        eg = egs[hh][i:i + 1]
        r = c.pop("r")
        delta = (vblk[i:i + 1, c["sl"]] - eg * r[0:1]) * beta[hh][i:i + 1]
        c["o"] = eg * r[1:2] + qks[hh][i:i + 1] * delta
        krow = c.pop("krow")
        k_hi = krow.astype(BF16).astype(F32)
        d_hi = delta.astype(BF16).astype(F32)
        kp = jnp.where(row8 == 0, k_hi, jnp.where(row8 == 1, krow - k_hi, jnp.where(row8 == 2, k_hi, 0.0)))
        dp = jnp.where(row8 == 0, d_hi, jnp.where(row8 == 1, d_hi, jnp.where(row8 == 2, delta - d_hi, 0.0)))
        c["upd"] = _dot_tn(kp, dp)
    for c in st:
        idx = c["i"] * N_HEADS + c["hh"]
        nssm_ref[idx] = ssm_ref[idx] * egs[c["hh"]][c["i"]:c["i"] + 1] + c.pop("upd")
    o_rows = [jnp.concatenate([c["o"] for c in st[i * N_HEADS:(i + 1) * N_HEADS]], axis=1) for i in range(nrow)]
    tile = o_s[pl.ds(tile0, SUBLANES), :]
    placed = jnp.concatenate(o_rows * per_tile, axis=0)
    row_group = lax.broadcasted_iota(jnp.int32, tile.shape, 0) // nrow
    o_s[pl.ds(tile0, SUBLANES), :] = jnp.where(row_group == sub, placed, tile)


def _decode_front(l, x, mod_s, state_pool, state_conv, lw, prev):
    nseq = x.shape[0]
    depth = lw["win"].shape[0]
    full = lambda shape: pl.BlockSpec(shape, lambda s, _n=len(shape): (0,) * _n)
    smem = pl.BlockSpec(memory_space=pltpu.SMEM)
    in_specs = [
        smem, smem,
        full((nseq, 1, D_MODEL)), pl.BlockSpec((None, nseq, 2 * D_MODEL), lambda s: (l, 0, 0)),
        _layer_spec((POOL_BUF, nseq, D_A), l), _layer_spec((CONV_K - 1, nseq, D_QKV), l),
        full((depth, D_MODEL)),
        pl.BlockSpec((None, D_MAIN, D_MODEL), lambda s: (l, 0, 0)),
        pl.BlockSpec((None, SUBLANES, D_MODEL), lambda s: (l, D_MAIN // SUBLANES, 0)),
        _layer_spec((CONV_K, D_QKV), l), _layer_spec((N_POOL, POOL_GC, POOL_GC), l), full((depth, D_A)),
    ] + [pl.BlockSpec(memory_space=pl.ANY)] * len(prev)
    row_out = lambda width: (full((nseq, width)), jax.ShapeDtypeStruct((nseq, width), F32))
    head_out = (full((N_HEADS, nseq, LANES)), jax.ShapeDtypeStruct((N_HEADS, nseq, LANES), F32))
    outs = [
        (_layer_spec((POOL_BUF, nseq, D_A), l), jax.ShapeDtypeStruct((depth, POOL_BUF, nseq, D_A), F32)),
        (_layer_spec((CONV_K - 1, nseq, D_QKV), l), jax.ShapeDtypeStruct((depth, CONV_K - 1, nseq, D_QKV), F32)),
        row_out(D_B), row_out(D_B), row_out(D_B), head_out, head_out, head_out, row_out(D_A), row_out(D_B),
    ]
    n_in = len(in_specs) - len(prev)
    npool, nconv, q, k, v, beta, eg, qk, ya, zb = pl.pallas_call(
        functools.partial(_decode_front_kernel, n_alias=len(prev), layer=l),
        grid=(1,),
        in_specs=in_specs,
        out_specs=[o[0] for o in outs],
        out_shape=[o[1] for o in outs],
        input_output_aliases={n_in + i: i for i in range(len(prev))},
        compiler_params=pltpu.CompilerParams(dimension_semantics=("arbitrary",), vmem_limit_bytes=VMEM_LIMIT),
        name="decode_front",
    )(lw["alog"], lw["dtb"], x, mod_s, state_pool, state_conv, lw["prew"], lw["win"], lw["win"], lw["convw"],
      lw["poolw"], lw["pscale"], *prev)
    return (npool, nconv), dict(x=x, mod=mod_s, q=q, k=k, v=v, beta=beta, eg=eg, qk=qk, ya=ya, zb=zb)


def _constants():
    t = np.arange(CHUNK)[:, None]
    j = np.arange(2 * CHUNK)[None, :]
    band = np.stack([((j <= CHUNK + t) & (j > CHUNK + t - w)) for w in POOL_WINDOWS]).astype(np.float32)
    tri = (np.arange(CHUNK)[None, :] <= t).astype(np.float32)
    ltri = np.concatenate([tri, tri], axis=1)
    return {"band": jnp.asarray(band, BF16), "ltri": jnp.asarray(ltri, BF16)}


def _stacked_weights(pre_norm_w, post_norm_w, w_in, conv_w, pool_w, pool_scale, a_log, dt_bias, o_norm_w, w_out):
    w_in_t = jnp.swapaxes(w_in, 1, 2)
    return {
        "prew": pre_norm_w, "postw": post_norm_w, "win": w_in_t.astype(BF16), "convw": conv_w, "poolw": pool_w,
        "pscale": pool_scale, "alog": a_log, "dtb": dt_bias, "onw": o_norm_w, "wout": w_out,
    }


def kernel(x_prompt, x_sample, c_prompt, c_sample, state_pool, state_conv, state_ssm, w_ada, b_ada, pre_norm_w,
           post_norm_w, w_in, conv_w, pool_w, pool_scale, a_log, dt_bias, o_norm_w, w_out):
    depth = w_in.shape[0]
    nb, seq, _ = x_prompt.shape
    nseq, dec_seq, _ = x_sample.shape
    assert dec_seq == 1 and seq % CHUNK == 0 and nb % 2 == 0 and nseq % (seq // CHUNK) == 0
    consts = _constants()
    lw = _stacked_weights(pre_norm_w, post_norm_w, w_in, conv_w, pool_w, pool_scale, a_log, dt_bias, o_norm_w, w_out)
    mod_p, mod_s = _mod_call(c_prompt, c_sample, w_ada, b_ada)
    ssm_in = state_ssm.reshape(depth, nseq * N_HEADS, HEAD_DIM, HEAD_DIM)
    pool_in = jnp.swapaxes(state_pool, 1, 2)
    conv_in = jnp.swapaxes(state_conv, 1, 2)
    yp, ys = x_prompt, x_sample
    prev_p, prev_s = (), ()
    for l in range(depth):
        prev_s, dec = _decode_front(l, ys, mod_s, pool_in, conv_in, lw, tuple(prev_s))
        yp, *prev_p, ys = _prompt_layer(l, yp, mod_p, lw, consts, dict(dec, ssm=ssm_in), tuple(prev_p))
    npool_p, nconv_p, nssm_p, nssm_s = prev_p
    npool_s, nconv_s = prev_s
    return (yp, ys, jnp.swapaxes(npool_p, 1, 2), jnp.swapaxes(nconv_p, 1, 2),
            nssm_p.reshape(depth, nb, N_HEADS, HEAD_DIM, HEAD_DIM),
            jnp.swapaxes(npool_s, 1, 2), jnp.swapaxes(nconv_s, 1, 2),
            nssm_s.reshape(depth, nseq, N_HEADS, HEAD_DIM, HEAD_DIM))
```
